```python
import math
import jax
import jax.numpy as jnp
from jax import lax
import numpy as np

D_MODEL = 1024
BATCH = 16
SEQ = 256
DEPTH = 4
DEC_BATCH = 4
DEC_SEQ = 4096
PAST_LEN = 512

GRID_W = 64
EPS = 1e-6
A_WIDTH = 512
A_CONV = 3
SSM_INNER = 1024
SSM_HEAD_DIM = 64
SSM_HEADS = SSM_INNER // SSM_HEAD_DIM
SSM_GROUPS = 2
SSM_STATE = 128
SSM_CONV = 3
SSM_CONV_DIM = SSM_INNER + 2 * SSM_GROUPS * SSM_STATE
CHUNK = 128
MLA_HEADS = 8
Q_LORA = 256
KV_LORA = 256
NOPE_DIM = 64
ROPE_DIM = 32
V_DIM = 64
QK_DIM = NOPE_DIM + ROPE_DIM
ROPE_BASE = 10000.0
Q_BLOCK = 128
FF_DIM = -(-8 * D_MODEL // (3 * 256)) * 256
N_BRANCH = 3
IN_SIZES = (A_WIDTH, A_WIDTH, A_WIDTH,
            SSM_INNER, SSM_CONV_DIM, SSM_HEADS,
            Q_LORA, KV_LORA, ROPE_DIM,
            N_BRANCH * D_MODEL)
IN_COLS = sum(IN_SIZES)

kernel_name = 'hybrid_diffusion_parallel_conv_ssd_mla_step'


def rms_norm(x, w):
    xf = x.astype(jnp.float32)
    y = xf * lax.rsqrt(jnp.mean(xf * xf, axis=-1, keepdims=True) + EPS)
    return (y * w.astype(jnp.float32)).astype(x.dtype)


def split_columns(proj):
    outs, start = [], 0
    for size in IN_SIZES:
        outs.append(proj[..., start:start + size])
        start += size
    return outs


def conv3_centred(x, w):
    xp = jnp.pad(x, ((0, 0), (1, 1), (0, 0)))
    return xp[:, :-2] * w[0] + xp[:, 1:-1] * w[1] + xp[:, 2:] * w[2]


def axial_rope_tables(n_tokens):
    n_rows = n_tokens // GRID_W
    row = jnp.repeat(jnp.arange(n_rows, dtype=jnp.float32), GRID_W)
    col = jnp.tile(jnp.arange(GRID_W, dtype=jnp.float32), n_rows)
    pairs_per_axis = ROPE_DIM // 4
    inv = ROPE_BASE ** (-jnp.arange(pairs_per_axis, dtype=jnp.float32) / pairs_per_axis)
    ang = jnp.concatenate([row[:, None] * inv, col[:, None] * inv], axis=-1)
    return jnp.cos(ang), jnp.sin(ang)


def apply_rope(x, cos, sin):
    half = ROPE_DIM // 2
    x1, x2 = x[..., :half], x[..., half:]
    cos, sin = cos.astype(x.dtype), sin.astype(x.dtype)
    return jnp.concatenate([x1 * cos - x2 * sin, x1 * sin + x2 * cos], axis=-1)


def short_conv_branch(a_x, a_b, a_c, conv_w, w_out):
    return (a_b * conv3_centred(a_c * a_x, conv_w)) @ w_out


def ssd_chunked(x, dt, a, bmat, cmat, h0):
    bsz, seq = x.shape[:2]
    nc = seq // CHUNK
    hg = SSM_HEADS // SSM_GROUPS
    xg = x.reshape(bsz, nc, CHUNK, SSM_GROUPS, hg, SSM_HEAD_DIM)
    dtg = dt.reshape(bsz, nc, CHUNK, SSM_GROUPS, hg)
    bg = bmat.reshape(bsz, nc, CHUNK, SSM_GROUPS, SSM_STATE)
    cg = cmat.reshape(bsz, nc, CHUNK, SSM_GROUPS, SSM_STATE)
    cum = jnp.cumsum(dtg * a.reshape(SSM_GROUPS, hg), axis=2)
    seg = cum[:, :, :, None] - cum[:, :, None, :]
    lower = jnp.tril(jnp.ones((CHUNK, CHUNK), dtype=bool))[:, :, None, None]
    decay = jnp.exp(jnp.where(lower, seg, -jnp.inf))
    cb = jnp.einsum('bcign,bcjgn->bcijg', cg, bg)
    w_intra = decay * cb[..., None] * dtg[:, :, None]
    y_diag = jnp.einsum('bcijgh,bcjghp->bcighp', w_intra, xg)
    to_end = jnp.exp(cum[:, :, -1:] - cum) * dtg
    chunk_states = jnp.einsum('bcjgh,bcjgn,bcjghp->bcghpn', to_end, bg, xg)
    chunk_decay = jnp.exp(cum[:, :, -1])

    def step(h, inp):
        st, dec = inp
        return h * dec[..., None, None] + st, h

    h_init = h0.reshape(bsz, SSM_GROUPS, hg, SSM_HEAD_DIM, SSM_STATE)
    h_final, h_enter = lax.scan(step, h_init,
                                (jnp.moveaxis(chunk_states, 1, 0), jnp.moveaxis(chunk_decay, 1, 0)))
    h_enter = jnp.moveaxis(h_enter, 0, 1)
    y_off = jnp.einsum('bcign,bcghpn->bcighp', cg, h_enter) * jnp.exp(cum)[..., None]
    y = (y_diag + y_off).reshape(bsz, seq, SSM_HEADS, SSM_HEAD_DIM)
    return y, h_final.reshape(bsz, SSM_HEADS, SSM_HEAD_DIM, SSM_STATE)


def ssm_branch(s_z, s_xbc, s_dt, conv_w, conv_b, a_log, dt_bias, d_skip, norm_w, w_out, h0_fwd, h0_bwd):
    bsz, seq = s_z.shape[:2]
    xbc = jax.nn.silu(conv3_centred(s_xbc, conv_w) + conv_b)
    gn = SSM_GROUPS * SSM_STATE
    x = xbc[..., :SSM_INNER].reshape(bsz, seq, SSM_HEADS, SSM_HEAD_DIM).astype(jnp.float32)
    bm = xbc[..., SSM_INNER:SSM_INNER + gn].reshape(bsz, seq, SSM_GROUPS, SSM_STATE).astype(jnp.float32)
    cm = xbc[..., SSM_INNER + gn:].reshape(bsz, seq, SSM_GROUPS, SSM_STATE).astype(jnp.float32)
    dt_raw = s_dt.astype(jnp.float32)
    ys, finals = [], []
    for d, h0 in enumerate((h0_fwd, h0_bwd)):
        dt = jax.nn.softplus(dt_raw + dt_bias[d].astype(jnp.float32))
        a = -jnp.exp(a_log[d].astype(jnp.float32))
        if d == 0:
            y_d, h_d = ssd_chunked(x, dt, a, bm, cm, h0.astype(jnp.float32))
        else:
            y_d, h_d = ssd_chunked(jnp.flip(x, 1), jnp.flip(dt, 1), a, jnp.flip(bm, 1),
                                   jnp.flip(cm, 1), h0.astype(jnp.float32))
            y_d = jnp.flip(y_d, 1)
        ys.append(y_d + d_skip[d].astype(jnp.float32)[:, None] * x)
        finals.append(h_d.astype(s_z.dtype))
    y = (ys[0] + ys[1]).reshape(bsz, seq, SSM_INNER)
    y = rms_norm(y * jax.nn.silu(s_z.astype(jnp.float32)), norm_w).astype(s_z.dtype)
    return y @ w_out, finals[0], finals[1]


def block_attention(q_nope, q_rope, k_nope, k_rope, v):
    bsz, lq = q_nope.shape[:2]
    nb = lq // Q_BLOCK
    scale = 1.0 / math.sqrt(QK_DIM)

    def to_blocks(t):
        return jnp.moveaxis(t.reshape(bsz, nb, Q_BLOCK, *t.shape[2:]), 1, 0)

    def one_block(qb):
        qn, qr = qb
        s = jnp.einsum('bqhd,bkhd->bhqk', qn, k_nope) + jnp.einsum('bqhd,bkd->bhqk', qr, k_rope)
        p = jax.nn.softmax(s.astype(jnp.float32) * scale, axis=-1).astype(v.dtype)
        return jnp.einsum('bhqk,bkhd->bqhd', p, v)

    o = lax.map(one_block, (to_blocks(q_nope), to_blocks(q_rope)))
    return jnp.moveaxis(o, 0, 1).reshape(bsz, lq, MLA_HEADS * V_DIM)


def mla_branch(m_cq, m_ckv, m_kr, q_norm_w, w_uq, kv_norm_w, w_ukv, w_out, ctx_kv, rope_tabs):
    bsz, seq = m_cq.shape[:2]
    q = (rms_norm(m_cq, q_norm_w) @ w_uq).reshape(bsz, seq, MLA_HEADS, QK_DIM)
    q_nope, q_rope = q[..., :NOPE_DIM], q[..., NOPE_DIM:]
    ckv = rms_norm(m_ckv, kv_norm_w)
    if rope_tabs is None:
        ckv_all, kr_all = ckv, m_kr
    else:
        cos, sin = rope_tabs
        q_rope = apply_rope(q_rope, cos[:, None], sin[:, None])
        ckv_all = jnp.concatenate([ctx_kv[0], ckv], axis=1)
        kr_all = jnp.concatenate([ctx_kv[1], apply_rope(m_kr, cos, sin)], axis=1)
    kv = (ckv_all @ w_ukv).reshape(bsz, ckv_all.shape[1], MLA_HEADS, NOPE_DIM + V_DIM)
    out = block_attention(q_nope, q_rope, kv[..., :NOPE_DIM], kr_all, kv[..., NOPE_DIM:])
    return out @ w_out, ckv, m_kr


def trunk_layer(x, cond, p, ctx_kv, h0_fwd, h0_bwd, rope_tabs):
    mod = jax.nn.silu(cond) @ p['w_ada'] + p['b_ada']
    sh1, sc1, g1, sh2, sc2, g2 = [m[:, None, :] for m in jnp.split(mod, 6, axis=-1)]
    h = rms_norm(x, p['norm1_w']) * (1.0 + sc1) + sh1
    a_x, a_b, a_c, s_z, s_xbc, s_dt, m_cq, m_ckv, m_kr, gate_logits = split_columns(h @ p['w_in'])
    y_a = short_conv_branch(a_x, a_b, a_c, p['a_conv_w'], p['w_a_out'])
    y_b, hf, hb = ssm_branch(s_z, s_xbc, s_dt, p['ssm_conv_w'], p['ssm_conv_b'], p['ssm_a_log'],
                             p['ssm_dt_bias'], p['ssm_d'], p['ssm_norm_w'], p['w_b_out'], h0_fwd, h0_bwd)
    y_c, ckv, kr = mla_branch(m_cq, m_ckv, m_kr, p['q_norm_w'], p['w_uq'], p['kv_norm_w'],
                              p['w_ukv'], p['w_c_out'], ctx_kv, rope_tabs)
    gates = jax.nn.sigmoid(gate_logits.astype(jnp.float32)).astype(x.dtype)
    gates = gates.reshape(x.shape[0], x.shape[1], N_BRANCH, D_MODEL)
    merged = gates[..., 0, :] * y_a + gates[..., 1, :] * y_b + gates[..., 2, :] * y_c
    x = x + g1 * (merged @ p['w_o'])
    h2 = rms_norm(x, p['norm2_w']) * (1.0 + sc2) + sh2
    ff = (jax.nn.silu(h2 @ p['w_ff1']) * (h2 @ p['w_ff3'])) @ p['w_ff2']
    x = x + g2 * ff
    return x, ckv, kr, hf, hb


def setup_inputs(seed: int = 0) -> dict:
    key = jax.random.key(seed)
    ks = jax.random.split(key, 32)
    f32 = jnp.float32

    def nrm(k, shape, scale=1.0):
        return jax.random.normal(k, shape, f32) * scale

    def gain(k, shape):
        return 1.0 + 0.1 * jax.random.normal(k, shape, f32)

    dt0 = jnp.exp(jax.random.uniform(ks[13], (DEPTH, 2, SSM_HEADS), f32, math.log(1e-3), math.log(1e-1)))
    st_shape = (DEC_BATCH, DEPTH, SSM_HEADS, SSM_HEAD_DIM, SSM_STATE)
    return {
        'x_prompt': nrm(ks[0], (BATCH, SEQ, D_MODEL)),
        'x_sample': nrm(ks[1], (DEC_BATCH, DEC_SEQ, D_MODEL)),
        'c': nrm(ks[2], (DEC_BATCH, D_MODEL)),
        'cache_ckv': nrm(ks[3], (DEC_BATCH, DEPTH, PAST_LEN, KV_LORA)),
        'cache_krope': nrm(ks[4], (DEC_BATCH, DEPTH, PAST_LEN, ROPE_DIM)),
        'state_ssm_fwd': nrm(ks[5], st_shape, 0.5),
        'state_ssm_bwd': nrm(ks[6], st_shape, 0.5),
        'c_ctx': nrm(ks[7], (D_MODEL,)),
        'w_in': nrm(ks[8], (DEPTH, D_MODEL, IN_COLS), D_MODEL ** -0.5),
        'a_conv_w': nrm(ks[9], (DEPTH, A_CONV, A_WIDTH), A_CONV ** -0.5),
        'w_a_out': nrm(ks[10], (DEPTH, A_WIDTH, D_MODEL), A_WIDTH ** -0.5),
        'ssm_conv_w': nrm(ks[11], (DEPTH, SSM_CONV, SSM_CONV_DIM), SSM_CONV ** -0.5),
        'ssm_conv_b': nrm(ks[12], (DEPTH, SSM_CONV_DIM), 0.02),
        'ssm_a_log': jnp.log(jax.random.uniform(ks[14], (DEPTH, 2, SSM_HEADS), f32, 1.0, 16.0)),
        'ssm_dt_bias': dt0 + jnp.log(-jnp.expm1(-dt0)),
        'ssm_d': gain(ks[15], (DEPTH, 2, SSM_HEADS)),
        'ssm_norm_w': gain(ks[16], (DEPTH, SSM_INNER)),
        'w_b_out': nrm(ks[17], (DEPTH, SSM_INNER, D_MODEL), SSM_INNER ** -0.5),
        'q_norm_w': gain(ks[18], (DEPTH, Q_LORA)),
        'w_uq': nrm(ks[19], (DEPTH, Q_LORA, MLA_HEADS * QK_DIM), Q_LORA ** -0.5),
        'kv_norm_w': gain(ks[20], (DEPTH, KV_LORA)),
        'w_ukv': nrm(ks[21], (DEPTH, KV_LORA, MLA_HEADS * (NOPE_DIM + V_DIM)), KV_LORA ** -0.5),
        'w_c_out': nrm(ks[22], (DEPTH, MLA_HEADS * V_DIM, D_MODEL), (MLA_HEADS * V_DIM) ** -0.5),
        'w_o': nrm(ks[23], (DEPTH, D_MODEL, D_MODEL), D_MODEL ** -0.5),
        'w_ada': nrm(ks[24], (DEPTH, D_MODEL, 6 * D_MODEL), 0.5 * D_MODEL ** -0.5),
        'b_ada': nrm(ks[25], (DEPTH, 6 * D_MODEL), 0.02),
        'norm1_w': gain(ks[26], (DEPTH, D_MODEL)),
        'norm2_w': gain(ks[27], (DEPTH, D_MODEL)),
        'w_ff1': nrm(ks[28], (DEPTH, D_MODEL, FF_DIM), D_MODEL ** -0.5),
        'w_ff3': nrm(ks[29], (DEPTH, D_MODEL, FF_DIM), D_MODEL ** -0.5),
        'w_ff2': nrm(ks[30], (DEPTH, FF_DIM, D_MODEL), FF_DIM ** -0.5),
        'final_norm_w': gain(ks[31], (D_MODEL,)),
    }


def reference(x_prompt, x_sample, c, cache_ckv, cache_krope, state_ssm_fwd, state_ssm_bwd, c_ctx,
              w_in, a_conv_w, w_a_out, ssm_conv_w, ssm_conv_b, ssm_a_log, ssm_dt_bias, ssm_d,
              ssm_norm_w, w_b_out, q_norm_w, w_uq, kv_norm_w, w_ukv, w_c_out, w_o, w_ada, b_ada,
              norm1_w, norm2_w, w_ff1, w_ff3, w_ff2, final_norm_w):
    def layer_params(l):
        return {
            'w_in': w_in[l], 'a_conv_w': a_conv_w[l], 'w_a_out': w_a_out[l],
            'ssm_conv_w': ssm_conv_w[l], 'ssm_conv_b': ssm_conv_b[l], 'ssm_a_log': ssm_a_log[l],
            'ssm_dt_bias': ssm_dt_bias[l], 'ssm_d': ssm_d[l], 'ssm_norm_w': ssm_norm_w[l],
            'w_b_out': w_b_out[l], 'q_norm_w': q_norm_w[l], 'w_uq': w_uq[l], 'kv_norm_w': kv_norm_w[l],
            'w_ukv': w_ukv[l], 'w_c_out': w_c_out[l], 'w_o': w_o[l], 'w_ada': w_ada[l],
            'b_ada': b_ada[l], 'norm1_w': norm1_w[l], 'norm2_w': norm2_w[l],
            'w_ff1': w_ff1[l], 'w_ff3': w_ff3[l], 'w_ff2': w_ff2[l],
        }

    h = x_prompt
    zero_state = jnp.zeros((x_prompt.shape[0], SSM_HEADS, SSM_HEAD_DIM, SSM_STATE), x_prompt.dtype)
    cond_ctx = c_ctx[None, :]
    ckv_list, kr_list, hf_list, hb_list = [], [], [], []
    for l in range(DEPTH):
        h, ckv_l, kr_l, hf_l, hb_l = trunk_layer(h, cond_ctx, layer_params(l), None,
                                                 zero_state, zero_state, None)
        ckv_list.append(ckv_l)
        kr_list.append(kr_l)
        hf_list.append(hf_l)
        hb_list.append(hb_l)
    y_prompt = rms_norm(h, final_norm_w)
    new_ckv = jnp.stack(ckv_list, axis=1)
    new_krope = jnp.stack(kr_list, axis=1)
    new_ssm_fwd = jnp.stack(hf_list, axis=1)
    new_ssm_bwd = jnp.stack(hb_list, axis=1)

    rope_tabs = axial_rope_tables(x_sample.shape[1])
    h = x_sample
    for l in range(DEPTH):
        h, _, _, _, _ = trunk_layer(h, c, layer_params(l), (cache_ckv[:, l], cache_krope[:, l]),
                                    state_ssm_fwd[:, l], state_ssm_bwd[:, l], rope_tabs)
    y_sample = rms_norm(h, final_norm_w)
    return (y_prompt, y_sample, new_ckv, new_krope, new_ssm_fwd, new_ssm_bwd)
```

```python
import functools
import math

import jax
import jax.numpy as jnp
import numpy as np
from jax import lax
from jax.experimental import pallas as pl
from jax.experimental.pallas import tpu as pltpu

F32 = jnp.float32
BF16 = jnp.bfloat16

D_MODEL = 1024
BATCH = 16
SEQ = 256
DEPTH = 4
DEC_BATCH = 4
DEC_SEQ = 4096
PAST_LEN = 512
GRID_W = 64
EPS = 1e-6
A_WIDTH = 512
SSM_INNER = 1024
SSM_HEAD_DIM = 64
SSM_HEADS = 16
SSM_GROUPS = 2
SSM_STATE = 128
CHUNK = 128
MLA_HEADS = 8
Q_LORA = 256
KV_LORA = 256
NOPE_DIM = 64
ROPE_DIM = 32
V_DIM = 64
QK_DIM = NOPE_DIM + ROPE_DIM
ROPE_BASE = 10000.0
FF_DIM = 2816

N_CTX_TOK = BATCH * SEQ
N_LAT_TOK = DEC_BATCH * DEC_SEQ
N_TOK = N_CTX_TOK + N_LAT_TOK
N_MOD_ROWS = 8

LANE = 128
SUBLANE = 8
VMEM_LIMIT = 56 * 1024 * 1024

OFF_G = 0
OFF_Z = 3072
OFF_SX = 4096
OFF_AX = 5120
OFF_AB = 5632
OFF_AC = 6144
OFF_BC = 6656
OFF_CQ = 7168
OFF_CKV = 7424
OFF_DT = 7680
OFF_KR = 7808
OFF_KRS = 7936
NP = 8192

NEG_BIG = -1e30


def _cparams(sem):
    return pltpu.CompilerParams(dimension_semantics=sem, vmem_limit_bytes=VMEM_LIMIT)


def _rms(x, w):
    ms = jnp.mean(x * x, axis=-1, keepdims=True)
    return x * lax.rsqrt(ms + EPS) * w


def _silu(x):
    return x * jax.nn.sigmoid(x)


def _mod_row(tile, tile_rows):
    n_ctx_tiles = N_CTX_TOK // tile_rows
    tiles_per_lat = DEC_SEQ // tile_rows
    return jnp.where(tile < n_ctx_tiles, 0, 1 + (tile - n_ctx_tiles) // tiles_per_lat)


MOD_TN = 1536


def _mod_kernel(c_ref, w_ref, b_ref, o_ref):
    c = c_ref[...]
    s = _silu(c).astype(BF16)
    o_ref[...] = jnp.dot(s, w_ref[...].astype(BF16), preferred_element_type=F32) + b_ref[...]


def _modulation(cond, w_ada, b_ada):
    n_col = 6 * D_MODEL
    return pl.pallas_call(
        _mod_kernel,
        grid=(DEPTH, n_col // MOD_TN),
        in_specs=[
            pl.BlockSpec((N_MOD_ROWS, D_MODEL), lambda l, j: (0, 0)),
            pl.BlockSpec((None, D_MODEL, MOD_TN), lambda l, j: (l, 0, j)),
            pl.BlockSpec((None, 1, MOD_TN), lambda l, j: (l, 0, j)),
        ],
        out_specs=pl.BlockSpec((None, N_MOD_ROWS, MOD_TN), lambda l, j: (l, 0, j)),
        out_shape=jax.ShapeDtypeStruct((DEPTH, N_MOD_ROWS, n_col), F32),
        compiler_params=_cparams(("arbitrary", "arbitrary")),
        name="modulation",
    )(cond, w_ada, b_ada.reshape(DEPTH, 1, n_col))


IN_TM = 1024
IN_TN = 1024
NORM_ROWS = 256


def _modulated_norm_to(h_ref, x_ref, nw_ref, sc_ref, sh_ref, rows):
    for r in range(0, rows, NORM_ROWS):
        x = x_ref[r:r + NORM_ROWS, :]
        h = _rms(x, nw_ref[...]) * (1.0 + sc_ref[...]) + sh_ref[...]
        h_ref[r:r + NORM_ROWS, :] = h.astype(BF16)


def _inproj_kernel(x_ref, nw_ref, sh_ref, sc_ref, w_ref, o_ref, h_ref):
    @pl.when(pl.program_id(1) == 0)
    def _():
        _modulated_norm_to(h_ref, x_ref, nw_ref, sc_ref, sh_ref, IN_TM)

    o_ref[...] = jnp.dot(h_ref[...], w_ref[...], preferred_element_type=F32)


def _in_projection(x, mod_l, norm_w, w_in_p):
    row = functools.partial(_mod_row, tile_rows=IN_TM)
    return pl.pallas_call(
        _inproj_kernel,
        grid=(N_TOK // IN_TM, NP // IN_TN),
        in_specs=[
            pl.BlockSpec((IN_TM, D_MODEL), lambda i, j: (i, 0)),
            pl.BlockSpec((1, D_MODEL), lambda i, j: (0, 0)),
            pl.BlockSpec((None, None, 1, D_MODEL), lambda i, j: (row(i), 0, 0, 0)),
            pl.BlockSpec((None, None, 1, D_MODEL), lambda i, j: (row(i), 1, 0, 0)),
            pl.BlockSpec((D_MODEL, IN_TN), lambda i, j: (0, j)),
        ],
        out_specs=pl.BlockSpec((IN_TM, IN_TN), lambda i, j: (i, j)),
        out_shape=jax.ShapeDtypeStruct((N_TOK, NP), F32),
        scratch_shapes=[pltpu.VMEM((IN_TM, D_MODEL), BF16)],
        compiler_params=_cparams(("arbitrary", "arbitrary")),
        name="in_projection",
    )(x, norm_w, mod_l, mod_l, w_in_p)


def _conv3_tile(u, prev_row, next_row, w_ref, rows):
    ridx = lax.broadcasted_iota(jnp.int32, (rows, 1), 0)
    up = jnp.where(ridx == 0, prev_row, pltpu.roll(u, 1, axis=0))
    dn = jnp.where(ridx == rows - 1, next_row, pltpu.roll(u, rows - 1, axis=0))
    return up * w_ref[0:1, :] + u * w_ref[1:2, :] + dn * w_ref[2:3, :]


SSD_TQ = 256
SSD_CPT = SSD_TQ // CHUNK
HALO = SUBLANE
N_SEQ = BATCH + DEC_BATCH


def _ssd_tables():
    blk, dirn, seq, first, last, hasp, hasn = [], [], [], [], [], [], []
    for d in range(2):
        for s in range(N_SEQ):
            if s < BATCH:
                base, nt = s * SEQ // SSD_TQ, SEQ // SSD_TQ
            else:
                base, nt = (N_CTX_TOK + (s - BATCH) * DEC_SEQ) // SSD_TQ, DEC_SEQ // SSD_TQ
            for k in range(nt):
                t = k if d == 0 else nt - 1 - k
                blk.append(base + t)
                dirn.append(d)
                seq.append(s)
                first.append(int(k == 0))
                last.append(int(k == nt - 1))
                hasp.append(int(t > 0))
                hasn.append(int(t < nt - 1))
    return [np.asarray(a, np.int32) for a in (blk, dirn, seq, first, last, hasp, hasn)]


def _split3(a):
    a1 = a.astype(BF16)
    r1 = a - a1.astype(F32)
    a2 = r1.astype(BF16)
    a3 = (r1 - a2.astype(F32)).astype(BF16)
    return a1, a2, a3


def _dot3(lhs_bf16, a):
    a1, a2, a3 = _split3(a)
    return (jnp.dot(lhs_bf16, a1, preferred_element_type=F32)
            + jnp.dot(lhs_bf16, a2, preferred_element_type=F32)
            + jnp.dot(lhs_bf16, a3, preferred_element_type=F32))


def _softplus(x):
    return jnp.maximum(x, 0.0) + jnp.log1p(jnp.exp(-jnp.abs(x)))


def _ssd_kernel(blk_t, dir_t, seq_t, first_t, last_t, hasp_t, hasn_t,
                x_ref, xp_ref, xn_ref, bc_ref, bcp_ref, bcn_ref, dt_ref,
                cwx_ref, cbx_ref, cwb_ref, cbb_ref, alog_ref, dtb_ref, dsk_ref,
                h0f_ref, h0b_ref,
                y_ref, fin_ref,
                st_ref, xc_ref, bcc_ref):
    s = pl.program_id(0)
    d = dir_t[s]
    seq = seq_t[s]
    is_fwd = d == 0

    hp = hasp_t[s].astype(F32)
    hn = hasn_t[s].astype(F32)
    cx = _conv3_tile(x_ref[...], xp_ref[HALO - 1:HALO, :] * hp, xn_ref[0:1, :] * hn, cwx_ref, SSD_TQ)
    xc_ref[...] = _silu(cx + cbx_ref[...])
    cb_ = _conv3_tile(bc_ref[...], bcp_ref[HALO - 1:HALO, :] * hp, bcn_ref[0:1, :] * hn, cwb_ref, SSD_TQ)
    bcc_ref[...] = _silu(cb_ + cbb_ref[...])

    @pl.when(jnp.logical_and(first_t[s] == 1, seq < BATCH))
    def _():
        st_ref[...] = jnp.zeros_like(st_ref)

    @pl.when(jnp.logical_and(first_t[s] == 1, seq >= BATCH))
    def _():
        h0 = jnp.where(is_fwd, h0f_ref[...], h0b_ref[...])
        st_ref[...] = h0.T

    ii = lax.broadcasted_iota(jnp.int32, (CHUNK, CHUNK), 0)
    jj = lax.broadcasted_iota(jnp.int32, (CHUNK, CHUNK), 1)
    sgn = jnp.where(is_fwd, 1, -1)
    mask = (jj - ii) * sgn <= 0
    mask_b = jnp.where(mask, 1.0, 0.0).astype(BF16)
    lo = jj < SSM_HEAD_DIM
    lo_row = lo[0:1, :]

    a_row = -jnp.exp(alog_ref[...])
    dtb = dtb_ref[...]

    e_r = lax.broadcasted_iota(jnp.int32, (LANE, SSM_INNER), 0)
    e_c = lax.broadcasted_iota(jnp.int32, (LANE, SSM_INNER), 1)
    expand = jnp.where(jnp.right_shift(e_c, 6) == e_r, 1.0, 0.0).astype(BF16)

    def chunk_body(k, carry):
        c = jnp.where(is_fwd, k, SSD_CPT - 1 - k)
        r0 = pl.multiple_of(c * CHUNK, CHUNK)
        dt = _softplus(dt_ref[pl.ds(r0, CHUNK), :] + dtb)
        a = dt * a_row
        cum = _dot3(mask_b, a)
        cum_t = cum.T
        dt_t = dt.T
        tot_row = jnp.where(is_fwd, cum[CHUNK - 1:CHUNK, :], cum[0:1, :])
        tot_col = jnp.where(is_fwd, cum_t[:, CHUNK - 1:CHUNK], cum_t[:, 0:1])
        w_rows = jnp.exp(tot_col - cum_t) * dt_t
        sdec = _dot3_rows(jnp.exp(tot_row), expand)
        ecum = jnp.exp(cum)

        bc = bcc_ref[pl.ds(r0, CHUNK), :]
        for g in range(SSM_GROUPS):
            b_g = bc[:, g * SSM_STATE:(g + 1) * SSM_STATE]
            c_g = bc[:, (SSM_GROUPS + g) * SSM_STATE:(SSM_GROUPS + g + 1) * SSM_STATE].astype(BF16)
            b_gt = b_g.T
            cb = jnp.dot(c_g, b_gt.astype(BF16), preferred_element_type=F32)
            pairs_per_group = SSM_HEADS // SSM_GROUPS // 2
            for kp in range(g * pairs_per_group, (g + 1) * pairs_per_group):
                h_a, h_b = 2 * kp, 2 * kp + 1
                ls = slice(kp * LANE, (kp + 1) * LANE)
                x_pair = xc_ref[pl.ds(r0, CHUNK), ls]
                rhs = jnp.concatenate([jnp.where(lo, x_pair, 0.0).astype(BF16),
                                       jnp.where(lo, 0.0, x_pair).astype(BF16)], axis=0)

                def head_lhs(h):
                    seg = cum[:, h:h + 1] - cum_t[h:h + 1, :]
                    dec = jnp.exp(jnp.where(mask, seg, NEG_BIG))
                    w_intra = dec * cb * dt_t[h:h + 1, :]
                    w_state = b_gt * w_rows[h:h + 1, :]
                    return w_intra.astype(BF16), w_state.astype(BF16)

                wi_a, ws_a = head_lhs(h_a)
                wi_b, ws_b = head_lhs(h_b)
                lhs = jnp.concatenate([jnp.concatenate([wi_a, wi_b], axis=1),
                                       jnp.concatenate([ws_a, ws_b], axis=1)], axis=0)
                both = jnp.dot(lhs, rhs, preferred_element_type=F32)
                y_diag = both[0:CHUNK, :]
                d_state = both[CHUNK:2 * CHUNK, :]

                h_pair = st_ref[:, ls]
                y_off = jnp.dot(c_g, h_pair.astype(BF16), preferred_element_type=F32)
                e_pair = jnp.where(lo, ecum[:, h_a:h_a + 1], ecum[:, h_b:h_b + 1])
                y_ref[pl.ds(r0, CHUNK), ls] = y_diag + y_off * e_pair + x_pair * dsk_ref[:, ls]
                st_ref[:, ls] = h_pair * sdec[:, ls] + d_state
        return carry

    lax.fori_loop(0, SSD_CPT, chunk_body, 0)

    @pl.when(jnp.logical_and(last_t[s] == 1, seq < BATCH))
    def _():
        fin_ref[...] = st_ref[...].T


def _dot3_rows(row, rhs_bf16):
    r8 = jnp.broadcast_to(row, (SUBLANE, row.shape[1]))
    r1, r2, r3 = _split3(r8)
    out = (jnp.dot(r1, rhs_bf16, preferred_element_type=F32)
           + jnp.dot(r2, rhs_bf16, preferred_element_type=F32)
           + jnp.dot(r3, rhs_bf16, preferred_element_type=F32))
    return out[0:1, :]


def _ssd(p, layer, conv_wx, conv_bx, conv_wbc, conv_bbc, alog, dtb, dskip, h0f, h0b):
    tables = [jnp.asarray(t) for t in _ssd_tables()]
    n_steps = int(tables[0].shape[0])
    halo_per_tile = SSD_TQ // HALO
    n_halo_blocks = N_TOK // HALO

    def tile_map(col):
        return lambda s, blk, *_: (blk[s], col)

    def prev_map(col):
        return lambda s, blk, *_: (jnp.maximum(blk[s] * halo_per_tile - 1, 0), col)

    def next_map(col):
        return lambda s, blk, *_: (jnp.minimum((blk[s] + 1) * halo_per_tile, n_halo_blocks - 1), col)

    def dir_map(s, blk, dirn, *_):
        return (dirn[s], 0, 0)

    def h0_map(s, blk, dirn, seq, *_):
        return (jnp.maximum(seq[s] - BATCH, 0), layer, 0, 0)

    def const2(s, *_):
        return (0, 0)

    cx, cbc, cdt = OFF_SX // SSM_INNER, OFF_BC // 512, OFF_DT // LANE
    grid_spec = pltpu.PrefetchScalarGridSpec(
        num_scalar_prefetch=7,
        grid=(n_steps,),
        in_specs=[
            pl.BlockSpec((SSD_TQ, SSM_INNER), tile_map(cx)),
            pl.BlockSpec((HALO, SSM_INNER), prev_map(cx)),
            pl.BlockSpec((HALO, SSM_INNER), next_map(cx)),
            pl.BlockSpec((SSD_TQ, 512), tile_map(cbc)),
            pl.BlockSpec((HALO, 512), prev_map(cbc)),
            pl.BlockSpec((HALO, 512), next_map(cbc)),
            pl.BlockSpec((SSD_TQ, LANE), tile_map(cdt)),
            pl.BlockSpec((3, SSM_INNER), const2),
            pl.BlockSpec((1, SSM_INNER), const2),
            pl.BlockSpec((3, 512), const2),
            pl.BlockSpec((1, 512), const2),
            pl.BlockSpec((None, 1, LANE), dir_map),
            pl.BlockSpec((None, 1, LANE), dir_map),
            pl.BlockSpec((None, 1, SSM_INNER), dir_map),
            pl.BlockSpec((None, None, SSM_INNER, SSM_STATE), h0_map),
            pl.BlockSpec((None, None, SSM_INNER, SSM_STATE), h0_map),
        ],
        out_specs=[
            pl.BlockSpec((None, SSD_TQ, SSM_INNER), lambda s, blk, dirn, *_: (dirn[s], blk[s], 0)),
            pl.BlockSpec((None, None, SSM_INNER, SSM_STATE),
                         lambda s, blk, dirn, seq, *_: (dirn[s], jnp.minimum(seq[s], BATCH - 1), 0, 0)),
        ],
        scratch_shapes=[
            pltpu.VMEM((SSM_STATE, SSM_INNER), F32),
            pltpu.VMEM((SSD_TQ, SSM_INNER), F32),
            pltpu.VMEM((SSD_TQ, 512), F32),
        ],
    )
    return pl.pallas_call(
        _ssd_kernel,
        grid_spec=grid_spec,
        out_shape=[
            jax.ShapeDtypeStruct((2, N_TOK, SSM_INNER), F32),
            jax.ShapeDtypeStruct((2, BATCH, SSM_INNER, SSM_STATE), F32),
        ],
        compiler_params=_cparams(("arbitrary",)),
        name="ssd_scan",
    )(*tables, p, p, p, p, p, p, p, conv_wx, conv_bx, conv_wbc, conv_bbc, alog, dtb, dskip, h0f, h0b)


QK_PAD = 128
N_QROWS = MLA_HEADS * QK_PAD
N_VROWS = MLA_HEADS * V_DIM
NT_DIMS = (((1,), (1,)), ((), ()))


def _mla_prep_kernel(*refs, tm, normalize, with_q):
    if with_q:
        (cq_ref, ckv_ref, kr_ref, krs_ref, cos_ref, sin_ref, cost_ref, sint_ref,
         qnw_ref, kvnw_ref, wq_ref, wk_ref, wv_ref,
         qt_ref, k_ref, vt_ref, ckvn_ref) = refs
    else:
        (ckv_ref, kr_ref, krs_ref, cos_ref, sin_ref, kvnw_ref, wk_ref, wv_ref,
         k_ref, vt_ref) = refs

    ckv = ckv_ref[...]
    if normalize:
        ckv = _rms(ckv, kvnw_ref[...])
        ckvn_ref[...] = ckv
    ckv_b = ckv.astype(BF16)
    kn = jnp.dot(ckv_b, wk_ref[...], preferred_element_type=F32)
    kr = kr_ref[...] * cos_ref[...] + krs_ref[...] * sin_ref[...]
    for h in range(MLA_HEADS):
        hs = slice(h * QK_PAD, (h + 1) * QK_PAD)
        k_ref[:, hs] = (kn[:, hs] + kr).astype(BF16)
    vt_ref[...] = lax.dot_general(wv_ref[...], ckv_b, NT_DIMS, preferred_element_type=F32).astype(BF16)

    if with_q:
        cqn = _rms(cq_ref[...], qnw_ref[...]).astype(BF16)
        qq = lax.dot_general(wq_ref[...], cqn, NT_DIMS, preferred_element_type=F32)
        for h in range(MLA_HEADS):
            q_h = qq[h * QK_PAD:(h + 1) * QK_PAD, :]
            qs_h = qq[N_QROWS + h * QK_PAD:N_QROWS + (h + 1) * QK_PAD, :]
            qt_ref[h * QK_PAD:(h + 1) * QK_PAD, :] = (q_h * cost_ref[...] + qs_h * sint_ref[...]).astype(BF16)


def _mla_prep_tokens(p, row0, n_rows, tm, tab_map, cos, sin, cos_t, sin_t, qnw, kvnw, wq_t, wk, wv_t):
    b0 = row0 // tm
    nt = n_rows // tm
    kernel = functools.partial(_mla_prep_kernel, tm=tm, normalize=True, with_q=True)

    def pcol(width, off):
        return pl.BlockSpec((tm, width), lambda i: (b0 + i, off // width))

    def full(a):
        return pl.BlockSpec(a.shape, lambda i: (0,) * a.ndim)

    return pl.pallas_call(
        kernel,
        grid=(nt,),
        in_specs=[
            pcol(Q_LORA, OFF_CQ), pcol(KV_LORA, OFF_CKV), pcol(LANE, OFF_KR), pcol(LANE, OFF_KRS),
            pl.BlockSpec((tm, LANE), lambda i: (tab_map(i), 0)),
            pl.BlockSpec((tm, LANE), lambda i: (tab_map(i), 0)),
            pl.BlockSpec((LANE, tm), lambda i: (0, tab_map(i))),
            pl.BlockSpec((LANE, tm), lambda i: (0, tab_map(i))),
            full(qnw), full(kvnw), full(wq_t), full(wk), full(wv_t),
        ],
        out_specs=[
            pl.BlockSpec((N_QROWS, tm), lambda i: (0, i)),
            pl.BlockSpec((None, tm, N_QROWS), lambda i: (i, 0, 0)),
            pl.BlockSpec((None, N_VROWS, tm), lambda i: (i, 0, 0)),
            pl.BlockSpec((tm, KV_LORA), lambda i: (i, 0)),
        ],
        out_shape=[
            jax.ShapeDtypeStruct((N_QROWS, n_rows), BF16),
            jax.ShapeDtypeStruct((nt, tm, N_QROWS), BF16),
            jax.ShapeDtypeStruct((nt, N_VROWS, tm), BF16),
            jax.ShapeDtypeStruct((n_rows, KV_LORA), F32),
        ],
        compiler_params=_cparams(("arbitrary",)),
        name="mla_prep",
    )(p, p, p, p, cos, sin, cos_t, sin_t, qnw, kvnw, wq_t, wk, wv_t)


def _mla_prep_cache(ckv, kr_pad, ones_tab, zeros_tab, kvnw, wk, wv_t, tm):
    n_rows = ckv.shape[0]
    nt = n_rows // tm
    kernel = functools.partial(_mla_prep_kernel, tm=tm, normalize=False, with_q=False)

    def full(a):
        return pl.BlockSpec(a.shape, lambda i: (0,) * a.ndim)

    return pl.pallas_call(
        kernel,
        grid=(nt,),
        in_specs=[
            pl.BlockSpec((tm, KV_LORA), lambda i: (i, 0)),
            pl.BlockSpec((tm, LANE), lambda i: (i, 0)),
            pl.BlockSpec((tm, LANE), lambda i: (i, 0)),
            pl.BlockSpec((tm, LANE), lambda i: (0, 0)),
            pl.BlockSpec((tm, LANE), lambda i: (0, 0)),
            full(kvnw), full(wk), full(wv_t),
        ],
        out_specs=[
            pl.BlockSpec((None, tm, N_QROWS), lambda i: (i, 0, 0)),
            pl.BlockSpec((None, N_VROWS, tm), lambda i: (i, 0, 0)),
        ],
        out_shape=[
            jax.ShapeDtypeStruct((nt, tm, N_QROWS), BF16),
            jax.ShapeDtypeStruct((nt, N_VROWS, tm), BF16),
        ],
        compiler_params=_cparams(("arbitrary",)),
        name="mla_prep_cache",
    )(ckv, kr_pad, kr_pad, ones_tab, zeros_tab, kvnw, wk, wv_t)


ATT_TQ = 256
SM_SCALE = 1.0 / math.sqrt(QK_DIM)


def _attn_kernel(qt_ref, k_ref, vt_ref, o_ref, ot_ref, *, n_kt, tq):
    for h in range(MLA_HEADS):
        q_t = qt_ref[h * QK_PAD:(h + 1) * QK_PAD, :]

        def kstep(kt, carry):
            m, l, acc = carry
            k = k_ref[kt, :, h * QK_PAD:(h + 1) * QK_PAD]
            s = jnp.dot(k, q_t, preferred_element_type=F32)
            m_new = jnp.maximum(m, jnp.max(s, axis=0, keepdims=True))
            alpha = jnp.exp((m - m_new) * SM_SCALE)
            pr = jnp.exp((s - m_new) * SM_SCALE)
            l = l * alpha + jnp.sum(pr, axis=0, keepdims=True)
            v_t = vt_ref[kt, h * V_DIM:(h + 1) * V_DIM, :]
            acc = acc * alpha + jnp.dot(v_t, pr.astype(BF16), preferred_element_type=F32)
            return m_new, l, acc

        init = (jnp.full((1, tq), NEG_BIG, F32), jnp.zeros((1, tq), F32), jnp.zeros((V_DIM, tq), F32))
        m, l, acc = lax.fori_loop(0, n_kt, kstep, init)
        ot_ref[h * V_DIM:(h + 1) * V_DIM, :] = acc / l
    o_ref[...] = ot_ref[...].T


def _attention(qt, k3, vt3, n_batch, lq, n_kt, tk):
    tq = min(ATT_TQ, lq)
    nq = lq // tq
    k4 = k3.reshape(n_batch, n_kt, tk, N_QROWS)
    v4 = vt3.reshape(n_batch, n_kt, N_VROWS, tk)
    kernel = functools.partial(_attn_kernel, n_kt=n_kt, tq=tq)
    return pl.pallas_call(
        kernel,
        grid=(n_batch, nq),
        in_specs=[
            pl.BlockSpec((N_QROWS, tq), lambda b, i: (0, b * nq + i)),
            pl.BlockSpec((None, n_kt, tk, N_QROWS), lambda b, i: (b, 0, 0, 0)),
            pl.BlockSpec((None, n_kt, N_VROWS, tk), lambda b, i: (b, 0, 0, 0)),
        ],
        out_specs=pl.BlockSpec((tq, N_VROWS), lambda b, i: (b * nq + i, 0)),
        out_shape=jax.ShapeDtypeStruct((n_batch * lq, N_VROWS), F32),
        scratch_shapes=[pltpu.VMEM((N_VROWS, tq), F32)],
        compiler_params=_cparams(("arbitrary", "arbitrary")),
        name="attention",
    )(qt, k4, v4)


MG_TM = 256


def _merge_kernel(ax_ref, axp_ref, axn_ref, ac_ref, acp_ref, acn_ref, ab_ref,
                  g_ref, z_ref, y_ref, zc_ref, x_ref, gm_ref,
                  cw_ref, wa_ref, nw_ref, wb_ref, wc_ref, wo_ref, o_ref):
    i = pl.program_id(0)
    n_ctx_tiles = N_CTX_TOK // MG_TM
    tiles_per_lat = DEC_SEQ // MG_TM
    t_in_seq = (i - n_ctx_tiles) % tiles_per_lat
    is_lat = i >= n_ctx_tiles
    hp = jnp.logical_and(is_lat, t_in_seq > 0).astype(F32)
    hn = jnp.logical_and(is_lat, t_in_seq < tiles_per_lat - 1).astype(F32)

    u = ac_ref[...] * ax_ref[...]
    u_prev = acp_ref[HALO - 1:HALO, :] * axp_ref[HALO - 1:HALO, :] * hp
    u_next = acn_ref[0:1, :] * axn_ref[0:1, :] * hn
    za = ab_ref[...] * _conv3_tile(u, u_prev, u_next, cw_ref, MG_TM)
    y_a = jnp.dot(za.astype(BF16), wa_ref[...], preferred_element_type=F32)

    yb = (y_ref[0] + y_ref[1]) * _silu(z_ref[...])
    zb = _rms(yb, nw_ref[...])
    y_b = jnp.dot(zb.astype(BF16), wb_ref[...], preferred_element_type=F32)

    y_c = jnp.dot(zc_ref[...].astype(BF16), wc_ref[...], preferred_element_type=F32)

    merged = (jax.nn.sigmoid(g_ref[:, 0:D_MODEL]) * y_a
              + jax.nn.sigmoid(g_ref[:, D_MODEL:2 * D_MODEL]) * y_b
              + jax.nn.sigmoid(g_ref[:, 2 * D_MODEL:3 * D_MODEL]) * y_c)
    o = jnp.dot(merged.astype(BF16), wo_ref[...], preferred_element_type=F32)
    o_ref[...] = x_ref[...] + gm_ref[...] * o


def _merge(p, y_ssd, zc, x, mod_l, conv_w, wa, nw, wb, wc, wo):
    halo_per_tile = MG_TM // HALO
    n_halo_blocks = N_TOK // HALO
    row = functools.partial(_mod_row, tile_rows=MG_TM)

    def pcol(width, off):
        return pl.BlockSpec((MG_TM, width), lambda i: (i, off // width))

    def pprev(off):
        return pl.BlockSpec((HALO, A_WIDTH), lambda i: (jnp.maximum(i * halo_per_tile - 1, 0), off // A_WIDTH))

    def pnext(off):
        return pl.BlockSpec((HALO, A_WIDTH),
                            lambda i: (jnp.minimum((i + 1) * halo_per_tile, n_halo_blocks - 1), off // A_WIDTH))

    def full(a):
        return pl.BlockSpec(a.shape, lambda i: (0,) * a.ndim)

    return pl.pallas_call(
        _merge_kernel,
        grid=(N_TOK // MG_TM,),
        in_specs=[
            pcol(A_WIDTH, OFF_AX), pprev(OFF_AX), pnext(OFF_AX),
            pcol(A_WIDTH, OFF_AC), pprev(OFF_AC), pnext(OFF_AC),
            pcol(A_WIDTH, OFF_AB),
            pcol(3 * D_MODEL, OFF_G),
            pcol(SSM_INNER, OFF_Z),
            pl.BlockSpec((2, MG_TM, SSM_INNER), lambda i: (0, i, 0)),
            pl.BlockSpec((MG_TM, N_VROWS), lambda i: (i, 0)),
            pl.BlockSpec((MG_TM, D_MODEL), lambda i: (i, 0)),
            pl.BlockSpec((None, None, 1, D_MODEL), lambda i: (row(i), 2, 0, 0)),
            full(conv_w), full(wa), full(nw), full(wb), full(wc), full(wo),
        ],
        out_specs=pl.BlockSpec((MG_TM, D_MODEL), lambda i: (i, 0)),
        out_shape=jax.ShapeDtypeStruct((N_TOK, D_MODEL), F32),
        compiler_params=_cparams(("arbitrary",)),
        name="merge",
    )(p, p, p, p, p, p, p, p, p, y_ssd, zc, x, mod_l, conv_w, wa, nw, wb, wc, wo)


FF_TM = 512
FF_TF = 1408


def _ffn_kernel(x_ref, nw_ref, sh_ref, sc_ref, gm_ref, w1_ref, w3_ref, w2_ref, o_ref, h_ref, acc_ref):
    j = pl.program_id(1)

    @pl.when(j == 0)
    def _():
        _modulated_norm_to(h_ref, x_ref, nw_ref, sc_ref, sh_ref, FF_TM)

    h = h_ref[...]
    a = jnp.dot(h, w1_ref[...], preferred_element_type=F32)
    b = jnp.dot(h, w3_ref[...], preferred_element_type=F32)
    part = jnp.dot((_silu(a) * b).astype(BF16), w2_ref[...], preferred_element_type=F32)

    @pl.when(j == 0)
    def _():
        acc_ref[...] = part

    @pl.when(j > 0)
    def _():
        acc_ref[...] += part

    @pl.when(j == pl.num_programs(1) - 1)
    def _():
        o_ref[...] = x_ref[...] + gm_ref[...] * acc_ref[...]


def _ffn(x, mod_l, norm_w, w1, w3, w2):
    row = functools.partial(_mod_row, tile_rows=FF_TM)
    return pl.pallas_call(
        _ffn_kernel,
        grid=(N_TOK // FF_TM, FF_DIM // FF_TF),
        in_specs=[
            pl.BlockSpec((FF_TM, D_MODEL), lambda i, j: (i, 0)),
            pl.BlockSpec((1, D_MODEL), lambda i, j: (0, 0)),
            pl.BlockSpec((None, None, 1, D_MODEL), lambda i, j: (row(i), 3, 0, 0)),
            pl.BlockSpec((None, None, 1, D_MODEL), lambda i, j: (row(i), 4, 0, 0)),
            pl.BlockSpec((None, None, 1, D_MODEL), lambda i, j: (row(i), 5, 0, 0)),
            pl.BlockSpec((D_MODEL, FF_TF), lambda i, j: (0, j)),
            pl.BlockSpec((D_MODEL, FF_TF), lambda i, j: (0, j)),
            pl.BlockSpec((FF_TF, D_MODEL), lambda i, j: (j, 0)),
        ],
        out_specs=pl.BlockSpec((FF_TM, D_MODEL), lambda i, j: (i, 0)),
        out_shape=jax.ShapeDtypeStruct((N_TOK, D_MODEL), F32),
        scratch_shapes=[pltpu.VMEM((FF_TM, D_MODEL), BF16), pltpu.VMEM((FF_TM, D_MODEL), F32)],
        compiler_params=_cparams(("arbitrary", "arbitrary")),
        name="ffn",
    )(x, norm_w, mod_l, mod_l, mod_l, w1, w3, w2)


FN_TM = 1024


def _final_norm_kernel(x_ref, w_ref, o_ref):
    o_ref[...] = _rms(x_ref[...], w_ref[...])


def _final_norm(x, w, row0, n_rows):
    b0 = row0 // FN_TM
    return pl.pallas_call(
        _final_norm_kernel,
        grid=(n_rows // FN_TM,),
        in_specs=[pl.BlockSpec((FN_TM, D_MODEL), lambda i: (b0 + i, 0)),
                  pl.BlockSpec((1, D_MODEL), lambda i: (0, 0))],
        out_specs=pl.BlockSpec((FN_TM, D_MODEL), lambda i: (i, 0)),
        out_shape=jax.ShapeDtypeStruct((n_rows, D_MODEL), F32),
        compiler_params=_cparams(("arbitrary",)),
        name="final_norm",
    )(x, w)


def _pad_in_weights(w_in):
    o = 0
    a_x = w_in[..., o:o + 512]; o += 512
    a_b = w_in[..., o:o + 512]; o += 512
    a_c = w_in[..., o:o + 512]; o += 512
    s_z = w_in[..., o:o + 1024]; o += 1024
    s_x = w_in[..., o:o + 1024]; o += 1024
    s_bc = w_in[..., o:o + 512]; o += 512
    s_dt = w_in[..., o:o + 16]; o += 16
    cq = w_in[..., o:o + 256]; o += 256
    ckv = w_in[..., o:o + 256]; o += 256
    kr = w_in[..., o:o + 32]; o += 32
    gates = w_in[..., o:o + 3072]; o += 3072

    def z(n):
        return jnp.zeros(w_in.shape[:-1] + (n,), w_in.dtype)

    kr_sw = jnp.concatenate([kr[..., 16:], kr[..., :16]], axis=-1)
    cols = [gates, s_z, s_x, a_x, a_b, a_c, s_bc, cq, ckv,
            s_dt, z(LANE - 16),
            z(64), kr, z(32),
            z(64), kr_sw, z(32),
            z(NP - OFF_KRS - LANE)]
    out = jnp.concatenate(cols, axis=-1).astype(BF16)
    assert out.shape[-1] == NP
    return out


def _q_weights_t(w_uq):
    w = w_uq.reshape(DEPTH, Q_LORA, MLA_HEADS, QK_DIM)
    nope, x1, x2 = w[..., :NOPE_DIM], w[..., NOPE_DIM:NOPE_DIM + 16], w[..., NOPE_DIM + 16:]
    z32 = jnp.zeros_like(w[..., :32])
    z64 = jnp.zeros_like(nope)
    q = jnp.concatenate([nope, x1, x2, z32], axis=-1).reshape(DEPTH, Q_LORA, N_QROWS)
    qs = jnp.concatenate([z64, x2, x1, z32], axis=-1).reshape(DEPTH, Q_LORA, N_QROWS)
    return jnp.swapaxes(jnp.concatenate([q, qs], axis=-1), 1, 2).astype(BF16)


def _kv_weights(w_ukv):
    w = w_ukv.reshape(DEPTH, KV_LORA, MLA_HEADS, NOPE_DIM + V_DIM)
    kn = jnp.concatenate([w[..., :NOPE_DIM], jnp.zeros_like(w[..., :QK_PAD - NOPE_DIM])], axis=-1)
    wk = kn.reshape(DEPTH, KV_LORA, N_QROWS).astype(BF16)
    wv_t = jnp.swapaxes(w[..., NOPE_DIM:].reshape(DEPTH, KV_LORA, N_VROWS), 1, 2).astype(BF16)
    return wk, wv_t


def _rope_tables(n_tokens, lead_rows):
    n_rows = n_tokens // GRID_W
    row = jnp.repeat(jnp.arange(n_rows, dtype=F32), GRID_W)
    col = jnp.tile(jnp.arange(GRID_W, dtype=F32), n_rows)
    pairs = ROPE_DIM // 4
    inv = ROPE_BASE ** (-jnp.arange(pairs, dtype=F32) / pairs)
    ang = jnp.concatenate([row[:, None] * inv, col[:, None] * inv], axis=-1)
    cos, sin = jnp.cos(ang), jnp.sin(ang)
    ones = jnp.ones((n_tokens, NOPE_DIM), F32)
    z32 = jnp.zeros((n_tokens, 32), F32)
    cos_l = jnp.concatenate([ones, cos, cos, z32], axis=-1)
    sin_l = jnp.concatenate([jnp.zeros_like(ones), -sin, sin, z32], axis=-1)
    ident_c = jnp.concatenate([jnp.ones((lead_rows, NOPE_DIM + ROPE_DIM), F32), jnp.zeros((lead_rows, 32), F32)], -1)
    ident_s = jnp.zeros((lead_rows, LANE), F32)
    return jnp.concatenate([ident_c, cos_l], axis=0), jnp.concatenate([ident_s, sin_l], axis=0)


PREP_TM_CTX = SEQ
PREP_TM_LAT = 512


def kernel(x_prompt, x_sample, c, cache_ckv, cache_krope, state_ssm_fwd, state_ssm_bwd, c_ctx, w_in, a_conv_w, w_a_out, ssm_conv_w, ssm_conv_b, ssm_a_log, ssm_dt_bias, ssm_d, ssm_norm_w, w_b_out, q_norm_w, w_uq, kv_norm_w, w_ukv, w_c_out, w_o, w_ada, b_ada, norm1_w, norm2_w, w_ff1, w_ff3, w_ff2, final_norm_w):
    w_in_p = _pad_in_weights(w_in)
    wq_t = _q_weights_t(w_uq)
    wk, wv_t = _kv_weights(w_ukv)
    wa, wb, wc, wo = (w.astype(BF16) for w in (w_a_out, w_b_out, w_c_out, w_o))
    w1, w3, w2 = (w.astype(BF16) for w in (w_ff1, w_ff3, w_ff2))

    cond = jnp.concatenate([c_ctx[None, :], c, jnp.zeros((N_MOD_ROWS - 1 - DEC_BATCH, D_MODEL), F32)], axis=0)
    mod = _modulation(cond, w_ada, b_ada).reshape(DEPTH, N_MOD_ROWS, 6, 1, D_MODEL)

    conv_wx = ssm_conv_w[..., :SSM_INNER]
    conv_wbc = ssm_conv_w[..., SSM_INNER:]
    conv_bx = ssm_conv_b[:, None, :SSM_INNER]
    conv_bbc = ssm_conv_b[:, None, SSM_INNER:]
    pad_h = ((0, 0), (0, 0), (0, 0), (0, LANE - SSM_HEADS))
    alog = jnp.pad(ssm_a_log[:, :, None, :], pad_h)
    dtb = jnp.pad(ssm_dt_bias[:, :, None, :], pad_h)
    dskip = jnp.repeat(ssm_d, SSM_HEAD_DIM, axis=-1)[:, :, None, :]
    h0f = state_ssm_fwd.reshape(DEC_BATCH, DEPTH, SSM_INNER, SSM_STATE)
    h0b = state_ssm_bwd.reshape(DEC_BATCH, DEPTH, SSM_INNER, SSM_STATE)

    cos_c, sin_c = _rope_tables(DEC_SEQ, PREP_TM_CTX)
    cos_l, sin_l = cos_c[PREP_TM_CTX:], sin_c[PREP_TM_CTX:]
    ident_cos, ident_sin = cos_c[:PREP_TM_CTX], sin_c[:PREP_TM_CTX]
    ones_tab = jnp.concatenate([ident_cos, ident_cos], axis=0)
    zeros_tab = jnp.zeros_like(ones_tab)
    cache_kr_pad = jnp.pad(cache_krope, ((0, 0), (0, 0), (0, 0), (64, 32)))

    x = jnp.concatenate([x_prompt.reshape(N_CTX_TOK, D_MODEL), x_sample.reshape(N_LAT_TOK, D_MODEL)], axis=0)

    lat_tiles_per_seq = DEC_SEQ // PREP_TM_LAT
    new_ckv, new_kr, new_f, new_b = [], [], [], []
    for l in range(DEPTH):
        p = _in_projection(x, mod[l], norm1_w[l][None, :], w_in_p[l])

        y_ssd, fin = _ssd(p, l, conv_wx[l], conv_bx[l], conv_wbc[l], conv_bbc[l],
                          alog[l], dtb[l], dskip[l], h0f, h0b)
        new_f.append(fin[0])
        new_b.append(fin[1])

        qnw, kvnw = q_norm_w[l][None, :], kv_norm_w[l][None, :]
        qt_c, k_c, vt_c, ckv_c = _mla_prep_tokens(
            p, 0, N_CTX_TOK, PREP_TM_CTX, lambda i: 0, ident_cos, ident_sin, ident_cos.T, ident_sin.T,
            qnw, kvnw, wq_t[l], wk[l], wv_t[l])
        qt_l, k_l, vt_l, _ = _mla_prep_tokens(
            p, N_CTX_TOK, N_LAT_TOK, PREP_TM_LAT, lambda i: i % lat_tiles_per_seq, cos_l, sin_l, cos_l.T, sin_l.T,
            qnw, kvnw, wq_t[l], wk[l], wv_t[l])
        k_p, vt_p = _mla_prep_cache(cache_ckv[:, l].reshape(DEC_BATCH * PAST_LEN, KV_LORA),
                                    cache_kr_pad[:, l].reshape(DEC_BATCH * PAST_LEN, LANE),
                                    ones_tab, zeros_tab, kvnw, wk[l], wv_t[l], PAST_LEN)
        new_ckv.append(ckv_c.reshape(BATCH, SEQ, KV_LORA))
        new_kr.append(p[:N_CTX_TOK, OFF_KR + 64:OFF_KR + 64 + ROPE_DIM].reshape(BATCH, SEQ, ROPE_DIM))

        zc_c = _attention(qt_c, k_c, vt_c, BATCH, SEQ, 1, SEQ)
        n_kt = 1 + lat_tiles_per_seq
        k_all = jnp.concatenate([k_p[:, None], k_l.reshape(DEC_BATCH, lat_tiles_per_seq, PREP_TM_LAT, N_QROWS)], axis=1)
        vt_all = jnp.concatenate([vt_p[:, None], vt_l.reshape(DEC_BATCH, lat_tiles_per_seq, N_VROWS, PREP_TM_LAT)], axis=1)
        zc_l = _attention(qt_l, k_all.reshape(DEC_BATCH * n_kt, PREP_TM_LAT, N_QROWS),
                          vt_all.reshape(DEC_BATCH * n_kt, N_VROWS, PREP_TM_LAT), DEC_BATCH, DEC_SEQ, n_kt, PREP_TM_LAT)
        zc = jnp.concatenate([zc_c, zc_l], axis=0)

        x = _merge(p, y_ssd, zc, x, mod[l], a_conv_w[l], wa[l], ssm_norm_w[l][None, :], wb[l], wc[l], wo[l])
        x = _ffn(x, mod[l], norm2_w[l][None, :], w1[l], w3[l], w2[l])

    fw = final_norm_w[None, :]
    y_prompt = _final_norm(x, fw, 0, N_CTX_TOK).reshape(BATCH, SEQ, D_MODEL)
    y_sample = _final_norm(x, fw, N_CTX_TOK, N_LAT_TOK).reshape(DEC_BATCH, DEC_SEQ, D_MODEL)
    hshape = (BATCH, DEPTH, SSM_HEADS, SSM_HEAD_DIM, SSM_STATE)
    return (y_prompt, y_sample,
            jnp.stack(new_ckv, axis=1), jnp.stack(new_kr, axis=1),
            jnp.stack(new_f, axis=1).reshape(hshape), jnp.stack(new_b, axis=1).reshape(hshape))
```

```python
import functools
import math

import jax
import jax.numpy as jnp
import numpy as np
from jax import lax
from jax.experimental import pallas as pl
from jax.experimental.pallas import tpu as pltpu

F32 = jnp.float32
BF16 = jnp.bfloat16

D_MODEL = 1024
BATCH = 16
SEQ = 256
DEPTH = 4
DEC_BATCH = 4
DEC_SEQ = 4096
PAST_LEN = 512
GRID_W = 64
EPS = 1e-6
A_WIDTH = 512
SSM_INNER = 1024
SSM_HEAD_DIM = 64
SSM_HEADS = 16
SSM_GROUPS = 2
SSM_STATE = 128
CHUNK = 128
MLA_HEADS = 8
Q_LORA = 256
KV_LORA = 256
NOPE_DIM = 64
ROPE_DIM = 32
V_DIM = 64
QK_DIM = NOPE_DIM + ROPE_DIM
ROPE_BASE = 10000.0
FF_DIM = 2816

N_CTX_TOK = BATCH * SEQ
N_LAT_TOK = DEC_BATCH * DEC_SEQ
N_TOK = N_CTX_TOK + N_LAT_TOK
N_MOD_ROWS = 8

LANE = 128
SUBLANE = 8
VMEM_LIMIT = 56 * 1024 * 1024

OFF_G = 0
OFF_Z = 3072
OFF_SX = 4096
OFF_AX = 5120
OFF_AB = 5632
OFF_AC = 6144
OFF_BC = 6656
OFF_CQ = 7168
OFF_CKV = 7424
OFF_DT = 7680
OFF_KR = 7808
OFF_KRS = 7936
NP = 8192

NEG_BIG = -1e30


def _cparams(sem):
    return pltpu.CompilerParams(dimension_semantics=sem, vmem_limit_bytes=VMEM_LIMIT)


def _rms(x, w):
    ms = jnp.mean(x * x, axis=-1, keepdims=True)
    return x * lax.rsqrt(ms + EPS) * w


def _silu(x):
    return x * jax.nn.sigmoid(x)


def _mod_row(tile, tile_rows):
    n_ctx_tiles = N_CTX_TOK // tile_rows
    tiles_per_lat = DEC_SEQ // tile_rows
    return jnp.where(tile < n_ctx_tiles, 0, 1 + (tile - n_ctx_tiles) // tiles_per_lat)


MOD_TN = 1536


def _mod_kernel(c_ref, w_ref, b_ref, o_ref):
    c = c_ref[...]
    s = _silu(c).astype(BF16)
    o_ref[...] = jnp.dot(s, w_ref[...].astype(BF16), preferred_element_type=F32) + b_ref[...]


def _modulation(cond, w_ada, b_ada):
    n_col = 6 * D_MODEL
    return pl.pallas_call(
        _mod_kernel,
        grid=(DEPTH, n_col // MOD_TN),
        in_specs=[
            pl.BlockSpec((N_MOD_ROWS, D_MODEL), lambda l, j: (0, 0)),
            pl.BlockSpec((None, D_MODEL, MOD_TN), lambda l, j: (l, 0, j)),
            pl.BlockSpec((None, 1, MOD_TN), lambda l, j: (l, 0, j)),
        ],
        out_specs=pl.BlockSpec((None, N_MOD_ROWS, MOD_TN), lambda l, j: (l, 0, j)),
        out_shape=jax.ShapeDtypeStruct((DEPTH, N_MOD_ROWS, n_col), F32),
        compiler_params=_cparams(("arbitrary", "arbitrary")),
        name="modulation",
    )(cond, w_ada, b_ada.reshape(DEPTH, 1, n_col))


IN_TM = 1024
IN_TN = 1024
NORM_ROWS = 256


def _modulated_norm_to(h_ref, x_ref, nw_ref, sc_ref, sh_ref, rows):
    for r in range(0, rows, NORM_ROWS):
        x = x_ref[r:r + NORM_ROWS, :]
        h = _rms(x, nw_ref[...]) * (1.0 + sc_ref[...]) + sh_ref[...]
        h_ref[r:r + NORM_ROWS, :] = h.astype(BF16)


def _inproj_kernel(x_ref, nw_ref, sh_ref, sc_ref, w_ref, o_ref, h_ref):
    @pl.when(pl.program_id(1) == 0)
    def _():
        _modulated_norm_to(h_ref, x_ref, nw_ref, sc_ref, sh_ref, IN_TM)

    o_ref[...] = jnp.dot(h_ref[...], w_ref[...], preferred_element_type=F32)


def _in_projection(x, mod_l, norm_w, w_in_p):
    row = functools.partial(_mod_row, tile_rows=IN_TM)
    return pl.pallas_call(
        _inproj_kernel,
        grid=(N_TOK // IN_TM, NP // IN_TN),
        in_specs=[
            pl.BlockSpec((IN_TM, D_MODEL), lambda i, j: (i, 0)),
            pl.BlockSpec((1, D_MODEL), lambda i, j: (0, 0)),
            pl.BlockSpec((None, None, 1, D_MODEL), lambda i, j: (row(i), 0, 0, 0)),
            pl.BlockSpec((None, None, 1, D_MODEL), lambda i, j: (row(i), 1, 0, 0)),
            pl.BlockSpec((D_MODEL, IN_TN), lambda i, j: (0, j)),
        ],
        out_specs=pl.BlockSpec((IN_TM, IN_TN), lambda i, j: (i, j)),
        out_shape=jax.ShapeDtypeStruct((N_TOK, NP), F32),
        scratch_shapes=[pltpu.VMEM((IN_TM, D_MODEL), BF16)],
        compiler_params=_cparams(("arbitrary", "arbitrary")),
        name="in_projection",
    )(x, norm_w, mod_l, mod_l, w_in_p)


def _conv3_tile(u, prev_row, next_row, w_ref, rows):
    ridx = lax.broadcasted_iota(jnp.int32, (rows, 1), 0)
    up = jnp.where(ridx == 0, prev_row, pltpu.roll(u, 1, axis=0))
    dn = jnp.where(ridx == rows - 1, next_row, pltpu.roll(u, rows - 1, axis=0))
    return up * w_ref[0:1, :] + u * w_ref[1:2, :] + dn * w_ref[2:3, :]


SSD_TQ = 256
SSD_CPT = SSD_TQ // CHUNK
HALO = SUBLANE
N_SEQ = BATCH + DEC_BATCH


def _ssd_tables():
    blk, dirn, seq, first, last, hasp, hasn = [], [], [], [], [], [], []
    for d in range(2):
        for s in range(N_SEQ):
            if s < BATCH:
                base, nt = s * SEQ // SSD_TQ, SEQ // SSD_TQ
            else:
                base, nt = (N_CTX_TOK + (s - BATCH) * DEC_SEQ) // SSD_TQ, DEC_SEQ // SSD_TQ
            for k in range(nt):
                t = k if d == 0 else nt - 1 - k
                blk.append(base + t)
                dirn.append(d)
                seq.append(s)
                first.append(int(k == 0))
                last.append(int(k == nt - 1))
                hasp.append(int(t > 0))
                hasn.append(int(t < nt - 1))
    return [np.asarray(a, np.int32) for a in (blk, dirn, seq, first, last, hasp, hasn)]


def _split3(a):
    a1 = a.astype(BF16)
    r1 = a - a1.astype(F32)
    a2 = r1.astype(BF16)
    a3 = (r1 - a2.astype(F32)).astype(BF16)
    return a1, a2, a3


def _dot3(lhs_bf16, a):
    a1, a2, a3 = _split3(a)
    return (jnp.dot(lhs_bf16, a1, preferred_element_type=F32)
            + jnp.dot(lhs_bf16, a2, preferred_element_type=F32)
            + jnp.dot(lhs_bf16, a3, preferred_element_type=F32))


def _softplus(x):
    return jnp.maximum(x, 0.0) + jnp.log1p(jnp.exp(-jnp.abs(x)))


def _ssd_kernel(blk_t, dir_t, seq_t, first_t, last_t, hasp_t, hasn_t,
                x_ref, xp_ref, xn_ref, bc_ref, bcp_ref, bcn_ref, dt_ref,
                cwx_ref, cbx_ref, cwb_ref, cbb_ref, alog_ref, dtb_ref, dsk_ref,
                h0f_ref, h0b_ref,
                y_ref, fin_ref,
                st_ref, xc_ref, bcc_ref):
    s = pl.program_id(0)
    d = dir_t[s]
    seq = seq_t[s]
    is_fwd = d == 0

    hp = hasp_t[s].astype(F32)
    hn = hasn_t[s].astype(F32)
    cx = _conv3_tile(x_ref[...], xp_ref[HALO - 1:HALO, :] * hp, xn_ref[0:1, :] * hn, cwx_ref, SSD_TQ)
    xc_ref[...] = _silu(cx + cbx_ref[...])
    cb_ = _conv3_tile(bc_ref[...], bcp_ref[HALO - 1:HALO, :] * hp, bcn_ref[0:1, :] * hn, cwb_ref, SSD_TQ)
    bcc_ref[...] = _silu(cb_ + cbb_ref[...])

    @pl.when(jnp.logical_and(first_t[s] == 1, seq < BATCH))
    def _():
        st_ref[...] = jnp.zeros_like(st_ref)

    @pl.when(jnp.logical_and(first_t[s] == 1, seq >= BATCH))
    def _():
        h0 = jnp.where(is_fwd, h0f_ref[...], h0b_ref[...])
        st_ref[...] = h0.T

    ii = lax.broadcasted_iota(jnp.int32, (CHUNK, CHUNK), 0)
    jj = lax.broadcasted_iota(jnp.int32, (CHUNK, CHUNK), 1)
    sgn = jnp.where(is_fwd, 1, -1)
    mask = (jj - ii) * sgn <= 0
    mask_b = jnp.where(mask, 1.0, 0.0).astype(BF16)
    lo = jj < SSM_HEAD_DIM
    lo_row = lo[0:1, :]

    a_row = -jnp.exp(alog_ref[...])
    dtb = dtb_ref[...]

    e_r = lax.broadcasted_iota(jnp.int32, (LANE, SSM_INNER), 0)
    e_c = lax.broadcasted_iota(jnp.int32, (LANE, SSM_INNER), 1)
    expand = jnp.where(jnp.right_shift(e_c, 6) == e_r, 1.0, 0.0).astype(BF16)

    def chunk_body(k, carry):
        c = jnp.where(is_fwd, k, SSD_CPT - 1 - k)
        r0 = pl.multiple_of(c * CHUNK, CHUNK)
        dt = _softplus(dt_ref[pl.ds(r0, CHUNK), :] + dtb)
        a = dt * a_row
        cum = _dot3(mask_b, a)
        cum_t = cum.T
        dt_t = dt.T
        tot_row = jnp.where(is_fwd, cum[CHUNK - 1:CHUNK, :], cum[0:1, :])
        tot_col = jnp.where(is_fwd, cum_t[:, CHUNK - 1:CHUNK], cum_t[:, 0:1])
        w_rows = jnp.exp(tot_col - cum_t) * dt_t
        sdec = _dot3_rows(jnp.exp(tot_row), expand)
        ecum = jnp.exp(cum)

        bc = bcc_ref[pl.ds(r0, CHUNK), :]
        for g in range(SSM_GROUPS):
            b_g = bc[:, g * SSM_STATE:(g + 1) * SSM_STATE]
            c_g = bc[:, (SSM_GROUPS + g) * SSM_STATE:(SSM_GROUPS + g + 1) * SSM_STATE].astype(BF16)
            b_gt = b_g.T
            cb = jnp.dot(c_g, b_gt.astype(BF16), preferred_element_type=F32)
            pairs_per_group = SSM_HEADS // SSM_GROUPS // 2
            for kp in range(g * pairs_per_group, (g + 1) * pairs_per_group):
                h_a, h_b = 2 * kp, 2 * kp + 1
                ls = slice(kp * LANE, (kp + 1) * LANE)
                x_pair = xc_ref[pl.ds(r0, CHUNK), ls]
                rhs = jnp.concatenate([jnp.where(lo, x_pair, 0.0).astype(BF16),
                                       jnp.where(lo, 0.0, x_pair).astype(BF16)], axis=0)

                def head_lhs(h):
                    seg = cum[:, h:h + 1] - cum_t[h:h + 1, :]
                    dec = jnp.exp(jnp.where(mask, seg, NEG_BIG))
                    w_intra = dec * cb * dt_t[h:h + 1, :]
                    w_state = b_gt * w_rows[h:h + 1, :]
                    return w_intra.astype(BF16), w_state.astype(BF16)

                wi_a, ws_a = head_lhs(h_a)
                wi_b, ws_b = head_lhs(h_b)
                lhs = jnp.concatenate([jnp.concatenate([wi_a, wi_b], axis=1),
                                       jnp.concatenate([ws_a, ws_b], axis=1)], axis=0)
                both = jnp.dot(lhs, rhs, preferred_element_type=F32)
                y_diag = both[0:CHUNK, :]
                d_state = both[CHUNK:2 * CHUNK, :]

                h_pair = st_ref[:, ls]
                y_off = jnp.dot(c_g, h_pair.astype(BF16), preferred_element_type=F32)
                e_pair = jnp.where(lo, ecum[:, h_a:h_a + 1], ecum[:, h_b:h_b + 1])
                y_ref[pl.ds(r0, CHUNK), ls] = y_diag + y_off * e_pair + x_pair * dsk_ref[:, ls]
                st_ref[:, ls] = h_pair * sdec[:, ls] + d_state
        return carry

    lax.fori_loop(0, SSD_CPT, chunk_body, 0)

    @pl.when(jnp.logical_and(last_t[s] == 1, seq < BATCH))
    def _():
        fin_ref[...] = st_ref[...].T


def _dot3_rows(row, rhs_bf16):
    r8 = jnp.broadcast_to(row, (SUBLANE, row.shape[1]))
    r1, r2, r3 = _split3(r8)
    out = (jnp.dot(r1, rhs_bf16, preferred_element_type=F32)
           + jnp.dot(r2, rhs_bf16, preferred_element_type=F32)
           + jnp.dot(r3, rhs_bf16, preferred_element_type=F32))
    return out[0:1, :]


def _ssd(p, layer, conv_wx, conv_bx, conv_wbc, conv_bbc, alog, dtb, dskip, h0f, h0b):
    tables = [jnp.asarray(t) for t in _ssd_tables()]
    n_steps = int(tables[0].shape[0])
    halo_per_tile = SSD_TQ // HALO
    n_halo_blocks = N_TOK // HALO

    def tile_map(col):
        return lambda s, blk, *_: (blk[s], col)

    def prev_map(col):
        return lambda s, blk, *_: (jnp.maximum(blk[s] * halo_per_tile - 1, 0), col)

    def next_map(col):
        return lambda s, blk, *_: (jnp.minimum((blk[s] + 1) * halo_per_tile, n_halo_blocks - 1), col)

    def dir_map(s, blk, dirn, *_):
        return (dirn[s], 0, 0)

    def h0_map(s, blk, dirn, seq, *_):
        return (jnp.maximum(seq[s] - BATCH, 0), layer, 0, 0)

    def const2(s, *_):
        return (0, 0)

    cx, cbc, cdt = OFF_SX // SSM_INNER, OFF_BC // 512, OFF_DT // LANE
    grid_spec = pltpu.PrefetchScalarGridSpec(
        num_scalar_prefetch=7,
        grid=(n_steps,),
        in_specs=[
            pl.BlockSpec((SSD_TQ, SSM_INNER), tile_map(cx)),
            pl.BlockSpec((HALO, SSM_INNER), prev_map(cx)),
            pl.BlockSpec((HALO, SSM_INNER), next_map(cx)),
            pl.BlockSpec((SSD_TQ, 512), tile_map(cbc)),
            pl.BlockSpec((HALO, 512), prev_map(cbc)),
            pl.BlockSpec((HALO, 512), next_map(cbc)),
            pl.BlockSpec((SSD_TQ, LANE), tile_map(cdt)),
            pl.BlockSpec((3, SSM_INNER), const2),
            pl.BlockSpec((1, SSM_INNER), const2),
            pl.BlockSpec((3, 512), const2),
            pl.BlockSpec((1, 512), const2),
            pl.BlockSpec((None, 1, LANE), dir_map),
            pl.BlockSpec((None, 1, LANE), dir_map),
            pl.BlockSpec((None, 1, SSM_INNER), dir_map),
            pl.BlockSpec((None, None, SSM_INNER, SSM_STATE), h0_map),
            pl.BlockSpec((None, None, SSM_INNER, SSM_STATE), h0_map),
        ],
        out_specs=[
            pl.BlockSpec((None, SSD_TQ, SSM_INNER), lambda s, blk, dirn, *_: (dirn[s], blk[s], 0)),
            pl.BlockSpec((None, None, SSM_INNER, SSM_STATE),
                         lambda s, blk, dirn, seq, *_: (dirn[s], jnp.minimum(seq[s], BATCH - 1), 0, 0)),
        ],
        scratch_shapes=[
            pltpu.VMEM((SSM_STATE, SSM_INNER), F32),
            pltpu.VMEM((SSD_TQ, SSM_INNER), F32),
            pltpu.VMEM((SSD_TQ, 512), F32),
        ],
    )
    return pl.pallas_call(
        _ssd_kernel,
        grid_spec=grid_spec,
        out_shape=[
            jax.ShapeDtypeStruct((2, N_TOK, SSM_INNER), F32),
            jax.ShapeDtypeStruct((2, BATCH, SSM_INNER, SSM_STATE), F32),
        ],
        compiler_params=_cparams(("arbitrary",)),
        name="ssd_scan",
    )(*tables, p, p, p, p, p, p, p, conv_wx, conv_bx, conv_wbc, conv_bbc, alog, dtb, dskip, h0f, h0b)


QK_PAD = 128
N_QROWS = MLA_HEADS * QK_PAD
N_VROWS = MLA_HEADS * V_DIM
NT_DIMS = (((1,), (1,)), ((), ()))


def _mla_prep_kernel(*refs, tm, normalize, with_q):
    if with_q:
        (cq_ref, ckv_ref, kr_ref, krs_ref, cos_ref, sin_ref, cost_ref, sint_ref,
         qnw_ref, kvnw_ref, wq_ref, wk_ref, wv_ref,
         qt_ref, k_ref, vt_ref, ckvn_ref) = refs
    else:
        (ckv_ref, kr_ref, krs_ref, cos_ref, sin_ref, kvnw_ref, wk_ref, wv_ref,
         k_ref, vt_ref) = refs

    ckv = ckv_ref[...]
    if normalize:
        ckv = _rms(ckv, kvnw_ref[...])
        ckvn_ref[...] = ckv
    ckv_b = ckv.astype(BF16)
    kn = jnp.dot(ckv_b, wk_ref[...], preferred_element_type=F32)
    kr = kr_ref[...] * cos_ref[...] + krs_ref[...] * sin_ref[...]
    for h in range(MLA_HEADS):
        hs = slice(h * QK_PAD, (h + 1) * QK_PAD)
        k_ref[:, hs] = (kn[:, hs] + kr).astype(BF16)
    vt_ref[...] = lax.dot_general(wv_ref[...], ckv_b, NT_DIMS, preferred_element_type=F32).astype(BF16)

    if with_q:
        cqn = _rms(cq_ref[...], qnw_ref[...]).astype(BF16)
        qq = lax.dot_general(wq_ref[...], cqn, NT_DIMS, preferred_element_type=F32)
        for h in range(MLA_HEADS):
            q_h = qq[h * QK_PAD:(h + 1) * QK_PAD, :]
            qs_h = qq[N_QROWS + h * QK_PAD:N_QROWS + (h + 1) * QK_PAD, :]
            qt_ref[h * QK_PAD:(h + 1) * QK_PAD, :] = (q_h * cost_ref[...] + qs_h * sint_ref[...]).astype(BF16)


def _mla_prep_tokens(p, row0, n_rows, tm, tab_map, cos, sin, cos_t, sin_t, qnw, kvnw, wq_t, wk, wv_t):
    b0 = row0 // tm
    nt = n_rows // tm
    kernel = functools.partial(_mla_prep_kernel, tm=tm, normalize=True, with_q=True)

    def pcol(width, off):
        return pl.BlockSpec((tm, width), lambda i: (b0 + i, off // width))

    def full(a):
        return pl.BlockSpec(a.shape, lambda i: (0,) * a.ndim)

    return pl.pallas_call(
        kernel,
        grid=(nt,),
        in_specs=[
            pcol(Q_LORA, OFF_CQ), pcol(KV_LORA, OFF_CKV), pcol(LANE, OFF_KR), pcol(LANE, OFF_KRS),
            pl.BlockSpec((tm, LANE), lambda i: (tab_map(i), 0)),
            pl.BlockSpec((tm, LANE), lambda i: (tab_map(i), 0)),
            pl.BlockSpec((LANE, tm), lambda i: (0, tab_map(i))),
            pl.BlockSpec((LANE, tm), lambda i: (0, tab_map(i))),
            full(qnw), full(kvnw), full(wq_t), full(wk), full(wv_t),
        ],
        out_specs=[
            pl.BlockSpec((N_QROWS, tm), lambda i: (0, i)),
            pl.BlockSpec((None, tm, N_QROWS), lambda i: (i, 0, 0)),
            pl.BlockSpec((None, N_VROWS, tm), lambda i: (i, 0, 0)),
            pl.BlockSpec((tm, KV_LORA), lambda i: (i, 0)),
        ],
        out_shape=[
            jax.ShapeDtypeStruct((N_QROWS, n_rows), BF16),
            jax.ShapeDtypeStruct((nt, tm, N_QROWS), BF16),
            jax.ShapeDtypeStruct((nt, N_VROWS, tm), BF16),
            jax.ShapeDtypeStruct((n_rows, KV_LORA), F32),
        ],
        compiler_params=_cparams(("arbitrary",)),
        name="mla_prep",
    )(p, p, p, p, cos, sin, cos_t, sin_t, qnw, kvnw, wq_t, wk, wv_t)


def _mla_prep_cache(ckv, kr_pad, ones_tab, zeros_tab, kvnw, wk, wv_t, tm):
    n_rows = ckv.shape[0]
    nt = n_rows // tm
    kernel = functools.partial(_mla_prep_kernel, tm=tm, normalize=False, with_q=False)

    def full(a):
        return pl.BlockSpec(a.shape, lambda i: (0,) * a.ndim)

    return pl.pallas_call(
        kernel,
        grid=(nt,),
        in_specs=[
            pl.BlockSpec((tm, KV_LORA), lambda i: (i, 0)),
            pl.BlockSpec((tm, LANE), lambda i: (i, 0)),
            pl.BlockSpec((tm, LANE), lambda i: (i, 0)),
            pl.BlockSpec((tm, LANE), lambda i: (0, 0)),
            pl.BlockSpec((tm, LANE), lambda i: (0, 0)),
            full(kvnw), full(wk), full(wv_t),
        ],
        out_specs=[
            pl.BlockSpec((None, tm, N_QROWS), lambda i: (i, 0, 0)),
            pl.BlockSpec((None, N_VROWS, tm), lambda i: (i, 0, 0)),
        ],
        out_shape=[
            jax.ShapeDtypeStruct((nt, tm, N_QROWS), BF16),
            jax.ShapeDtypeStruct((nt, N_VROWS, tm), BF16),
        ],
        compiler_params=_cparams(("arbitrary",)),
        name="mla_prep_cache",
    )(ckv, kr_pad, kr_pad, ones_tab, zeros_tab, kvnw, wk, wv_t)


ATT_TQ = 256
SM_SCALE = 1.0 / math.sqrt(QK_DIM)


def _attn_kernel(qt_ref, k_ref, vt_ref, o_ref, ot_ref, s_ref, *, n_kt, tq, tk):
    c2 = SM_SCALE * math.log2(math.e)
    for h in range(MLA_HEADS):
        q_t = qt_ref[h * QK_PAD:(h + 1) * QK_PAD, :]
        sb = s_ref.at[h % 2]
        m8 = jnp.full((SUBLANE, tq), NEG_BIG, F32)
        for kt in range(n_kt):
            s = jnp.dot(k_ref[kt, :, h * QK_PAD:(h + 1) * QK_PAD], q_t, preferred_element_type=F32)
            sb[kt] = s
            m8 = jnp.maximum(m8, jnp.max(s.reshape(tk // SUBLANE, SUBLANE, tq), axis=0))
        m = jnp.max(m8, axis=0, keepdims=True)
        l8 = jnp.zeros((SUBLANE, tq), F32)
        acc = jnp.zeros((V_DIM, tq), F32)
        for kt in range(n_kt):
            pr = jnp.exp2((sb[kt] - m) * c2)
            l8 = l8 + jnp.sum(pr.reshape(tk // SUBLANE, SUBLANE, tq), axis=0)
            acc = acc + jnp.dot(vt_ref[kt, h * V_DIM:(h + 1) * V_DIM, :], pr.astype(BF16),
                                preferred_element_type=F32)
        ot_ref[h * V_DIM:(h + 1) * V_DIM, :] = acc / jnp.sum(l8, axis=0, keepdims=True)
    o_ref[...] = ot_ref[...].T


def _attention(qt, k3, vt3, n_batch, lq, n_kt, tk):
    tq = min(ATT_TQ, lq)
    nq = lq // tq
    k4 = k3.reshape(n_batch, n_kt, tk, N_QROWS)
    v4 = vt3.reshape(n_batch, n_kt, N_VROWS, tk)
    kernel = functools.partial(_attn_kernel, n_kt=n_kt, tq=tq, tk=tk)
    return pl.pallas_call(
        kernel,
        grid=(n_batch, nq),
        in_specs=[
            pl.BlockSpec((N_QROWS, tq), lambda b, i: (0, b * nq + i)),
            pl.BlockSpec((None, n_kt, tk, N_QROWS), lambda b, i: (b, 0, 0, 0)),
            pl.BlockSpec((None, n_kt, N_VROWS, tk), lambda b, i: (b, 0, 0, 0)),
        ],
        out_specs=pl.BlockSpec((tq, N_VROWS), lambda b, i: (b * nq + i, 0)),
        out_shape=jax.ShapeDtypeStruct((n_batch * lq, N_VROWS), F32),
        scratch_shapes=[pltpu.VMEM((N_VROWS, tq), F32), pltpu.VMEM((2, n_kt, tk, tq), F32)],
        compiler_params=_cparams(("arbitrary", "arbitrary")),
        name="attention",
    )(qt, k4, v4)


MG_TM = 256


def _merge_kernel(ax_ref, axp_ref, axn_ref, ac_ref, acp_ref, acn_ref, ab_ref,
                  g_ref, z_ref, y_ref, zc_ref, x_ref, gm_ref,
                  cw_ref, wa_ref, nw_ref, wb_ref, wc_ref, wo_ref, o_ref):
    i = pl.program_id(0)
    n_ctx_tiles = N_CTX_TOK // MG_TM
    tiles_per_lat = DEC_SEQ // MG_TM
    t_in_seq = (i - n_ctx_tiles) % tiles_per_lat
    is_lat = i >= n_ctx_tiles
    hp = jnp.logical_and(is_lat, t_in_seq > 0).astype(F32)
    hn = jnp.logical_and(is_lat, t_in_seq < tiles_per_lat - 1).astype(F32)

    u = ac_ref[...] * ax_ref[...]
    u_prev = acp_ref[HALO - 1:HALO, :] * axp_ref[HALO - 1:HALO, :] * hp
    u_next = acn_ref[0:1, :] * axn_ref[0:1, :] * hn
    za = ab_ref[...] * _conv3_tile(u, u_prev, u_next, cw_ref, MG_TM)
    y_a = jnp.dot(za.astype(BF16), wa_ref[...], preferred_element_type=F32)

    yb = (y_ref[0] + y_ref[1]) * _silu(z_ref[...])
    zb = _rms(yb, nw_ref[...])
    y_b = jnp.dot(zb.astype(BF16), wb_ref[...], preferred_element_type=F32)

    y_c = jnp.dot(zc_ref[...].astype(BF16), wc_ref[...], preferred_element_type=F32)

    merged = (jax.nn.sigmoid(g_ref[:, 0:D_MODEL]) * y_a
              + jax.nn.sigmoid(g_ref[:, D_MODEL:2 * D_MODEL]) * y_b
              + jax.nn.sigmoid(g_ref[:, 2 * D_MODEL:3 * D_MODEL]) * y_c)
    o = jnp.dot(merged.astype(BF16), wo_ref[...], preferred_element_type=F32)
    o_ref[...] = x_ref[...] + gm_ref[...] * o


def _merge(p, y_ssd, zc, x, mod_l, conv_w, wa, nw, wb, wc, wo):
    halo_per_tile = MG_TM // HALO
    n_halo_blocks = N_TOK // HALO
    row = functools.partial(_mod_row, tile_rows=MG_TM)

    def pcol(width, off):
        return pl.BlockSpec((MG_TM, width), lambda i: (i, off // width))

    def pprev(off):
        return pl.BlockSpec((HALO, A_WIDTH), lambda i: (jnp.maximum(i * halo_per_tile - 1, 0), off // A_WIDTH))

    def pnext(off):
        return pl.BlockSpec((HALO, A_WIDTH),
                            lambda i: (jnp.minimum((i + 1) * halo_per_tile, n_halo_blocks - 1), off // A_WIDTH))

    def full(a):
        return pl.BlockSpec(a.shape, lambda i: (0,) * a.ndim)

    return pl.pallas_call(
        _merge_kernel,
        grid=(N_TOK // MG_TM,),
        in_specs=[
            pcol(A_WIDTH, OFF_AX), pprev(OFF_AX), pnext(OFF_AX),
            pcol(A_WIDTH, OFF_AC), pprev(OFF_AC), pnext(OFF_AC),
            pcol(A_WIDTH, OFF_AB),
            pcol(3 * D_MODEL, OFF_G),
            pcol(SSM_INNER, OFF_Z),
            pl.BlockSpec((2, MG_TM, SSM_INNER), lambda i: (0, i, 0)),
            pl.BlockSpec((MG_TM, N_VROWS), lambda i: (i, 0)),
            pl.BlockSpec((MG_TM, D_MODEL), lambda i: (i, 0)),
            pl.BlockSpec((None, None, 1, D_MODEL), lambda i: (row(i), 2, 0, 0)),
            full(conv_w), full(wa), full(nw), full(wb), full(wc), full(wo),
        ],
        out_specs=pl.BlockSpec((MG_TM, D_MODEL), lambda i: (i, 0)),
        out_shape=jax.ShapeDtypeStruct((N_TOK, D_MODEL), F32),
        compiler_params=_cparams(("arbitrary",)),
        name="merge",
    )(p, p, p, p, p, p, p, p, p, y_ssd, zc, x, mod_l, conv_w, wa, nw, wb, wc, wo)


FF_TM = 512
FF_TF = 1408


def _ffn_kernel(x_ref, nw_ref, sh_ref, sc_ref, gm_ref, w1_ref, w3_ref, w2_ref, o_ref, h_ref, acc_ref):
    j = pl.program_id(1)

    @pl.when(j == 0)
    def _():
        _modulated_norm_to(h_ref, x_ref, nw_ref, sc_ref, sh_ref, FF_TM)

    h = h_ref[...]
    a = jnp.dot(h, w1_ref[...], preferred_element_type=F32)
    b = jnp.dot(h, w3_ref[...], preferred_element_type=F32)
    part = jnp.dot((_silu(a) * b).astype(BF16), w2_ref[...], preferred_element_type=F32)

    @pl.when(j == 0)
    def _():
        acc_ref[...] = part

    @pl.when(j > 0)
    def _():
        acc_ref[...] += part

    @pl.when(j == pl.num_programs(1) - 1)
    def _():
        o_ref[...] = x_ref[...] + gm_ref[...] * acc_ref[...]


def _ffn(x, mod_l, norm_w, w1, w3, w2):
    row = functools.partial(_mod_row, tile_rows=FF_TM)
    return pl.pallas_call(
        _ffn_kernel,
        grid=(N_TOK // FF_TM, FF_DIM // FF_TF),
        in_specs=[
            pl.BlockSpec((FF_TM, D_MODEL), lambda i, j: (i, 0)),
            pl.BlockSpec((1, D_MODEL), lambda i, j: (0, 0)),
            pl.BlockSpec((None, None, 1, D_MODEL), lambda i, j: (row(i), 3, 0, 0)),
            pl.BlockSpec((None, None, 1, D_MODEL), lambda i, j: (row(i), 4, 0, 0)),
            pl.BlockSpec((None, None, 1, D_MODEL), lambda i, j: (row(i), 5, 0, 0)),
            pl.BlockSpec((D_MODEL, FF_TF), lambda i, j: (0, j)),
            pl.BlockSpec((D_MODEL, FF_TF), lambda i, j: (0, j)),
            pl.BlockSpec((FF_TF, D_MODEL), lambda i, j: (j, 0)),
        ],
        out_specs=pl.BlockSpec((FF_TM, D_MODEL), lambda i, j: (i, 0)),
        out_shape=jax.ShapeDtypeStruct((N_TOK, D_MODEL), F32),
        scratch_shapes=[pltpu.VMEM((FF_TM, D_MODEL), BF16), pltpu.VMEM((FF_TM, D_MODEL), F32)],
        compiler_params=_cparams(("arbitrary", "arbitrary")),
        name="ffn",
    )(x, norm_w, mod_l, mod_l, mod_l, w1, w3, w2)


FN_TM = 1024


def _final_norm_kernel(x_ref, w_ref, o_ref):
    o_ref[...] = _rms(x_ref[...], w_ref[...])


def _final_norm(x, w, row0, n_rows):
    b0 = row0 // FN_TM
    return pl.pallas_call(
        _final_norm_kernel,
        grid=(n_rows // FN_TM,),
        in_specs=[pl.BlockSpec((FN_TM, D_MODEL), lambda i: (b0 + i, 0)),
                  pl.BlockSpec((1, D_MODEL), lambda i: (0, 0))],
        out_specs=pl.BlockSpec((FN_TM, D_MODEL), lambda i: (i, 0)),
        out_shape=jax.ShapeDtypeStruct((n_rows, D_MODEL), F32),
        compiler_params=_cparams(("arbitrary",)),
        name="final_norm",
    )(x, w)


def _pad_in_weights(w_in):
    o = 0
    a_x = w_in[..., o:o + 512]; o += 512
    a_b = w_in[..., o:o + 512]; o += 512
    a_c = w_in[..., o:o + 512]; o += 512
    s_z = w_in[..., o:o + 1024]; o += 1024
    s_x = w_in[..., o:o + 1024]; o += 1024
    s_bc = w_in[..., o:o + 512]; o += 512
    s_dt = w_in[..., o:o + 16]; o += 16
    cq = w_in[..., o:o + 256]; o += 256
    ckv = w_in[..., o:o + 256]; o += 256
    kr = w_in[..., o:o + 32]; o += 32
    gates = w_in[..., o:o + 3072]; o += 3072

    def z(n):
        return jnp.zeros(w_in.shape[:-1] + (n,), w_in.dtype)

    kr_sw = jnp.concatenate([kr[..., 16:], kr[..., :16]], axis=-1)
    cols = [gates, s_z, s_x, a_x, a_b, a_c, s_bc, cq, ckv,
            s_dt, z(LANE - 16),
            z(64), kr, z(32),
            z(64), kr_sw, z(32),
            z(NP - OFF_KRS - LANE)]
    out = jnp.concatenate(cols, axis=-1).astype(BF16)
    assert out.shape[-1] == NP
    return out


def _q_weights_t(w_uq):
    w = w_uq.reshape(DEPTH, Q_LORA, MLA_HEADS, QK_DIM)
    nope, x1, x2 = w[..., :NOPE_DIM], w[..., NOPE_DIM:NOPE_DIM + 16], w[..., NOPE_DIM + 16:]
    z32 = jnp.zeros_like(w[..., :32])
    z64 = jnp.zeros_like(nope)
    q = jnp.concatenate([nope, x1, x2, z32], axis=-1).reshape(DEPTH, Q_LORA, N_QROWS)
    qs = jnp.concatenate([z64, x2, x1, z32], axis=-1).reshape(DEPTH, Q_LORA, N_QROWS)
    return jnp.swapaxes(jnp.concatenate([q, qs], axis=-1), 1, 2).astype(BF16)


def _kv_weights(w_ukv):
    w = w_ukv.reshape(DEPTH, KV_LORA, MLA_HEADS, NOPE_DIM + V_DIM)
    kn = jnp.concatenate([w[..., :NOPE_DIM], jnp.zeros_like(w[..., :QK_PAD - NOPE_DIM])], axis=-1)
    wk = kn.reshape(DEPTH, KV_LORA, N_QROWS).astype(BF16)
    wv_t = jnp.swapaxes(w[..., NOPE_DIM:].reshape(DEPTH, KV_LORA, N_VROWS), 1, 2).astype(BF16)
    return wk, wv_t


def _rope_tables(n_tokens, lead_rows):
    n_rows = n_tokens // GRID_W
    row = jnp.repeat(jnp.arange(n_rows, dtype=F32), GRID_W)
    col = jnp.tile(jnp.arange(GRID_W, dtype=F32), n_rows)
    pairs = ROPE_DIM // 4
    inv = ROPE_BASE ** (-jnp.arange(pairs, dtype=F32) / pairs)
    ang = jnp.concatenate([row[:, None] * inv, col[:, None] * inv], axis=-1)
    cos, sin = jnp.cos(ang), jnp.sin(ang)
    ones = jnp.ones((n_tokens, NOPE_DIM), F32)
    z32 = jnp.zeros((n_tokens, 32), F32)
    cos_l = jnp.concatenate([ones, cos, cos, z32], axis=-1)
    sin_l = jnp.concatenate([jnp.zeros_like(ones), -sin, sin, z32], axis=-1)
    ident_c = jnp.concatenate([jnp.ones((lead_rows, NOPE_DIM + ROPE_DIM), F32), jnp.zeros((lead_rows, 32), F32)], -1)
    ident_s = jnp.zeros((lead_rows, LANE), F32)
    return jnp.concatenate([ident_c, cos_l], axis=0), jnp.concatenate([ident_s, sin_l], axis=0)


PREP_TM_CTX = SEQ
PREP_TM_LAT = 512


def kernel(x_prompt, x_sample, c, cache_ckv, cache_krope, state_ssm_fwd, state_ssm_bwd, c_ctx, w_in, a_conv_w, w_a_out, ssm_conv_w, ssm_conv_b, ssm_a_log, ssm_dt_bias, ssm_d, ssm_norm_w, w_b_out, q_norm_w, w_uq, kv_norm_w, w_ukv, w_c_out, w_o, w_ada, b_ada, norm1_w, norm2_w, w_ff1, w_ff3, w_ff2, final_norm_w):
    w_in_p = _pad_in_weights(w_in)
    wq_t = _q_weights_t(w_uq)
    wk, wv_t = _kv_weights(w_ukv)
    wa, wb, wc, wo = (w.astype(BF16) for w in (w_a_out, w_b_out, w_c_out, w_o))
    w1, w3, w2 = (w.astype(BF16) for w in (w_ff1, w_ff3, w_ff2))

    cond = jnp.concatenate([c_ctx[None, :], c, jnp.zeros((N_MOD_ROWS - 1 - DEC_BATCH, D_MODEL), F32)], axis=0)
    mod = _modulation(cond, w_ada, b_ada).reshape(DEPTH, N_MOD_ROWS, 6, 1, D_MODEL)

    conv_wx = ssm_conv_w[..., :SSM_INNER]
    conv_wbc = ssm_conv_w[..., SSM_INNER:]
    conv_bx = ssm_conv_b[:, None, :SSM_INNER]
    conv_bbc = ssm_conv_b[:, None, SSM_INNER:]
    pad_h = ((0, 0), (0, 0), (0, 0), (0, LANE - SSM_HEADS))
    alog = jnp.pad(ssm_a_log[:, :, None, :], pad_h)
    dtb = jnp.pad(ssm_dt_bias[:, :, None, :], pad_h)
    dskip = jnp.repeat(ssm_d, SSM_HEAD_DIM, axis=-1)[:, :, None, :]
    h0f = state_ssm_fwd.reshape(DEC_BATCH, DEPTH, SSM_INNER, SSM_STATE)
    h0b = state_ssm_bwd.reshape(DEC_BATCH, DEPTH, SSM_INNER, SSM_STATE)

    cos_c, sin_c = _rope_tables(DEC_SEQ, PREP_TM_CTX)
    cos_l, sin_l = cos_c[PREP_TM_CTX:], sin_c[PREP_TM_CTX:]
    ident_cos, ident_sin = cos_c[:PREP_TM_CTX], sin_c[:PREP_TM_CTX]
    ones_tab = jnp.concatenate([ident_cos, ident_cos], axis=0)
    zeros_tab = jnp.zeros_like(ones_tab)
    cache_kr_pad = jnp.pad(cache_krope, ((0, 0), (0, 0), (0, 0), (64, 32)))

    x = jnp.concatenate([x_prompt.reshape(N_CTX_TOK, D_MODEL), x_sample.reshape(N_LAT_TOK, D_MODEL)], axis=0)

    lat_tiles_per_seq = DEC_SEQ // PREP_TM_LAT
    new_ckv, new_kr, new_f, new_b = [], [], [], []
    for l in range(DEPTH):
        p = _in_projection(x, mod[l], norm1_w[l][None, :], w_in_p[l])

        y_ssd, fin = _ssd(p, l, conv_wx[l], conv_bx[l], conv_wbc[l], conv_bbc[l],
                          alog[l], dtb[l], dskip[l], h0f, h0b)
        new_f.append(fin[0])
        new_b.append(fin[1])

        qnw, kvnw = q_norm_w[l][None, :], kv_norm_w[l][None, :]
        qt_c, k_c, vt_c, ckv_c = _mla_prep_tokens(
            p, 0, N_CTX_TOK, PREP_TM_CTX, lambda i: 0, ident_cos, ident_sin, ident_cos.T, ident_sin.T,
            qnw, kvnw, wq_t[l], wk[l], wv_t[l])
        qt_l, k_l, vt_l, _ = _mla_prep_tokens(
            p, N_CTX_TOK, N_LAT_TOK, PREP_TM_LAT, lambda i: i % lat_tiles_per_seq, cos_l, sin_l, cos_l.T, sin_l.T,
            qnw, kvnw, wq_t[l], wk[l], wv_t[l])
        k_p, vt_p = _mla_prep_cache(cache_ckv[:, l].reshape(DEC_BATCH * PAST_LEN, KV_LORA),
                                    cache_kr_pad[:, l].reshape(DEC_BATCH * PAST_LEN, LANE),
                                    ones_tab, zeros_tab, kvnw, wk[l], wv_t[l], PAST_LEN)
        new_ckv.append(ckv_c.reshape(BATCH, SEQ, KV_LORA))
        new_kr.append(p[:N_CTX_TOK, OFF_KR + 64:OFF_KR + 64 + ROPE_DIM].reshape(BATCH, SEQ, ROPE_DIM))

        zc_c = _attention(qt_c, k_c, vt_c, BATCH, SEQ, 1, SEQ)
        n_kt = 1 + lat_tiles_per_seq
        k_all = jnp.concatenate([k_p[:, None], k_l.reshape(DEC_BATCH, lat_tiles_per_seq, PREP_TM_LAT, N_QROWS)], axis=1)
        vt_all = jnp.concatenate([vt_p[:, None], vt_l.reshape(DEC_BATCH, lat_tiles_per_seq, N_VROWS, PREP_TM_LAT)], axis=1)
        zc_l = _attention(qt_l, k_all.reshape(DEC_BATCH * n_kt, PREP_TM_LAT, N_QROWS),
                          vt_all.reshape(DEC_BATCH * n_kt, N_VROWS, PREP_TM_LAT), DEC_BATCH, DEC_SEQ, n_kt, PREP_TM_LAT)
        zc = jnp.concatenate([zc_c, zc_l], axis=0)

        x = _merge(p, y_ssd, zc, x, mod[l], a_conv_w[l], wa[l], ssm_norm_w[l][None, :], wb[l], wc[l], wo[l])
        x = _ffn(x, mod[l], norm2_w[l][None, :], w1[l], w3[l], w2[l])

    fw = final_norm_w[None, :]
    y_prompt = _final_norm(x, fw, 0, N_CTX_TOK).reshape(BATCH, SEQ, D_MODEL)
    y_sample = _final_norm(x, fw, N_CTX_TOK, N_LAT_TOK).reshape(DEC_BATCH, DEC_SEQ, D_MODEL)
    hshape = (BATCH, DEPTH, SSM_HEADS, SSM_HEAD_DIM, SSM_STATE)
    return (y_prompt, y_sample,
            jnp.stack(new_ckv, axis=1), jnp.stack(new_kr, axis=1),
            jnp.stack(new_f, axis=1).reshape(hshape), jnp.stack(new_b, axis=1).reshape(hshape))
```

```python
import functools
import math

import jax
import jax.numpy as jnp
import numpy as np
from jax import lax
from jax.experimental import pallas as pl
from jax.experimental.pallas import tpu as pltpu

F32 = jnp.float32
BF16 = jnp.bfloat16

D_MODEL = 1024
BATCH = 16
SEQ = 256
DEPTH = 4
DEC_BATCH = 4
DEC_SEQ = 4096
PAST_LEN = 512
GRID_W = 64
EPS = 1e-6
A_WIDTH = 512
SSM_INNER = 1024
SSM_HEAD_DIM = 64
SSM_HEADS = 16
SSM_GROUPS = 2
SSM_STATE = 128
CHUNK = 128
MLA_HEADS = 8
Q_LORA = 256
KV_LORA = 256
NOPE_DIM = 64
ROPE_DIM = 32
V_DIM = 64
QK_DIM = NOPE_DIM + ROPE_DIM
ROPE_BASE = 10000.0
FF_DIM = 2816

N_CTX_TOK = BATCH * SEQ
N_LAT_TOK = DEC_BATCH * DEC_SEQ
N_TOK = N_CTX_TOK + N_LAT_TOK
N_MOD_ROWS = 8

LANE = 128
SUBLANE = 8
VMEM_LIMIT = 56 * 1024 * 1024

OFF_G = 0
OFF_Z = 3072
OFF_SX = 4096
OFF_AX = 5120
OFF_AB = 5632
OFF_AC = 6144
OFF_BC = 6656
OFF_CQ = 7168
OFF_CKV = 7424
OFF_DT = 7680
OFF_KR = 7808
OFF_KRS = 7936
NP = 8192

NEG_BIG = -1e30


def _cparams(sem):
    return pltpu.CompilerParams(dimension_semantics=sem, vmem_limit_bytes=VMEM_LIMIT)


def _rms(x, w):
    ms = jnp.mean(x * x, axis=-1, keepdims=True)
    return x * lax.rsqrt(ms + EPS) * w


def _silu(x):
    return x * jax.nn.sigmoid(x)


def _mod_row(tile, tile_rows):
    n_ctx_tiles = N_CTX_TOK // tile_rows
    tiles_per_lat = DEC_SEQ // tile_rows
    return jnp.where(tile < n_ctx_tiles, 0, 1 + (tile - n_ctx_tiles) // tiles_per_lat)


MOD_TN = 1536


def _mod_kernel(c_ref, w_ref, b_ref, o_ref):
    c = c_ref[...]
    s = _silu(c).astype(BF16)
    o_ref[...] = jnp.dot(s, w_ref[...].astype(BF16), preferred_element_type=F32) + b_ref[...]


def _modulation(cond, w_ada, b_ada):
    n_col = 6 * D_MODEL
    return pl.pallas_call(
        _mod_kernel,
        grid=(DEPTH, n_col // MOD_TN),
        in_specs=[
            pl.BlockSpec((N_MOD_ROWS, D_MODEL), lambda l, j: (0, 0)),
            pl.BlockSpec((None, D_MODEL, MOD_TN), lambda l, j: (l, 0, j)),
            pl.BlockSpec((None, 1, MOD_TN), lambda l, j: (l, 0, j)),
        ],
        out_specs=pl.BlockSpec((None, N_MOD_ROWS, MOD_TN), lambda l, j: (l, 0, j)),
        out_shape=jax.ShapeDtypeStruct((DEPTH, N_MOD_ROWS, n_col), F32),
        compiler_params=_cparams(("arbitrary", "arbitrary")),
        name="modulation",
    )(cond, w_ada, b_ada.reshape(DEPTH, 1, n_col))


IN_TM = 1024
IN_TN = 1024
NORM_ROWS = 256


def _modulated_norm_to(h_ref, x_ref, nw_ref, sc_ref, sh_ref, rows):
    for r in range(0, rows, NORM_ROWS):
        x = x_ref[r:r + NORM_ROWS, :]
        h = _rms(x, nw_ref[...]) * (1.0 + sc_ref[...]) + sh_ref[...]
        h_ref[r:r + NORM_ROWS, :] = h.astype(BF16)


def _inproj_kernel(x_ref, nw_ref, sh_ref, sc_ref, w_ref, o_ref, h_ref):
    @pl.when(pl.program_id(1) == 0)
    def _():
        _modulated_norm_to(h_ref, x_ref, nw_ref, sc_ref, sh_ref, IN_TM)

    o_ref[...] = jnp.dot(h_ref[...], w_ref[...], preferred_element_type=F32)


def _in_projection(x, mod_l, norm_w, w_in_p):
    row = functools.partial(_mod_row, tile_rows=IN_TM)
    return pl.pallas_call(
        _inproj_kernel,
        grid=(N_TOK // IN_TM, NP // IN_TN),
        in_specs=[
            pl.BlockSpec((IN_TM, D_MODEL), lambda i, j: (i, 0)),
            pl.BlockSpec((1, D_MODEL), lambda i, j: (0, 0)),
            pl.BlockSpec((None, None, 1, D_MODEL), lambda i, j: (row(i), 0, 0, 0)),
            pl.BlockSpec((None, None, 1, D_MODEL), lambda i, j: (row(i), 1, 0, 0)),
            pl.BlockSpec((D_MODEL, IN_TN), lambda i, j: (0, j)),
        ],
        out_specs=pl.BlockSpec((IN_TM, IN_TN), lambda i, j: (i, j)),
        out_shape=jax.ShapeDtypeStruct((N_TOK, NP), F32),
        scratch_shapes=[pltpu.VMEM((IN_TM, D_MODEL), BF16)],
        compiler_params=_cparams(("arbitrary", "arbitrary")),
        name="in_projection",
    )(x, norm_w, mod_l, mod_l, w_in_p)


def _conv3_tile(u, prev_row, next_row, w_ref, rows):
    ridx = lax.broadcasted_iota(jnp.int32, (SUBLANE, 1), 0)
    up = pltpu.roll(u, 1, axis=0)
    up = jnp.concatenate([jnp.where(ridx == 0, prev_row, up[0:SUBLANE]), up[SUBLANE:]], axis=0)
    dn = pltpu.roll(u, rows - 1, axis=0)
    dn = jnp.concatenate([dn[:rows - SUBLANE], jnp.where(ridx == SUBLANE - 1, next_row, dn[rows - SUBLANE:])],
                         axis=0)
    return up * w_ref[0:1, :] + u * w_ref[1:2, :] + dn * w_ref[2:3, :]


SSD_TQ = 256
SSD_CPT = SSD_TQ // CHUNK
HALO = SUBLANE
N_SEQ = BATCH + DEC_BATCH


def _ssd_tables():
    blk, dirn, seq, first, last, hasp, hasn = [], [], [], [], [], [], []
    for d in range(2):
        for s in range(N_SEQ):
            if s < BATCH:
                base, nt = s * SEQ // SSD_TQ, SEQ // SSD_TQ
            else:
                base, nt = (N_CTX_TOK + (s - BATCH) * DEC_SEQ) // SSD_TQ, DEC_SEQ // SSD_TQ
            for k in range(nt):
                t = k if d == 0 else nt - 1 - k
                blk.append(base + t)
                dirn.append(d)
                seq.append(s)
                first.append(int(k == 0))
                last.append(int(k == nt - 1))
                hasp.append(int(t > 0))
                hasn.append(int(t < nt - 1))
    return [np.asarray(a, np.int32) for a in (blk, dirn, seq, first, last, hasp, hasn)]


def _split3(a):
    a1 = a.astype(BF16)
    r1 = a - a1.astype(F32)
    a2 = r1.astype(BF16)
    a3 = (r1 - a2.astype(F32)).astype(BF16)
    return a1, a2, a3


def _dot3(lhs_bf16, a):
    a1, a2, a3 = _split3(a)
    return (jnp.dot(lhs_bf16, a1, preferred_element_type=F32)
            + jnp.dot(lhs_bf16, a2, preferred_element_type=F32)
            + jnp.dot(lhs_bf16, a3, preferred_element_type=F32))


def _softplus(x):
    return jnp.maximum(x, 0.0) + jnp.log1p(jnp.exp(-jnp.abs(x)))


def _ssd_kernel(blk_t, dir_t, seq_t, first_t, last_t, hasp_t, hasn_t,
                x_ref, xp_ref, xn_ref, bc_ref, bcp_ref, bcn_ref, dt_ref,
                cwx_ref, cbx_ref, cwb_ref, cbb_ref, alog_ref, dtb_ref, dsk_ref,
                h0f_ref, h0b_ref,
                y_ref, fin_ref,
                st_ref, xc_ref, bcc_ref):
    s = pl.program_id(0)
    d = dir_t[s]
    seq = seq_t[s]
    is_fwd = d == 0

    hp = hasp_t[s].astype(F32)
    hn = hasn_t[s].astype(F32)
    cx = _conv3_tile(x_ref[...], xp_ref[HALO - 1:HALO, :] * hp, xn_ref[0:1, :] * hn, cwx_ref, SSD_TQ)
    xc_ref[...] = _silu(cx + cbx_ref[...])
    cb_ = _conv3_tile(bc_ref[...], bcp_ref[HALO - 1:HALO, :] * hp, bcn_ref[0:1, :] * hn, cwb_ref, SSD_TQ)
    bcc_ref[...] = _silu(cb_ + cbb_ref[...])

    @pl.when(jnp.logical_and(first_t[s] == 1, seq < BATCH))
    def _():
        st_ref[...] = jnp.zeros_like(st_ref)

    @pl.when(jnp.logical_and(first_t[s] == 1, seq >= BATCH))
    def _():
        h0 = jnp.where(is_fwd, h0f_ref[...], h0b_ref[...])
        st_ref[...] = h0.T

    ii = lax.broadcasted_iota(jnp.int32, (CHUNK, CHUNK), 0)
    jj = lax.broadcasted_iota(jnp.int32, (CHUNK, CHUNK), 1)
    sgn = jnp.where(is_fwd, 1, -1)
    mask = (jj - ii) * sgn <= 0
    mask_b = jnp.where(mask, 1.0, 0.0).astype(BF16)
    lo = jj < SSM_HEAD_DIM
    lo_row = lo[0:1, :]

    a_row = -jnp.exp(alog_ref[...])
    dtb = dtb_ref[...]

    e_r = lax.broadcasted_iota(jnp.int32, (LANE, SSM_INNER), 0)
    e_c = lax.broadcasted_iota(jnp.int32, (LANE, SSM_INNER), 1)
    expand = jnp.where(jnp.right_shift(e_c, 6) == e_r, 1.0, 0.0).astype(BF16)

    def chunk_body(k):
        c = jnp.where(is_fwd, k, SSD_CPT - 1 - k)
        r0 = pl.multiple_of(c * CHUNK, CHUNK)
        dt = _softplus(dt_ref[pl.ds(r0, CHUNK), :] + dtb)
        a = dt * a_row
        cum = _dot3(mask_b, a)
        cum_t = cum.T
        dt_t = dt.T
        tot_row = jnp.where(is_fwd, cum[CHUNK - 1:CHUNK, :], cum[0:1, :])
        tot_col = jnp.where(is_fwd, cum_t[:, CHUNK - 1:CHUNK], cum_t[:, 0:1])
        w_rows = jnp.exp(tot_col - cum_t) * dt_t
        sdec = _dot3_rows(jnp.exp(tot_row), expand)
        ecum = jnp.exp(cum)

        bc = bcc_ref[pl.ds(r0, CHUNK), :]
        for g in range(SSM_GROUPS):
            b_g = bc[:, g * SSM_STATE:(g + 1) * SSM_STATE]
            c_g = bc[:, (SSM_GROUPS + g) * SSM_STATE:(SSM_GROUPS + g + 1) * SSM_STATE].astype(BF16)
            b_gt = b_g.T
            cb = jnp.dot(c_g, b_gt.astype(BF16), preferred_element_type=F32)
            pairs_per_group = SSM_HEADS // SSM_GROUPS // 2
            for kp in range(g * pairs_per_group, (g + 1) * pairs_per_group):
                h_a, h_b = 2 * kp, 2 * kp + 1
                ls = slice(kp * LANE, (kp + 1) * LANE)
                x_pair = xc_ref[pl.ds(r0, CHUNK), ls]
                rhs = jnp.concatenate([jnp.where(lo, x_pair, 0.0).astype(BF16),
                                       jnp.where(lo, 0.0, x_pair).astype(BF16)], axis=0)

                def head_lhs(h):
                    seg = cum[:, h:h + 1] - cum_t[h:h + 1, :]
                    dec = jnp.exp(jnp.where(mask, seg, NEG_BIG))
                    w_intra = dec * cb * dt_t[h:h + 1, :]
                    w_state = b_gt * w_rows[h:h + 1, :]
                    return w_intra.astype(BF16), w_state.astype(BF16)

                wi_a, ws_a = head_lhs(h_a)
                wi_b, ws_b = head_lhs(h_b)
                lhs = jnp.concatenate([jnp.concatenate([wi_a, wi_b], axis=1),
                                       jnp.concatenate([ws_a, ws_b], axis=1)], axis=0)
                both = jnp.dot(lhs, rhs, preferred_element_type=F32)
                y_diag = both[0:CHUNK, :]
                d_state = both[CHUNK:2 * CHUNK, :]

                h_pair = st_ref[:, ls]
                y_off = jnp.dot(c_g, h_pair.astype(BF16), preferred_element_type=F32)
                e_pair = jnp.where(lo, ecum[:, h_a:h_a + 1], ecum[:, h_b:h_b + 1])
                y_ref[pl.ds(r0, CHUNK), ls] = y_diag + y_off * e_pair + x_pair * dsk_ref[:, ls]
                st_ref[:, ls] = h_pair * sdec[:, ls] + d_state

    for k in range(SSD_CPT):
        chunk_body(k)

    @pl.when(jnp.logical_and(last_t[s] == 1, seq < BATCH))
    def _():
        fin_ref[...] = st_ref[...].T


def _dot3_rows(row, rhs_bf16):
    r8 = jnp.broadcast_to(row, (SUBLANE, row.shape[1]))
    r1, r2, r3 = _split3(r8)
    out = (jnp.dot(r1, rhs_bf16, preferred_element_type=F32)
           + jnp.dot(r2, rhs_bf16, preferred_element_type=F32)
           + jnp.dot(r3, rhs_bf16, preferred_element_type=F32))
    return out[0:1, :]


def _ssd(p, layer, conv_wx, conv_bx, conv_wbc, conv_bbc, alog, dtb, dskip, h0f, h0b):
    tables = [jnp.asarray(t) for t in _ssd_tables()]
    n_steps = int(tables[0].shape[0])
    halo_per_tile = SSD_TQ // HALO
    n_halo_blocks = N_TOK // HALO

    def tile_map(col):
        return lambda s, blk, *_: (blk[s], col)

    def prev_map(col):
        return lambda s, blk, *_: (jnp.maximum(blk[s] * halo_per_tile - 1, 0), col)

    def next_map(col):
        return lambda s, blk, *_: (jnp.minimum((blk[s] + 1) * halo_per_tile, n_halo_blocks - 1), col)

    def dir_map(s, blk, dirn, *_):
        return (dirn[s], 0, 0)

    def h0_map(s, blk, dirn, seq, *_):
        return (jnp.maximum(seq[s] - BATCH, 0), layer, 0, 0)

    def const2(s, *_):
        return (0, 0)

    cx, cbc, cdt = OFF_SX // SSM_INNER, OFF_BC // 512, OFF_DT // LANE
    grid_spec = pltpu.PrefetchScalarGridSpec(
        num_scalar_prefetch=7,
        grid=(n_steps,),
        in_specs=[
            pl.BlockSpec((SSD_TQ, SSM_INNER), tile_map(cx)),
            pl.BlockSpec((HALO, SSM_INNER), prev_map(cx)),
            pl.BlockSpec((HALO, SSM_INNER), next_map(cx)),
            pl.BlockSpec((SSD_TQ, 512), tile_map(cbc)),
            pl.BlockSpec((HALO, 512), prev_map(cbc)),
            pl.BlockSpec((HALO, 512), next_map(cbc)),
            pl.BlockSpec((SSD_TQ, LANE), tile_map(cdt)),
            pl.BlockSpec((3, SSM_INNER), const2),
            pl.BlockSpec((1, SSM_INNER), const2),
            pl.BlockSpec((3, 512), const2),
            pl.BlockSpec((1, 512), const2),
            pl.BlockSpec((None, 1, LANE), dir_map),
            pl.BlockSpec((None, 1, LANE), dir_map),
            pl.BlockSpec((None, 1, SSM_INNER), dir_map),
            pl.BlockSpec((None, None, SSM_INNER, SSM_STATE), h0_map),
            pl.BlockSpec((None, None, SSM_INNER, SSM_STATE), h0_map),
        ],
        out_specs=[
            pl.BlockSpec((None, SSD_TQ, SSM_INNER), lambda s, blk, dirn, *_: (dirn[s], blk[s], 0)),
            pl.BlockSpec((None, None, SSM_INNER, SSM_STATE),
                         lambda s, blk, dirn, seq, *_: (dirn[s], jnp.minimum(seq[s], BATCH - 1), 0, 0)),
        ],
        scratch_shapes=[
            pltpu.VMEM((SSM_STATE, SSM_INNER), F32),
            pltpu.VMEM((SSD_TQ, SSM_INNER), F32),
            pltpu.VMEM((SSD_TQ, 512), F32),
        ],
    )
    return pl.pallas_call(
        _ssd_kernel,
        grid_spec=grid_spec,
        out_shape=[
            jax.ShapeDtypeStruct((2, N_TOK, SSM_INNER), F32),
            jax.ShapeDtypeStruct((2, BATCH, SSM_INNER, SSM_STATE), F32),
        ],
        compiler_params=_cparams(("arbitrary",)),
        name="ssd_scan",
    )(*tables, p, p, p, p, p, p, p, conv_wx, conv_bx, conv_wbc, conv_bbc, alog, dtb, dskip, h0f, h0b)


QK_PAD = 128
N_QROWS = MLA_HEADS * QK_PAD
N_VROWS = MLA_HEADS * V_DIM
NT_DIMS = (((1,), (1,)), ((), ()))
Q_PRESCALE = (1.0 / math.sqrt(QK_DIM)) * math.log2(math.e)


def _mla_prep_kernel(*refs, tm, normalize, with_q):
    if with_q:
        (cq_ref, ckv_ref, kr_ref, krs_ref, cos_ref, sin_ref, cost_ref, sint_ref,
         qnw_ref, kvnw_ref, wq_ref, wk_ref, wv_ref,
         qt_ref, k_ref, vt_ref, ckvn_ref) = refs
    else:
        (ckv_ref, kr_ref, krs_ref, cos_ref, sin_ref, kvnw_ref, wk_ref, wv_ref,
         k_ref, vt_ref) = refs

    ckv = ckv_ref[...]
    if normalize:
        ckv = _rms(ckv, kvnw_ref[...])
        ckvn_ref[...] = ckv
    ckv_b = ckv.astype(BF16)
    kn = jnp.dot(ckv_b, wk_ref[...], preferred_element_type=F32)
    kr = kr_ref[...] * cos_ref[...] + krs_ref[...] * sin_ref[...]
    for h in range(MLA_HEADS):
        hs = slice(h * QK_PAD, (h + 1) * QK_PAD)
        k_ref[:, hs] = (kn[:, hs] + kr).astype(BF16)
    vt_ref[...] = lax.dot_general(wv_ref[...], ckv_b, NT_DIMS, preferred_element_type=F32).astype(BF16)

    if with_q:
        cqn = _rms(cq_ref[...], qnw_ref[...]).astype(BF16)
        qq = lax.dot_general(wq_ref[...], cqn, NT_DIMS, preferred_element_type=F32)
        for h in range(MLA_HEADS):
            q_h = qq[h * QK_PAD:(h + 1) * QK_PAD, :]
            qs_h = qq[N_QROWS + h * QK_PAD:N_QROWS + (h + 1) * QK_PAD, :]
            q_rot = q_h * cost_ref[...] + qs_h * sint_ref[...]
            qt_ref[h * QK_PAD:(h + 1) * QK_PAD, :] = (q_rot * Q_PRESCALE).astype(BF16)


def _mla_prep_tokens(p, row0, n_rows, tm, tab_map, cos, sin, cos_t, sin_t, qnw, kvnw, wq_t, wk, wv_t):
    b0 = row0 // tm
    nt = n_rows // tm
    kernel = functools.partial(_mla_prep_kernel, tm=tm, normalize=True, with_q=True)

    def pcol(width, off):
        return pl.BlockSpec((tm, width), lambda i: (b0 + i, off // width))

    def full(a):
        return pl.BlockSpec(a.shape, lambda i: (0,) * a.ndim)

    return pl.pallas_call(
        kernel,
        grid=(nt,),
        in_specs=[
            pcol(Q_LORA, OFF_CQ), pcol(KV_LORA, OFF_CKV), pcol(LANE, OFF_KR), pcol(LANE, OFF_KRS),
            pl.BlockSpec((tm, LANE), lambda i: (tab_map(i), 0)),
            pl.BlockSpec((tm, LANE), lambda i: (tab_map(i), 0)),
            pl.BlockSpec((LANE, tm), lambda i: (0, tab_map(i))),
            pl.BlockSpec((LANE, tm), lambda i: (0, tab_map(i))),
            full(qnw), full(kvnw), full(wq_t), full(wk), full(wv_t),
        ],
        out_specs=[
            pl.BlockSpec((N_QROWS, tm), lambda i: (0, i)),
            pl.BlockSpec((None, tm, N_QROWS), lambda i: (i, 0, 0)),
            pl.BlockSpec((None, N_VROWS, tm), lambda i: (i, 0, 0)),
            pl.BlockSpec((tm, KV_LORA), lambda i: (i, 0)),
        ],
        out_shape=[
            jax.ShapeDtypeStruct((N_QROWS, n_rows), BF16),
            jax.ShapeDtypeStruct((nt, tm, N_QROWS), BF16),
            jax.ShapeDtypeStruct((nt, N_VROWS, tm), BF16),
            jax.ShapeDtypeStruct((n_rows, KV_LORA), F32),
        ],
        compiler_params=_cparams(("arbitrary",)),
        name="mla_prep",
    )(p, p, p, p, cos, sin, cos_t, sin_t, qnw, kvnw, wq_t, wk, wv_t)


def _mla_prep_cache(ckv, kr_pad, ones_tab, zeros_tab, kvnw, wk, wv_t, tm):
    n_rows = ckv.shape[0]
    nt = n_rows // tm
    kernel = functools.partial(_mla_prep_kernel, tm=tm, normalize=False, with_q=False)

    def full(a):
        return pl.BlockSpec(a.shape, lambda i: (0,) * a.ndim)

    return pl.pallas_call(
        kernel,
        grid=(nt,),
        in_specs=[
            pl.BlockSpec((tm, KV_LORA), lambda i: (i, 0)),
            pl.BlockSpec((tm, LANE), lambda i: (i, 0)),
            pl.BlockSpec((tm, LANE), lambda i: (i, 0)),
            pl.BlockSpec((tm, LANE), lambda i: (0, 0)),
            pl.BlockSpec((tm, LANE), lambda i: (0, 0)),
            full(kvnw), full(wk), full(wv_t),
        ],
        out_specs=[
            pl.BlockSpec((None, tm, N_QROWS), lambda i: (i, 0, 0)),
            pl.BlockSpec((None, N_VROWS, tm), lambda i: (i, 0, 0)),
        ],
        out_shape=[
            jax.ShapeDtypeStruct((nt, tm, N_QROWS), BF16),
            jax.ShapeDtypeStruct((nt, N_VROWS, tm), BF16),
        ],
        compiler_params=_cparams(("arbitrary",)),
        name="mla_prep_cache",
    )(ckv, kr_pad, kr_pad, ones_tab, zeros_tab, kvnw, wk, wv_t)


ATT_TQ = 256


def _attn_kernel(qt_ref, k_ref, vt_ref, o_ref, ot_ref, s_ref, *, n_kt, tq, tk):

    def scores_step(h, kt, m8):
        q_t = qt_ref[h * QK_PAD:(h + 1) * QK_PAD, :]
        s = jnp.dot(k_ref[kt, :, h * QK_PAD:(h + 1) * QK_PAD], q_t, preferred_element_type=F32)
        s_ref[h % 2, kt] = s
        return jnp.maximum(m8, jnp.max(s.reshape(tk // SUBLANE, SUBLANE, tq), axis=0))

    def probs_step(h, kt, m, l8, acc):
        pr = jnp.exp2(s_ref[h % 2, kt] - m)
        l8 = l8 + jnp.sum(pr.reshape(tk // SUBLANE, SUBLANE, tq), axis=0)
        acc = acc + jnp.dot(vt_ref[kt, h * V_DIM:(h + 1) * V_DIM, :], pr.astype(BF16),
                            preferred_element_type=F32)
        return l8, acc

    m8_init = jnp.full((SUBLANE, tq), NEG_BIG, F32)
    m8 = m8_init
    for kt in range(n_kt):
        m8 = scores_step(0, kt, m8)
    for h in range(MLA_HEADS):
        m = jnp.max(m8, axis=0, keepdims=True)
        l8 = jnp.zeros((SUBLANE, tq), F32)
        acc = jnp.zeros((V_DIM, tq), F32)
        m8 = m8_init
        for kt in range(n_kt):
            l8, acc = probs_step(h, kt, m, l8, acc)
            if h + 1 < MLA_HEADS:
                m8 = scores_step(h + 1, kt, m8)
        ot_ref[h * V_DIM:(h + 1) * V_DIM, :] = acc / jnp.sum(l8, axis=0, keepdims=True)
    o_ref[...] = ot_ref[...].T


def _attention(qt, k3, vt3, n_batch, lq, n_kt, tk):
    tq = min(ATT_TQ, lq)
    nq = lq // tq
    k4 = k3.reshape(n_batch, n_kt, tk, N_QROWS)
    v4 = vt3.reshape(n_batch, n_kt, N_VROWS, tk)
    kernel = functools.partial(_attn_kernel, n_kt=n_kt, tq=tq, tk=tk)
    return pl.pallas_call(
        kernel,
        grid=(n_batch, nq),
        in_specs=[
            pl.BlockSpec((N_QROWS, tq), lambda b, i: (0, b * nq + i)),
            pl.BlockSpec((None, n_kt, tk, N_QROWS), lambda b, i: (b, 0, 0, 0)),
            pl.BlockSpec((None, n_kt, N_VROWS, tk), lambda b, i: (b, 0, 0, 0)),
        ],
        out_specs=pl.BlockSpec((tq, N_VROWS), lambda b, i: (b * nq + i, 0)),
        out_shape=jax.ShapeDtypeStruct((n_batch * lq, N_VROWS), F32),
        scratch_shapes=[pltpu.VMEM((N_VROWS, tq), F32), pltpu.VMEM((2, n_kt, tk, tq), F32)],
        compiler_params=_cparams(("arbitrary", "arbitrary")),
        name="attention",
    )(qt, k4, v4)


MG_TM = 256


def _merge_kernel(ax_ref, axp_ref, axn_ref, ac_ref, acp_ref, acn_ref, ab_ref,
                  g_ref, z_ref, y_ref, zc_ref, x_ref, gm_ref,
                  cw_ref, wa_ref, nw_ref, wb_ref, wc_ref, wo_ref, o_ref):
    i = pl.program_id(0)
    n_ctx_tiles = N_CTX_TOK // MG_TM
    tiles_per_lat = DEC_SEQ // MG_TM
    t_in_seq = (i - n_ctx_tiles) % tiles_per_lat
    is_lat = i >= n_ctx_tiles
    hp = jnp.logical_and(is_lat, t_in_seq > 0).astype(F32)
    hn = jnp.logical_and(is_lat, t_in_seq < tiles_per_lat - 1).astype(F32)

    u = ac_ref[...] * ax_ref[...]
    u_prev = acp_ref[HALO - 1:HALO, :] * axp_ref[HALO - 1:HALO, :] * hp
    u_next = acn_ref[0:1, :] * axn_ref[0:1, :] * hn
    za = ab_ref[...] * _conv3_tile(u, u_prev, u_next, cw_ref, MG_TM)
    y_a = jnp.dot(za.astype(BF16), wa_ref[...], preferred_element_type=F32)

    yb = (y_ref[0] + y_ref[1]) * _silu(z_ref[...])
    zb = _rms(yb, nw_ref[...])
    y_b = jnp.dot(zb.astype(BF16), wb_ref[...], preferred_element_type=F32)

    y_c = jnp.dot(zc_ref[...].astype(BF16), wc_ref[...], preferred_element_type=F32)

    merged = (jax.nn.sigmoid(g_ref[:, 0:D_MODEL]) * y_a
              + jax.nn.sigmoid(g_ref[:, D_MODEL:2 * D_MODEL]) * y_b
              + jax.nn.sigmoid(g_ref[:, 2 * D_MODEL:3 * D_MODEL]) * y_c)
    o = jnp.dot(merged.astype(BF16), wo_ref[...], preferred_element_type=F32)
    o_ref[...] = x_ref[...] + gm_ref[...] * o


def _merge(p, y_ssd, zc, x, mod_l, conv_w, wa, nw, wb, wc, wo):
    halo_per_tile = MG_TM // HALO
    n_halo_blocks = N_TOK // HALO
    row = functools.partial(_mod_row, tile_rows=MG_TM)

    def pcol(width, off):
        return pl.BlockSpec((MG_TM, width), lambda i: (i, off // width))

    def pprev(off):
        return pl.BlockSpec((HALO, A_WIDTH), lambda i: (jnp.maximum(i * halo_per_tile - 1, 0), off // A_WIDTH))

    def pnext(off):
        return pl.BlockSpec((HALO, A_WIDTH),
                            lambda i: (jnp.minimum((i + 1) * halo_per_tile, n_halo_blocks - 1), off // A_WIDTH))

    def full(a):
        return pl.BlockSpec(a.shape, lambda i: (0,) * a.ndim)

    return pl.pallas_call(
        _merge_kernel,
        grid=(N_TOK // MG_TM,),
        in_specs=[
            pcol(A_WIDTH, OFF_AX), pprev(OFF_AX), pnext(OFF_AX),
            pcol(A_WIDTH, OFF_AC), pprev(OFF_AC), pnext(OFF_AC),
            pcol(A_WIDTH, OFF_AB),
            pcol(3 * D_MODEL, OFF_G),
            pcol(SSM_INNER, OFF_Z),
            pl.BlockSpec((2, MG_TM, SSM_INNER), lambda i: (0, i, 0)),
            pl.BlockSpec((MG_TM, N_VROWS), lambda i: (i, 0)),
            pl.BlockSpec((MG_TM, D_MODEL), lambda i: (i, 0)),
            pl.BlockSpec((None, None, 1, D_MODEL), lambda i: (row(i), 2, 0, 0)),
            full(conv_w), full(wa), full(nw), full(wb), full(wc), full(wo),
        ],
        out_specs=pl.BlockSpec((MG_TM, D_MODEL), lambda i: (i, 0)),
        out_shape=jax.ShapeDtypeStruct((N_TOK, D_MODEL), F32),
        compiler_params=_cparams(("arbitrary",)),
        name="merge",
    )(p, p, p, p, p, p, p, p, p, y_ssd, zc, x, mod_l, conv_w, wa, nw, wb, wc, wo)


FF_TM = 512
FF_CHUNK = 256


def _ffn_kernel(x_ref, nw_ref, sh_ref, sc_ref, gm_ref, w1_ref, w3_ref, w2_ref, o_ref, h_ref, g_ref):
    _modulated_norm_to(h_ref, x_ref, nw_ref, sc_ref, sh_ref, FF_TM)
    h = h_ref[...]
    for c in range(0, FF_DIM, FF_CHUNK):
        a = jnp.dot(h, w1_ref[:, c:c + FF_CHUNK], preferred_element_type=F32)
        b = jnp.dot(h, w3_ref[:, c:c + FF_CHUNK], preferred_element_type=F32)
        g_ref[:, c:c + FF_CHUNK] = (_silu(a) * b).astype(BF16)
    ff = jnp.dot(g_ref[...], w2_ref[...], preferred_element_type=F32)
    o_ref[...] = x_ref[...] + gm_ref[...] * ff


def _ffn(x, mod_l, norm_w, w1, w3, w2):
    row = functools.partial(_mod_row, tile_rows=FF_TM)
    return pl.pallas_call(
        _ffn_kernel,
        grid=(N_TOK // FF_TM,),
        in_specs=[
            pl.BlockSpec((FF_TM, D_MODEL), lambda i: (i, 0)),
            pl.BlockSpec((1, D_MODEL), lambda i: (0, 0)),
            pl.BlockSpec((None, None, 1, D_MODEL), lambda i: (row(i), 3, 0, 0)),
            pl.BlockSpec((None, None, 1, D_MODEL), lambda i: (row(i), 4, 0, 0)),
            pl.BlockSpec((None, None, 1, D_MODEL), lambda i: (row(i), 5, 0, 0)),
            pl.BlockSpec((D_MODEL, FF_DIM), lambda i: (0, 0)),
            pl.BlockSpec((D_MODEL, FF_DIM), lambda i: (0, 0)),
            pl.BlockSpec((FF_DIM, D_MODEL), lambda i: (0, 0)),
        ],
        out_specs=pl.BlockSpec((FF_TM, D_MODEL), lambda i: (i, 0)),
        out_shape=jax.ShapeDtypeStruct((N_TOK, D_MODEL), F32),
        scratch_shapes=[pltpu.VMEM((FF_TM, D_MODEL), BF16), pltpu.VMEM((FF_TM, FF_DIM), BF16)],
        compiler_params=_cparams(("arbitrary",)),
        name="ffn",
    )(x, norm_w, mod_l, mod_l, mod_l, w1, w3, w2)


FN_TM = 1024


def _final_norm_kernel(x_ref, w_ref, o_ref):
    o_ref[...] = _rms(x_ref[...], w_ref[...])


def _final_norm(x, w, row0, n_rows):
    b0 = row0 // FN_TM
    return pl.pallas_call(
        _final_norm_kernel,
        grid=(n_rows // FN_TM,),
        in_specs=[pl.BlockSpec((FN_TM, D_MODEL), lambda i: (b0 + i, 0)),
                  pl.BlockSpec((1, D_MODEL), lambda i: (0, 0))],
        out_specs=pl.BlockSpec((FN_TM, D_MODEL), lambda i: (i, 0)),
        out_shape=jax.ShapeDtypeStruct((n_rows, D_MODEL), F32),
        compiler_params=_cparams(("arbitrary",)),
        name="final_norm",
    )(x, w)


def _pad_in_weights(w_in):
    o = 0
    a_x = w_in[..., o:o + 512]; o += 512
    a_b = w_in[..., o:o + 512]; o += 512
    a_c = w_in[..., o:o + 512]; o += 512
    s_z = w_in[..., o:o + 1024]; o += 1024
    s_x = w_in[..., o:o + 1024]; o += 1024
    s_bc = w_in[..., o:o + 512]; o += 512
    s_dt = w_in[..., o:o + 16]; o += 16
    cq = w_in[..., o:o + 256]; o += 256
    ckv = w_in[..., o:o + 256]; o += 256
    kr = w_in[..., o:o + 32]; o += 32
    gates = w_in[..., o:o + 3072]; o += 3072

    def z(n):
        return jnp.zeros(w_in.shape[:-1] + (n,), w_in.dtype)

    kr_sw = jnp.concatenate([kr[..., 16:], kr[..., :16]], axis=-1)
    cols = [gates, s_z, s_x, a_x, a_b, a_c, s_bc, cq, ckv,
            s_dt, z(LANE - 16),
            z(64), kr, z(32),
            z(64), kr_sw, z(32),
            z(NP - OFF_KRS - LANE)]
    out = jnp.concatenate(cols, axis=-1).astype(BF16)
    assert out.shape[-1] == NP
    return out


def _q_weights_t(w_uq):
    w = w_uq.reshape(DEPTH, Q_LORA, MLA_HEADS, QK_DIM)
    nope, x1, x2 = w[..., :NOPE_DIM], w[..., NOPE_DIM:NOPE_DIM + 16], w[..., NOPE_DIM + 16:]
    z32 = jnp.zeros_like(w[..., :32])
    z64 = jnp.zeros_like(nope)
    q = jnp.concatenate([nope, x1, x2, z32], axis=-1).reshape(DEPTH, Q_LORA, N_QROWS)
    qs = jnp.concatenate([z64, x2, x1, z32], axis=-1).reshape(DEPTH, Q_LORA, N_QROWS)
    return jnp.swapaxes(jnp.concatenate([q, qs], axis=-1), 1, 2).astype(BF16)


def _kv_weights(w_ukv):
    w = w_ukv.reshape(DEPTH, KV_LORA, MLA_HEADS, NOPE_DIM + V_DIM)
    kn = jnp.concatenate([w[..., :NOPE_DIM], jnp.zeros_like(w[..., :QK_PAD - NOPE_DIM])], axis=-1)
    wk = kn.reshape(DEPTH, KV_LORA, N_QROWS).astype(BF16)
    wv_t = jnp.swapaxes(w[..., NOPE_DIM:].reshape(DEPTH, KV_LORA, N_VROWS), 1, 2).astype(BF16)
    return wk, wv_t


def _rope_tables(n_tokens, lead_rows):
    n_rows = n_tokens // GRID_W
    row = jnp.repeat(jnp.arange(n_rows, dtype=F32), GRID_W)
    col = jnp.tile(jnp.arange(GRID_W, dtype=F32), n_rows)
    pairs = ROPE_DIM // 4
    inv = ROPE_BASE ** (-jnp.arange(pairs, dtype=F32) / pairs)
    ang = jnp.concatenate([row[:, None] * inv, col[:, None] * inv], axis=-1)
    cos, sin = jnp.cos(ang), jnp.sin(ang)
    ones = jnp.ones((n_tokens, NOPE_DIM), F32)
    z32 = jnp.zeros((n_tokens, 32), F32)
    cos_l = jnp.concatenate([ones, cos, cos, z32], axis=-1)
    sin_l = jnp.concatenate([jnp.zeros_like(ones), -sin, sin, z32], axis=-1)
    ident_c = jnp.concatenate([jnp.ones((lead_rows, NOPE_DIM + ROPE_DIM), F32), jnp.zeros((lead_rows, 32), F32)], -1)
    ident_s = jnp.zeros((lead_rows, LANE), F32)
    return jnp.concatenate([ident_c, cos_l], axis=0), jnp.concatenate([ident_s, sin_l], axis=0)


PREP_TM_CTX = SEQ
PREP_TM_LAT = 512


def kernel(x_prompt, x_sample, c, cache_ckv, cache_krope, state_ssm_fwd, state_ssm_bwd, c_ctx, w_in, a_conv_w, w_a_out, ssm_conv_w, ssm_conv_b, ssm_a_log, ssm_dt_bias, ssm_d, ssm_norm_w, w_b_out, q_norm_w, w_uq, kv_norm_w, w_ukv, w_c_out, w_o, w_ada, b_ada, norm1_w, norm2_w, w_ff1, w_ff3, w_ff2, final_norm_w):
    w_in_p = _pad_in_weights(w_in)
    wq_t = _q_weights_t(w_uq)
    wk, wv_t = _kv_weights(w_ukv)
    wa, wb, wc, wo = (w.astype(BF16) for w in (w_a_out, w_b_out, w_c_out, w_o))
    w1, w3, w2 = (w.astype(BF16) for w in (w_ff1, w_ff3, w_ff2))

    cond = jnp.concatenate([c_ctx[None, :], c, jnp.zeros((N_MOD_ROWS - 1 - DEC_BATCH, D_MODEL), F32)], axis=0)
    mod = _modulation(cond, w_ada, b_ada).reshape(DEPTH, N_MOD_ROWS, 6, 1, D_MODEL)

    conv_wx = ssm_conv_w[..., :SSM_INNER]
    conv_wbc = ssm_conv_w[..., SSM_INNER:]
    conv_bx = ssm_conv_b[:, None, :SSM_INNER]
    conv_bbc = ssm_conv_b[:, None, SSM_INNER:]
    pad_h = ((0, 0), (0, 0), (0, 0), (0, LANE - SSM_HEADS))
    alog = jnp.pad(ssm_a_log[:, :, None, :], pad_h)
    dtb = jnp.pad(ssm_dt_bias[:, :, None, :], pad_h)
    dskip = jnp.repeat(ssm_d, SSM_HEAD_DIM, axis=-1)[:, :, None, :]
    h0f = state_ssm_fwd.reshape(DEC_BATCH, DEPTH, SSM_INNER, SSM_STATE)
    h0b = state_ssm_bwd.reshape(DEC_BATCH, DEPTH, SSM_INNER, SSM_STATE)

    cos_c, sin_c = _rope_tables(DEC_SEQ, PREP_TM_CTX)
    cos_l, sin_l = cos_c[PREP_TM_CTX:], sin_c[PREP_TM_CTX:]
    ident_cos, ident_sin = cos_c[:PREP_TM_CTX], sin_c[:PREP_TM_CTX]
    ones_tab = jnp.concatenate([ident_cos, ident_cos], axis=0)
    zeros_tab = jnp.zeros_like(ones_tab)
    cache_kr_pad = jnp.pad(cache_krope, ((0, 0), (0, 0), (0, 0), (64, 32)))

    x = jnp.concatenate([x_prompt.reshape(N_CTX_TOK, D_MODEL), x_sample.reshape(N_LAT_TOK, D_MODEL)], axis=0)

    lat_tiles_per_seq = DEC_SEQ // PREP_TM_LAT
    new_ckv, new_kr, new_f, new_b = [], [], [], []
    for l in range(DEPTH):
        p = _in_projection(x, mod[l], norm1_w[l][None, :], w_in_p[l])

        y_ssd, fin = _ssd(p, l, conv_wx[l], conv_bx[l], conv_wbc[l], conv_bbc[l],
                          alog[l], dtb[l], dskip[l], h0f, h0b)
        new_f.append(fin[0])
        new_b.append(fin[1])

        qnw, kvnw = q_norm_w[l][None, :], kv_norm_w[l][None, :]
        qt_c, k_c, vt_c, ckv_c = _mla_prep_tokens(
            p, 0, N_CTX_TOK, PREP_TM_CTX, lambda i: 0, ident_cos, ident_sin, ident_cos.T, ident_sin.T,
            qnw, kvnw, wq_t[l], wk[l], wv_t[l])
        qt_l, k_l, vt_l, _ = _mla_prep_tokens(
            p, N_CTX_TOK, N_LAT_TOK, PREP_TM_LAT, lambda i: i % lat_tiles_per_seq, cos_l, sin_l, cos_l.T, sin_l.T,
            qnw, kvnw, wq_t[l], wk[l], wv_t[l])
        k_p, vt_p = _mla_prep_cache(cache_ckv[:, l].reshape(DEC_BATCH * PAST_LEN, KV_LORA),
                                    cache_kr_pad[:, l].reshape(DEC_BATCH * PAST_LEN, LANE),
                                    ones_tab, zeros_tab, kvnw, wk[l], wv_t[l], PAST_LEN)
        new_ckv.append(ckv_c.reshape(BATCH, SEQ, KV_LORA))
        new_kr.append(p[:N_CTX_TOK, OFF_KR + 64:OFF_KR + 64 + ROPE_DIM].reshape(BATCH, SEQ, ROPE_DIM))

        zc_c = _attention(qt_c, k_c, vt_c, BATCH, SEQ, 1, SEQ)
        n_kt = 1 + lat_tiles_per_seq
        k_all = jnp.concatenate([k_p[:, None], k_l.reshape(DEC_BATCH, lat_tiles_per_seq, PREP_TM_LAT, N_QROWS)], axis=1)
        vt_all = jnp.concatenate([vt_p[:, None], vt_l.reshape(DEC_BATCH, lat_tiles_per_seq, N_VROWS, PREP_TM_LAT)], axis=1)
        zc_l = _attention(qt_l, k_all.reshape(DEC_BATCH * n_kt, PREP_TM_LAT, N_QROWS),
                          vt_all.reshape(DEC_BATCH * n_kt, N_VROWS, PREP_TM_LAT), DEC_BATCH, DEC_SEQ, n_kt, PREP_TM_LAT)
        zc = jnp.concatenate([zc_c, zc_l], axis=0)

        x = _merge(p, y_ssd, zc, x, mod[l], a_conv_w[l], wa[l], ssm_norm_w[l][None, :], wb[l], wc[l], wo[l])
        x = _ffn(x, mod[l], norm2_w[l][None, :], w1[l], w3[l], w2[l])

    fw = final_norm_w[None, :]
    y_prompt = _final_norm(x, fw, 0, N_CTX_TOK).reshape(BATCH, SEQ, D_MODEL)
    y_sample = _final_norm(x, fw, N_CTX_TOK, N_LAT_TOK).reshape(DEC_BATCH, DEC_SEQ, D_MODEL)
    hshape = (BATCH, DEPTH, SSM_HEADS, SSM_HEAD_DIM, SSM_STATE)
    return (y_prompt, y_sample,
            jnp.stack(new_ckv, axis=1), jnp.stack(new_kr, axis=1),
            jnp.stack(new_f, axis=1).reshape(hshape), jnp.stack(new_b, axis=1).reshape(hshape))
```

```python
import functools
import math

import jax
import jax.numpy as jnp
import numpy as np
from jax import lax
from jax.experimental import pallas as pl
from jax.experimental.pallas import tpu as pltpu

F32 = jnp.float32
BF16 = jnp.bfloat16

D_MODEL = 1024
BATCH = 16
SEQ = 256
DEPTH = 4
DEC_BATCH = 4
DEC_SEQ = 4096
PAST_LEN = 512
GRID_W = 64
EPS = 1e-6
A_WIDTH = 512
SSM_INNER = 1024
SSM_HEAD_DIM = 64
SSM_HEADS = 16
SSM_GROUPS = 2
SSM_STATE = 128
CHUNK = 128
MLA_HEADS = 8
Q_LORA = 256
KV_LORA = 256
NOPE_DIM = 64
ROPE_DIM = 32
V_DIM = 64
QK_DIM = NOPE_DIM + ROPE_DIM
ROPE_BASE = 10000.0
FF_DIM = 2816

N_CTX_TOK = BATCH * SEQ
N_LAT_TOK = DEC_BATCH * DEC_SEQ
N_TOK = N_CTX_TOK + N_LAT_TOK
N_MOD_ROWS = 8

LANE = 128
SUBLANE = 8
VMEM_LIMIT = 56 * 1024 * 1024

OFF_G = 0
OFF_Z = 3072
OFF_SX = 4096
OFF_AX = 5120
OFF_AB = 5632
OFF_AC = 6144
OFF_BC = 6656
OFF_CQ = 7168
OFF_CKV = 7424
OFF_DT = 7680
OFF_KR = 7808
OFF_KRS = 7936
NP = 8192

NEG_BIG = -1e30


def _cparams(sem):
    return pltpu.CompilerParams(dimension_semantics=sem, vmem_limit_bytes=VMEM_LIMIT)


def _rms(x, w):
    ms = jnp.mean(x * x, axis=-1, keepdims=True)
    return x * lax.rsqrt(ms + EPS) * w


def _silu(x):
    return x * jax.nn.sigmoid(x)


def _mod_row(tile, tile_rows):
    n_ctx_tiles = N_CTX_TOK // tile_rows
    tiles_per_lat = DEC_SEQ // tile_rows
    return jnp.where(tile < n_ctx_tiles, 0, 1 + (tile - n_ctx_tiles) // tiles_per_lat)


MOD_TN = 1536


def _mod_kernel(c_ref, w_ref, b_ref, o_ref):
    c = c_ref[...]
    s = _silu(c).astype(BF16)
    o_ref[...] = jnp.dot(s, w_ref[...].astype(BF16), preferred_element_type=F32) + b_ref[...]


def _modulation(cond, w_ada, b_ada):
    n_col = 6 * D_MODEL
    return pl.pallas_call(
        _mod_kernel,
        grid=(DEPTH, n_col // MOD_TN),
        in_specs=[
            pl.BlockSpec((N_MOD_ROWS, D_MODEL), lambda l, j: (0, 0)),
            pl.BlockSpec((None, D_MODEL, MOD_TN), lambda l, j: (l, 0, j)),
            pl.BlockSpec((None, 1, MOD_TN), lambda l, j: (l, 0, j)),
        ],
        out_specs=pl.BlockSpec((None, N_MOD_ROWS, MOD_TN), lambda l, j: (l, 0, j)),
        out_shape=jax.ShapeDtypeStruct((DEPTH, N_MOD_ROWS, n_col), F32),
        compiler_params=_cparams(("arbitrary", "arbitrary")),
        name="modulation",
    )(cond, w_ada, b_ada.reshape(DEPTH, 1, n_col))


IN_TM = 1024
IN_TN = 1024
NORM_ROWS = 256


def _modulated_norm_to(h_ref, x_ref, nw_ref, sc_ref, sh_ref, rows):
    for r in range(0, rows, NORM_ROWS):
        x = x_ref[r:r + NORM_ROWS, :]
        h = _rms(x, nw_ref[...]) * (1.0 + sc_ref[...]) + sh_ref[...]
        h_ref[r:r + NORM_ROWS, :] = h.astype(BF16)


def _inproj_kernel(x_ref, nw_ref, sh_ref, sc_ref, w_ref, o_ref, h_ref):
    @pl.when(pl.program_id(1) == 0)
    def _():
        _modulated_norm_to(h_ref, x_ref, nw_ref, sc_ref, sh_ref, IN_TM)

    o_ref[...] = jnp.dot(h_ref[...], w_ref[...], preferred_element_type=F32)


def _in_projection(x, mod_l, norm_w, w_in_p):
    row = functools.partial(_mod_row, tile_rows=IN_TM)
    return pl.pallas_call(
        _inproj_kernel,
        grid=(N_TOK // IN_TM, NP // IN_TN),
        in_specs=[
            pl.BlockSpec((IN_TM, D_MODEL), lambda i, j: (i, 0)),
            pl.BlockSpec((1, D_MODEL), lambda i, j: (0, 0)),
            pl.BlockSpec((None, None, 1, D_MODEL), lambda i, j: (row(i), 0, 0, 0)),
            pl.BlockSpec((None, None, 1, D_MODEL), lambda i, j: (row(i), 1, 0, 0)),
            pl.BlockSpec((D_MODEL, IN_TN), lambda i, j: (0, j)),
        ],
        out_specs=pl.BlockSpec((IN_TM, IN_TN), lambda i, j: (i, j)),
        out_shape=jax.ShapeDtypeStruct((N_TOK, NP), F32),
        scratch_shapes=[pltpu.VMEM((IN_TM, D_MODEL), BF16)],
        compiler_params=_cparams(("arbitrary", "arbitrary")),
        name="in_projection",
    )(x, norm_w, mod_l, mod_l, w_in_p)


def _conv3_tile(u, prev_row, next_row, w_ref, rows):
    ridx = lax.broadcasted_iota(jnp.int32, (SUBLANE, 1), 0)
    up = pltpu.roll(u, 1, axis=0)
    up = jnp.concatenate([jnp.where(ridx == 0, prev_row, up[0:SUBLANE]), up[SUBLANE:]], axis=0)
    dn = pltpu.roll(u, rows - 1, axis=0)
    dn = jnp.concatenate([dn[:rows - SUBLANE], jnp.where(ridx == SUBLANE - 1, next_row, dn[rows - SUBLANE:])],
                         axis=0)
    return up * w_ref[0:1, :] + u * w_ref[1:2, :] + dn * w_ref[2:3, :]


SSD_TQ = 256
SSD_CPT = SSD_TQ // CHUNK
HALO = SUBLANE
N_SEQ = BATCH + DEC_BATCH
BC_WIDTH = 2 * SSM_GROUPS * SSM_STATE


def _tile_neighbours(i, tile_rows):
    n_ctx_tiles = N_CTX_TOK // tile_rows
    tiles_per_lat = DEC_SEQ // tile_rows
    t_in_seq = (i - n_ctx_tiles) % tiles_per_lat
    is_lat = i >= n_ctx_tiles
    has_prev = jnp.logical_and(is_lat, t_in_seq > 0).astype(F32)
    has_next = jnp.logical_and(is_lat, t_in_seq < tiles_per_lat - 1).astype(F32)
    return has_prev, has_next


def _ssd_conv_kernel(x_ref, xp_ref, xn_ref, bc_ref, bcp_ref, bcn_ref,
                     cwx_ref, cbx_ref, cwb_ref, cbb_ref, xc_ref, bcc_ref):
    hp, hn = _tile_neighbours(pl.program_id(0), SSD_TQ)
    cx = _conv3_tile(x_ref[...], xp_ref[HALO - 1:HALO, :] * hp, xn_ref[0:1, :] * hn, cwx_ref, SSD_TQ)
    xc_ref[...] = _silu(cx + cbx_ref[...])
    cb_ = _conv3_tile(bc_ref[...], bcp_ref[HALO - 1:HALO, :] * hp, bcn_ref[0:1, :] * hn, cwb_ref, SSD_TQ)
    bcc_ref[...] = _silu(cb_ + cbb_ref[...])


def _ssd_conv(p, conv_wx, conv_bx, conv_wbc, conv_bbc):
    halo_per_tile = SSD_TQ // HALO
    n_halo_blocks = N_TOK // HALO
    cx, cbc = OFF_SX // SSM_INNER, OFF_BC // BC_WIDTH

    def prev_map(col):
        return lambda i: (jnp.maximum(i * halo_per_tile - 1, 0), col)

    def next_map(col):
        return lambda i: (jnp.minimum((i + 1) * halo_per_tile, n_halo_blocks - 1), col)

    def const2(i):
        return (0, 0)

    return pl.pallas_call(
        _ssd_conv_kernel,
        grid=(N_TOK // SSD_TQ,),
        in_specs=[
            pl.BlockSpec((SSD_TQ, SSM_INNER), lambda i: (i, cx)),
            pl.BlockSpec((HALO, SSM_INNER), prev_map(cx)),
            pl.BlockSpec((HALO, SSM_INNER), next_map(cx)),
            pl.BlockSpec((SSD_TQ, BC_WIDTH), lambda i: (i, cbc)),
            pl.BlockSpec((HALO, BC_WIDTH), prev_map(cbc)),
            pl.BlockSpec((HALO, BC_WIDTH), next_map(cbc)),
            pl.BlockSpec((3, SSM_INNER), const2),
            pl.BlockSpec((1, SSM_INNER), const2),
            pl.BlockSpec((3, BC_WIDTH), const2),
            pl.BlockSpec((1, BC_WIDTH), const2),
        ],
        out_specs=[
            pl.BlockSpec((SSD_TQ, SSM_INNER), lambda i: (i, 0)),
            pl.BlockSpec((SSD_TQ, BC_WIDTH), lambda i: (i, 0)),
        ],
        out_shape=[
            jax.ShapeDtypeStruct((N_TOK, SSM_INNER), F32),
            jax.ShapeDtypeStruct((N_TOK, BC_WIDTH), F32),
        ],
        compiler_params=_cparams(("arbitrary",)),
        name="ssd_conv",
    )(p, p, p, p, p, p, conv_wx, conv_bx, conv_wbc, conv_bbc)


def _ssd_tables():
    blk_f, blk_b, seq, first, last = [], [], [], [], []
    for s in range(N_SEQ):
        if s < BATCH:
            base, nt = s * SEQ // SSD_TQ, SEQ // SSD_TQ
        else:
            base, nt = (N_CTX_TOK + (s - BATCH) * DEC_SEQ) // SSD_TQ, DEC_SEQ // SSD_TQ
        for k in range(nt):
            blk_f.append(base + k)
            blk_b.append(base + nt - 1 - k)
            seq.append(s)
            first.append(int(k == 0))
            last.append(int(k == nt - 1))
    return [np.asarray(a, np.int32) for a in (blk_f, blk_b, seq, first, last)]


def _split3(a):
    a1 = a.astype(BF16)
    r1 = a - a1.astype(F32)
    a2 = r1.astype(BF16)
    a3 = (r1 - a2.astype(F32)).astype(BF16)
    return a1, a2, a3


def _dot3(lhs_bf16, a):
    a1, a2, a3 = _split3(a)
    return (jnp.dot(lhs_bf16, a1, preferred_element_type=F32)
            + jnp.dot(lhs_bf16, a2, preferred_element_type=F32)
            + jnp.dot(lhs_bf16, a3, preferred_element_type=F32))


def _softplus(x):
    return jnp.maximum(x, 0.0) + jnp.log1p(jnp.exp(-jnp.abs(x)))


def _ssd_kernel(blkf_t, blkb_t, seq_t, first_t, last_t,
                xf_ref, bcf_ref, dtf_ref, xb_ref, bcb_ref, dtb_ref,
                alog_ref, dtbias_ref, dsk_ref, h0f_ref, h0b_ref,
                yf_ref, yb_ref, finf_ref, finb_ref,
                stf_ref, stb_ref):
    s = pl.program_id(0)
    seq = seq_t[s]
    dirs = ((xf_ref, bcf_ref, dtf_ref, stf_ref, yf_ref), (xb_ref, bcb_ref, dtb_ref, stb_ref, yb_ref))

    @pl.when(jnp.logical_and(first_t[s] == 1, seq < BATCH))
    def _():
        stf_ref[...] = jnp.zeros_like(stf_ref)
        stb_ref[...] = jnp.zeros_like(stb_ref)

    @pl.when(jnp.logical_and(first_t[s] == 1, seq >= BATCH))
    def _():
        stf_ref[...] = h0f_ref[...].T
        stb_ref[...] = h0b_ref[...].T

    ii = lax.broadcasted_iota(jnp.int32, (CHUNK, CHUNK), 0)
    jj = lax.broadcasted_iota(jnp.int32, (CHUNK, CHUNK), 1)
    masks = (jj <= ii, jj >= ii)
    masks_b = tuple(jnp.where(m, 1.0, 0.0).astype(BF16) for m in masks)
    lo = jj < SSM_HEAD_DIM

    e_r = lax.broadcasted_iota(jnp.int32, (LANE, SSM_INNER), 0)
    e_c = lax.broadcasted_iota(jnp.int32, (LANE, SSM_INNER), 1)
    expand = jnp.where(jnp.right_shift(e_c, 6) == e_r, 1.0, 0.0).astype(BF16)

    def chunk_setup(d, c):
        _, _, dt_ref, _, _ = dirs[d]
        rows = slice(c * CHUNK, (c + 1) * CHUNK)
        a_row = -jnp.exp(alog_ref[d])
        dt = _softplus(dt_ref[rows, :] + dtbias_ref[d])
        cum = _dot3(masks_b[d], dt * a_row)
        cum_t = cum.T
        dt_t = dt.T
        end = CHUNK - 1 if d == 0 else 0
        tot_row = cum[end:end + 1, :]
        tot_col = cum_t[:, end:end + 1]
        return dict(
            rows=rows, cum=cum, cum_t=cum_t, dt_t=dt_t,
            w_rows=jnp.exp(tot_col - cum_t) * dt_t,
            sdec=_dot3_rows(jnp.exp(tot_row), expand))

    def group_setup(d, cs, g):
        _, bc_ref, _, _, _ = dirs[d]
        b_g = bc_ref[cs["rows"], g * SSM_STATE:(g + 1) * SSM_STATE]
        c_g = bc_ref[cs["rows"], (SSM_GROUPS + g) * SSM_STATE:(SSM_GROUPS + g + 1) * SSM_STATE].astype(BF16)
        b_gt = b_g.T
        cb = jnp.dot(c_g, b_gt.astype(BF16), preferred_element_type=F32)
        return c_g, b_gt, cb

    def pair_step(d, cs, gs, kp):
        x_ref, _, _, st_ref, y_ref = dirs[d]
        c_g, b_gt, cb = gs
        cum, cum_t, dt_t = cs["cum"], cs["cum_t"], cs["dt_t"]
        h_a, h_b = 2 * kp, 2 * kp + 1
        ls = slice(kp * LANE, (kp + 1) * LANE)
        x_pair = x_ref[cs["rows"], ls]
        rhs = jnp.concatenate([jnp.where(lo, x_pair, 0.0).astype(BF16),
                               jnp.where(lo, 0.0, x_pair).astype(BF16)], axis=0)

        def head_lhs(h):
            col = jnp.broadcast_to(cum[:, h:h + 1], (CHUNK, CHUNK))
            seg = col - cum_t[h:h + 1, :]
            dec = jnp.exp(jnp.where(masks[d], seg, NEG_BIG))
            w_intra = dec * cb * dt_t[h:h + 1, :]
            w_state = b_gt * cs["w_rows"][h:h + 1, :]
            return w_intra.astype(BF16), w_state.astype(BF16), col

        wi_a, ws_a, col_a = head_lhs(h_a)
        wi_b, ws_b, col_b = head_lhs(h_b)
        lhs = jnp.concatenate([jnp.concatenate([wi_a, wi_b], axis=1),
                               jnp.concatenate([ws_a, ws_b], axis=1)], axis=0)
        both = jnp.dot(lhs, rhs, preferred_element_type=F32)
        y_diag = both[0:CHUNK, :]
        d_state = both[CHUNK:2 * CHUNK, :]

        h_pair = st_ref[:, ls]
        y_off = jnp.dot(c_g, h_pair.astype(BF16), preferred_element_type=F32)
        e_pair = jnp.exp(jnp.where(lo, col_a, col_b))
        y_ref[cs["rows"], ls] = y_diag + y_off * e_pair + x_pair * dsk_ref[d, :, ls]
        st_ref[:, ls] = h_pair * cs["sdec"][:, ls] + d_state

    pairs_per_group = SSM_HEADS // SSM_GROUPS // 2
    for k in range(SSD_CPT):
        cs = (chunk_setup(0, k), chunk_setup(1, SSD_CPT - 1 - k))
        for g in range(SSM_GROUPS):
            gs = (group_setup(0, cs[0], g), group_setup(1, cs[1], g))
            for kp in range(g * pairs_per_group, (g + 1) * pairs_per_group):
                pair_step(0, cs[0], gs[0], kp)
                pair_step(1, cs[1], gs[1], kp)

    @pl.when(jnp.logical_and(last_t[s] == 1, seq < BATCH))
    def _():
        finf_ref[...] = stf_ref[...].T
        finb_ref[...] = stb_ref[...].T


def _dot3_rows(row, rhs_bf16):
    r8 = jnp.broadcast_to(row, (SUBLANE, row.shape[1]))
    r1, r2, r3 = _split3(r8)
    out = (jnp.dot(r1, rhs_bf16, preferred_element_type=F32)
           + jnp.dot(r2, rhs_bf16, preferred_element_type=F32)
           + jnp.dot(r3, rhs_bf16, preferred_element_type=F32))
    return out[0:1, :]


def _ssd(p, xc, bcc, layer, alog, dtb, dskip, h0f, h0b):
    tables = [jnp.asarray(t) for t in _ssd_tables()]
    n_steps = int(tables[0].shape[0])
    cdt = OFF_DT // LANE

    def fwd_tile(col):
        return lambda s, blkf, *_: (blkf[s], col)

    def bwd_tile(col):
        return lambda s, blkf, blkb, *_: (blkb[s], col)

    def h0_map(s, blkf, blkb, seq, *_):
        return (jnp.maximum(seq[s] - BATCH, 0), layer, 0, 0)

    def fin_map(s, blkf, blkb, seq, *_):
        return (jnp.minimum(seq[s], BATCH - 1), 0, 0)

    def const3(s, *_):
        return (0, 0, 0)

    grid_spec = pltpu.PrefetchScalarGridSpec(
        num_scalar_prefetch=5,
        grid=(n_steps,),
        in_specs=[
            pl.BlockSpec((SSD_TQ, SSM_INNER), fwd_tile(0)),
            pl.BlockSpec((SSD_TQ, BC_WIDTH), fwd_tile(0)),
            pl.BlockSpec((SSD_TQ, LANE), fwd_tile(cdt)),
            pl.BlockSpec((SSD_TQ, SSM_INNER), bwd_tile(0)),
            pl.BlockSpec((SSD_TQ, BC_WIDTH), bwd_tile(0)),
            pl.BlockSpec((SSD_TQ, LANE), bwd_tile(cdt)),
            pl.BlockSpec((2, 1, LANE), const3),
            pl.BlockSpec((2, 1, LANE), const3),
            pl.BlockSpec((2, 1, SSM_INNER), const3),
            pl.BlockSpec((None, None, SSM_INNER, SSM_STATE), h0_map),
            pl.BlockSpec((None, None, SSM_INNER, SSM_STATE), h0_map),
        ],
        out_specs=[
            pl.BlockSpec((SSD_TQ, SSM_INNER), fwd_tile(0)),
            pl.BlockSpec((SSD_TQ, SSM_INNER), bwd_tile(0)),
            pl.BlockSpec((None, SSM_INNER, SSM_STATE), fin_map),
            pl.BlockSpec((None, SSM_INNER, SSM_STATE), fin_map),
        ],
        scratch_shapes=[
            pltpu.VMEM((SSM_STATE, SSM_INNER), F32),
            pltpu.VMEM((SSM_STATE, SSM_INNER), F32),
        ],
    )
    return pl.pallas_call(
        _ssd_kernel,
        grid_spec=grid_spec,
        out_shape=[
            jax.ShapeDtypeStruct((N_TOK, SSM_INNER), F32),
            jax.ShapeDtypeStruct((N_TOK, SSM_INNER), F32),
            jax.ShapeDtypeStruct((BATCH, SSM_INNER, SSM_STATE), F32),
            jax.ShapeDtypeStruct((BATCH, SSM_INNER, SSM_STATE), F32),
        ],
        compiler_params=_cparams(("arbitrary",)),
        name="ssd_scan",
    )(*tables, xc, bcc, p, xc, bcc, p, alog, dtb, dskip, h0f, h0b)


QK_PAD = 128
N_QROWS = MLA_HEADS * QK_PAD
N_VROWS = MLA_HEADS * V_DIM
NT_DIMS = (((1,), (1,)), ((), ()))
Q_PRESCALE = (1.0 / math.sqrt(QK_DIM)) * math.log2(math.e)


def _mla_prep_kernel(*refs, tm, normalize, with_q):
    if with_q:
        (cq_ref, ckv_ref, kr_ref, krs_ref, cos_ref, sin_ref, cost_ref, sint_ref,
         qnw_ref, kvnw_ref, wq_ref, wk_ref, wv_ref,
         qt_ref, k_ref, vt_ref, ckvn_ref) = refs
    else:
        (ckv_ref, kr_ref, krs_ref, cos_ref, sin_ref, kvnw_ref, wk_ref, wv_ref,
         k_ref, vt_ref) = refs

    ckv = ckv_ref[...]
    if normalize:
        ckv = _rms(ckv, kvnw_ref[...])
        ckvn_ref[...] = ckv
    ckv_b = ckv.astype(BF16)
    kn = jnp.dot(ckv_b, wk_ref[...], preferred_element_type=F32)
    kr = kr_ref[...] * cos_ref[...] + krs_ref[...] * sin_ref[...]
    for h in range(MLA_HEADS):
        hs = slice(h * QK_PAD, (h + 1) * QK_PAD)
        k_ref[:, hs] = (kn[:, hs] + kr).astype(BF16)
    vt_ref[...] = lax.dot_general(wv_ref[...], ckv_b, NT_DIMS, preferred_element_type=F32).astype(BF16)

    if with_q:
        cqn = _rms(cq_ref[...], qnw_ref[...]).astype(BF16)
        qq = lax.dot_general(wq_ref[...], cqn, NT_DIMS, preferred_element_type=F32)
        for h in range(MLA_HEADS):
            q_h = qq[h * QK_PAD:(h + 1) * QK_PAD, :]
            qs_h = qq[N_QROWS + h * QK_PAD:N_QROWS + (h + 1) * QK_PAD, :]
            q_rot = q_h * cost_ref[...] + qs_h * sint_ref[...]
            qt_ref[h * QK_PAD:(h + 1) * QK_PAD, :] = (q_rot * Q_PRESCALE).astype(BF16)


def _mla_prep_tokens(p, row0, n_rows, tm, tab_map, cos, sin, cos_t, sin_t, qnw, kvnw, wq_t, wk, wv_t):
    b0 = row0 // tm
    nt = n_rows // tm
    kernel = functools.partial(_mla_prep_kernel, tm=tm, normalize=True, with_q=True)

    def pcol(width, off):
        return pl.BlockSpec((tm, width), lambda i: (b0 + i, off // width))

    def full(a):
        return pl.BlockSpec(a.shape, lambda i: (0,) * a.ndim)

    return pl.pallas_call(
        kernel,
        grid=(nt,),
        in_specs=[
            pcol(Q_LORA, OFF_CQ), pcol(KV_LORA, OFF_CKV), pcol(LANE, OFF_KR), pcol(LANE, OFF_KRS),
            pl.BlockSpec((tm, LANE), lambda i: (tab_map(i), 0)),
            pl.BlockSpec((tm, LANE), lambda i: (tab_map(i), 0)),
            pl.BlockSpec((LANE, tm), lambda i: (0, tab_map(i))),
            pl.BlockSpec((LANE, tm), lambda i: (0, tab_map(i))),
            full(qnw), full(kvnw), full(wq_t), full(wk), full(wv_t),
        ],
        out_specs=[
            pl.BlockSpec((N_QROWS, tm), lambda i: (0, i)),
            pl.BlockSpec((None, tm, N_QROWS), lambda i: (i, 0, 0)),
            pl.BlockSpec((None, N_VROWS, tm), lambda i: (i, 0, 0)),
            pl.BlockSpec((tm, KV_LORA), lambda i: (i, 0)),
        ],
        out_shape=[
            jax.ShapeDtypeStruct((N_QROWS, n_rows), BF16),
            jax.ShapeDtypeStruct((nt, tm, N_QROWS), BF16),
            jax.ShapeDtypeStruct((nt, N_VROWS, tm), BF16),
            jax.ShapeDtypeStruct((n_rows, KV_LORA), F32),
        ],
        compiler_params=_cparams(("arbitrary",)),
        name="mla_prep",
    )(p, p, p, p, cos, sin, cos_t, sin_t, qnw, kvnw, wq_t, wk, wv_t)


def _mla_prep_cache(ckv, kr_pad, ones_tab, zeros_tab, kvnw, wk, wv_t, tm):
    n_rows = ckv.shape[0]
    nt = n_rows // tm
    kernel = functools.partial(_mla_prep_kernel, tm=tm, normalize=False, with_q=False)

    def full(a):
        return pl.BlockSpec(a.shape, lambda i: (0,) * a.ndim)

    return pl.pallas_call(
        kernel,
        grid=(nt,),
        in_specs=[
            pl.BlockSpec((tm, KV_LORA), lambda i: (i, 0)),
            pl.BlockSpec((tm, LANE), lambda i: (i, 0)),
            pl.BlockSpec((tm, LANE), lambda i: (i, 0)),
            pl.BlockSpec((tm, LANE), lambda i: (0, 0)),
            pl.BlockSpec((tm, LANE), lambda i: (0, 0)),
            full(kvnw), full(wk), full(wv_t),
        ],
        out_specs=[
            pl.BlockSpec((None, tm, N_QROWS), lambda i: (i, 0, 0)),
            pl.BlockSpec((None, N_VROWS, tm), lambda i: (i, 0, 0)),
        ],
        out_shape=[
            jax.ShapeDtypeStruct((nt, tm, N_QROWS), BF16),
            jax.ShapeDtypeStruct((nt, N_VROWS, tm), BF16),
        ],
        compiler_params=_cparams(("arbitrary",)),
        name="mla_prep_cache",
    )(ckv, kr_pad, kr_pad, ones_tab, zeros_tab, kvnw, wk, wv_t)


ATT_TQ = 256


def _attn_kernel(*refs, n_kt, tq, tk, with_cache):
    if with_cache:
        qt_ref, kp_ref, vtp_ref, k_ref, vt_ref, o_ref, ot_ref, s_ref = refs
    else:
        qt_ref, k_ref, vt_ref, o_ref, ot_ref, s_ref = refs

    def k_tile(kt, h):
        cols = slice(h * QK_PAD, (h + 1) * QK_PAD)
        if with_cache:
            return kp_ref[:, cols] if kt == 0 else k_ref[kt - 1, :, cols]
        return k_ref[kt, :, cols]

    def vt_tile(kt, h):
        rows = slice(h * V_DIM, (h + 1) * V_DIM)
        if with_cache:
            return vtp_ref[rows, :] if kt == 0 else vt_ref[kt - 1, rows, :]
        return vt_ref[kt, rows, :]

    def scores_step(h, kt, m8):
        q_t = qt_ref[h * QK_PAD:(h + 1) * QK_PAD, :]
        s = jnp.dot(k_tile(kt, h), q_t, preferred_element_type=F32)
        s_ref[h % 2, kt] = s
        return jnp.maximum(m8, jnp.max(s.reshape(tk // SUBLANE, SUBLANE, tq), axis=0))

    def probs_step(h, kt, m, l8, acc):
        pr = jnp.exp2(s_ref[h % 2, kt] - m)
        l8 = l8 + jnp.sum(pr.reshape(tk // SUBLANE, SUBLANE, tq), axis=0)
        acc = acc + jnp.dot(vt_tile(kt, h), pr.astype(BF16), preferred_element_type=F32)
        return l8, acc

    m8_init = jnp.full((SUBLANE, tq), NEG_BIG, F32)
    m8 = m8_init
    for kt in range(n_kt):
        m8 = scores_step(0, kt, m8)
    for h in range(MLA_HEADS):
        m = jnp.max(m8, axis=0, keepdims=True)
        l8 = jnp.zeros((SUBLANE, tq), F32)
        acc = jnp.zeros((V_DIM, tq), F32)
        m8 = m8_init
        for kt in range(n_kt):
            l8, acc = probs_step(h, kt, m, l8, acc)
            if h + 1 < MLA_HEADS:
                m8 = scores_step(h + 1, kt, m8)
        ot_ref[h * V_DIM:(h + 1) * V_DIM, :] = acc / jnp.sum(l8, axis=0, keepdims=True)
    o_ref[...] = ot_ref[...].T


def _attention(qt, k3, vt3, n_batch, lq, tk, cache=None):
    tq = min(ATT_TQ, lq)
    nq = lq // tq
    n_new = k3.shape[0] // n_batch
    k4 = k3.reshape(n_batch, n_new, tk, N_QROWS)
    v4 = vt3.reshape(n_batch, n_new, N_VROWS, tk)
    n_kt = n_new + (0 if cache is None else 1)
    kernel = functools.partial(_attn_kernel, n_kt=n_kt, tq=tq, tk=tk, with_cache=cache is not None)
    cache_specs = [] if cache is None else [
        pl.BlockSpec((None, tk, N_QROWS), lambda b, i: (b, 0, 0)),
        pl.BlockSpec((None, N_VROWS, tk), lambda b, i: (b, 0, 0)),
    ]
    return pl.pallas_call(
        kernel,
        grid=(n_batch, nq),
        in_specs=[pl.BlockSpec((N_QROWS, tq), lambda b, i: (0, b * nq + i))] + cache_specs + [
            pl.BlockSpec((None, n_new, tk, N_QROWS), lambda b, i: (b, 0, 0, 0)),
            pl.BlockSpec((None, n_new, N_VROWS, tk), lambda b, i: (b, 0, 0, 0)),
        ],
        out_specs=pl.BlockSpec((tq, N_VROWS), lambda b, i: (b * nq + i, 0)),
        out_shape=jax.ShapeDtypeStruct((n_batch * lq, N_VROWS), F32),
        scratch_shapes=[pltpu.VMEM((N_VROWS, tq), F32), pltpu.VMEM((2, n_kt, tk, tq), F32)],
        compiler_params=_cparams(("arbitrary", "arbitrary")),
        name="attention",
    )(qt, *(() if cache is None else cache), k4, v4)


MG_TM = 256


def _merge_kernel(ax_ref, axp_ref, axn_ref, ac_ref, acp_ref, acn_ref, ab_ref,
                  g_ref, z_ref, yf_ref, yb_ref, zc_ref, x_ref, gm_ref,
                  cw_ref, wa_ref, nw_ref, wb_ref, wc_ref, wo_ref, o_ref):
    hp, hn = _tile_neighbours(pl.program_id(0), MG_TM)

    u = ac_ref[...] * ax_ref[...]
    u_prev = acp_ref[HALO - 1:HALO, :] * axp_ref[HALO - 1:HALO, :] * hp
    u_next = acn_ref[0:1, :] * axn_ref[0:1, :] * hn
    za = ab_ref[...] * _conv3_tile(u, u_prev, u_next, cw_ref, MG_TM)
    y_a = jnp.dot(za.astype(BF16), wa_ref[...], preferred_element_type=F32)

    yb = (yf_ref[...] + yb_ref[...]) * _silu(z_ref[...])
    zb = _rms(yb, nw_ref[...])
    y_b = jnp.dot(zb.astype(BF16), wb_ref[...], preferred_element_type=F32)

    y_c = jnp.dot(zc_ref[...].astype(BF16), wc_ref[...], preferred_element_type=F32)

    merged = (jax.nn.sigmoid(g_ref[:, 0:D_MODEL]) * y_a
              + jax.nn.sigmoid(g_ref[:, D_MODEL:2 * D_MODEL]) * y_b
              + jax.nn.sigmoid(g_ref[:, 2 * D_MODEL:3 * D_MODEL]) * y_c)
    o = jnp.dot(merged.astype(BF16), wo_ref[...], preferred_element_type=F32)
    o_ref[...] = x_ref[...] + gm_ref[...] * o


def _merge(p, y_f, y_b, zc, x, mod_l, conv_w, wa, nw, wb, wc, wo):
    halo_per_tile = MG_TM // HALO
    n_halo_blocks = N_TOK // HALO
    row = functools.partial(_mod_row, tile_rows=MG_TM)

    def pcol(width, off):
        return pl.BlockSpec((MG_TM, width), lambda i: (i, off // width))

    def pprev(off):
        return pl.BlockSpec((HALO, A_WIDTH), lambda i: (jnp.maximum(i * halo_per_tile - 1, 0), off // A_WIDTH))

    def pnext(off):
        return pl.BlockSpec((HALO, A_WIDTH),
                            lambda i: (jnp.minimum((i + 1) * halo_per_tile, n_halo_blocks - 1), off // A_WIDTH))

    def full(a):
        return pl.BlockSpec(a.shape, lambda i: (0,) * a.ndim)

    return pl.pallas_call(
        _merge_kernel,
        grid=(N_TOK // MG_TM,),
        in_specs=[
            pcol(A_WIDTH, OFF_AX), pprev(OFF_AX), pnext(OFF_AX),
            pcol(A_WIDTH, OFF_AC), pprev(OFF_AC), pnext(OFF_AC),
            pcol(A_WIDTH, OFF_AB),
            pcol(3 * D_MODEL, OFF_G),
            pcol(SSM_INNER, OFF_Z),
            pl.BlockSpec((MG_TM, SSM_INNER), lambda i: (i, 0)),
            pl.BlockSpec((MG_TM, SSM_INNER), lambda i: (i, 0)),
            pl.BlockSpec((MG_TM, N_VROWS), lambda i: (i, 0)),
            pl.BlockSpec((MG_TM, D_MODEL), lambda i: (i, 0)),
            pl.BlockSpec((None, None, 1, D_MODEL), lambda i: (row(i), 2, 0, 0)),
            full(conv_w), full(wa), full(nw), full(wb), full(wc), full(wo),
        ],
        out_specs=pl.BlockSpec((MG_TM, D_MODEL), lambda i: (i, 0)),
        out_shape=jax.ShapeDtypeStruct((N_TOK, D_MODEL), F32),
        compiler_params=_cparams(("arbitrary",)),
        name="merge",
    )(p, p, p, p, p, p, p, p, p, y_f, y_b, zc, x, mod_l, conv_w, wa, nw, wb, wc, wo)


FF_TM = 512
FF_CHUNK = 256


def _ffn_kernel(x_ref, nw_ref, sh_ref, sc_ref, gm_ref, w1_ref, w3_ref, w2_ref, o_ref, h_ref, g_ref):
    _modulated_norm_to(h_ref, x_ref, nw_ref, sc_ref, sh_ref, FF_TM)
    h = h_ref[...]
    for c in range(0, FF_DIM, FF_CHUNK):
        a = jnp.dot(h, w1_ref[:, c:c + FF_CHUNK], preferred_element_type=F32)
        b = jnp.dot(h, w3_ref[:, c:c + FF_CHUNK], preferred_element_type=F32)
        g_ref[:, c:c + FF_CHUNK] = (_silu(a) * b).astype(BF16)
    ff = jnp.dot(g_ref[...], w2_ref[...], preferred_element_type=F32)
    o_ref[...] = x_ref[...] + gm_ref[...] * ff


def _ffn(x, mod_l, norm_w, w1, w3, w2):
    row = functools.partial(_mod_row, tile_rows=FF_TM)
    return pl.pallas_call(
        _ffn_kernel,
        grid=(N_TOK // FF_TM,),
        in_specs=[
            pl.BlockSpec((FF_TM, D_MODEL), lambda i: (i, 0)),
            pl.BlockSpec((1, D_MODEL), lambda i: (0, 0)),
            pl.BlockSpec((None, None, 1, D_MODEL), lambda i: (row(i), 3, 0, 0)),
            pl.BlockSpec((None, None, 1, D_MODEL), lambda i: (row(i), 4, 0, 0)),
            pl.BlockSpec((None, None, 1, D_MODEL), lambda i: (row(i), 5, 0, 0)),
            pl.BlockSpec((D_MODEL, FF_DIM), lambda i: (0, 0)),
            pl.BlockSpec((D_MODEL, FF_DIM), lambda i: (0, 0)),
            pl.BlockSpec((FF_DIM, D_MODEL), lambda i: (0, 0)),
        ],
        out_specs=pl.BlockSpec((FF_TM, D_MODEL), lambda i: (i, 0)),
        out_shape=jax.ShapeDtypeStruct((N_TOK, D_MODEL), F32),
        scratch_shapes=[pltpu.VMEM((FF_TM, D_MODEL), BF16), pltpu.VMEM((FF_TM, FF_DIM), BF16)],
        compiler_params=_cparams(("arbitrary",)),
        name="ffn",
    )(x, norm_w, mod_l, mod_l, mod_l, w1, w3, w2)


FN_TM = 1024


def _final_norm_kernel(x_ref, w_ref, o_ref):
    o_ref[...] = _rms(x_ref[...], w_ref[...])


def _final_norm(x, w, row0, n_rows):
    b0 = row0 // FN_TM
    return pl.pallas_call(
        _final_norm_kernel,
        grid=(n_rows // FN_TM,),
        in_specs=[pl.BlockSpec((FN_TM, D_MODEL), lambda i: (b0 + i, 0)),
                  pl.BlockSpec((1, D_MODEL), lambda i: (0, 0))],
        out_specs=pl.BlockSpec((FN_TM, D_MODEL), lambda i: (i, 0)),
        out_shape=jax.ShapeDtypeStruct((n_rows, D_MODEL), F32),
        compiler_params=_cparams(("arbitrary",)),
        name="final_norm",
    )(x, w)


def _pad_in_weights(w_in):
    o = 0
    a_x = w_in[..., o:o + 512]; o += 512
    a_b = w_in[..., o:o + 512]; o += 512
    a_c = w_in[..., o:o + 512]; o += 512
    s_z = w_in[..., o:o + 1024]; o += 1024
    s_x = w_in[..., o:o + 1024]; o += 1024
    s_bc = w_in[..., o:o + 512]; o += 512
    s_dt = w_in[..., o:o + 16]; o += 16
    cq = w_in[..., o:o + 256]; o += 256
    ckv = w_in[..., o:o + 256]; o += 256
    kr = w_in[..., o:o + 32]; o += 32
    gates = w_in[..., o:o + 3072]; o += 3072

    def z(n):
        return jnp.zeros(w_in.shape[:-1] + (n,), w_in.dtype)

    kr_sw = jnp.concatenate([kr[..., 16:], kr[..., :16]], axis=-1)
    cols = [gates, s_z, s_x, a_x, a_b, a_c, s_bc, cq, ckv,
            s_dt, z(LANE - 16),
            z(64), kr, z(32),
            z(64), kr_sw, z(32),
            z(NP - OFF_KRS - LANE)]
    out = jnp.concatenate(cols, axis=-1).astype(BF16)
    assert out.shape[-1] == NP
    return out


def _q_weights_t(w_uq):
    w = w_uq.reshape(DEPTH, Q_LORA, MLA_HEADS, QK_DIM)
    nope, x1, x2 = w[..., :NOPE_DIM], w[..., NOPE_DIM:NOPE_DIM + 16], w[..., NOPE_DIM + 16:]
    z32 = jnp.zeros_like(w[..., :32])
    z64 = jnp.zeros_like(nope)
    q = jnp.concatenate([nope, x1, x2, z32], axis=-1).reshape(DEPTH, Q_LORA, N_QROWS)
    qs = jnp.concatenate([z64, x2, x1, z32], axis=-1).reshape(DEPTH, Q_LORA, N_QROWS)
    return jnp.swapaxes(jnp.concatenate([q, qs], axis=-1), 1, 2).astype(BF16)


def _kv_weights(w_ukv):
    w = w_ukv.reshape(DEPTH, KV_LORA, MLA_HEADS, NOPE_DIM + V_DIM)
    kn = jnp.concatenate([w[..., :NOPE_DIM], jnp.zeros_like(w[..., :QK_PAD - NOPE_DIM])], axis=-1)
    wk = kn.reshape(DEPTH, KV_LORA, N_QROWS).astype(BF16)
    wv_t = jnp.swapaxes(w[..., NOPE_DIM:].reshape(DEPTH, KV_LORA, N_VROWS), 1, 2).astype(BF16)
    return wk, wv_t


def _rope_tables(n_tokens, lead_rows):
    n_rows = n_tokens // GRID_W
    row = jnp.repeat(jnp.arange(n_rows, dtype=F32), GRID_W)
    col = jnp.tile(jnp.arange(GRID_W, dtype=F32), n_rows)
    pairs = ROPE_DIM // 4
    inv = ROPE_BASE ** (-jnp.arange(pairs, dtype=F32) / pairs)
    ang = jnp.concatenate([row[:, None] * inv, col[:, None] * inv], axis=-1)
    cos, sin = jnp.cos(ang), jnp.sin(ang)
    ones = jnp.ones((n_tokens, NOPE_DIM), F32)
    z32 = jnp.zeros((n_tokens, 32), F32)
    cos_l = jnp.concatenate([ones, cos, cos, z32], axis=-1)
    sin_l = jnp.concatenate([jnp.zeros_like(ones), -sin, sin, z32], axis=-1)
    ident_c = jnp.concatenate([jnp.ones((lead_rows, NOPE_DIM + ROPE_DIM), F32), jnp.zeros((lead_rows, 32), F32)], -1)
    ident_s = jnp.zeros((lead_rows, LANE), F32)
    return jnp.concatenate([ident_c, cos_l], axis=0), jnp.concatenate([ident_s, sin_l], axis=0)


PREP_TM_CTX = SEQ
PREP_TM_LAT = 512


def kernel(x_prompt, x_sample, c, cache_ckv, cache_krope, state_ssm_fwd, state_ssm_bwd, c_ctx, w_in, a_conv_w, w_a_out, ssm_conv_w, ssm_conv_b, ssm_a_log, ssm_dt_bias, ssm_d, ssm_norm_w, w_b_out, q_norm_w, w_uq, kv_norm_w, w_ukv, w_c_out, w_o, w_ada, b_ada, norm1_w, norm2_w, w_ff1, w_ff3, w_ff2, final_norm_w):
    w_in_p = _pad_in_weights(w_in)
    wq_t = _q_weights_t(w_uq)
    wk, wv_t = _kv_weights(w_ukv)
    wa, wb, wc, wo = (w.astype(BF16) for w in (w_a_out, w_b_out, w_c_out, w_o))
    w1, w3, w2 = (w.astype(BF16) for w in (w_ff1, w_ff3, w_ff2))

    cond = jnp.concatenate([c_ctx[None, :], c, jnp.zeros((N_MOD_ROWS - 1 - DEC_BATCH, D_MODEL), F32)], axis=0)
    mod = _modulation(cond, w_ada, b_ada).reshape(DEPTH, N_MOD_ROWS, 6, 1, D_MODEL)

    conv_wx = ssm_conv_w[..., :SSM_INNER]
    conv_wbc = ssm_conv_w[..., SSM_INNER:]
    conv_bx = ssm_conv_b[:, None, :SSM_INNER]
    conv_bbc = ssm_conv_b[:, None, SSM_INNER:]
    pad_h = ((0, 0), (0, 0), (0, 0), (0, LANE - SSM_HEADS))
    alog = jnp.pad(ssm_a_log[:, :, None, :], pad_h)
    dtb = jnp.pad(ssm_dt_bias[:, :, None, :], pad_h)
    dskip = jnp.repeat(ssm_d, SSM_HEAD_DIM, axis=-1)[:, :, None, :]
    h0f = state_ssm_fwd.reshape(DEC_BATCH, DEPTH, SSM_INNER, SSM_STATE)
    h0b = state_ssm_bwd.reshape(DEC_BATCH, DEPTH, SSM_INNER, SSM_STATE)

    cos_c, sin_c = _rope_tables(DEC_SEQ, PREP_TM_CTX)
    cos_l, sin_l = cos_c[PREP_TM_CTX:], sin_c[PREP_TM_CTX:]
    ident_cos, ident_sin = cos_c[:PREP_TM_CTX], sin_c[:PREP_TM_CTX]
    ones_tab = jnp.concatenate([ident_cos, ident_cos], axis=0)
    zeros_tab = jnp.zeros_like(ones_tab)
    cache_kr_pad = jnp.pad(cache_krope, ((0, 0), (0, 0), (0, 0), (64, 32)))

    x = jnp.concatenate([x_prompt.reshape(N_CTX_TOK, D_MODEL), x_sample.reshape(N_LAT_TOK, D_MODEL)], axis=0)

    lat_tiles_per_seq = DEC_SEQ // PREP_TM_LAT
    new_ckv, new_kr, new_f, new_b = [], [], [], []
    for l in range(DEPTH):
        p = _in_projection(x, mod[l], norm1_w[l][None, :], w_in_p[l])

        xc, bcc = _ssd_conv(p, conv_wx[l], conv_bx[l], conv_wbc[l], conv_bbc[l])
        y_f, y_b, fin_f, fin_b = _ssd(p, xc, bcc, l, alog[l], dtb[l], dskip[l], h0f, h0b)
        new_f.append(fin_f)
        new_b.append(fin_b)

        qnw, kvnw = q_norm_w[l][None, :], kv_norm_w[l][None, :]
        qt_c, k_c, vt_c, ckv_c = _mla_prep_tokens(
            p, 0, N_CTX_TOK, PREP_TM_CTX, lambda i: 0, ident_cos, ident_sin, ident_cos.T, ident_sin.T,
            qnw, kvnw, wq_t[l], wk[l], wv_t[l])
        qt_l, k_l, vt_l, _ = _mla_prep_tokens(
            p, N_CTX_TOK, N_LAT_TOK, PREP_TM_LAT, lambda i: i % lat_tiles_per_seq, cos_l, sin_l, cos_l.T, sin_l.T,
            qnw, kvnw, wq_t[l], wk[l], wv_t[l])
        k_p, vt_p = _mla_prep_cache(cache_ckv[:, l].reshape(DEC_BATCH * PAST_LEN, KV_LORA),
                                    cache_kr_pad[:, l].reshape(DEC_BATCH * PAST_LEN, LANE),
                                    ones_tab, zeros_tab, kvnw, wk[l], wv_t[l], PAST_LEN)
        new_ckv.append(ckv_c.reshape(BATCH, SEQ, KV_LORA))
        new_kr.append(p[:N_CTX_TOK, OFF_KR + 64:OFF_KR + 64 + ROPE_DIM].reshape(BATCH, SEQ, ROPE_DIM))

        zc_c = _attention(qt_c, k_c, vt_c, BATCH, SEQ, SEQ)
        zc_l = _attention(qt_l, k_l, vt_l, DEC_BATCH, DEC_SEQ, PREP_TM_LAT, cache=(k_p, vt_p))
        zc = jnp.concatenate([zc_c, zc_l], axis=0)

        x = _merge(p, y_f, y_b, zc, x, mod[l], a_conv_w[l], wa[l], ssm_norm_w[l][None, :], wb[l], wc[l], wo[l])
        x = _ffn(x, mod[l], norm2_w[l][None, :], w1[l], w3[l], w2[l])

    fw = final_norm_w[None, :]
    y_prompt = _final_norm(x, fw, 0, N_CTX_TOK).reshape(BATCH, SEQ, D_MODEL)
    y_sample = _final_norm(x, fw, N_CTX_TOK, N_LAT_TOK).reshape(DEC_BATCH, DEC_SEQ, D_MODEL)
    hshape = (BATCH, DEPTH, SSM_HEADS, SSM_HEAD_DIM, SSM_STATE)
    return (y_prompt, y_sample,
            jnp.stack(new_ckv, axis=1), jnp.stack(new_kr, axis=1),
            jnp.stack(new_f, axis=1).reshape(hshape), jnp.stack(new_b, axis=1).reshape(hshape))
```

```python
import functools
import math

import jax
import jax.numpy as jnp
import numpy as np
from jax import lax
from jax.experimental import pallas as pl
from jax.experimental.pallas import tpu as pltpu

F32 = jnp.float32
BF16 = jnp.bfloat16

D_MODEL = 1024
BATCH = 16
SEQ = 256
DEPTH = 4
DEC_BATCH = 4
DEC_SEQ = 4096
PAST_LEN = 512
GRID_W = 64
EPS = 1e-6
A_WIDTH = 512
SSM_INNER = 1024
SSM_HEAD_DIM = 64
SSM_HEADS = 16
SSM_GROUPS = 2
SSM_STATE = 128
CHUNK = 128
MLA_HEADS = 8
Q_LORA = 256
KV_LORA = 256
NOPE_DIM = 64
ROPE_DIM = 32
V_DIM = 64
QK_DIM = NOPE_DIM + ROPE_DIM
ROPE_BASE = 10000.0
FF_DIM = 2816

N_CTX_TOK = BATCH * SEQ
N_LAT_TOK = DEC_BATCH * DEC_SEQ
N_TOK = N_CTX_TOK + N_LAT_TOK
N_MOD_ROWS = 8

LANE = 128
SUBLANE = 8
VMEM_LIMIT = 56 * 1024 * 1024

OFF_G = 0
OFF_Z = 3072
OFF_SX = 4096
OFF_AX = 5120
OFF_AB = 5632
OFF_AC = 6144
OFF_BC = 6656
NP_MAIN = 7168
OFF_CQ = 7168
OFF_CKV = 7424
OFF_DT = 7680
OFF_KR = 7808
OFF_KRS = 7936
NP = 8192
N_MAIN_TILES = 7

NEG_BIG = -1e30


def _cparams(sem):
    return pltpu.CompilerParams(dimension_semantics=sem, vmem_limit_bytes=VMEM_LIMIT)


def _rms(x, w):
    ms = jnp.mean(x * x, axis=-1, keepdims=True)
    return x * lax.rsqrt(ms + EPS) * w


def _silu(x):
    return x * jax.nn.sigmoid(x)


def _mod_row(tile, tile_rows):
    n_ctx_tiles = N_CTX_TOK // tile_rows
    tiles_per_lat = DEC_SEQ // tile_rows
    return jnp.where(tile < n_ctx_tiles, 0, 1 + (tile - n_ctx_tiles) // tiles_per_lat)


MOD_TN = 1536


def _mod_kernel(c_ref, w_ref, b_ref, o_ref):
    c = c_ref[...]
    s = _silu(c).astype(BF16)
    o_ref[...] = jnp.dot(s, w_ref[...].astype(BF16), preferred_element_type=F32) + b_ref[...]


def _modulation(cond, w_ada, b_ada):
    n_col = 6 * D_MODEL
    return pl.pallas_call(
        _mod_kernel,
        grid=(DEPTH, n_col // MOD_TN),
        in_specs=[
            pl.BlockSpec((N_MOD_ROWS, D_MODEL), lambda l, j: (0, 0)),
            pl.BlockSpec((None, D_MODEL, MOD_TN), lambda l, j: (l, 0, j)),
            pl.BlockSpec((None, 1, MOD_TN), lambda l, j: (l, 0, j)),
        ],
        out_specs=pl.BlockSpec((None, N_MOD_ROWS, MOD_TN), lambda l, j: (l, 0, j)),
        out_shape=jax.ShapeDtypeStruct((DEPTH, N_MOD_ROWS, n_col), F32),
        compiler_params=_cparams(("arbitrary", "arbitrary")),
        name="modulation",
    )(cond, w_ada, b_ada.reshape(DEPTH, 1, n_col))


IN_TM = 1024
IN_TN = 1024
NORM_ROWS = 256


def _modulated_norm_to(h_ref, x_ref, nw_ref, sc_ref, sh_ref, rows):
    for r in range(0, rows, NORM_ROWS):
        x = x_ref[r:r + NORM_ROWS, :]
        h = _rms(x, nw_ref[...]) * (1.0 + sc_ref[...]) + sh_ref[...]
        h_ref[r:r + NORM_ROWS, :] = h.astype(BF16)


def _inproj_kernel(x_ref, nw_ref, sh_ref, sc_ref, w_ref, om_ref, os_ref, h_ref):
    j = pl.program_id(1)

    @pl.when(j == 0)
    def _():
        _modulated_norm_to(h_ref, x_ref, nw_ref, sc_ref, sh_ref, IN_TM)

    @pl.when(j < N_MAIN_TILES)
    def _():
        om_ref[...] = jnp.dot(h_ref[...], w_ref[...], preferred_element_type=F32).astype(BF16)

    @pl.when(j == N_MAIN_TILES)
    def _():
        os_ref[...] = jnp.dot(h_ref[...], w_ref[...], preferred_element_type=F32)


def _in_projection(x, mod_l, norm_w, w_in_p):
    row = functools.partial(_mod_row, tile_rows=IN_TM)
    return pl.pallas_call(
        _inproj_kernel,
        grid=(N_TOK // IN_TM, NP // IN_TN),
        in_specs=[
            pl.BlockSpec((IN_TM, D_MODEL), lambda i, j: (i, 0)),
            pl.BlockSpec((1, D_MODEL), lambda i, j: (0, 0)),
            pl.BlockSpec((None, None, 1, D_MODEL), lambda i, j: (row(i), 0, 0, 0)),
            pl.BlockSpec((None, None, 1, D_MODEL), lambda i, j: (row(i), 1, 0, 0)),
            pl.BlockSpec((D_MODEL, IN_TN), lambda i, j: (0, j)),
        ],
        out_specs=[
            pl.BlockSpec((IN_TM, IN_TN), lambda i, j: (i, jnp.minimum(j, N_MAIN_TILES - 1))),
            pl.BlockSpec((IN_TM, IN_TN), lambda i, j: (i, 0)),
        ],
        out_shape=[
            jax.ShapeDtypeStruct((N_TOK, NP_MAIN), BF16),
            jax.ShapeDtypeStruct((N_TOK, IN_TN), F32),
        ],
        scratch_shapes=[pltpu.VMEM((IN_TM, D_MODEL), BF16)],
        compiler_params=_cparams(("arbitrary", "arbitrary")),
        name="in_projection",
    )(x, norm_w, mod_l, mod_l, w_in_p)


def _conv3_tile(u, prev_row, next_row, w_ref, rows):
    ridx = lax.broadcasted_iota(jnp.int32, (SUBLANE, 1), 0)
    up = pltpu.roll(u, 1, axis=0)
    up = jnp.concatenate([jnp.where(ridx == 0, prev_row, up[0:SUBLANE]), up[SUBLANE:]], axis=0)
    dn = pltpu.roll(u, rows - 1, axis=0)
    dn = jnp.concatenate([dn[:rows - SUBLANE], jnp.where(ridx == SUBLANE - 1, next_row, dn[rows - SUBLANE:])],
                         axis=0)
    return up * w_ref[0:1, :] + u * w_ref[1:2, :] + dn * w_ref[2:3, :]


SSD_TQ = 256
SSD_CPT = SSD_TQ // CHUNK
HALO = 2 * SUBLANE
N_SEQ = BATCH + DEC_BATCH
BC_WIDTH = 2 * SSM_GROUPS * SSM_STATE


def _tile_neighbours(i, tile_rows):
    n_ctx_tiles = N_CTX_TOK // tile_rows
    tiles_per_lat = DEC_SEQ // tile_rows
    t_in_seq = (i - n_ctx_tiles) % tiles_per_lat
    is_lat = i >= n_ctx_tiles
    has_prev = jnp.logical_and(is_lat, t_in_seq > 0).astype(F32)
    has_next = jnp.logical_and(is_lat, t_in_seq < tiles_per_lat - 1).astype(F32)
    return has_prev, has_next


def _ssd_conv_kernel(x_ref, xp_ref, xn_ref, bc_ref, bcp_ref, bcn_ref,
                     cwx_ref, cbx_ref, cwb_ref, cbb_ref, xc_ref, bcc_ref):
    hp, hn = _tile_neighbours(pl.program_id(0), SSD_TQ)

    def conv_silu(u_ref, up_ref, un_ref, w_ref, b_ref):
        prev_row = up_ref[HALO - 1:HALO, :].astype(F32) * hp
        next_row = un_ref[0:1, :].astype(F32) * hn
        return _silu(_conv3_tile(u_ref[...].astype(F32), prev_row, next_row, w_ref, SSD_TQ) + b_ref[...])

    xc_ref[...] = conv_silu(x_ref, xp_ref, xn_ref, cwx_ref, cbx_ref)
    bcc_ref[...] = conv_silu(bc_ref, bcp_ref, bcn_ref, cwb_ref, cbb_ref)


def _ssd_conv(p, conv_wx, conv_bx, conv_wbc, conv_bbc):
    halo_per_tile = SSD_TQ // HALO
    n_halo_blocks = N_TOK // HALO
    cx, cbc = OFF_SX // SSM_INNER, OFF_BC // BC_WIDTH

    def prev_map(col):
        return lambda i: (jnp.maximum(i * halo_per_tile - 1, 0), col)

    def next_map(col):
        return lambda i: (jnp.minimum((i + 1) * halo_per_tile, n_halo_blocks - 1), col)

    def const2(i):
        return (0, 0)

    return pl.pallas_call(
        _ssd_conv_kernel,
        grid=(N_TOK // SSD_TQ,),
        in_specs=[
            pl.BlockSpec((SSD_TQ, SSM_INNER), lambda i: (i, cx)),
            pl.BlockSpec((HALO, SSM_INNER), prev_map(cx)),
            pl.BlockSpec((HALO, SSM_INNER), next_map(cx)),
            pl.BlockSpec((SSD_TQ, BC_WIDTH), lambda i: (i, cbc)),
            pl.BlockSpec((HALO, BC_WIDTH), prev_map(cbc)),
            pl.BlockSpec((HALO, BC_WIDTH), next_map(cbc)),
            pl.BlockSpec((3, SSM_INNER), const2),
            pl.BlockSpec((1, SSM_INNER), const2),
            pl.BlockSpec((3, BC_WIDTH), const2),
            pl.BlockSpec((1, BC_WIDTH), const2),
        ],
        out_specs=[
            pl.BlockSpec((SSD_TQ, SSM_INNER), lambda i: (i, 0)),
            pl.BlockSpec((SSD_TQ, BC_WIDTH), lambda i: (i, 0)),
        ],
        out_shape=[
            jax.ShapeDtypeStruct((N_TOK, SSM_INNER), F32),
            jax.ShapeDtypeStruct((N_TOK, BC_WIDTH), F32),
        ],
        compiler_params=_cparams(("arbitrary",)),
        name="ssd_conv",
    )(p, p, p, p, p, p, conv_wx, conv_bx, conv_wbc, conv_bbc)


def _ssd_tables():
    blk_f, blk_b, seq, first, last = [], [], [], [], []
    for s in range(N_SEQ):
        if s < BATCH:
            base, nt = s * SEQ // SSD_TQ, SEQ // SSD_TQ
        else:
            base, nt = (N_CTX_TOK + (s - BATCH) * DEC_SEQ) // SSD_TQ, DEC_SEQ // SSD_TQ
        for k in range(nt):
            blk_f.append(base + k)
            blk_b.append(base + nt - 1 - k)
            seq.append(s)
            first.append(int(k == 0))
            last.append(int(k == nt - 1))
    return [np.asarray(a, np.int32) for a in (blk_f, blk_b, seq, first, last)]


def _split3(a):
    a1 = a.astype(BF16)
    r1 = a - a1.astype(F32)
    a2 = r1.astype(BF16)
    a3 = (r1 - a2.astype(F32)).astype(BF16)
    return a1, a2, a3


def _dot3(lhs_bf16, a):
    a1, a2, a3 = _split3(a)
    return (jnp.dot(lhs_bf16, a1, preferred_element_type=F32)
            + jnp.dot(lhs_bf16, a2, preferred_element_type=F32)
            + jnp.dot(lhs_bf16, a3, preferred_element_type=F32))


def _softplus(x):
    return jnp.maximum(x, 0.0) + jnp.log1p(jnp.exp(-jnp.abs(x)))


def _ssd_kernel(blkf_t, blkb_t, seq_t, first_t, last_t,
                xf_ref, bcf_ref, dtf_ref, xb_ref, bcb_ref, dtb_ref,
                alog_ref, dtbias_ref, dsk_ref, h0f_ref, h0b_ref,
                yf_ref, yb_ref, finf_ref, finb_ref,
                stf_ref, stb_ref):
    s = pl.program_id(0)
    seq = seq_t[s]
    dirs = ((xf_ref, bcf_ref, dtf_ref, stf_ref, yf_ref), (xb_ref, bcb_ref, dtb_ref, stb_ref, yb_ref))

    @pl.when(jnp.logical_and(first_t[s] == 1, seq < BATCH))
    def _():
        stf_ref[...] = jnp.zeros_like(stf_ref)
        stb_ref[...] = jnp.zeros_like(stb_ref)

    @pl.when(jnp.logical_and(first_t[s] == 1, seq >= BATCH))
    def _():
        stf_ref[...] = h0f_ref[...].T
        stb_ref[...] = h0b_ref[...].T

    ii = lax.broadcasted_iota(jnp.int32, (CHUNK, CHUNK), 0)
    jj = lax.broadcasted_iota(jnp.int32, (CHUNK, CHUNK), 1)
    masks = (jj <= ii, jj >= ii)
    masks_b = tuple(jnp.where(m, 1.0, 0.0).astype(BF16) for m in masks)
    lo = jj < SSM_HEAD_DIM

    e_r = lax.broadcasted_iota(jnp.int32, (LANE, SSM_INNER), 0)
    e_c = lax.broadcasted_iota(jnp.int32, (LANE, SSM_INNER), 1)
    expand = jnp.where(jnp.right_shift(e_c, 6) == e_r, 1.0, 0.0).astype(BF16)

    def chunk_setup(d, c):
        _, _, dt_ref, _, _ = dirs[d]
        rows = slice(c * CHUNK, (c + 1) * CHUNK)
        a_row = -jnp.exp(alog_ref[d])
        dt = _softplus(dt_ref[rows, :] + dtbias_ref[d])
        cum = _dot3(masks_b[d], dt * a_row)
        cum_t = cum.T
        dt_t = dt.T
        end = CHUNK - 1 if d == 0 else 0
        tot_row = cum[end:end + 1, :]
        tot_col = cum_t[:, end:end + 1]
        return dict(
            rows=rows, cum=cum, cum_t=cum_t, dt_t=dt_t,
            w_rows=jnp.exp(tot_col - cum_t) * dt_t,
            sdec=_dot3_rows(jnp.exp(tot_row), expand))

    def group_setup(d, cs, g):
        _, bc_ref, _, _, _ = dirs[d]
        b_g = bc_ref[cs["rows"], g * SSM_STATE:(g + 1) * SSM_STATE]
        c_g = bc_ref[cs["rows"], (SSM_GROUPS + g) * SSM_STATE:(SSM_GROUPS + g + 1) * SSM_STATE].astype(BF16)
        b_gt = b_g.T
        cb = jnp.dot(c_g, b_gt.astype(BF16), preferred_element_type=F32)
        return c_g, b_gt, cb

    def pair_step(d, cs, gs, kp):
        x_ref, _, _, st_ref, y_ref = dirs[d]
        c_g, b_gt, cb = gs
        cum, cum_t, dt_t = cs["cum"], cs["cum_t"], cs["dt_t"]
        h_a, h_b = 2 * kp, 2 * kp + 1
        ls = slice(kp * LANE, (kp + 1) * LANE)
        x_pair = x_ref[cs["rows"], ls]
        rhs = jnp.concatenate([jnp.where(lo, x_pair, 0.0).astype(BF16),
                               jnp.where(lo, 0.0, x_pair).astype(BF16)], axis=0)

        def head_lhs(h):
            col = jnp.broadcast_to(cum[:, h:h + 1], (CHUNK, CHUNK))
            seg = col - cum_t[h:h + 1, :]
            dec = jnp.exp(jnp.where(masks[d], seg, NEG_BIG))
            w_intra = dec * cb * dt_t[h:h + 1, :]
            w_state = b_gt * cs["w_rows"][h:h + 1, :]
            return w_intra.astype(BF16), w_state.astype(BF16), col

        wi_a, ws_a, col_a = head_lhs(h_a)
        wi_b, ws_b, col_b = head_lhs(h_b)
        lhs = jnp.concatenate([jnp.concatenate([wi_a, wi_b], axis=1),
                               jnp.concatenate([ws_a, ws_b], axis=1)], axis=0)
        both = jnp.dot(lhs, rhs, preferred_element_type=F32)
        y_diag = both[0:CHUNK, :]
        d_state = both[CHUNK:2 * CHUNK, :]

        h_pair = st_ref[:, ls]
        y_off = jnp.dot(c_g, h_pair.astype(BF16), preferred_element_type=F32)
        e_pair = jnp.exp(jnp.where(lo, col_a, col_b))
        y_ref[cs["rows"], ls] = (y_diag + y_off * e_pair + x_pair * dsk_ref[d, :, ls]).astype(y_ref.dtype)
        st_ref[:, ls] = h_pair * cs["sdec"][:, ls] + d_state

    pairs_per_group = SSM_HEADS // SSM_GROUPS // 2
    for k in range(SSD_CPT):
        cs = (chunk_setup(0, k), chunk_setup(1, SSD_CPT - 1 - k))
        for g in range(SSM_GROUPS):
            gs = (group_setup(0, cs[0], g), group_setup(1, cs[1], g))
            for kp in range(g * pairs_per_group, (g + 1) * pairs_per_group):
                pair_step(0, cs[0], gs[0], kp)
                pair_step(1, cs[1], gs[1], kp)

    @pl.when(jnp.logical_and(last_t[s] == 1, seq < BATCH))
    def _():
        finf_ref[...] = stf_ref[...].T
        finb_ref[...] = stb_ref[...].T


def _dot3_rows(row, rhs_bf16):
    r8 = jnp.broadcast_to(row, (SUBLANE, row.shape[1]))
    r1, r2, r3 = _split3(r8)
    out = (jnp.dot(r1, rhs_bf16, preferred_element_type=F32)
           + jnp.dot(r2, rhs_bf16, preferred_element_type=F32)
           + jnp.dot(r3, rhs_bf16, preferred_element_type=F32))
    return out[0:1, :]


def _ssd(ps, xc, bcc, layer, alog, dtb, dskip, h0f, h0b):
    tables = [jnp.asarray(t) for t in _ssd_tables()]
    n_steps = int(tables[0].shape[0])
    cdt = (OFF_DT - NP_MAIN) // LANE

    def fwd_tile(col):
        return lambda s, blkf, *_: (blkf[s], col)

    def bwd_tile(col):
        return lambda s, blkf, blkb, *_: (blkb[s], col)

    def h0_map(s, blkf, blkb, seq, *_):
        return (jnp.maximum(seq[s] - BATCH, 0), layer, 0, 0)

    def fin_map(s, blkf, blkb, seq, *_):
        return (jnp.minimum(seq[s], BATCH - 1), 0, 0)

    def const3(s, *_):
        return (0, 0, 0)

    grid_spec = pltpu.PrefetchScalarGridSpec(
        num_scalar_prefetch=5,
        grid=(n_steps,),
        in_specs=[
            pl.BlockSpec((SSD_TQ, SSM_INNER), fwd_tile(0)),
            pl.BlockSpec((SSD_TQ, BC_WIDTH), fwd_tile(0)),
            pl.BlockSpec((SSD_TQ, LANE), fwd_tile(cdt)),
            pl.BlockSpec((SSD_TQ, SSM_INNER), bwd_tile(0)),
            pl.BlockSpec((SSD_TQ, BC_WIDTH), bwd_tile(0)),
            pl.BlockSpec((SSD_TQ, LANE), bwd_tile(cdt)),
            pl.BlockSpec((2, 1, LANE), const3),
            pl.BlockSpec((2, 1, LANE), const3),
            pl.BlockSpec((2, 1, SSM_INNER), const3),
            pl.BlockSpec((None, None, SSM_INNER, SSM_STATE), h0_map),
            pl.BlockSpec((None, None, SSM_INNER, SSM_STATE), h0_map),
        ],
        out_specs=[
            pl.BlockSpec((SSD_TQ, SSM_INNER), fwd_tile(0)),
            pl.BlockSpec((SSD_TQ, SSM_INNER), bwd_tile(0)),
            pl.BlockSpec((None, SSM_INNER, SSM_STATE), fin_map),
            pl.BlockSpec((None, SSM_INNER, SSM_STATE), fin_map),
        ],
        scratch_shapes=[
            pltpu.VMEM((SSM_STATE, SSM_INNER), F32),
            pltpu.VMEM((SSM_STATE, SSM_INNER), F32),
        ],
    )
    return pl.pallas_call(
        _ssd_kernel,
        grid_spec=grid_spec,
        out_shape=[
            jax.ShapeDtypeStruct((N_TOK, SSM_INNER), BF16),
            jax.ShapeDtypeStruct((N_TOK, SSM_INNER), BF16),
            jax.ShapeDtypeStruct((BATCH, SSM_INNER, SSM_STATE), F32),
            jax.ShapeDtypeStruct((BATCH, SSM_INNER, SSM_STATE), F32),
        ],
        compiler_params=_cparams(("arbitrary",)),
        name="ssd_scan",
    )(*tables, xc, bcc, ps, xc, bcc, ps, alog, dtb, dskip, h0f, h0b)


QK_PAD = 128
N_QROWS = MLA_HEADS * QK_PAD
N_VROWS = MLA_HEADS * V_DIM
NT_DIMS = (((1,), (1,)), ((), ()))
Q_PRESCALE = (1.0 / math.sqrt(QK_DIM)) * math.log2(math.e)


def _mla_prep_kernel(*refs, tm, normalize, with_q):
    if with_q:
        (cq_ref, ckv_ref, kr_ref, krs_ref, cos_ref, sin_ref, cost_ref, sint_ref,
         qnw_ref, kvnw_ref, wq_ref, wk_ref, wv_ref,
         qt_ref, k_ref, vt_ref, ckvn_ref) = refs
    else:
        (ckv_ref, kr_ref, krs_ref, cos_ref, sin_ref, kvnw_ref, wk_ref, wv_ref,
         k_ref, vt_ref) = refs

    ckv = ckv_ref[...]
    if normalize:
        ckv = _rms(ckv, kvnw_ref[...])
        ckvn_ref[...] = ckv
    ckv_b = ckv.astype(BF16)
    kn = jnp.dot(ckv_b, wk_ref[...], preferred_element_type=F32)
    kr = kr_ref[...] * cos_ref[...] + krs_ref[...] * sin_ref[...]
    for h in range(MLA_HEADS):
        hs = slice(h * QK_PAD, (h + 1) * QK_PAD)
        k_ref[:, hs] = (kn[:, hs] + kr).astype(BF16)
    vt_ref[...] = lax.dot_general(wv_ref[...], ckv_b, NT_DIMS, preferred_element_type=F32).astype(BF16)

    if with_q:
        cqn = _rms(cq_ref[...], qnw_ref[...]).astype(BF16)
        qq = lax.dot_general(wq_ref[...], cqn, NT_DIMS, preferred_element_type=F32)
        for h in range(MLA_HEADS):
            q_h = qq[h * QK_PAD:(h + 1) * QK_PAD, :]
            qs_h = qq[N_QROWS + h * QK_PAD:N_QROWS + (h + 1) * QK_PAD, :]
            q_rot = q_h * cost_ref[...] + qs_h * sint_ref[...]
            qt_ref[h * QK_PAD:(h + 1) * QK_PAD, :] = (q_rot * Q_PRESCALE).astype(BF16)


def _mla_prep_tokens(p, row0, n_rows, tm, tab_map, cos, sin, cos_t, sin_t, qnw, kvnw, wq_t, wk, wv_t):
    b0 = row0 // tm
    nt = n_rows // tm
    kernel = functools.partial(_mla_prep_kernel, tm=tm, normalize=True, with_q=True)

    def pcol(width, off):
        return pl.BlockSpec((tm, width), lambda i: (b0 + i, (off - NP_MAIN) // width))

    def full(a):
        return pl.BlockSpec(a.shape, lambda i: (0,) * a.ndim)

    return pl.pallas_call(
        kernel,
        grid=(nt,),
        in_specs=[
            pcol(Q_LORA, OFF_CQ), pcol(KV_LORA, OFF_CKV), pcol(LANE, OFF_KR), pcol(LANE, OFF_KRS),
            pl.BlockSpec((tm, LANE), lambda i: (tab_map(i), 0)),
            pl.BlockSpec((tm, LANE), lambda i: (tab_map(i), 0)),
            pl.BlockSpec((LANE, tm), lambda i: (0, tab_map(i))),
            pl.BlockSpec((LANE, tm), lambda i: (0, tab_map(i))),
            full(qnw), full(kvnw), full(wq_t), full(wk), full(wv_t),
        ],
        out_specs=[
            pl.BlockSpec((N_QROWS, tm), lambda i: (0, i)),
            pl.BlockSpec((None, tm, N_QROWS), lambda i: (i, 0, 0)),
            pl.BlockSpec((None, N_VROWS, tm), lambda i: (i, 0, 0)),
            pl.BlockSpec((tm, KV_LORA), lambda i: (i, 0)),
        ],
        out_shape=[
            jax.ShapeDtypeStruct((N_QROWS, n_rows), BF16),
            jax.ShapeDtypeStruct((nt, tm, N_QROWS), BF16),
            jax.ShapeDtypeStruct((nt, N_VROWS, tm), BF16),
            jax.ShapeDtypeStruct((n_rows, KV_LORA), F32),
        ],
        compiler_params=_cparams(("arbitrary",)),
        name="mla_prep",
    )(p, p, p, p, cos, sin, cos_t, sin_t, qnw, kvnw, wq_t, wk, wv_t)


def _mla_prep_cache(ckv, kr_pad, ones_tab, zeros_tab, kvnw, wk, wv_t, tm):
    n_rows = ckv.shape[0]
    nt = n_rows // tm
    kernel = functools.partial(_mla_prep_kernel, tm=tm, normalize=False, with_q=False)

    def full(a):
        return pl.BlockSpec(a.shape, lambda i: (0,) * a.ndim)

    return pl.pallas_call(
        kernel,
        grid=(nt,),
        in_specs=[
            pl.BlockSpec((tm, KV_LORA), lambda i: (i, 0)),
            pl.BlockSpec((tm, LANE), lambda i: (i, 0)),
            pl.BlockSpec((tm, LANE), lambda i: (i, 0)),
            pl.BlockSpec((tm, LANE), lambda i: (0, 0)),
            pl.BlockSpec((tm, LANE), lambda i: (0, 0)),
            full(kvnw), full(wk), full(wv_t),
        ],
        out_specs=[
            pl.BlockSpec((None, tm, N_QROWS), lambda i: (i, 0, 0)),
            pl.BlockSpec((None, N_VROWS, tm), lambda i: (i, 0, 0)),
        ],
        out_shape=[
            jax.ShapeDtypeStruct((nt, tm, N_QROWS), BF16),
            jax.ShapeDtypeStruct((nt, N_VROWS, tm), BF16),
        ],
        compiler_params=_cparams(("arbitrary",)),
        name="mla_prep_cache",
    )(ckv, kr_pad, kr_pad, ones_tab, zeros_tab, kvnw, wk, wv_t)


ATT_TQ = 256


def _attn_kernel(*refs, n_kt, tq, tk, with_cache):
    if with_cache:
        qt_ref, kp_ref, vtp_ref, k_ref, vt_ref, o_ref, ot_ref, s_ref = refs
    else:
        qt_ref, k_ref, vt_ref, o_ref, ot_ref, s_ref = refs

    def k_tile(kt, h):
        cols = slice(h * QK_PAD, (h + 1) * QK_PAD)
        if with_cache:
            return kp_ref[:, cols] if kt == 0 else k_ref[kt - 1, :, cols]
        return k_ref[kt, :, cols]

    def vt_tile(kt, h):
        rows = slice(h * V_DIM, (h + 1) * V_DIM)
        if with_cache:
            return vtp_ref[rows, :] if kt == 0 else vt_ref[kt - 1, rows, :]
        return vt_ref[kt, rows, :]

    def scores_step(h, kt, m8):
        q_t = qt_ref[h * QK_PAD:(h + 1) * QK_PAD, :]
        s = jnp.dot(k_tile(kt, h), q_t, preferred_element_type=F32)
        s_ref[h % 2, kt] = s
        return jnp.maximum(m8, jnp.max(s.reshape(tk // SUBLANE, SUBLANE, tq), axis=0))

    def probs_step(h, kt, m, l8, acc):
        pr = jnp.exp2(s_ref[h % 2, kt] - m)
        l8 = l8 + jnp.sum(pr.reshape(tk // SUBLANE, SUBLANE, tq), axis=0)
        acc = acc + jnp.dot(vt_tile(kt, h), pr.astype(BF16), preferred_element_type=F32)
        return l8, acc

    m8_init = jnp.full((SUBLANE, tq), NEG_BIG, F32)
    m8 = m8_init
    for kt in range(n_kt):
        m8 = scores_step(0, kt, m8)
    for h in range(MLA_HEADS):
        m = jnp.max(m8, axis=0, keepdims=True)
        l8 = jnp.zeros((SUBLANE, tq), F32)
        acc = jnp.zeros((V_DIM, tq), F32)
        m8 = m8_init
        for kt in range(n_kt):
            l8, acc = probs_step(h, kt, m, l8, acc)
            if h + 1 < MLA_HEADS:
                m8 = scores_step(h + 1, kt, m8)
        ot_ref[h * V_DIM:(h + 1) * V_DIM, :] = acc / jnp.sum(l8, axis=0, keepdims=True)
    o_ref[...] = ot_ref[...].T.astype(o_ref.dtype)


def _attention(qt, k3, vt3, n_batch, lq, tk, cache=None):
    tq = min(ATT_TQ, lq)
    nq = lq // tq
    n_new = k3.shape[0] // n_batch
    k4 = k3.reshape(n_batch, n_new, tk, N_QROWS)
    v4 = vt3.reshape(n_batch, n_new, N_VROWS, tk)
    n_kt = n_new + (0 if cache is None else 1)
    kernel = functools.partial(_attn_kernel, n_kt=n_kt, tq=tq, tk=tk, with_cache=cache is not None)
    cache_specs = [] if cache is None else [
        pl.BlockSpec((None, tk, N_QROWS), lambda b, i: (b, 0, 0)),
        pl.BlockSpec((None, N_VROWS, tk), lambda b, i: (b, 0, 0)),
    ]
    return pl.pallas_call(
        kernel,
        grid=(n_batch, nq),
        in_specs=[pl.BlockSpec((N_QROWS, tq), lambda b, i: (0, b * nq + i))] + cache_specs + [
            pl.BlockSpec((None, n_new, tk, N_QROWS), lambda b, i: (b, 0, 0, 0)),
            pl.BlockSpec((None, n_new, N_VROWS, tk), lambda b, i: (b, 0, 0, 0)),
        ],
        out_specs=pl.BlockSpec((tq, N_VROWS), lambda b, i: (b * nq + i, 0)),
        out_shape=jax.ShapeDtypeStruct((n_batch * lq, N_VROWS), BF16),
        scratch_shapes=[pltpu.VMEM((N_VROWS, tq), F32), pltpu.VMEM((2, n_kt, tk, tq), F32)],
        compiler_params=_cparams(("arbitrary", "arbitrary")),
        name="attention",
    )(qt, *(() if cache is None else cache), k4, v4)


MG_TM = 256


def _merge_kernel(ax_ref, axp_ref, axn_ref, ac_ref, acp_ref, acn_ref, ab_ref,
                  g_ref, z_ref, yf_ref, yb_ref, zc_ref, x_ref, gm_ref,
                  cw_ref, wa_ref, nw_ref, wb_ref, wc_ref, wo_ref, o_ref):
    hp, hn = _tile_neighbours(pl.program_id(0), MG_TM)

    def f32(v):
        return v.astype(F32)

    u = f32(ac_ref[...]) * f32(ax_ref[...])
    u_prev = f32(acp_ref[HALO - 1:HALO, :]) * f32(axp_ref[HALO - 1:HALO, :]) * hp
    u_next = f32(acn_ref[0:1, :]) * f32(axn_ref[0:1, :]) * hn
    za = f32(ab_ref[...]) * _conv3_tile(u, u_prev, u_next, cw_ref, MG_TM)
    y_a = jnp.dot(za.astype(BF16), wa_ref[...], preferred_element_type=F32)

    yb = (f32(yf_ref[...]) + f32(yb_ref[...])) * _silu(f32(z_ref[...]))
    zb = _rms(yb, nw_ref[...])
    y_b = jnp.dot(zb.astype(BF16), wb_ref[...], preferred_element_type=F32)

    y_c = jnp.dot(zc_ref[...], wc_ref[...], preferred_element_type=F32)

    merged = (jax.nn.sigmoid(f32(g_ref[:, 0:D_MODEL])) * y_a
              + jax.nn.sigmoid(f32(g_ref[:, D_MODEL:2 * D_MODEL])) * y_b
              + jax.nn.sigmoid(f32(g_ref[:, 2 * D_MODEL:3 * D_MODEL])) * y_c)
    o = jnp.dot(merged.astype(BF16), wo_ref[...], preferred_element_type=F32)
    o_ref[...] = x_ref[...] + gm_ref[...] * o


def _merge(p, y_f, y_b, zc, x, mod_l, conv_w, wa, nw, wb, wc, wo):
    halo_per_tile = MG_TM // HALO
    n_halo_blocks = N_TOK // HALO
    row = functools.partial(_mod_row, tile_rows=MG_TM)

    def pcol(width, off):
        return pl.BlockSpec((MG_TM, width), lambda i: (i, off // width))

    def pprev(off):
        return pl.BlockSpec((HALO, A_WIDTH), lambda i: (jnp.maximum(i * halo_per_tile - 1, 0), off // A_WIDTH))

    def pnext(off):
        return pl.BlockSpec((HALO, A_WIDTH),
                            lambda i: (jnp.minimum((i + 1) * halo_per_tile, n_halo_blocks - 1), off // A_WIDTH))

    def full(a):
        return pl.BlockSpec(a.shape, lambda i: (0,) * a.ndim)

    return pl.pallas_call(
        _merge_kernel,
        grid=(N_TOK // MG_TM,),
        in_specs=[
            pcol(A_WIDTH, OFF_AX), pprev(OFF_AX), pnext(OFF_AX),
            pcol(A_WIDTH, OFF_AC), pprev(OFF_AC), pnext(OFF_AC),
            pcol(A_WIDTH, OFF_AB),
            pcol(3 * D_MODEL, OFF_G),
            pcol(SSM_INNER, OFF_Z),
            pl.BlockSpec((MG_TM, SSM_INNER), lambda i: (i, 0)),
            pl.BlockSpec((MG_TM, SSM_INNER), lambda i: (i, 0)),
            pl.BlockSpec((MG_TM, N_VROWS), lambda i: (i, 0)),
            pl.BlockSpec((MG_TM, D_MODEL), lambda i: (i, 0)),
            pl.BlockSpec((None, None, 1, D_MODEL), lambda i: (row(i), 2, 0, 0)),
            full(conv_w), full(wa), full(nw), full(wb), full(wc), full(wo),
        ],
        out_specs=pl.BlockSpec((MG_TM, D_MODEL), lambda i: (i, 0)),
        out_shape=jax.ShapeDtypeStruct((N_TOK, D_MODEL), F32),
        compiler_params=_cparams(("arbitrary",)),
        name="merge",
    )(p, p, p, p, p, p, p, p, p, y_f, y_b, zc, x, mod_l, conv_w, wa, nw, wb, wc, wo)


FF_TM = 512
FF_CHUNK = 256


def _ffn_kernel(x_ref, nw_ref, sh_ref, sc_ref, gm_ref, w1_ref, w3_ref, w2_ref, o_ref, h_ref, g_ref):
    _modulated_norm_to(h_ref, x_ref, nw_ref, sc_ref, sh_ref, FF_TM)
    h = h_ref[...]
    for c in range(0, FF_DIM, FF_CHUNK):
        a = jnp.dot(h, w1_ref[:, c:c + FF_CHUNK], preferred_element_type=F32)
        b = jnp.dot(h, w3_ref[:, c:c + FF_CHUNK], preferred_element_type=F32)
        g_ref[:, c:c + FF_CHUNK] = (_silu(a) * b).astype(BF16)
    ff = jnp.dot(g_ref[...], w2_ref[...], preferred_element_type=F32)
    o_ref[...] = x_ref[...] + gm_ref[...] * ff


def _ffn(x, mod_l, norm_w, w1, w3, w2):
    row = functools.partial(_mod_row, tile_rows=FF_TM)
    return pl.pallas_call(
        _ffn_kernel,
        grid=(N_TOK // FF_TM,),
        in_specs=[
            pl.BlockSpec((FF_TM, D_MODEL), lambda i: (i, 0)),
            pl.BlockSpec((1, D_MODEL), lambda i: (0, 0)),
            pl.BlockSpec((None, None, 1, D_MODEL), lambda i: (row(i), 3, 0, 0)),
            pl.BlockSpec((None, None, 1, D_MODEL), lambda i: (row(i), 4, 0, 0)),
            pl.BlockSpec((None, None, 1, D_MODEL), lambda i: (row(i), 5, 0, 0)),
            pl.BlockSpec((D_MODEL, FF_DIM), lambda i: (0, 0)),
            pl.BlockSpec((D_MODEL, FF_DIM), lambda i: (0, 0)),
            pl.BlockSpec((FF_DIM, D_MODEL), lambda i: (0, 0)),
        ],
        out_specs=pl.BlockSpec((FF_TM, D_MODEL), lambda i: (i, 0)),
        out_shape=jax.ShapeDtypeStruct((N_TOK, D_MODEL), F32),
        scratch_shapes=[pltpu.VMEM((FF_TM, D_MODEL), BF16), pltpu.VMEM((FF_TM, FF_DIM), BF16)],
        compiler_params=_cparams(("arbitrary",)),
        name="ffn",
    )(x, norm_w, mod_l, mod_l, mod_l, w1, w3, w2)


FN_TM = 1024


def _final_norm_kernel(x_ref, w_ref, o_ref):
    o_ref[...] = _rms(x_ref[...], w_ref[...])


def _final_norm(x, w, row0, n_rows):
    b0 = row0 // FN_TM
    return pl.pallas_call(
        _final_norm_kernel,
        grid=(n_rows // FN_TM,),
        in_specs=[pl.BlockSpec((FN_TM, D_MODEL), lambda i: (b0 + i, 0)),
                  pl.BlockSpec((1, D_MODEL), lambda i: (0, 0))],
        out_specs=pl.BlockSpec((FN_TM, D_MODEL), lambda i: (i, 0)),
        out_shape=jax.ShapeDtypeStruct((n_rows, D_MODEL), F32),
        compiler_params=_cparams(("arbitrary",)),
        name="final_norm",
    )(x, w)


def _pad_in_weights(w_in):
    o = 0
    a_x = w_in[..., o:o + 512]; o += 512
    a_b = w_in[..., o:o + 512]; o += 512
    a_c = w_in[..., o:o + 512]; o += 512
    s_z = w_in[..., o:o + 1024]; o += 1024
    s_x = w_in[..., o:o + 1024]; o += 1024
    s_bc = w_in[..., o:o + 512]; o += 512
    s_dt = w_in[..., o:o + 16]; o += 16
    cq = w_in[..., o:o + 256]; o += 256
    ckv = w_in[..., o:o + 256]; o += 256
    kr = w_in[..., o:o + 32]; o += 32
    gates = w_in[..., o:o + 3072]; o += 3072

    def z(n):
        return jnp.zeros(w_in.shape[:-1] + (n,), w_in.dtype)

    kr_sw = jnp.concatenate([kr[..., 16:], kr[..., :16]], axis=-1)
    cols = [gates, s_z, s_x, a_x, a_b, a_c, s_bc, cq, ckv,
            s_dt, z(LANE - 16),
            z(64), kr, z(32),
            z(64), kr_sw, z(32),
            z(NP - OFF_KRS - LANE)]
    out = jnp.concatenate(cols, axis=-1).astype(BF16)
    assert out.shape[-1] == NP
    return out


def _q_weights_t(w_uq):
    w = w_uq.reshape(DEPTH, Q_LORA, MLA_HEADS, QK_DIM)
    nope, x1, x2 = w[..., :NOPE_DIM], w[..., NOPE_DIM:NOPE_DIM + 16], w[..., NOPE_DIM + 16:]
    z32 = jnp.zeros_like(w[..., :32])
    z64 = jnp.zeros_like(nope)
    q = jnp.concatenate([nope, x1, x2, z32], axis=-1).reshape(DEPTH, Q_LORA, N_QROWS)
    qs = jnp.concatenate([z64, x2, x1, z32], axis=-1).reshape(DEPTH, Q_LORA, N_QROWS)
    return jnp.swapaxes(jnp.concatenate([q, qs], axis=-1), 1, 2).astype(BF16)


def _kv_weights(w_ukv):
    w = w_ukv.reshape(DEPTH, KV_LORA, MLA_HEADS, NOPE_DIM + V_DIM)
    kn = jnp.concatenate([w[..., :NOPE_DIM], jnp.zeros_like(w[..., :QK_PAD - NOPE_DIM])], axis=-1)
    wk = kn.reshape(DEPTH, KV_LORA, N_QROWS).astype(BF16)
    wv_t = jnp.swapaxes(w[..., NOPE_DIM:].reshape(DEPTH, KV_LORA, N_VROWS), 1, 2).astype(BF16)
    return wk, wv_t


def _rope_tables(n_tokens, lead_rows):
    n_rows = n_tokens // GRID_W
    row = jnp.repeat(jnp.arange(n_rows, dtype=F32), GRID_W)
    col = jnp.tile(jnp.arange(GRID_W, dtype=F32), n_rows)
    pairs = ROPE_DIM // 4
    inv = ROPE_BASE ** (-jnp.arange(pairs, dtype=F32) / pairs)
    ang = jnp.concatenate([row[:, None] * inv, col[:, None] * inv], axis=-1)
    cos, sin = jnp.cos(ang), jnp.sin(ang)
    ones = jnp.ones((n_tokens, NOPE_DIM), F32)
    z32 = jnp.zeros((n_tokens, 32), F32)
    cos_l = jnp.concatenate([ones, cos, cos, z32], axis=-1)
    sin_l = jnp.concatenate([jnp.zeros_like(ones), -sin, sin, z32], axis=-1)
    ident_c = jnp.concatenate([jnp.ones((lead_rows, NOPE_DIM + ROPE_DIM), F32), jnp.zeros((lead_rows, 32), F32)], -1)
    ident_s = jnp.zeros((lead_rows, LANE), F32)
    return jnp.concatenate([ident_c, cos_l], axis=0), jnp.concatenate([ident_s, sin_l], axis=0)


PREP_TM_CTX = SEQ
PREP_TM_LAT = 512


def kernel(x_prompt, x_sample, c, cache_ckv, cache_krope, state_ssm_fwd, state_ssm_bwd, c_ctx, w_in, a_conv_w, w_a_out, ssm_conv_w, ssm_conv_b, ssm_a_log, ssm_dt_bias, ssm_d, ssm_norm_w, w_b_out, q_norm_w, w_uq, kv_norm_w, w_ukv, w_c_out, w_o, w_ada, b_ada, norm1_w, norm2_w, w_ff1, w_ff3, w_ff2, final_norm_w):
    w_in_p = _pad_in_weights(w_in)
    wq_t = _q_weights_t(w_uq)
    wk, wv_t = _kv_weights(w_ukv)
    wa, wb, wc, wo = (w.astype(BF16) for w in (w_a_out, w_b_out, w_c_out, w_o))
    w1, w3, w2 = (w.astype(BF16) for w in (w_ff1, w_ff3, w_ff2))

    cond = jnp.concatenate([c_ctx[None, :], c, jnp.zeros((N_MOD_ROWS - 1 - DEC_BATCH, D_MODEL), F32)], axis=0)
    mod = _modulation(cond, w_ada, b_ada).reshape(DEPTH, N_MOD_ROWS, 6, 1, D_MODEL)

    conv_wx = ssm_conv_w[..., :SSM_INNER]
    conv_wbc = ssm_conv_w[..., SSM_INNER:]
    conv_bx = ssm_conv_b[:, None, :SSM_INNER]
    conv_bbc = ssm_conv_b[:, None, SSM_INNER:]
    pad_h = ((0, 0), (0, 0), (0, 0), (0, LANE - SSM_HEADS))
    alog = jnp.pad(ssm_a_log[:, :, None, :], pad_h)
    dtb = jnp.pad(ssm_dt_bias[:, :, None, :], pad_h)
    dskip = jnp.repeat(ssm_d, SSM_HEAD_DIM, axis=-1)[:, :, None, :]
    h0f = state_ssm_fwd.reshape(DEC_BATCH, DEPTH, SSM_INNER, SSM_STATE)
    h0b = state_ssm_bwd.reshape(DEC_BATCH, DEPTH, SSM_INNER, SSM_STATE)

    cos_c, sin_c = _rope_tables(DEC_SEQ, PREP_TM_CTX)
    cos_l, sin_l = cos_c[PREP_TM_CTX:], sin_c[PREP_TM_CTX:]
    ident_cos, ident_sin = cos_c[:PREP_TM_CTX], sin_c[:PREP_TM_CTX]
    ones_tab = jnp.concatenate([ident_cos, ident_cos], axis=0)
    zeros_tab = jnp.zeros_like(ones_tab)
    cache_kr_pad = jnp.pad(cache_krope, ((0, 0), (0, 0), (0, 0), (64, 32)))

    x = jnp.concatenate([x_prompt.reshape(N_CTX_TOK, D_MODEL), x_sample.reshape(N_LAT_TOK, D_MODEL)], axis=0)

    lat_tiles_per_seq = DEC_SEQ // PREP_TM_LAT
    new_ckv, new_kr, new_f, new_b = [], [], [], []
    for l in range(DEPTH):
        p, ps = _in_projection(x, mod[l], norm1_w[l][None, :], w_in_p[l])

        xc, bcc = _ssd_conv(p, conv_wx[l], conv_bx[l], conv_wbc[l], conv_bbc[l])
        y_f, y_b, fin_f, fin_b = _ssd(ps, xc, bcc, l, alog[l], dtb[l], dskip[l], h0f, h0b)
        new_f.append(fin_f)
        new_b.append(fin_b)

        qnw, kvnw = q_norm_w[l][None, :], kv_norm_w[l][None, :]
        qt_c, k_c, vt_c, ckv_c = _mla_prep_tokens(
            ps, 0, N_CTX_TOK, PREP_TM_CTX, lambda i: 0, ident_cos, ident_sin, ident_cos.T, ident_sin.T,
            qnw, kvnw, wq_t[l], wk[l], wv_t[l])
        qt_l, k_l, vt_l, _ = _mla_prep_tokens(
            ps, N_CTX_TOK, N_LAT_TOK, PREP_TM_LAT, lambda i: i % lat_tiles_per_seq, cos_l, sin_l, cos_l.T, sin_l.T,
            qnw, kvnw, wq_t[l], wk[l], wv_t[l])
        k_p, vt_p = _mla_prep_cache(cache_ckv[:, l].reshape(DEC_BATCH * PAST_LEN, KV_LORA),
                                    cache_kr_pad[:, l].reshape(DEC_BATCH * PAST_LEN, LANE),
                                    ones_tab, zeros_tab, kvnw, wk[l], wv_t[l], PAST_LEN)
        new_ckv.append(ckv_c.reshape(BATCH, SEQ, KV_LORA))
        kr0 = OFF_KR - NP_MAIN + 64
        new_kr.append(ps[:N_CTX_TOK, kr0:kr0 + ROPE_DIM].reshape(BATCH, SEQ, ROPE_DIM))

        zc_c = _attention(qt_c, k_c, vt_c, BATCH, SEQ, SEQ)
        zc_l = _attention(qt_l, k_l, vt_l, DEC_BATCH, DEC_SEQ, PREP_TM_LAT, cache=(k_p, vt_p))
        zc = jnp.concatenate([zc_c, zc_l], axis=0)

        x = _merge(p, y_f, y_b, zc, x, mod[l], a_conv_w[l], wa[l], ssm_norm_w[l][None, :], wb[l], wc[l], wo[l])
        x = _ffn(x, mod[l], norm2_w[l][None, :], w1[l], w3[l], w2[l])

    fw = final_norm_w[None, :]
    y_prompt = _final_norm(x, fw, 0, N_CTX_TOK).reshape(BATCH, SEQ, D_MODEL)
    y_sample = _final_norm(x, fw, N_CTX_TOK, N_LAT_TOK).reshape(DEC_BATCH, DEC_SEQ, D_MODEL)
    hshape = (BATCH, DEPTH, SSM_HEADS, SSM_HEAD_DIM, SSM_STATE)
    return (y_prompt, y_sample,
            jnp.stack(new_ckv, axis=1), jnp.stack(new_kr, axis=1),
            jnp.stack(new_f, axis=1).reshape(hshape), jnp.stack(new_b, axis=1).reshape(hshape))
```

```python
import functools
import math

import jax
import jax.numpy as jnp
import numpy as np
from jax import lax
from jax.experimental import pallas as pl
from jax.experimental.pallas import tpu as pltpu

F32 = jnp.float32
BF16 = jnp.bfloat16

D_MODEL = 1024
BATCH = 16
SEQ = 256
DEPTH = 4
DEC_BATCH = 4
DEC_SEQ = 4096
PAST_LEN = 512
GRID_W = 64
EPS = 1e-6
A_WIDTH = 512
SSM_INNER = 1024
SSM_HEAD_DIM = 64
SSM_HEADS = 16
SSM_GROUPS = 2
SSM_STATE = 128
CHUNK = 128
MLA_HEADS = 8
Q_LORA = 256
KV_LORA = 256
NOPE_DIM = 64
ROPE_DIM = 32
V_DIM = 64
QK_DIM = NOPE_DIM + ROPE_DIM
ROPE_BASE = 10000.0
FF_DIM = 2816

N_CTX_TOK = BATCH * SEQ
N_LAT_TOK = DEC_BATCH * DEC_SEQ
N_TOK = N_CTX_TOK + N_LAT_TOK
N_MOD_ROWS = 8

LANE = 128
SUBLANE = 8
VMEM_LIMIT = 56 * 1024 * 1024

OFF_G = 0
OFF_Z = 3072
OFF_SX = 4096
OFF_AX = 5120
OFF_AB = 5632
OFF_AC = 6144
OFF_BC = 6656
NP_MAIN = 7168
OFF_CQ = 7168
OFF_CKV = 7424
OFF_DT = 7680
OFF_KR = 7808
OFF_KRS = 7936
NP = 8192
N_MAIN_TILES = 7

NEG_BIG = -1e30


def _cparams(sem):
    return pltpu.CompilerParams(dimension_semantics=sem, vmem_limit_bytes=VMEM_LIMIT)


def _rms(x, w):
    ms = jnp.mean(x * x, axis=-1, keepdims=True)
    return x * lax.rsqrt(ms + EPS) * w


def _silu(x):
    return x * jax.nn.sigmoid(x)


def _mod_row(tile, tile_rows):
    n_ctx_tiles = N_CTX_TOK // tile_rows
    tiles_per_lat = DEC_SEQ // tile_rows
    return jnp.where(tile < n_ctx_tiles, 0, 1 + (tile - n_ctx_tiles) // tiles_per_lat)


MOD_TN = 1536


def _mod_kernel(c_ref, w_ref, b_ref, o_ref):
    c = c_ref[...]
    s = _silu(c).astype(BF16)
    o_ref[...] = jnp.dot(s, w_ref[...].astype(BF16), preferred_element_type=F32) + b_ref[...]


def _modulation(cond, w_ada, b_ada):
    n_col = 6 * D_MODEL
    return pl.pallas_call(
        _mod_kernel,
        grid=(DEPTH, n_col // MOD_TN),
        in_specs=[
            pl.BlockSpec((N_MOD_ROWS, D_MODEL), lambda l, j: (0, 0)),
            pl.BlockSpec((None, D_MODEL, MOD_TN), lambda l, j: (l, 0, j)),
            pl.BlockSpec((None, 1, MOD_TN), lambda l, j: (l, 0, j)),
        ],
        out_specs=pl.BlockSpec((None, N_MOD_ROWS, MOD_TN), lambda l, j: (l, 0, j)),
        out_shape=jax.ShapeDtypeStruct((DEPTH, N_MOD_ROWS, n_col), F32),
        compiler_params=_cparams(("arbitrary", "arbitrary")),
        name="modulation",
    )(cond, w_ada, b_ada.reshape(DEPTH, 1, n_col))


IN_TM = 2048
IN_TN = 1024
NORM_ROWS = 256


def _stream_specs(xs, tm, grid_rank):
    def imap(f):
        return (lambda i: f(i)) if grid_rank == 1 else (lambda i, j: f(i))

    width = xs[0].shape[1]
    if len(xs) == 1:
        return [pl.BlockSpec((tm, width), imap(lambda i: (i, 0)))]
    n_ctx_tiles = N_CTX_TOK // tm
    return [pl.BlockSpec((tm, width), imap(lambda i: (jnp.minimum(i, n_ctx_tiles - 1), 0))),
            pl.BlockSpec((tm, width), imap(lambda i: (jnp.maximum(i - n_ctx_tiles, 0), 0)))]


def _stream_rows(x_refs, tile, tm, rows):
    if len(x_refs) == 1:
        return x_refs[0][rows, :]
    return jnp.where(tile < N_CTX_TOK // tm, x_refs[0][rows, :], x_refs[1][rows, :])


def _modulated_norm_to(h_ref, x_rows, nw_ref, sc_ref, sh_ref, rows):
    for r in range(0, rows, NORM_ROWS):
        x = x_rows(slice(r, r + NORM_ROWS))
        h = _rms(x, nw_ref[...]) * (1.0 + sc_ref[...]) + sh_ref[...]
        h_ref[r:r + NORM_ROWS, :] = h.astype(BF16)


def _inproj_kernel(*refs, n_x, tm):
    x_refs = refs[:n_x]
    nw_ref, sh_ref, sc_ref, w_ref, om_ref, os_ref, h_ref = refs[n_x:]
    i = pl.program_id(0)
    j = pl.program_id(1)

    @pl.when(j == 0)
    def _():
        _modulated_norm_to(h_ref, lambda rows: _stream_rows(x_refs, i, tm, rows),
                           nw_ref, sc_ref, sh_ref, tm)

    @pl.when(j < N_MAIN_TILES)
    def _():
        om_ref[...] = jnp.dot(h_ref[...], w_ref[...], preferred_element_type=F32).astype(BF16)

    @pl.when(j == N_MAIN_TILES)
    def _():
        os_ref[...] = jnp.dot(h_ref[...], w_ref[...], preferred_element_type=F32)


def _in_projection(xs, mod_l, norm_w, w_in_p):
    tm = IN_TM if len(xs) == 1 else IN_TM // 2
    row = functools.partial(_mod_row, tile_rows=tm)
    return pl.pallas_call(
        functools.partial(_inproj_kernel, n_x=len(xs), tm=tm),
        grid=(N_TOK // tm, NP // IN_TN),
        in_specs=_stream_specs(xs, tm, 2) + [
            pl.BlockSpec((1, D_MODEL), lambda i, j: (0, 0)),
            pl.BlockSpec((None, None, 1, D_MODEL), lambda i, j: (row(i), 0, 0, 0)),
            pl.BlockSpec((None, None, 1, D_MODEL), lambda i, j: (row(i), 1, 0, 0)),
            pl.BlockSpec((D_MODEL, IN_TN), lambda i, j: (0, j)),
        ],
        out_specs=[
            pl.BlockSpec((tm, IN_TN), lambda i, j: (i, jnp.minimum(j, N_MAIN_TILES - 1))),
            pl.BlockSpec((tm, IN_TN), lambda i, j: (i, 0)),
        ],
        out_shape=[
            jax.ShapeDtypeStruct((N_TOK, NP_MAIN), BF16),
            jax.ShapeDtypeStruct((N_TOK, IN_TN), F32),
        ],
        scratch_shapes=[pltpu.VMEM((tm, D_MODEL), BF16)],
        compiler_params=_cparams(("arbitrary", "arbitrary")),
        name="in_projection",
    )(*xs, norm_w, mod_l, mod_l, w_in_p)


def _conv3_tile(u, prev_row, next_row, w_ref, rows):
    ridx = lax.broadcasted_iota(jnp.int32, (SUBLANE, 1), 0)
    up = pltpu.roll(u, 1, axis=0)
    up = jnp.concatenate([jnp.where(ridx == 0, prev_row, up[0:SUBLANE]), up[SUBLANE:]], axis=0)
    dn = pltpu.roll(u, rows - 1, axis=0)
    dn = jnp.concatenate([dn[:rows - SUBLANE], jnp.where(ridx == SUBLANE - 1, next_row, dn[rows - SUBLANE:])],
                         axis=0)
    return up * w_ref[0:1, :] + u * w_ref[1:2, :] + dn * w_ref[2:3, :]


SSD_TQ = 256
SSD_CPT = SSD_TQ // CHUNK
HALO = 2 * SUBLANE
N_SEQ = BATCH + DEC_BATCH
BC_WIDTH = 2 * SSM_GROUPS * SSM_STATE


def _tile_neighbours(i, tile_rows):
    n_ctx_tiles = N_CTX_TOK // tile_rows
    tiles_per_lat = DEC_SEQ // tile_rows
    t_in_seq = (i - n_ctx_tiles) % tiles_per_lat
    is_lat = i >= n_ctx_tiles
    has_prev = jnp.logical_and(is_lat, t_in_seq > 0).astype(F32)
    has_next = jnp.logical_and(is_lat, t_in_seq < tiles_per_lat - 1).astype(F32)
    return has_prev, has_next


def _ssd_conv_kernel(x_ref, xp_ref, xn_ref, bc_ref, bcp_ref, bcn_ref,
                     cwx_ref, cbx_ref, cwb_ref, cbb_ref, xc_ref, bcc_ref):
    hp, hn = _tile_neighbours(pl.program_id(0), SSD_TQ)

    def conv_silu(u_ref, up_ref, un_ref, w_ref, b_ref):
        prev_row = up_ref[HALO - 1:HALO, :].astype(F32) * hp
        next_row = un_ref[0:1, :].astype(F32) * hn
        return _silu(_conv3_tile(u_ref[...].astype(F32), prev_row, next_row, w_ref, SSD_TQ) + b_ref[...])

    xc_ref[...] = conv_silu(x_ref, xp_ref, xn_ref, cwx_ref, cbx_ref)
    bcc_ref[...] = conv_silu(bc_ref, bcp_ref, bcn_ref, cwb_ref, cbb_ref)


def _ssd_conv(p, conv_wx, conv_bx, conv_wbc, conv_bbc):
    halo_per_tile = SSD_TQ // HALO
    n_halo_blocks = N_TOK // HALO
    cx, cbc = OFF_SX // SSM_INNER, OFF_BC // BC_WIDTH

    def prev_map(col):
        return lambda i: (jnp.maximum(i * halo_per_tile - 1, 0), col)

    def next_map(col):
        return lambda i: (jnp.minimum((i + 1) * halo_per_tile, n_halo_blocks - 1), col)

    def const2(i):
        return (0, 0)

    return pl.pallas_call(
        _ssd_conv_kernel,
        grid=(N_TOK // SSD_TQ,),
        in_specs=[
            pl.BlockSpec((SSD_TQ, SSM_INNER), lambda i: (i, cx)),
            pl.BlockSpec((HALO, SSM_INNER), prev_map(cx)),
            pl.BlockSpec((HALO, SSM_INNER), next_map(cx)),
            pl.BlockSpec((SSD_TQ, BC_WIDTH), lambda i: (i, cbc)),
            pl.BlockSpec((HALO, BC_WIDTH), prev_map(cbc)),
            pl.BlockSpec((HALO, BC_WIDTH), next_map(cbc)),
            pl.BlockSpec((3, SSM_INNER), const2),
            pl.BlockSpec((1, SSM_INNER), const2),
            pl.BlockSpec((3, BC_WIDTH), const2),
            pl.BlockSpec((1, BC_WIDTH), const2),
        ],
        out_specs=[
            pl.BlockSpec((SSD_TQ, SSM_INNER), lambda i: (i, 0)),
            pl.BlockSpec((SSD_TQ, BC_WIDTH), lambda i: (i, 0)),
        ],
        out_shape=[
            jax.ShapeDtypeStruct((N_TOK, SSM_INNER), F32),
            jax.ShapeDtypeStruct((N_TOK, BC_WIDTH), F32),
        ],
        compiler_params=_cparams(("arbitrary",)),
        name="ssd_conv",
    )(p, p, p, p, p, p, conv_wx, conv_bx, conv_wbc, conv_bbc)


def _ssd_tables():
    blk_f, blk_b, seq, first, last = [], [], [], [], []
    for s in range(N_SEQ):
        if s < BATCH:
            base, nt = s * SEQ // SSD_TQ, SEQ // SSD_TQ
        else:
            base, nt = (N_CTX_TOK + (s - BATCH) * DEC_SEQ) // SSD_TQ, DEC_SEQ // SSD_TQ
        for k in range(nt):
            blk_f.append(base + k)
            blk_b.append(base + nt - 1 - k)
            seq.append(s)
            first.append(int(k == 0))
            last.append(int(k == nt - 1))
    return [np.asarray(a, np.int32) for a in (blk_f, blk_b, seq, first, last)]


def _split3(a):
    a1 = a.astype(BF16)
    r1 = a - a1.astype(F32)
    a2 = r1.astype(BF16)
    a3 = (r1 - a2.astype(F32)).astype(BF16)
    return a1, a2, a3


def _dot3(lhs_bf16, a):
    a1, a2, a3 = _split3(a)
    return (jnp.dot(lhs_bf16, a1, preferred_element_type=F32)
            + jnp.dot(lhs_bf16, a2, preferred_element_type=F32)
            + jnp.dot(lhs_bf16, a3, preferred_element_type=F32))


def _softplus(x):
    return jnp.maximum(x, 0.0) + jnp.log1p(jnp.exp(-jnp.abs(x)))


def _ssd_kernel(blkf_t, blkb_t, seq_t, first_t, last_t,
                xf_ref, bcf_ref, dtf_ref, xb_ref, bcb_ref, dtb_ref,
                alog_ref, dtbias_ref, dsk_ref, h0f_ref, h0b_ref,
                yf_ref, yb_ref, finf_ref, finb_ref,
                stf_ref, stb_ref):
    s = pl.program_id(0)
    seq = seq_t[s]
    dirs = ((xf_ref, bcf_ref, dtf_ref, stf_ref, yf_ref), (xb_ref, bcb_ref, dtb_ref, stb_ref, yb_ref))

    @pl.when(jnp.logical_and(first_t[s] == 1, seq < BATCH))
    def _():
        stf_ref[...] = jnp.zeros_like(stf_ref)
        stb_ref[...] = jnp.zeros_like(stb_ref)

    @pl.when(jnp.logical_and(first_t[s] == 1, seq >= BATCH))
    def _():
        stf_ref[...] = h0f_ref[...].T
        stb_ref[...] = h0b_ref[...].T

    ii = lax.broadcasted_iota(jnp.int32, (CHUNK, CHUNK), 0)
    jj = lax.broadcasted_iota(jnp.int32, (CHUNK, CHUNK), 1)
    masks = (jj <= ii, jj >= ii)
    masks_b = tuple(jnp.where(m, 1.0, 0.0).astype(BF16) for m in masks)
    masks_neg = tuple(jnp.where(m, 0.0, NEG_BIG) for m in masks)
    lo = jj < SSM_HEAD_DIM

    e_r = lax.broadcasted_iota(jnp.int32, (LANE, SSM_INNER), 0)
    e_c = lax.broadcasted_iota(jnp.int32, (LANE, SSM_INNER), 1)
    expand = jnp.where(jnp.right_shift(e_c, 6) == e_r, 1.0, 0.0).astype(BF16)

    def chunk_setup(d, c):
        _, _, dt_ref, _, _ = dirs[d]
        rows = slice(c * CHUNK, (c + 1) * CHUNK)
        a_row = -jnp.exp(alog_ref[d])
        dt = _softplus(dt_ref[rows, :] + dtbias_ref[d])
        cum = _dot3(masks_b[d], dt * a_row)
        cum_t = cum.T
        dt_t = dt.T
        end = CHUNK - 1 if d == 0 else 0
        tot_row = cum[end:end + 1, :]
        tot_col = cum_t[:, end:end + 1]
        return dict(
            rows=rows, cum=cum, cum_t=cum_t, dt_t=dt_t,
            w_rows=jnp.exp(tot_col - cum_t) * dt_t,
            sdec=_dot3_rows(jnp.exp(tot_row), expand))

    def group_setup(d, cs, g):
        _, bc_ref, _, _, _ = dirs[d]
        b_g = bc_ref[cs["rows"], g * SSM_STATE:(g + 1) * SSM_STATE]
        c_g = bc_ref[cs["rows"], (SSM_GROUPS + g) * SSM_STATE:(SSM_GROUPS + g + 1) * SSM_STATE].astype(BF16)
        b_gt = b_g.T
        cb = jnp.dot(c_g, b_gt.astype(BF16), preferred_element_type=F32)
        return c_g, b_gt, cb

    def pair_step(d, cs, gs, kp):
        x_ref, _, _, st_ref, y_ref = dirs[d]
        c_g, b_gt, cb = gs
        cum, cum_t, dt_t = cs["cum"], cs["cum_t"], cs["dt_t"]
        h_a, h_b = 2 * kp, 2 * kp + 1
        ls = slice(kp * LANE, (kp + 1) * LANE)
        x_pair = x_ref[cs["rows"], ls]
        rhs = jnp.concatenate([jnp.where(lo, x_pair, 0.0).astype(BF16),
                               jnp.where(lo, 0.0, x_pair).astype(BF16)], axis=0)

        def head_lhs(h):
            col = jnp.broadcast_to(cum[:, h:h + 1], (CHUNK, CHUNK))
            seg = col - cum_t[h:h + 1, :]
            dec = jnp.exp(seg + masks_neg[d])
            w_intra = dec * cb * dt_t[h:h + 1, :]
            w_state = b_gt * cs["w_rows"][h:h + 1, :]
            return w_intra.astype(BF16), w_state.astype(BF16), col

        wi_a, ws_a, col_a = head_lhs(h_a)
        wi_b, ws_b, col_b = head_lhs(h_b)
        lhs = jnp.concatenate([jnp.concatenate([wi_a, wi_b], axis=1),
                               jnp.concatenate([ws_a, ws_b], axis=1)], axis=0)
        both = jnp.dot(lhs, rhs, preferred_element_type=F32)
        y_diag = both[0:CHUNK, :]
        d_state = both[CHUNK:2 * CHUNK, :]

        h_pair = st_ref[:, ls]
        y_off = jnp.dot(c_g, h_pair.astype(BF16), preferred_element_type=F32)
        e_pair = jnp.exp(jnp.where(lo, col_a, col_b))
        y_ref[cs["rows"], ls] = (y_diag + y_off * e_pair + x_pair * dsk_ref[d, :, ls]).astype(y_ref.dtype)
        st_ref[:, ls] = h_pair * cs["sdec"][:, ls] + d_state

    pairs_per_group = SSM_HEADS // SSM_GROUPS // 2
    for k in range(SSD_CPT):
        cs = (chunk_setup(0, k), chunk_setup(1, SSD_CPT - 1 - k))
        for g in range(SSM_GROUPS):
            gs = (group_setup(0, cs[0], g), group_setup(1, cs[1], g))
            for kp in range(g * pairs_per_group, (g + 1) * pairs_per_group):
                pair_step(0, cs[0], gs[0], kp)
                pair_step(1, cs[1], gs[1], kp)

    @pl.when(jnp.logical_and(last_t[s] == 1, seq < BATCH))
    def _():
        finf_ref[...] = stf_ref[...].T
        finb_ref[...] = stb_ref[...].T


def _dot3_rows(row, rhs_bf16):
    r8 = jnp.broadcast_to(row, (SUBLANE, row.shape[1]))
    r1, r2, r3 = _split3(r8)
    out = (jnp.dot(r1, rhs_bf16, preferred_element_type=F32)
           + jnp.dot(r2, rhs_bf16, preferred_element_type=F32)
           + jnp.dot(r3, rhs_bf16, preferred_element_type=F32))
    return out[0:1, :]


def _ssd(ps, xc, bcc, layer, alog, dtb, dskip, h0f, h0b):
    tables = [jnp.asarray(t) for t in _ssd_tables()]
    n_steps = int(tables[0].shape[0])
    cdt = (OFF_DT - NP_MAIN) // LANE

    def fwd_tile(col):
        return lambda s, blkf, *_: (blkf[s], col)

    def bwd_tile(col):
        return lambda s, blkf, blkb, *_: (blkb[s], col)

    def h0_map(s, blkf, blkb, seq, *_):
        return (jnp.maximum(seq[s] - BATCH, 0), layer, 0, 0)

    def fin_map(s, blkf, blkb, seq, *_):
        return (jnp.minimum(seq[s], BATCH - 1), 0, 0)

    def const3(s, *_):
        return (0, 0, 0)

    grid_spec = pltpu.PrefetchScalarGridSpec(
        num_scalar_prefetch=5,
        grid=(n_steps,),
        in_specs=[
            pl.BlockSpec((SSD_TQ, SSM_INNER), fwd_tile(0)),
            pl.BlockSpec((SSD_TQ, BC_WIDTH), fwd_tile(0)),
            pl.BlockSpec((SSD_TQ, LANE), fwd_tile(cdt)),
            pl.BlockSpec((SSD_TQ, SSM_INNER), bwd_tile(0)),
            pl.BlockSpec((SSD_TQ, BC_WIDTH), bwd_tile(0)),
            pl.BlockSpec((SSD_TQ, LANE), bwd_tile(cdt)),
            pl.BlockSpec((2, 1, LANE), const3),
            pl.BlockSpec((2, 1, LANE), const3),
            pl.BlockSpec((2, 1, SSM_INNER), const3),
            pl.BlockSpec((None, None, SSM_INNER, SSM_STATE), h0_map),
            pl.BlockSpec((None, None, SSM_INNER, SSM_STATE), h0_map),
        ],
        out_specs=[
            pl.BlockSpec((SSD_TQ, SSM_INNER), fwd_tile(0)),
            pl.BlockSpec((SSD_TQ, SSM_INNER), bwd_tile(0)),
            pl.BlockSpec((None, SSM_INNER, SSM_STATE), fin_map),
            pl.BlockSpec((None, SSM_INNER, SSM_STATE), fin_map),
        ],
        scratch_shapes=[
            pltpu.VMEM((SSM_STATE, SSM_INNER), F32),
            pltpu.VMEM((SSM_STATE, SSM_INNER), F32),
        ],
    )
    return pl.pallas_call(
        _ssd_kernel,
        grid_spec=grid_spec,
        out_shape=[
            jax.ShapeDtypeStruct((N_TOK, SSM_INNER), BF16),
            jax.ShapeDtypeStruct((N_TOK, SSM_INNER), BF16),
            jax.ShapeDtypeStruct((BATCH, SSM_INNER, SSM_STATE), F32),
            jax.ShapeDtypeStruct((BATCH, SSM_INNER, SSM_STATE), F32),
        ],
        compiler_params=_cparams(("arbitrary",)),
        name="ssd_scan",
    )(*tables, xc, bcc, ps, xc, bcc, ps, alog, dtb, dskip, h0f, h0b)


QK_PAD = 128
N_QROWS = MLA_HEADS * QK_PAD
N_VROWS = MLA_HEADS * V_DIM
NT_DIMS = (((1,), (1,)), ((), ()))
Q_PRESCALE = (1.0 / math.sqrt(QK_DIM)) * math.log2(math.e)


def _mla_prep_kernel(*refs, tm, normalize, with_q):
    if with_q:
        (cq_ref, ckv_ref, kr_ref, krs_ref, cos_ref, sin_ref, cost_ref, sint_ref,
         qnw_ref, kvnw_ref, wq_ref, wk_ref, wv_ref,
         qt_ref, k_ref, vt_ref, ckvn_ref) = refs
    else:
        (ckv_ref, kr_ref, krs_ref, cos_ref, sin_ref, kvnw_ref, wk_ref, wv_ref,
         k_ref, vt_ref) = refs

    ckv = ckv_ref[...]
    if normalize:
        ckv = _rms(ckv, kvnw_ref[...])
        ckvn_ref[...] = ckv
    ckv_b = ckv.astype(BF16)
    kn = jnp.dot(ckv_b, wk_ref[...], preferred_element_type=F32)
    kr = kr_ref[...] * cos_ref[...] + krs_ref[...] * sin_ref[...]
    for h in range(MLA_HEADS):
        hs = slice(h * QK_PAD, (h + 1) * QK_PAD)
        k_ref[:, hs] = (kn[:, hs] + kr).astype(BF16)
    vt_ref[...] = lax.dot_general(wv_ref[...], ckv_b, NT_DIMS, preferred_element_type=F32).astype(BF16)

    if with_q:
        cqn = _rms(cq_ref[...], qnw_ref[...]).astype(BF16)
        qq = lax.dot_general(wq_ref[...], cqn, NT_DIMS, preferred_element_type=F32)
        for h in range(MLA_HEADS):
            q_h = qq[h * QK_PAD:(h + 1) * QK_PAD, :]
            qs_h = qq[N_QROWS + h * QK_PAD:N_QROWS + (h + 1) * QK_PAD, :]
            q_rot = q_h * cost_ref[...] + qs_h * sint_ref[...]
            qt_ref[h * QK_PAD:(h + 1) * QK_PAD, :] = (q_rot * Q_PRESCALE).astype(BF16)


def _mla_prep_tokens(p, row0, n_rows, tm, tab_map, cos, sin, cos_t, sin_t, qnw, kvnw, wq_t, wk, wv_t):
    b0 = row0 // tm
    nt = n_rows // tm
    kernel = functools.partial(_mla_prep_kernel, tm=tm, normalize=True, with_q=True)

    def pcol(width, off):
        return pl.BlockSpec((tm, width), lambda i: (b0 + i, (off - NP_MAIN) // width))

    def full(a):
        return pl.BlockSpec(a.shape, lambda i: (0,) * a.ndim)

    return pl.pallas_call(
        kernel,
        grid=(nt,),
        in_specs=[
            pcol(Q_LORA, OFF_CQ), pcol(KV_LORA, OFF_CKV), pcol(LANE, OFF_KR), pcol(LANE, OFF_KRS),
            pl.BlockSpec((tm, LANE), lambda i: (tab_map(i), 0)),
            pl.BlockSpec((tm, LANE), lambda i: (tab_map(i), 0)),
            pl.BlockSpec((LANE, tm), lambda i: (0, tab_map(i))),
            pl.BlockSpec((LANE, tm), lambda i: (0, tab_map(i))),
            full(qnw), full(kvnw), full(wq_t), full(wk), full(wv_t),
        ],
        out_specs=[
            pl.BlockSpec((N_QROWS, tm), lambda i: (0, i)),
            pl.BlockSpec((None, tm, N_QROWS), lambda i: (i, 0, 0)),
            pl.BlockSpec((None, N_VROWS, tm), lambda i: (i, 0, 0)),
            pl.BlockSpec((tm, KV_LORA), lambda i: (i, 0)),
        ],
        out_shape=[
            jax.ShapeDtypeStruct((N_QROWS, n_rows), BF16),
            jax.ShapeDtypeStruct((nt, tm, N_QROWS), BF16),
            jax.ShapeDtypeStruct((nt, N_VROWS, tm), BF16),
            jax.ShapeDtypeStruct((n_rows, KV_LORA), F32),
        ],
        compiler_params=_cparams(("arbitrary",)),
        name="mla_prep",
    )(p, p, p, p, cos, sin, cos_t, sin_t, qnw, kvnw, wq_t, wk, wv_t)


def _mla_prep_cache(ckv, kr_pad, ones_tab, zeros_tab, kvnw, wk, wv_t, tm):
    n_rows = ckv.shape[0]
    nt = n_rows // tm
    kernel = functools.partial(_mla_prep_kernel, tm=tm, normalize=False, with_q=False)

    def full(a):
        return pl.BlockSpec(a.shape, lambda i: (0,) * a.ndim)

    return pl.pallas_call(
        kernel,
        grid=(nt,),
        in_specs=[
            pl.BlockSpec((tm, KV_LORA), lambda i: (i, 0)),
            pl.BlockSpec((tm, LANE), lambda i: (i, 0)),
            pl.BlockSpec((tm, LANE), lambda i: (i, 0)),
            pl.BlockSpec((tm, LANE), lambda i: (0, 0)),
            pl.BlockSpec((tm, LANE), lambda i: (0, 0)),
            full(kvnw), full(wk), full(wv_t),
        ],
        out_specs=[
            pl.BlockSpec((None, tm, N_QROWS), lambda i: (i, 0, 0)),
            pl.BlockSpec((None, N_VROWS, tm), lambda i: (i, 0, 0)),
        ],
        out_shape=[
            jax.ShapeDtypeStruct((nt, tm, N_QROWS), BF16),
            jax.ShapeDtypeStruct((nt, N_VROWS, tm), BF16),
        ],
        compiler_params=_cparams(("arbitrary",)),
        name="mla_prep_cache",
    )(ckv, kr_pad, kr_pad, ones_tab, zeros_tab, kvnw, wk, wv_t)


ATT_TQ = 256
ATT_SUB = 256


def _attn_kernel(*refs, n_kt, tq, tk, with_cache):
    if with_cache:
        qt_ref, kp_ref, vtp_ref, k_ref, vt_ref, o_ref, ot_ref, s_ref = refs
    else:
        qt_ref, k_ref, vt_ref, o_ref, ot_ref, s_ref = refs

    def k_tile(kt, h):
        cols = slice(h * QK_PAD, (h + 1) * QK_PAD)
        if with_cache:
            return kp_ref[:, cols] if kt == 0 else k_ref[kt - 1, :, cols]
        return k_ref[kt, :, cols]

    def vt_tile(kt, h):
        rows = slice(h * V_DIM, (h + 1) * V_DIM)
        if with_cache:
            return vtp_ref[rows, :] if kt == 0 else vt_ref[kt - 1, rows, :]
        return vt_ref[kt, rows, :]

    sub = min(ATT_SUB, tk)

    def scores_step(h, kt, m8):
        q_t = qt_ref[h * QK_PAD:(h + 1) * QK_PAD, :]
        k = k_tile(kt, h)
        for r in range(0, tk, sub):
            s = jnp.dot(k[r:r + sub, :], q_t, preferred_element_type=F32)
            s_ref[h % 2, kt, r:r + sub, :] = s
            m8 = jnp.maximum(m8, jnp.max(s.reshape(sub // SUBLANE, SUBLANE, tq), axis=0))
        return m8

    def probs_step(h, kt, m, l8, acc):
        v_t = vt_tile(kt, h)
        for r in range(0, tk, sub):
            pr = jnp.exp2(s_ref[h % 2, kt, r:r + sub, :] - m)
            l8 = l8 + jnp.sum(pr.reshape(sub // SUBLANE, SUBLANE, tq), axis=0)
            acc = acc + jnp.dot(v_t[:, r:r + sub], pr.astype(BF16), preferred_element_type=F32)
        return l8, acc

    m8_init = jnp.full((SUBLANE, tq), NEG_BIG, F32)
    m8 = m8_init
    for kt in range(n_kt):
        m8 = scores_step(0, kt, m8)
    for h in range(MLA_HEADS):
        m = jnp.max(m8, axis=0, keepdims=True)
        l8 = jnp.zeros((SUBLANE, tq), F32)
        acc = jnp.zeros((V_DIM, tq), F32)
        m8 = m8_init
        for kt in range(n_kt):
            l8, acc = probs_step(h, kt, m, l8, acc)
            if h + 1 < MLA_HEADS:
                m8 = scores_step(h + 1, kt, m8)
        ot_ref[h * V_DIM:(h + 1) * V_DIM, :] = acc / jnp.sum(l8, axis=0, keepdims=True)
    o_ref[...] = ot_ref[...].T.astype(o_ref.dtype)


def _attention(qt, k3, vt3, n_batch, lq, tk, cache=None):
    tq = min(ATT_TQ, lq)
    nq = lq // tq
    n_new = k3.shape[0] // n_batch
    k4 = k3.reshape(n_batch, n_new, tk, N_QROWS)
    v4 = vt3.reshape(n_batch, n_new, N_VROWS, tk)
    n_kt = n_new + (0 if cache is None else 1)
    kernel = functools.partial(_attn_kernel, n_kt=n_kt, tq=tq, tk=tk, with_cache=cache is not None)
    cache_specs = [] if cache is None else [
        pl.BlockSpec((None, tk, N_QROWS), lambda b, i: (b, 0, 0)),
        pl.BlockSpec((None, N_VROWS, tk), lambda b, i: (b, 0, 0)),
    ]
    return pl.pallas_call(
        kernel,
        grid=(n_batch, nq),
        in_specs=[pl.BlockSpec((N_QROWS, tq), lambda b, i: (0, b * nq + i))] + cache_specs + [
            pl.BlockSpec((None, n_new, tk, N_QROWS), lambda b, i: (b, 0, 0, 0)),
            pl.BlockSpec((None, n_new, N_VROWS, tk), lambda b, i: (b, 0, 0, 0)),
        ],
        out_specs=pl.BlockSpec((tq, N_VROWS), lambda b, i: (b * nq + i, 0)),
        out_shape=jax.ShapeDtypeStruct((n_batch * lq, N_VROWS), BF16),
        scratch_shapes=[pltpu.VMEM((N_VROWS, tq), F32), pltpu.VMEM((2, n_kt, tk, tq), F32)],
        compiler_params=_cparams(("arbitrary", "arbitrary")),
        name="attention",
    )(qt, *(() if cache is None else cache), k4, v4)


MG_TM = 256


def _merge_kernel(*refs, n_x):
    x_refs = refs[:n_x]
    zc_refs = refs[n_x:n_x + 2]
    (ax_ref, axp_ref, axn_ref, ac_ref, acp_ref, acn_ref, ab_ref,
     g_ref, z_ref, yf_ref, yb_ref, gm_ref,
     cw_ref, wa_ref, nw_ref, wb_ref, wc_ref, wo_ref, o_ref) = refs[n_x + 2:]
    i = pl.program_id(0)
    hp, hn = _tile_neighbours(i, MG_TM)
    all_rows = slice(0, MG_TM)

    def f32(v):
        return v.astype(F32)

    u = f32(ac_ref[...]) * f32(ax_ref[...])
    u_prev = f32(acp_ref[HALO - 1:HALO, :]) * f32(axp_ref[HALO - 1:HALO, :]) * hp
    u_next = f32(acn_ref[0:1, :]) * f32(axn_ref[0:1, :]) * hn
    za = f32(ab_ref[...]) * _conv3_tile(u, u_prev, u_next, cw_ref, MG_TM)
    y_a = jnp.dot(za.astype(BF16), wa_ref[...], preferred_element_type=F32)

    yb = (f32(yf_ref[...]) + f32(yb_ref[...])) * _silu(f32(z_ref[...]))
    zb = _rms(yb, nw_ref[...])
    y_b = jnp.dot(zb.astype(BF16), wb_ref[...], preferred_element_type=F32)

    y_c = jnp.dot(_stream_rows(zc_refs, i, MG_TM, all_rows), wc_ref[...], preferred_element_type=F32)

    merged = (jax.nn.sigmoid(f32(g_ref[:, 0:D_MODEL])) * y_a
              + jax.nn.sigmoid(f32(g_ref[:, D_MODEL:2 * D_MODEL])) * y_b
              + jax.nn.sigmoid(f32(g_ref[:, 2 * D_MODEL:3 * D_MODEL])) * y_c)
    o = jnp.dot(merged.astype(BF16), wo_ref[...], preferred_element_type=F32)
    o_ref[...] = _stream_rows(x_refs, i, MG_TM, all_rows) + gm_ref[...] * o


def _merge(p, y_f, y_b, zcs, xs, mod_l, conv_w, wa, nw, wb, wc, wo):
    halo_per_tile = MG_TM // HALO
    n_halo_blocks = N_TOK // HALO
    row = functools.partial(_mod_row, tile_rows=MG_TM)

    def pcol(width, off):
        return pl.BlockSpec((MG_TM, width), lambda i: (i, off // width))

    def pprev(off):
        return pl.BlockSpec((HALO, A_WIDTH), lambda i: (jnp.maximum(i * halo_per_tile - 1, 0), off // A_WIDTH))

    def pnext(off):
        return pl.BlockSpec((HALO, A_WIDTH),
                            lambda i: (jnp.minimum((i + 1) * halo_per_tile, n_halo_blocks - 1), off // A_WIDTH))

    def full(a):
        return pl.BlockSpec(a.shape, lambda i: (0,) * a.ndim)

    return pl.pallas_call(
        functools.partial(_merge_kernel, n_x=len(xs)),
        grid=(N_TOK // MG_TM,),
        in_specs=_stream_specs(xs, MG_TM, 1) + _stream_specs(zcs, MG_TM, 1) + [
            pcol(A_WIDTH, OFF_AX), pprev(OFF_AX), pnext(OFF_AX),
            pcol(A_WIDTH, OFF_AC), pprev(OFF_AC), pnext(OFF_AC),
            pcol(A_WIDTH, OFF_AB),
            pcol(3 * D_MODEL, OFF_G),
            pcol(SSM_INNER, OFF_Z),
            pl.BlockSpec((MG_TM, SSM_INNER), lambda i: (i, 0)),
            pl.BlockSpec((MG_TM, SSM_INNER), lambda i: (i, 0)),
            pl.BlockSpec((None, None, 1, D_MODEL), lambda i: (row(i), 2, 0, 0)),
            full(conv_w), full(wa), full(nw), full(wb), full(wc), full(wo),
        ],
        out_specs=pl.BlockSpec((MG_TM, D_MODEL), lambda i: (i, 0)),
        out_shape=jax.ShapeDtypeStruct((N_TOK, D_MODEL), F32),
        compiler_params=_cparams(("arbitrary",)),
        name="merge",
    )(*xs, *zcs, p, p, p, p, p, p, p, p, p, y_f, y_b, mod_l, conv_w, wa, nw, wb, wc, wo)


FF_TM = 512
FF_CHUNK = 256


def _ffn_kernel(x_ref, nw_ref, sh_ref, sc_ref, gm_ref, w1_ref, w3_ref, w2_ref, o_ref, h_ref, g_ref):
    _modulated_norm_to(h_ref, lambda rows: x_ref[rows, :], nw_ref, sc_ref, sh_ref, FF_TM)
    h = h_ref[...]
    for c in range(0, FF_DIM, FF_CHUNK):
        a = jnp.dot(h, w1_ref[:, c:c + FF_CHUNK], preferred_element_type=F32)
        b = jnp.dot(h, w3_ref[:, c:c + FF_CHUNK], preferred_element_type=F32)
        g_ref[:, c:c + FF_CHUNK] = (_silu(a) * b).astype(BF16)
    ff = jnp.dot(g_ref[...], w2_ref[...], preferred_element_type=F32)
    o_ref[...] = x_ref[...] + gm_ref[...] * ff


def _ffn(x, mod_l, norm_w, w1, w3, w2):
    row = functools.partial(_mod_row, tile_rows=FF_TM)
    return pl.pallas_call(
        _ffn_kernel,
        grid=(N_TOK // FF_TM,),
        in_specs=[
            pl.BlockSpec((FF_TM, D_MODEL), lambda i: (i, 0)),
            pl.BlockSpec((1, D_MODEL), lambda i: (0, 0)),
            pl.BlockSpec((None, None, 1, D_MODEL), lambda i: (row(i), 3, 0, 0)),
            pl.BlockSpec((None, None, 1, D_MODEL), lambda i: (row(i), 4, 0, 0)),
            pl.BlockSpec((None, None, 1, D_MODEL), lambda i: (row(i), 5, 0, 0)),
            pl.BlockSpec((D_MODEL, FF_DIM), lambda i: (0, 0)),
            pl.BlockSpec((D_MODEL, FF_DIM), lambda i: (0, 0)),
            pl.BlockSpec((FF_DIM, D_MODEL), lambda i: (0, 0)),
        ],
        out_specs=pl.BlockSpec((FF_TM, D_MODEL), lambda i: (i, 0)),
        out_shape=jax.ShapeDtypeStruct((N_TOK, D_MODEL), F32),
        scratch_shapes=[pltpu.VMEM((FF_TM, D_MODEL), BF16), pltpu.VMEM((FF_TM, FF_DIM), BF16)],
        compiler_params=_cparams(("arbitrary",)),
        name="ffn",
    )(x, norm_w, mod_l, mod_l, mod_l, w1, w3, w2)


FN_TM = 1024


def _final_norm_kernel(x_ref, w_ref, o_ref):
    o_ref[...] = _rms(x_ref[...], w_ref[...])


def _final_norm(x, w, row0, n_rows):
    b0 = row0 // FN_TM
    return pl.pallas_call(
        _final_norm_kernel,
        grid=(n_rows // FN_TM,),
        in_specs=[pl.BlockSpec((FN_TM, D_MODEL), lambda i: (b0 + i, 0)),
                  pl.BlockSpec((1, D_MODEL), lambda i: (0, 0))],
        out_specs=pl.BlockSpec((FN_TM, D_MODEL), lambda i: (i, 0)),
        out_shape=jax.ShapeDtypeStruct((n_rows, D_MODEL), F32),
        compiler_params=_cparams(("arbitrary",)),
        name="final_norm",
    )(x, w)


def _pad_in_weights(w_in):
    o = 0
    a_x = w_in[..., o:o + 512]; o += 512
    a_b = w_in[..., o:o + 512]; o += 512
    a_c = w_in[..., o:o + 512]; o += 512
    s_z = w_in[..., o:o + 1024]; o += 1024
    s_x = w_in[..., o:o + 1024]; o += 1024
    s_bc = w_in[..., o:o + 512]; o += 512
    s_dt = w_in[..., o:o + 16]; o += 16
    cq = w_in[..., o:o + 256]; o += 256
    ckv = w_in[..., o:o + 256]; o += 256
    kr = w_in[..., o:o + 32]; o += 32
    gates = w_in[..., o:o + 3072]; o += 3072

    def z(n):
        return jnp.zeros(w_in.shape[:-1] + (n,), w_in.dtype)

    kr_sw = jnp.concatenate([kr[..., 16:], kr[..., :16]], axis=-1)
    cols = [gates, s_z, s_x, a_x, a_b, a_c, s_bc, cq, ckv,
            s_dt, z(LANE - 16),
            z(64), kr, z(32),
            z(64), kr_sw, z(32),
            z(NP - OFF_KRS - LANE)]
    out = jnp.concatenate(cols, axis=-1).astype(BF16)
    assert out.shape[-1] == NP
    return out


def _q_weights_t(w_uq):
    w = w_uq.reshape(DEPTH, Q_LORA, MLA_HEADS, QK_DIM)
    nope, x1, x2 = w[..., :NOPE_DIM], w[..., NOPE_DIM:NOPE_DIM + 16], w[..., NOPE_DIM + 16:]
    z32 = jnp.zeros_like(w[..., :32])
    z64 = jnp.zeros_like(nope)
    q = jnp.concatenate([nope, x1, x2, z32], axis=-1).reshape(DEPTH, Q_LORA, N_QROWS)
    qs = jnp.concatenate([z64, x2, x1, z32], axis=-1).reshape(DEPTH, Q_LORA, N_QROWS)
    return jnp.swapaxes(jnp.concatenate([q, qs], axis=-1), 1, 2).astype(BF16)


def _kv_weights(w_ukv):
    w = w_ukv.reshape(DEPTH, KV_LORA, MLA_HEADS, NOPE_DIM + V_DIM)
    kn = jnp.concatenate([w[..., :NOPE_DIM], jnp.zeros_like(w[..., :QK_PAD - NOPE_DIM])], axis=-1)
    wk = kn.reshape(DEPTH, KV_LORA, N_QROWS).astype(BF16)
    wv_t = jnp.swapaxes(w[..., NOPE_DIM:].reshape(DEPTH, KV_LORA, N_VROWS), 1, 2).astype(BF16)
    return wk, wv_t


def _rope_tables(n_tokens, lead_rows):
    n_rows = n_tokens // GRID_W
    row = jnp.repeat(jnp.arange(n_rows, dtype=F32), GRID_W)
    col = jnp.tile(jnp.arange(GRID_W, dtype=F32), n_rows)
    pairs = ROPE_DIM // 4
    inv = ROPE_BASE ** (-jnp.arange(pairs, dtype=F32) / pairs)
    ang = jnp.concatenate([row[:, None] * inv, col[:, None] * inv], axis=-1)
    cos, sin = jnp.cos(ang), jnp.sin(ang)
    ones = jnp.ones((n_tokens, NOPE_DIM), F32)
    z32 = jnp.zeros((n_tokens, 32), F32)
    cos_l = jnp.concatenate([ones, cos, cos, z32], axis=-1)
    sin_l = jnp.concatenate([jnp.zeros_like(ones), -sin, sin, z32], axis=-1)
    ident_c = jnp.concatenate([jnp.ones((lead_rows, NOPE_DIM + ROPE_DIM), F32), jnp.zeros((lead_rows, 32), F32)], -1)
    ident_s = jnp.zeros((lead_rows, LANE), F32)
    return jnp.concatenate([ident_c, cos_l], axis=0), jnp.concatenate([ident_s, sin_l], axis=0)


PREP_TM_CTX = SEQ
PREP_TM_LAT = 512


def kernel(x_prompt, x_sample, c, cache_ckv, cache_krope, state_ssm_fwd, state_ssm_bwd, c_ctx, w_in, a_conv_w, w_a_out, ssm_conv_w, ssm_conv_b, ssm_a_log, ssm_dt_bias, ssm_d, ssm_norm_w, w_b_out, q_norm_w, w_uq, kv_norm_w, w_ukv, w_c_out, w_o, w_ada, b_ada, norm1_w, norm2_w, w_ff1, w_ff3, w_ff2, final_norm_w):
    w_in_p = _pad_in_weights(w_in)
    wq_t = _q_weights_t(w_uq)
    wk, wv_t = _kv_weights(w_ukv)
    wa, wb, wc, wo = (w.astype(BF16) for w in (w_a_out, w_b_out, w_c_out, w_o))
    w1, w3, w2 = (w.astype(BF16) for w in (w_ff1, w_ff3, w_ff2))

    cond = jnp.concatenate([c_ctx[None, :], c, jnp.zeros((N_MOD_ROWS - 1 - DEC_BATCH, D_MODEL), F32)], axis=0)
    mod = _modulation(cond, w_ada, b_ada).reshape(DEPTH, N_MOD_ROWS, 6, 1, D_MODEL)

    conv_wx = ssm_conv_w[..., :SSM_INNER]
    conv_wbc = ssm_conv_w[..., SSM_INNER:]
    conv_bx = ssm_conv_b[:, None, :SSM_INNER]
    conv_bbc = ssm_conv_b[:, None, SSM_INNER:]
    pad_h = ((0, 0), (0, 0), (0, 0), (0, LANE - SSM_HEADS))
    alog = jnp.pad(ssm_a_log[:, :, None, :], pad_h)
    dtb = jnp.pad(ssm_dt_bias[:, :, None, :], pad_h)
    dskip = jnp.repeat(ssm_d, SSM_HEAD_DIM, axis=-1)[:, :, None, :]
    h0f = state_ssm_fwd.reshape(DEC_BATCH, DEPTH, SSM_INNER, SSM_STATE)
    h0b = state_ssm_bwd.reshape(DEC_BATCH, DEPTH, SSM_INNER, SSM_STATE)

    cos_c, sin_c = _rope_tables(DEC_SEQ, PREP_TM_CTX)
    cos_l, sin_l = cos_c[PREP_TM_CTX:], sin_c[PREP_TM_CTX:]
    ident_cos, ident_sin = cos_c[:PREP_TM_CTX], sin_c[:PREP_TM_CTX]
    ones_tab = jnp.concatenate([ident_cos, ident_cos], axis=0)
    zeros_tab = jnp.zeros_like(ones_tab)
    cache_kr_pad = jnp.pad(cache_krope, ((0, 0), (0, 0), (0, 0), (64, 32)))

    xs = (x_prompt.reshape(N_CTX_TOK, D_MODEL), x_sample.reshape(N_LAT_TOK, D_MODEL))

    lat_tiles_per_seq = DEC_SEQ // PREP_TM_LAT
    new_ckv, new_kr, new_f, new_b = [], [], [], []
    for l in range(DEPTH):
        p, ps = _in_projection(xs, mod[l], norm1_w[l][None, :], w_in_p[l])

        xc, bcc = _ssd_conv(p, conv_wx[l], conv_bx[l], conv_wbc[l], conv_bbc[l])
        y_f, y_b, fin_f, fin_b = _ssd(ps, xc, bcc, l, alog[l], dtb[l], dskip[l], h0f, h0b)
        new_f.append(fin_f)
        new_b.append(fin_b)

        qnw, kvnw = q_norm_w[l][None, :], kv_norm_w[l][None, :]
        qt_c, k_c, vt_c, ckv_c = _mla_prep_tokens(
            ps, 0, N_CTX_TOK, PREP_TM_CTX, lambda i: 0, ident_cos, ident_sin, ident_cos.T, ident_sin.T,
            qnw, kvnw, wq_t[l], wk[l], wv_t[l])
        qt_l, k_l, vt_l, _ = _mla_prep_tokens(
            ps, N_CTX_TOK, N_LAT_TOK, PREP_TM_LAT, lambda i: i % lat_tiles_per_seq, cos_l, sin_l, cos_l.T, sin_l.T,
            qnw, kvnw, wq_t[l], wk[l], wv_t[l])
        k_p, vt_p = _mla_prep_cache(cache_ckv[:, l].reshape(DEC_BATCH * PAST_LEN, KV_LORA),
                                    cache_kr_pad[:, l].reshape(DEC_BATCH * PAST_LEN, LANE),
                                    ones_tab, zeros_tab, kvnw, wk[l], wv_t[l], PAST_LEN)
        new_ckv.append(ckv_c.reshape(BATCH, SEQ, KV_LORA))
        kr0 = OFF_KR - NP_MAIN + 64
        new_kr.append(ps[:N_CTX_TOK, kr0:kr0 + ROPE_DIM].reshape(BATCH, SEQ, ROPE_DIM))

        zc_c = _attention(qt_c, k_c, vt_c, BATCH, SEQ, SEQ)
        zc_l = _attention(qt_l, k_l, vt_l, DEC_BATCH, DEC_SEQ, PREP_TM_LAT, cache=(k_p, vt_p))

        x = _merge(p, y_f, y_b, (zc_c, zc_l), xs, mod[l], a_conv_w[l], wa[l], ssm_norm_w[l][None, :],
                   wb[l], wc[l], wo[l])
        x = _ffn(x, mod[l], norm2_w[l][None, :], w1[l], w3[l], w2[l])
        xs = (x,)

    fw = final_norm_w[None, :]
    y_prompt = _final_norm(x, fw, 0, N_CTX_TOK).reshape(BATCH, SEQ, D_MODEL)
    y_sample = _final_norm(x, fw, N_CTX_TOK, N_LAT_TOK).reshape(DEC_BATCH, DEC_SEQ, D_MODEL)
    hshape = (BATCH, DEPTH, SSM_HEADS, SSM_HEAD_DIM, SSM_STATE)
    return (y_prompt, y_sample,
            jnp.stack(new_ckv, axis=1), jnp.stack(new_kr, axis=1),
            jnp.stack(new_f, axis=1).reshape(hshape), jnp.stack(new_b, axis=1).reshape(hshape))
```

```python
import functools
import math

import jax
import jax.numpy as jnp
import numpy as np
from jax import lax
from jax.experimental import pallas as pl
from jax.experimental.pallas import tpu as pltpu

F32 = jnp.float32
BF16 = jnp.bfloat16

D_MODEL = 1024
BATCH = 16
SEQ = 256
DEPTH = 4
DEC_BATCH = 4
DEC_SEQ = 4096
PAST_LEN = 512
GRID_W = 64
EPS = 1e-6
A_WIDTH = 512
SSM_INNER = 1024
SSM_HEAD_DIM = 64
SSM_HEADS = 16
SSM_GROUPS = 2
SSM_STATE = 128
CHUNK = 128
MLA_HEADS = 8
Q_LORA = 256
KV_LORA = 256
NOPE_DIM = 64
ROPE_DIM = 32
V_DIM = 64
QK_DIM = NOPE_DIM + ROPE_DIM
ROPE_BASE = 10000.0
FF_DIM = 2816

N_CTX_TOK = BATCH * SEQ
N_LAT_TOK = DEC_BATCH * DEC_SEQ
N_TOK = N_CTX_TOK + N_LAT_TOK
N_MOD_ROWS = 8

LANE = 128
SUBLANE = 8
VMEM_LIMIT = 56 * 1024 * 1024

OFF_G = 0
OFF_Z = 3072
OFF_SX = 4096
OFF_AX = 5120
OFF_AB = 5632
OFF_AC = 6144
OFF_BC = 6656
NP_MAIN = 7168
OFF_CQ = 7168
OFF_CKV = 7424
OFF_DT = 7680
OFF_KR = 7808
OFF_KRS = 7936
NP = 8192
N_MAIN_TILES = 7

NEG_BIG = -1e30


def _cparams(sem):
    return pltpu.CompilerParams(dimension_semantics=sem, vmem_limit_bytes=VMEM_LIMIT)


def _rms(x, w):
    ms = jnp.mean(x * x, axis=-1, keepdims=True)
    return x * lax.rsqrt(ms + EPS) * w


def _silu(x):
    return x * jax.nn.sigmoid(x)


def _layer_spec(w, layer, grid_rank):
    zeros = (0,) * (w.ndim - 1)
    imap = (lambda i: (layer,) + zeros) if grid_rank == 1 else (lambda i, j: (layer,) + zeros)
    return pl.BlockSpec((None,) + tuple(w.shape[1:]), imap)


def _mod_row(tile, tile_rows):
    n_ctx_tiles = N_CTX_TOK // tile_rows
    tiles_per_lat = DEC_SEQ // tile_rows
    return jnp.where(tile < n_ctx_tiles, 0, 1 + (tile - n_ctx_tiles) // tiles_per_lat)


MOD_TN = 1536


def _mod_kernel(c_ref, w_ref, b_ref, o_ref):
    c = c_ref[...]
    s = _silu(c).astype(BF16)
    o_ref[...] = jnp.dot(s, w_ref[...].astype(BF16), preferred_element_type=F32) + b_ref[...]


def _modulation(cond, w_ada, b_ada):
    n_col = 6 * D_MODEL
    return pl.pallas_call(
        _mod_kernel,
        grid=(DEPTH, n_col // MOD_TN),
        in_specs=[
            pl.BlockSpec((N_MOD_ROWS, D_MODEL), lambda l, j: (0, 0)),
            pl.BlockSpec((None, D_MODEL, MOD_TN), lambda l, j: (l, 0, j)),
            pl.BlockSpec((None, 1, MOD_TN), lambda l, j: (l, 0, j)),
        ],
        out_specs=pl.BlockSpec((None, N_MOD_ROWS, MOD_TN), lambda l, j: (l, 0, j)),
        out_shape=jax.ShapeDtypeStruct((DEPTH, N_MOD_ROWS, n_col), F32),
        compiler_params=_cparams(("arbitrary", "arbitrary")),
        name="modulation",
    )(cond, w_ada, b_ada.reshape(DEPTH, 1, n_col))


IN_TM = 2048
IN_TN = 1024
NORM_ROWS = 256


def _stream_specs(xs, tm, grid_rank):
    def imap(f):
        return (lambda i: f(i)) if grid_rank == 1 else (lambda i, j: f(i))

    width = xs[0].shape[1]
    if len(xs) == 1:
        return [pl.BlockSpec((tm, width), imap(lambda i: (i, 0)))]
    n_ctx_tiles = N_CTX_TOK // tm
    return [pl.BlockSpec((tm, width), imap(lambda i: (jnp.minimum(i, n_ctx_tiles - 1), 0))),
            pl.BlockSpec((tm, width), imap(lambda i: (jnp.maximum(i - n_ctx_tiles, 0), 0)))]


def _stream_rows(x_refs, tile, tm, rows):
    if len(x_refs) == 1:
        return x_refs[0][rows, :]
    return jnp.where(tile < N_CTX_TOK // tm, x_refs[0][rows, :], x_refs[1][rows, :])


def _modulated_norm_to(h_ref, x_rows, nw_ref, sc_ref, sh_ref, rows):
    for r in range(0, rows, NORM_ROWS):
        x = x_rows(slice(r, r + NORM_ROWS))
        h = _rms(x, nw_ref[...]) * (1.0 + sc_ref[...]) + sh_ref[...]
        h_ref[r:r + NORM_ROWS, :] = h.astype(BF16)


def _inproj_kernel(*refs, n_x, tm):
    x_refs = refs[:n_x]
    nw_ref, sh_ref, sc_ref, w_ref, om_ref, os_ref, h_ref = refs[n_x:]
    i = pl.program_id(0)
    j = pl.program_id(1)

    @pl.when(j == 0)
    def _():
        _modulated_norm_to(h_ref, lambda rows: _stream_rows(x_refs, i, tm, rows),
                           nw_ref, sc_ref, sh_ref, tm)

    @pl.when(j < N_MAIN_TILES)
    def _():
        om_ref[...] = jnp.dot(h_ref[...], w_ref[...], preferred_element_type=F32).astype(BF16)

    @pl.when(j == N_MAIN_TILES)
    def _():
        os_ref[...] = jnp.dot(h_ref[...], w_ref[...], preferred_element_type=F32)


def _in_projection(xs, mod_l, norm_w, w_in_p, layer):
    tm = IN_TM if len(xs) == 1 else IN_TM // 2
    row = functools.partial(_mod_row, tile_rows=tm)
    return pl.pallas_call(
        functools.partial(_inproj_kernel, n_x=len(xs), tm=tm),
        grid=(N_TOK // tm, NP // IN_TN),
        in_specs=_stream_specs(xs, tm, 2) + [
            pl.BlockSpec((1, D_MODEL), lambda i, j: (0, 0)),
            pl.BlockSpec((None, None, 1, D_MODEL), lambda i, j: (row(i), 0, 0, 0)),
            pl.BlockSpec((None, None, 1, D_MODEL), lambda i, j: (row(i), 1, 0, 0)),
            pl.BlockSpec((None, D_MODEL, IN_TN), lambda i, j: (layer, 0, j)),
        ],
        out_specs=[
            pl.BlockSpec((tm, IN_TN), lambda i, j: (i, jnp.minimum(j, N_MAIN_TILES - 1))),
            pl.BlockSpec((tm, IN_TN), lambda i, j: (i, 0)),
        ],
        out_shape=[
            jax.ShapeDtypeStruct((N_TOK, NP_MAIN), BF16),
            jax.ShapeDtypeStruct((N_TOK, IN_TN), F32),
        ],
        scratch_shapes=[pltpu.VMEM((tm, D_MODEL), BF16)],
        compiler_params=_cparams(("arbitrary", "arbitrary")),
        name="in_projection",
    )(*xs, norm_w, mod_l, mod_l, w_in_p)


def _conv3_tile(u, prev_row, next_row, w_ref, rows):
    ridx = lax.broadcasted_iota(jnp.int32, (SUBLANE, 1), 0)
    up = pltpu.roll(u, 1, axis=0)
    up = jnp.concatenate([jnp.where(ridx == 0, prev_row, up[0:SUBLANE]), up[SUBLANE:]], axis=0)
    dn = pltpu.roll(u, rows - 1, axis=0)
    dn = jnp.concatenate([dn[:rows - SUBLANE], jnp.where(ridx == SUBLANE - 1, next_row, dn[rows - SUBLANE:])],
                         axis=0)
    return up * w_ref[0:1, :] + u * w_ref[1:2, :] + dn * w_ref[2:3, :]


SSD_TQ = 256
SSD_CPT = SSD_TQ // CHUNK
HALO = 2 * SUBLANE
N_SEQ = BATCH + DEC_BATCH
BC_WIDTH = 2 * SSM_GROUPS * SSM_STATE


def _tile_neighbours(i, tile_rows):
    n_ctx_tiles = N_CTX_TOK // tile_rows
    tiles_per_lat = DEC_SEQ // tile_rows
    t_in_seq = (i - n_ctx_tiles) % tiles_per_lat
    is_lat = i >= n_ctx_tiles
    has_prev = jnp.logical_and(is_lat, t_in_seq > 0).astype(F32)
    has_next = jnp.logical_and(is_lat, t_in_seq < tiles_per_lat - 1).astype(F32)
    return has_prev, has_next


def _ssd_conv_kernel(x_ref, xp_ref, xn_ref, bc_ref, bcp_ref, bcn_ref,
                     cwx_ref, cbx_ref, cwb_ref, cbb_ref, xc_ref, bcc_ref):
    hp, hn = _tile_neighbours(pl.program_id(0), SSD_TQ)

    def conv_silu(u_ref, up_ref, un_ref, w_ref, b_ref):
        prev_row = up_ref[HALO - 1:HALO, :].astype(F32) * hp
        next_row = un_ref[0:1, :].astype(F32) * hn
        return _silu(_conv3_tile(u_ref[...].astype(F32), prev_row, next_row, w_ref, SSD_TQ) + b_ref[...])

    xc_ref[...] = conv_silu(x_ref, xp_ref, xn_ref, cwx_ref, cbx_ref)
    bcc_ref[...] = conv_silu(bc_ref, bcp_ref, bcn_ref, cwb_ref, cbb_ref)


def _ssd_conv(p, conv_wx, conv_bx, conv_wbc, conv_bbc):
    halo_per_tile = SSD_TQ // HALO
    n_halo_blocks = N_TOK // HALO
    cx, cbc = OFF_SX // SSM_INNER, OFF_BC // BC_WIDTH

    def prev_map(col):
        return lambda i: (jnp.maximum(i * halo_per_tile - 1, 0), col)

    def next_map(col):
        return lambda i: (jnp.minimum((i + 1) * halo_per_tile, n_halo_blocks - 1), col)

    def const2(i):
        return (0, 0)

    return pl.pallas_call(
        _ssd_conv_kernel,
        grid=(N_TOK // SSD_TQ,),
        in_specs=[
            pl.BlockSpec((SSD_TQ, SSM_INNER), lambda i: (i, cx)),
            pl.BlockSpec((HALO, SSM_INNER), prev_map(cx)),
            pl.BlockSpec((HALO, SSM_INNER), next_map(cx)),
            pl.BlockSpec((SSD_TQ, BC_WIDTH), lambda i: (i, cbc)),
            pl.BlockSpec((HALO, BC_WIDTH), prev_map(cbc)),
            pl.BlockSpec((HALO, BC_WIDTH), next_map(cbc)),
            pl.BlockSpec((3, SSM_INNER), const2),
            pl.BlockSpec((1, SSM_INNER), const2),
            pl.BlockSpec((3, BC_WIDTH), const2),
            pl.BlockSpec((1, BC_WIDTH), const2),
        ],
        out_specs=[
            pl.BlockSpec((SSD_TQ, SSM_INNER), lambda i: (i, 0)),
            pl.BlockSpec((SSD_TQ, BC_WIDTH), lambda i: (i, 0)),
        ],
        out_shape=[
            jax.ShapeDtypeStruct((N_TOK, SSM_INNER), F32),
            jax.ShapeDtypeStruct((N_TOK, BC_WIDTH), F32),
        ],
        compiler_params=_cparams(("arbitrary",)),
        name="ssd_conv",
    )(p, p, p, p, p, p, conv_wx, conv_bx, conv_wbc, conv_bbc)


def _ssd_tables():
    blk_f, blk_b, seq, first, last = [], [], [], [], []
    for s in range(N_SEQ):
        if s < BATCH:
            base, nt = s * SEQ // SSD_TQ, SEQ // SSD_TQ
        else:
            base, nt = (N_CTX_TOK + (s - BATCH) * DEC_SEQ) // SSD_TQ, DEC_SEQ // SSD_TQ
        for k in range(nt):
            blk_f.append(base + k)
            blk_b.append(base + nt - 1 - k)
            seq.append(s)
            first.append(int(k == 0))
            last.append(int(k == nt - 1))
    return [np.asarray(a, np.int32) for a in (blk_f, blk_b, seq, first, last)]


def _split3(a):
    a1 = a.astype(BF16)
    r1 = a - a1.astype(F32)
    a2 = r1.astype(BF16)
    a3 = (r1 - a2.astype(F32)).astype(BF16)
    return a1, a2, a3


def _dot3(lhs_bf16, a):
    a1, a2, a3 = _split3(a)
    return (jnp.dot(lhs_bf16, a1, preferred_element_type=F32)
            + jnp.dot(lhs_bf16, a2, preferred_element_type=F32)
            + jnp.dot(lhs_bf16, a3, preferred_element_type=F32))


def _softplus(x):
    return jnp.maximum(x, 0.0) + jnp.log1p(jnp.exp(-jnp.abs(x)))


def _ssd_kernel(blkf_t, blkb_t, seq_t, first_t, last_t,
                xf_ref, bcf_ref, dtf_ref, xb_ref, bcb_ref, dtb_ref,
                alog_ref, dtbias_ref, dsk_ref, h0f_ref, h0b_ref,
                yf_ref, yb_ref, finf_ref, finb_ref,
                stf_ref, stb_ref):
    s = pl.program_id(0)
    seq = seq_t[s]
    dirs = ((xf_ref, bcf_ref, dtf_ref, stf_ref, yf_ref), (xb_ref, bcb_ref, dtb_ref, stb_ref, yb_ref))

    @pl.when(jnp.logical_and(first_t[s] == 1, seq < BATCH))
    def _():
        stf_ref[...] = jnp.zeros_like(stf_ref)
        stb_ref[...] = jnp.zeros_like(stb_ref)

    @pl.when(jnp.logical_and(first_t[s] == 1, seq >= BATCH))
    def _():
        stf_ref[...] = h0f_ref[...].T
        stb_ref[...] = h0b_ref[...].T

    ii = lax.broadcasted_iota(jnp.int32, (CHUNK, CHUNK), 0)
    jj = lax.broadcasted_iota(jnp.int32, (CHUNK, CHUNK), 1)
    masks = (jj <= ii, jj >= ii)
    masks_b = tuple(jnp.where(m, 1.0, 0.0).astype(BF16) for m in masks)
    masks_neg = tuple(jnp.where(m, 0.0, NEG_BIG) for m in masks)
    lo = jj < SSM_HEAD_DIM

    e_r = lax.broadcasted_iota(jnp.int32, (LANE, SSM_INNER), 0)
    e_c = lax.broadcasted_iota(jnp.int32, (LANE, SSM_INNER), 1)
    expand = jnp.where(jnp.right_shift(e_c, 6) == e_r, 1.0, 0.0).astype(BF16)

    def chunk_setup(d, c):
        _, _, dt_ref, _, _ = dirs[d]
        rows = slice(c * CHUNK, (c + 1) * CHUNK)
        a_row = -jnp.exp(alog_ref[d])
        dt = _softplus(dt_ref[rows, :] + dtbias_ref[d])
        cum = _dot3(masks_b[d], dt * a_row)
        cum_t = cum.T
        dt_t = dt.T
        end = CHUNK - 1 if d == 0 else 0
        tot_row = cum[end:end + 1, :]
        tot_col = cum_t[:, end:end + 1]
        return dict(
            rows=rows, cum=cum, cum_t=cum_t, dt_t=dt_t,
            w_rows=jnp.exp(tot_col - cum_t) * dt_t,
            sdec=_dot3_rows(jnp.exp(tot_row), expand))

    def group_setup(d, cs, g):
        _, bc_ref, _, _, _ = dirs[d]
        b_g = bc_ref[cs["rows"], g * SSM_STATE:(g + 1) * SSM_STATE]
        c_g = bc_ref[cs["rows"], (SSM_GROUPS + g) * SSM_STATE:(SSM_GROUPS + g + 1) * SSM_STATE].astype(BF16)
        b_gt = b_g.T
        cb = jnp.dot(c_g, b_gt.astype(BF16), preferred_element_type=F32)
        return c_g, b_gt, cb

    def pair_step(d, cs, gs, kp):
        x_ref, _, _, st_ref, y_ref = dirs[d]
        c_g, b_gt, cb = gs
        cum, cum_t, dt_t = cs["cum"], cs["cum_t"], cs["dt_t"]
        h_a, h_b = 2 * kp, 2 * kp + 1
        ls = slice(kp * LANE, (kp + 1) * LANE)
        x_pair = x_ref[cs["rows"], ls]
        rhs = jnp.concatenate([jnp.where(lo, x_pair, 0.0).astype(BF16),
                               jnp.where(lo, 0.0, x_pair).astype(BF16)], axis=0)

        def head_lhs(h):
            col = jnp.broadcast_to(cum[:, h:h + 1], (CHUNK, CHUNK))
            seg = col - cum_t[h:h + 1, :]
            dec = jnp.exp(seg + masks_neg[d])
            w_intra = dec * cb * dt_t[h:h + 1, :]
            w_state = b_gt * cs["w_rows"][h:h + 1, :]
            return w_intra.astype(BF16), w_state.astype(BF16), col

        wi_a, ws_a, col_a = head_lhs(h_a)
        wi_b, ws_b, col_b = head_lhs(h_b)
        lhs = jnp.concatenate([jnp.concatenate([wi_a, wi_b], axis=1),
                               jnp.concatenate([ws_a, ws_b], axis=1)], axis=0)
        both = jnp.dot(lhs, rhs, preferred_element_type=F32)
        y_diag = both[0:CHUNK, :]
        d_state = both[CHUNK:2 * CHUNK, :]

        h_pair = st_ref[:, ls]
        y_off = jnp.dot(c_g, h_pair.astype(BF16), preferred_element_type=F32)
        e_pair = jnp.exp(jnp.where(lo, col_a, col_b))
        y_ref[cs["rows"], ls] = (y_diag + y_off * e_pair + x_pair * dsk_ref[d, :, ls]).astype(y_ref.dtype)
        st_ref[:, ls] = h_pair * cs["sdec"][:, ls] + d_state

    pairs_per_group = SSM_HEADS // SSM_GROUPS // 2
    for k in range(SSD_CPT):
        cs = (chunk_setup(0, k), chunk_setup(1, SSD_CPT - 1 - k))
        for g in range(SSM_GROUPS):
            gs = (group_setup(0, cs[0], g), group_setup(1, cs[1], g))
            for kp in range(g * pairs_per_group, (g + 1) * pairs_per_group):
                pair_step(0, cs[0], gs[0], kp)
                pair_step(1, cs[1], gs[1], kp)

    @pl.when(jnp.logical_and(last_t[s] == 1, seq < BATCH))
    def _():
        finf_ref[...] = stf_ref[...].T
        finb_ref[...] = stb_ref[...].T


def _dot3_rows(row, rhs_bf16):
    r8 = jnp.broadcast_to(row, (SUBLANE, row.shape[1]))
    r1, r2, r3 = _split3(r8)
    out = (jnp.dot(r1, rhs_bf16, preferred_element_type=F32)
           + jnp.dot(r2, rhs_bf16, preferred_element_type=F32)
           + jnp.dot(r3, rhs_bf16, preferred_element_type=F32))
    return out[0:1, :]


def _ssd(ps, xc, bcc, layer, alog, dtb, dskip, h0f, h0b):
    tables = [jnp.asarray(t) for t in _ssd_tables()]
    n_steps = int(tables[0].shape[0])
    cdt = (OFF_DT - NP_MAIN) // LANE

    def fwd_tile(col):
        return lambda s, blkf, *_: (blkf[s], col)

    def bwd_tile(col):
        return lambda s, blkf, blkb, *_: (blkb[s], col)

    def h0_map(s, blkf, blkb, seq, *_):
        return (jnp.maximum(seq[s] - BATCH, 0), layer, 0, 0)

    def fin_map(s, blkf, blkb, seq, *_):
        return (jnp.minimum(seq[s], BATCH - 1), 0, 0)

    def const3(s, *_):
        return (0, 0, 0)

    grid_spec = pltpu.PrefetchScalarGridSpec(
        num_scalar_prefetch=5,
        grid=(n_steps,),
        in_specs=[
            pl.BlockSpec((SSD_TQ, SSM_INNER), fwd_tile(0)),
            pl.BlockSpec((SSD_TQ, BC_WIDTH), fwd_tile(0)),
            pl.BlockSpec((SSD_TQ, LANE), fwd_tile(cdt)),
            pl.BlockSpec((SSD_TQ, SSM_INNER), bwd_tile(0)),
            pl.BlockSpec((SSD_TQ, BC_WIDTH), bwd_tile(0)),
            pl.BlockSpec((SSD_TQ, LANE), bwd_tile(cdt)),
            pl.BlockSpec((2, 1, LANE), const3),
            pl.BlockSpec((2, 1, LANE), const3),
            pl.BlockSpec((2, 1, SSM_INNER), const3),
            pl.BlockSpec((None, None, SSM_INNER, SSM_STATE), h0_map),
            pl.BlockSpec((None, None, SSM_INNER, SSM_STATE), h0_map),
        ],
        out_specs=[
            pl.BlockSpec((SSD_TQ, SSM_INNER), fwd_tile(0)),
            pl.BlockSpec((SSD_TQ, SSM_INNER), bwd_tile(0)),
            pl.BlockSpec((None, SSM_INNER, SSM_STATE), fin_map),
            pl.BlockSpec((None, SSM_INNER, SSM_STATE), fin_map),
        ],
        scratch_shapes=[
            pltpu.VMEM((SSM_STATE, SSM_INNER), F32),
            pltpu.VMEM((SSM_STATE, SSM_INNER), F32),
        ],
    )
    return pl.pallas_call(
        _ssd_kernel,
        grid_spec=grid_spec,
        out_shape=[
            jax.ShapeDtypeStruct((N_TOK, SSM_INNER), BF16),
            jax.ShapeDtypeStruct((N_TOK, SSM_INNER), BF16),
            jax.ShapeDtypeStruct((BATCH, SSM_INNER, SSM_STATE), F32),
            jax.ShapeDtypeStruct((BATCH, SSM_INNER, SSM_STATE), F32),
        ],
        compiler_params=_cparams(("arbitrary",)),
        name="ssd_scan",
    )(*tables, xc, bcc, ps, xc, bcc, ps, alog, dtb, dskip, h0f, h0b)


QK_PAD = 128
N_QROWS = MLA_HEADS * QK_PAD
N_VROWS = MLA_HEADS * V_DIM
NT_DIMS = (((1,), (1,)), ((), ()))
Q_PRESCALE = (1.0 / math.sqrt(QK_DIM)) * math.log2(math.e)


def _mla_prep_kernel(*refs, tm, normalize, with_q):
    if with_q:
        (cq_ref, ckv_ref, kr_ref, krs_ref, cos_ref, sin_ref, cost_ref, sint_ref,
         qnw_ref, kvnw_ref, wq_ref, wk_ref, wv_ref,
         qt_ref, k_ref, vt_ref, ckvn_ref) = refs
    else:
        (ckv_ref, kr_ref, krs_ref, cos_ref, sin_ref, kvnw_ref, wk_ref, wv_ref,
         k_ref, vt_ref) = refs

    ckv = ckv_ref[...]
    if normalize:
        ckv = _rms(ckv, kvnw_ref[...])
        ckvn_ref[...] = ckv
    ckv_b = ckv.astype(BF16)
    kn = jnp.dot(ckv_b, wk_ref[...], preferred_element_type=F32)
    kr = kr_ref[...] * cos_ref[...] + krs_ref[...] * sin_ref[...]
    for h in range(MLA_HEADS):
        hs = slice(h * QK_PAD, (h + 1) * QK_PAD)
        k_ref[:, hs] = (kn[:, hs] + kr).astype(BF16)
    vt_ref[...] = lax.dot_general(wv_ref[...], ckv_b, NT_DIMS, preferred_element_type=F32).astype(BF16)

    if with_q:
        cqn = _rms(cq_ref[...], qnw_ref[...]).astype(BF16)
        qq = lax.dot_general(wq_ref[...], cqn, NT_DIMS, preferred_element_type=F32)
        for h in range(MLA_HEADS):
            q_h = qq[h * QK_PAD:(h + 1) * QK_PAD, :]
            qs_h = qq[N_QROWS + h * QK_PAD:N_QROWS + (h + 1) * QK_PAD, :]
            q_rot = q_h * cost_ref[...] + qs_h * sint_ref[...]
            qt_ref[h * QK_PAD:(h + 1) * QK_PAD, :] = (q_rot * Q_PRESCALE).astype(BF16)


def _mla_prep_tokens(p, row0, n_rows, tm, tab_map, cos, sin, cos_t, sin_t, qnw, kvnw, wq_t, wk, wv_t, layer):
    b0 = row0 // tm
    nt = n_rows // tm
    kernel = functools.partial(_mla_prep_kernel, tm=tm, normalize=True, with_q=True)

    def pcol(width, off):
        return pl.BlockSpec((tm, width), lambda i: (b0 + i, (off - NP_MAIN) // width))

    def full(a):
        return pl.BlockSpec(a.shape, lambda i: (0,) * a.ndim)

    return pl.pallas_call(
        kernel,
        grid=(nt,),
        in_specs=[
            pcol(Q_LORA, OFF_CQ), pcol(KV_LORA, OFF_CKV), pcol(LANE, OFF_KR), pcol(LANE, OFF_KRS),
            pl.BlockSpec((tm, LANE), lambda i: (tab_map(i), 0)),
            pl.BlockSpec((tm, LANE), lambda i: (tab_map(i), 0)),
            pl.BlockSpec((LANE, tm), lambda i: (0, tab_map(i))),
            pl.BlockSpec((LANE, tm), lambda i: (0, tab_map(i))),
            full(qnw), full(kvnw), _layer_spec(wq_t, layer, 1), _layer_spec(wk, layer, 1), _layer_spec(wv_t, layer, 1),
        ],
        out_specs=[
            pl.BlockSpec((N_QROWS, tm), lambda i: (0, i)),
            pl.BlockSpec((None, tm, N_QROWS), lambda i: (i, 0, 0)),
            pl.BlockSpec((None, N_VROWS, tm), lambda i: (i, 0, 0)),
            pl.BlockSpec((tm, KV_LORA), lambda i: (i, 0)),
        ],
        out_shape=[
            jax.ShapeDtypeStruct((N_QROWS, n_rows), BF16),
            jax.ShapeDtypeStruct((nt, tm, N_QROWS), BF16),
            jax.ShapeDtypeStruct((nt, N_VROWS, tm), BF16),
            jax.ShapeDtypeStruct((n_rows, KV_LORA), F32),
        ],
        compiler_params=_cparams(("arbitrary",)),
        name="mla_prep",
    )(p, p, p, p, cos, sin, cos_t, sin_t, qnw, kvnw, wq_t, wk, wv_t)


def _mla_prep_cache(ckv, kr_pad, ones_tab, zeros_tab, kvnw, wk, wv_t, tm, layer):
    n_rows = ckv.shape[0]
    nt = n_rows // tm
    kernel = functools.partial(_mla_prep_kernel, tm=tm, normalize=False, with_q=False)

    def full(a):
        return pl.BlockSpec(a.shape, lambda i: (0,) * a.ndim)

    return pl.pallas_call(
        kernel,
        grid=(nt,),
        in_specs=[
            pl.BlockSpec((tm, KV_LORA), lambda i: (i, 0)),
            pl.BlockSpec((tm, LANE), lambda i: (i, 0)),
            pl.BlockSpec((tm, LANE), lambda i: (i, 0)),
            pl.BlockSpec((tm, LANE), lambda i: (0, 0)),
            pl.BlockSpec((tm, LANE), lambda i: (0, 0)),
            full(kvnw), _layer_spec(wk, layer, 1), _layer_spec(wv_t, layer, 1),
        ],
        out_specs=[
            pl.BlockSpec((None, tm, N_QROWS), lambda i: (i, 0, 0)),
            pl.BlockSpec((None, N_VROWS, tm), lambda i: (i, 0, 0)),
        ],
        out_shape=[
            jax.ShapeDtypeStruct((nt, tm, N_QROWS), BF16),
            jax.ShapeDtypeStruct((nt, N_VROWS, tm), BF16),
        ],
        compiler_params=_cparams(("arbitrary",)),
        name="mla_prep_cache",
    )(ckv, kr_pad, kr_pad, ones_tab, zeros_tab, kvnw, wk, wv_t)


ATT_TQ = 256
ATT_SUB = 256


def _attn_kernel(*refs, n_kt, tq, tk, with_cache):
    if with_cache:
        qt_ref, kp_ref, vtp_ref, k_ref, vt_ref, o_ref, ot_ref, s_ref = refs
    else:
        qt_ref, k_ref, vt_ref, o_ref, ot_ref, s_ref = refs

    def k_tile(kt, h):
        cols = slice(h * QK_PAD, (h + 1) * QK_PAD)
        if with_cache:
            return kp_ref[:, cols] if kt == 0 else k_ref[kt - 1, :, cols]
        return k_ref[kt, :, cols]

    def vt_tile(kt, h):
        rows = slice(h * V_DIM, (h + 1) * V_DIM)
        if with_cache:
            return vtp_ref[rows, :] if kt == 0 else vt_ref[kt - 1, rows, :]
        return vt_ref[kt, rows, :]

    sub = min(ATT_SUB, tk)

    def scores_step(h, kt, m8):
        q_t = qt_ref[h * QK_PAD:(h + 1) * QK_PAD, :]
        k = k_tile(kt, h)
        for r in range(0, tk, sub):
            s = jnp.dot(k[r:r + sub, :], q_t, preferred_element_type=F32)
            s_ref[h % 2, kt, r:r + sub, :] = s
            m8 = jnp.maximum(m8, jnp.max(s.reshape(sub // SUBLANE, SUBLANE, tq), axis=0))
        return m8

    def probs_step(h, kt, m, l8, acc):
        v_t = vt_tile(kt, h)
        for r in range(0, tk, sub):
            pr = jnp.exp2(s_ref[h % 2, kt, r:r + sub, :] - m)
            l8 = l8 + jnp.sum(pr.reshape(sub // SUBLANE, SUBLANE, tq), axis=0)
            acc = acc + jnp.dot(v_t[:, r:r + sub], pr.astype(BF16), preferred_element_type=F32)
        return l8, acc

    m8_init = jnp.full((SUBLANE, tq), NEG_BIG, F32)
    m8 = m8_init
    for kt in range(n_kt):
        m8 = scores_step(0, kt, m8)
    for h in range(MLA_HEADS):
        m = jnp.max(m8, axis=0, keepdims=True)
        l8 = jnp.zeros((SUBLANE, tq), F32)
        acc = jnp.zeros((V_DIM, tq), F32)
        m8 = m8_init
        for kt in range(n_kt):
            l8, acc = probs_step(h, kt, m, l8, acc)
            if h + 1 < MLA_HEADS:
                m8 = scores_step(h + 1, kt, m8)
        ot_ref[h * V_DIM:(h + 1) * V_DIM, :] = acc / jnp.sum(l8, axis=0, keepdims=True)
    o_ref[...] = ot_ref[...].T.astype(o_ref.dtype)


def _attention(qt, k3, vt3, n_batch, lq, tk, cache=None):
    tq = min(ATT_TQ, lq)
    nq = lq // tq
    n_new = k3.shape[0] // n_batch
    k4 = k3.reshape(n_batch, n_new, tk, N_QROWS)
    v4 = vt3.reshape(n_batch, n_new, N_VROWS, tk)
    n_kt = n_new + (0 if cache is None else 1)
    kernel = functools.partial(_attn_kernel, n_kt=n_kt, tq=tq, tk=tk, with_cache=cache is not None)
    cache_specs = [] if cache is None else [
        pl.BlockSpec((None, tk, N_QROWS), lambda b, i: (b, 0, 0)),
        pl.BlockSpec((None, N_VROWS, tk), lambda b, i: (b, 0, 0)),
    ]
    return pl.pallas_call(
        kernel,
        grid=(n_batch, nq),
        in_specs=[pl.BlockSpec((N_QROWS, tq), lambda b, i: (0, b * nq + i))] + cache_specs + [
            pl.BlockSpec((None, n_new, tk, N_QROWS), lambda b, i: (b, 0, 0, 0)),
            pl.BlockSpec((None, n_new, N_VROWS, tk), lambda b, i: (b, 0, 0, 0)),
        ],
        out_specs=pl.BlockSpec((tq, N_VROWS), lambda b, i: (b * nq + i, 0)),
        out_shape=jax.ShapeDtypeStruct((n_batch * lq, N_VROWS), BF16),
        scratch_shapes=[pltpu.VMEM((N_VROWS, tq), F32), pltpu.VMEM((2, n_kt, tk, tq), F32)],
        compiler_params=_cparams(("arbitrary", "arbitrary")),
        name="attention",
    )(qt, *(() if cache is None else cache), k4, v4)


MG_TM = 256


def _merge_kernel(*refs, n_x):
    x_refs = refs[:n_x]
    zc_refs = refs[n_x:n_x + 2]
    (ax_ref, axp_ref, axn_ref, ac_ref, acp_ref, acn_ref, ab_ref,
     g_ref, z_ref, yf_ref, yb_ref, gm_ref,
     cw_ref, wa_ref, nw_ref, wb_ref, wc_ref, wo_ref, o_ref) = refs[n_x + 2:]
    i = pl.program_id(0)
    hp, hn = _tile_neighbours(i, MG_TM)
    all_rows = slice(0, MG_TM)

    def f32(v):
        return v.astype(F32)

    u = f32(ac_ref[...]) * f32(ax_ref[...])
    u_prev = f32(acp_ref[HALO - 1:HALO, :]) * f32(axp_ref[HALO - 1:HALO, :]) * hp
    u_next = f32(acn_ref[0:1, :]) * f32(axn_ref[0:1, :]) * hn
    za = f32(ab_ref[...]) * _conv3_tile(u, u_prev, u_next, cw_ref, MG_TM)
    y_a = jnp.dot(za.astype(BF16), wa_ref[...], preferred_element_type=F32)

    yb = (f32(yf_ref[...]) + f32(yb_ref[...])) * _silu(f32(z_ref[...]))
    zb = _rms(yb, nw_ref[...])
    y_b = jnp.dot(zb.astype(BF16), wb_ref[...], preferred_element_type=F32)

    y_c = jnp.dot(_stream_rows(zc_refs, i, MG_TM, all_rows), wc_ref[...], preferred_element_type=F32)

    merged = (jax.nn.sigmoid(f32(g_ref[:, 0:D_MODEL])) * y_a
              + jax.nn.sigmoid(f32(g_ref[:, D_MODEL:2 * D_MODEL])) * y_b
              + jax.nn.sigmoid(f32(g_ref[:, 2 * D_MODEL:3 * D_MODEL])) * y_c)
    o = jnp.dot(merged.astype(BF16), wo_ref[...], preferred_element_type=F32)
    o_ref[...] = _stream_rows(x_refs, i, MG_TM, all_rows) + gm_ref[...] * o


def _merge(p, y_f, y_b, zcs, xs, mod_l, conv_w, wa, nw, wb, wc, wo, layer):
    halo_per_tile = MG_TM // HALO
    n_halo_blocks = N_TOK // HALO
    row = functools.partial(_mod_row, tile_rows=MG_TM)

    def pcol(width, off):
        return pl.BlockSpec((MG_TM, width), lambda i: (i, off // width))

    def pprev(off):
        return pl.BlockSpec((HALO, A_WIDTH), lambda i: (jnp.maximum(i * halo_per_tile - 1, 0), off // A_WIDTH))

    def pnext(off):
        return pl.BlockSpec((HALO, A_WIDTH),
                            lambda i: (jnp.minimum((i + 1) * halo_per_tile, n_halo_blocks - 1), off // A_WIDTH))

    def full(a):
        return pl.BlockSpec(a.shape, lambda i: (0,) * a.ndim)

    return pl.pallas_call(
        functools.partial(_merge_kernel, n_x=len(xs)),
        grid=(N_TOK // MG_TM,),
        in_specs=_stream_specs(xs, MG_TM, 1) + _stream_specs(zcs, MG_TM, 1) + [
            pcol(A_WIDTH, OFF_AX), pprev(OFF_AX), pnext(OFF_AX),
            pcol(A_WIDTH, OFF_AC), pprev(OFF_AC), pnext(OFF_AC),
            pcol(A_WIDTH, OFF_AB),
            pcol(3 * D_MODEL, OFF_G),
            pcol(SSM_INNER, OFF_Z),
            pl.BlockSpec((MG_TM, SSM_INNER), lambda i: (i, 0)),
            pl.BlockSpec((MG_TM, SSM_INNER), lambda i: (i, 0)),
            pl.BlockSpec((None, None, 1, D_MODEL), lambda i: (row(i), 2, 0, 0)),
            full(conv_w), _layer_spec(wa, layer, 1), full(nw), _layer_spec(wb, layer, 1),
            _layer_spec(wc, layer, 1), _layer_spec(wo, layer, 1),
        ],
        out_specs=pl.BlockSpec((MG_TM, D_MODEL), lambda i: (i, 0)),
        out_shape=jax.ShapeDtypeStruct((N_TOK, D_MODEL), F32),
        compiler_params=_cparams(("arbitrary",)),
        name="merge",
    )(*xs, *zcs, p, p, p, p, p, p, p, p, p, y_f, y_b, mod_l, conv_w, wa, nw, wb, wc, wo)


FF_TM = 512
FF_CHUNK = 256


def _ffn_kernel(x_ref, nw_ref, sh_ref, sc_ref, gm_ref, w1_ref, w3_ref, w2_ref, o_ref, h_ref, g_ref):
    _modulated_norm_to(h_ref, lambda rows: x_ref[rows, :], nw_ref, sc_ref, sh_ref, FF_TM)
    h = h_ref[...]
    for c in range(0, FF_DIM, FF_CHUNK):
        a = jnp.dot(h, w1_ref[:, c:c + FF_CHUNK], preferred_element_type=F32)
        b = jnp.dot(h, w3_ref[:, c:c + FF_CHUNK], preferred_element_type=F32)
        g_ref[:, c:c + FF_CHUNK] = (_silu(a) * b).astype(BF16)
    ff = jnp.dot(g_ref[...], w2_ref[...], preferred_element_type=F32)
    o_ref[...] = x_ref[...] + gm_ref[...] * ff


def _ffn(x, mod_l, norm_w, w1, w3, w2, layer):
    row = functools.partial(_mod_row, tile_rows=FF_TM)
    return pl.pallas_call(
        _ffn_kernel,
        grid=(N_TOK // FF_TM,),
        in_specs=[
            pl.BlockSpec((FF_TM, D_MODEL), lambda i: (i, 0)),
            pl.BlockSpec((1, D_MODEL), lambda i: (0, 0)),
            pl.BlockSpec((None, None, 1, D_MODEL), lambda i: (row(i), 3, 0, 0)),
            pl.BlockSpec((None, None, 1, D_MODEL), lambda i: (row(i), 4, 0, 0)),
            pl.BlockSpec((None, None, 1, D_MODEL), lambda i: (row(i), 5, 0, 0)),
            _layer_spec(w1, layer, 1), _layer_spec(w3, layer, 1), _layer_spec(w2, layer, 1),
        ],
        out_specs=pl.BlockSpec((FF_TM, D_MODEL), lambda i: (i, 0)),
        out_shape=jax.ShapeDtypeStruct((N_TOK, D_MODEL), F32),
        scratch_shapes=[pltpu.VMEM((FF_TM, D_MODEL), BF16), pltpu.VMEM((FF_TM, FF_DIM), BF16)],
        compiler_params=_cparams(("arbitrary",)),
        name="ffn",
    )(x, norm_w, mod_l, mod_l, mod_l, w1, w3, w2)


FN_TM = 1024


def _final_norm_kernel(x_ref, w_ref, o_ref):
    o_ref[...] = _rms(x_ref[...], w_ref[...])


def _final_norm(x, w, row0, n_rows):
    b0 = row0 // FN_TM
    return pl.pallas_call(
        _final_norm_kernel,
        grid=(n_rows // FN_TM,),
        in_specs=[pl.BlockSpec((FN_TM, D_MODEL), lambda i: (b0 + i, 0)),
                  pl.BlockSpec((1, D_MODEL), lambda i: (0, 0))],
        out_specs=pl.BlockSpec((FN_TM, D_MODEL), lambda i: (i, 0)),
        out_shape=jax.ShapeDtypeStruct((n_rows, D_MODEL), F32),
        compiler_params=_cparams(("arbitrary",)),
        name="final_norm",
    )(x, w)


def _pad_in_weights(w_in):
    o = 0
    a_x = w_in[..., o:o + 512]; o += 512
    a_b = w_in[..., o:o + 512]; o += 512
    a_c = w_in[..., o:o + 512]; o += 512
    s_z = w_in[..., o:o + 1024]; o += 1024
    s_x = w_in[..., o:o + 1024]; o += 1024
    s_bc = w_in[..., o:o + 512]; o += 512
    s_dt = w_in[..., o:o + 16]; o += 16
    cq = w_in[..., o:o + 256]; o += 256
    ckv = w_in[..., o:o + 256]; o += 256
    kr = w_in[..., o:o + 32]; o += 32
    gates = w_in[..., o:o + 3072]; o += 3072

    def z(n):
        return jnp.zeros(w_in.shape[:-1] + (n,), w_in.dtype)

    kr_sw = jnp.concatenate([kr[..., 16:], kr[..., :16]], axis=-1)
    cols = [gates, s_z, s_x, a_x, a_b, a_c, s_bc, cq, ckv,
            s_dt, z(LANE - 16),
            z(64), kr, z(32),
            z(64), kr_sw, z(32),
            z(NP - OFF_KRS - LANE)]
    out = jnp.concatenate(cols, axis=-1).astype(BF16)
    assert out.shape[-1] == NP
    return out


def _q_weights_t(w_uq):
    w = w_uq.reshape(DEPTH, Q_LORA, MLA_HEADS, QK_DIM)
    nope, x1, x2 = w[..., :NOPE_DIM], w[..., NOPE_DIM:NOPE_DIM + 16], w[..., NOPE_DIM + 16:]
    z32 = jnp.zeros_like(w[..., :32])
    z64 = jnp.zeros_like(nope)
    q = jnp.concatenate([nope, x1, x2, z32], axis=-1).reshape(DEPTH, Q_LORA, N_QROWS)
    qs = jnp.concatenate([z64, x2, x1, z32], axis=-1).reshape(DEPTH, Q_LORA, N_QROWS)
    return jnp.swapaxes(jnp.concatenate([q, qs], axis=-1), 1, 2).astype(BF16)


def _kv_weights(w_ukv):
    w = w_ukv.reshape(DEPTH, KV_LORA, MLA_HEADS, NOPE_DIM + V_DIM)
    kn = jnp.concatenate([w[..., :NOPE_DIM], jnp.zeros_like(w[..., :QK_PAD - NOPE_DIM])], axis=-1)
    wk = kn.reshape(DEPTH, KV_LORA, N_QROWS).astype(BF16)
    wv_t = jnp.swapaxes(w[..., NOPE_DIM:].reshape(DEPTH, KV_LORA, N_VROWS), 1, 2).astype(BF16)
    return wk, wv_t


def _rope_tables(n_tokens, lead_rows):
    n_rows = n_tokens // GRID_W
    row = jnp.repeat(jnp.arange(n_rows, dtype=F32), GRID_W)
    col = jnp.tile(jnp.arange(GRID_W, dtype=F32), n_rows)
    pairs = ROPE_DIM // 4
    inv = ROPE_BASE ** (-jnp.arange(pairs, dtype=F32) / pairs)
    ang = jnp.concatenate([row[:, None] * inv, col[:, None] * inv], axis=-1)
    cos, sin = jnp.cos(ang), jnp.sin(ang)
    ones = jnp.ones((n_tokens, NOPE_DIM), F32)
    z32 = jnp.zeros((n_tokens, 32), F32)
    cos_l = jnp.concatenate([ones, cos, cos, z32], axis=-1)
    sin_l = jnp.concatenate([jnp.zeros_like(ones), -sin, sin, z32], axis=-1)
    ident_c = jnp.concatenate([jnp.ones((lead_rows, NOPE_DIM + ROPE_DIM), F32), jnp.zeros((lead_rows, 32), F32)], -1)
    ident_s = jnp.zeros((lead_rows, LANE), F32)
    return jnp.concatenate([ident_c, cos_l], axis=0), jnp.concatenate([ident_s, sin_l], axis=0)


PREP_TM_CTX = SEQ
PREP_TM_LAT = 512


def kernel(x_prompt, x_sample, c, cache_ckv, cache_krope, state_ssm_fwd, state_ssm_bwd, c_ctx, w_in, a_conv_w, w_a_out, ssm_conv_w, ssm_conv_b, ssm_a_log, ssm_dt_bias, ssm_d, ssm_norm_w, w_b_out, q_norm_w, w_uq, kv_norm_w, w_ukv, w_c_out, w_o, w_ada, b_ada, norm1_w, norm2_w, w_ff1, w_ff3, w_ff2, final_norm_w):
    w_in_p = _pad_in_weights(w_in)
    wq_t = _q_weights_t(w_uq)
    wk, wv_t = _kv_weights(w_ukv)
    wa, wb, wc, wo = (w.astype(BF16) for w in (w_a_out, w_b_out, w_c_out, w_o))
    w1, w3, w2 = (w.astype(BF16) for w in (w_ff1, w_ff3, w_ff2))

    cond = jnp.concatenate([c_ctx[None, :], c, jnp.zeros((N_MOD_ROWS - 1 - DEC_BATCH, D_MODEL), F32)], axis=0)
    mod = _modulation(cond, w_ada, b_ada).reshape(DEPTH, N_MOD_ROWS, 6, 1, D_MODEL)

    conv_wx = ssm_conv_w[..., :SSM_INNER]
    conv_wbc = ssm_conv_w[..., SSM_INNER:]
    conv_bx = ssm_conv_b[:, None, :SSM_INNER]
    conv_bbc = ssm_conv_b[:, None, SSM_INNER:]
    pad_h = ((0, 0), (0, 0), (0, 0), (0, LANE - SSM_HEADS))
    alog = jnp.pad(ssm_a_log[:, :, None, :], pad_h)
    dtb = jnp.pad(ssm_dt_bias[:, :, None, :], pad_h)
    dskip = jnp.repeat(ssm_d, SSM_HEAD_DIM, axis=-1)[:, :, None, :]
    h0f = state_ssm_fwd.reshape(DEC_BATCH, DEPTH, SSM_INNER, SSM_STATE)
    h0b = state_ssm_bwd.reshape(DEC_BATCH, DEPTH, SSM_INNER, SSM_STATE)

    cos_c, sin_c = _rope_tables(DEC_SEQ, PREP_TM_CTX)
    cos_l, sin_l = cos_c[PREP_TM_CTX:], sin_c[PREP_TM_CTX:]
    ident_cos, ident_sin = cos_c[:PREP_TM_CTX], sin_c[:PREP_TM_CTX]
    ones_tab = jnp.concatenate([ident_cos, ident_cos], axis=0)
    zeros_tab = jnp.zeros_like(ones_tab)
    cache_kr_pad = jnp.pad(cache_krope, ((0, 0), (0, 0), (0, 0), (64, 32)))

    xs = (x_prompt.reshape(N_CTX_TOK, D_MODEL), x_sample.reshape(N_LAT_TOK, D_MODEL))

    lat_tiles_per_seq = DEC_SEQ // PREP_TM_LAT
    new_ckv, new_kr, new_f, new_b = [], [], [], []
    for l in range(DEPTH):
        p, ps = _in_projection(xs, mod[l], norm1_w[l][None, :], w_in_p, l)

        xc, bcc = _ssd_conv(p, conv_wx[l], conv_bx[l], conv_wbc[l], conv_bbc[l])
        y_f, y_b, fin_f, fin_b = _ssd(ps, xc, bcc, l, alog[l], dtb[l], dskip[l], h0f, h0b)
        new_f.append(fin_f)
        new_b.append(fin_b)

        qnw, kvnw = q_norm_w[l][None, :], kv_norm_w[l][None, :]
        qt_c, k_c, vt_c, ckv_c = _mla_prep_tokens(
            ps, 0, N_CTX_TOK, PREP_TM_CTX, lambda i: 0, ident_cos, ident_sin, ident_cos.T, ident_sin.T,
            qnw, kvnw, wq_t, wk, wv_t, l)
        qt_l, k_l, vt_l, _ = _mla_prep_tokens(
            ps, N_CTX_TOK, N_LAT_TOK, PREP_TM_LAT, lambda i: i % lat_tiles_per_seq, cos_l, sin_l, cos_l.T, sin_l.T,
            qnw, kvnw, wq_t, wk, wv_t, l)
        k_p, vt_p = _mla_prep_cache(cache_ckv[:, l].reshape(DEC_BATCH * PAST_LEN, KV_LORA),
                                    cache_kr_pad[:, l].reshape(DEC_BATCH * PAST_LEN, LANE),
                                    ones_tab, zeros_tab, kvnw, wk, wv_t, PAST_LEN, l)
        new_ckv.append(ckv_c.reshape(BATCH, SEQ, KV_LORA))
        kr0 = OFF_KR - NP_MAIN + 64
        new_kr.append(ps[:N_CTX_TOK, kr0:kr0 + ROPE_DIM].reshape(BATCH, SEQ, ROPE_DIM))

        zc_c = _attention(qt_c, k_c, vt_c, BATCH, SEQ, SEQ)
        zc_l = _attention(qt_l, k_l, vt_l, DEC_BATCH, DEC_SEQ, PREP_TM_LAT, cache=(k_p, vt_p))

        x = _merge(p, y_f, y_b, (zc_c, zc_l), xs, mod[l], a_conv_w[l], wa, ssm_norm_w[l][None, :],
                   wb, wc, wo, l)
        x = _ffn(x, mod[l], norm2_w[l][None, :], w1, w3, w2, l)
        xs = (x,)

    fw = final_norm_w[None, :]
    y_prompt = _final_norm(x, fw, 0, N_CTX_TOK).reshape(BATCH, SEQ, D_MODEL)
    y_sample = _final_norm(x, fw, N_CTX_TOK, N_LAT_TOK).reshape(DEC_BATCH, DEC_SEQ, D_MODEL)
    hshape = (BATCH, DEPTH, SSM_HEADS, SSM_HEAD_DIM, SSM_STATE)
    return (y_prompt, y_sample,
            jnp.stack(new_ckv, axis=1), jnp.stack(new_kr, axis=1),
            jnp.stack(new_f, axis=1).reshape(hshape), jnp.stack(new_b, axis=1).reshape(hshape))
```

```python
import functools
import math

import jax
import jax.numpy as jnp
import numpy as np
from jax import lax
from jax.experimental import pallas as pl
from jax.experimental.pallas import tpu as pltpu

F32 = jnp.float32
BF16 = jnp.bfloat16

D_MODEL = 1024
BATCH = 16
SEQ = 256
DEPTH = 4
DEC_BATCH = 4
DEC_SEQ = 4096
PAST_LEN = 512
GRID_W = 64
EPS = 1e-6
A_WIDTH = 512
SSM_INNER = 1024
SSM_HEAD_DIM = 64
SSM_HEADS = 16
SSM_GROUPS = 2
SSM_STATE = 128
CHUNK = 128
MLA_HEADS = 8
Q_LORA = 256
KV_LORA = 256
NOPE_DIM = 64
ROPE_DIM = 32
V_DIM = 64
QK_DIM = NOPE_DIM + ROPE_DIM
ROPE_BASE = 10000.0
FF_DIM = 2816

N_CTX_TOK = BATCH * SEQ
N_LAT_TOK = DEC_BATCH * DEC_SEQ
N_TOK = N_CTX_TOK + N_LAT_TOK
N_MOD_ROWS = 8

LANE = 128
SUBLANE = 8
VMEM_LIMIT = 56 * 1024 * 1024

OFF_G = 0
OFF_Z = 3072
OFF_SX = 4096
OFF_AX = 5120
OFF_AB = 5632
OFF_AC = 6144
OFF_BC = 6656
NP_MAIN = 7168
OFF_CQ = 7168
OFF_CKV = 7424
OFF_DT = 7680
OFF_KR = 7808
OFF_KRS = 7936
NP = 8192
N_MAIN_TILES = 7

NEG_BIG = -1e30


def _cparams(sem):
    return pltpu.CompilerParams(dimension_semantics=sem, vmem_limit_bytes=VMEM_LIMIT)


def _rms(x, w):
    ms = jnp.mean(x * x, axis=-1, keepdims=True)
    return x * lax.rsqrt(ms + EPS) * w


def _sigmoid(x):
    return 0.5 * jnp.tanh(0.5 * x) + 0.5


def _silu(x):
    h = 0.5 * x
    return h * jnp.tanh(h) + h


def _layer_spec(w, layer, grid_rank):
    zeros = (0,) * (w.ndim - 1)
    imap = (lambda i: (layer,) + zeros) if grid_rank == 1 else (lambda i, j: (layer,) + zeros)
    return pl.BlockSpec((None,) + tuple(w.shape[1:]), imap)


def _mod_row(tile, tile_rows):
    n_ctx_tiles = N_CTX_TOK // tile_rows
    tiles_per_lat = DEC_SEQ // tile_rows
    return jnp.where(tile < n_ctx_tiles, 0, 1 + (tile - n_ctx_tiles) // tiles_per_lat)


MOD_TN = 1536


def _mod_kernel(c_ref, w_ref, b_ref, o_ref):
    c = c_ref[...]
    s = _silu(c).astype(BF16)
    o_ref[...] = jnp.dot(s, w_ref[...].astype(BF16), preferred_element_type=F32) + b_ref[...]


def _modulation(cond, w_ada, b_ada):
    n_col = 6 * D_MODEL
    return pl.pallas_call(
        _mod_kernel,
        grid=(DEPTH, n_col // MOD_TN),
        in_specs=[
            pl.BlockSpec((N_MOD_ROWS, D_MODEL), lambda l, j: (0, 0)),
            pl.BlockSpec((None, D_MODEL, MOD_TN), lambda l, j: (l, 0, j)),
            pl.BlockSpec((None, 1, MOD_TN), lambda l, j: (l, 0, j)),
        ],
        out_specs=pl.BlockSpec((None, N_MOD_ROWS, MOD_TN), lambda l, j: (l, 0, j)),
        out_shape=jax.ShapeDtypeStruct((DEPTH, N_MOD_ROWS, n_col), F32),
        compiler_params=_cparams(("arbitrary", "arbitrary")),
        name="modulation",
    )(cond, w_ada, b_ada.reshape(DEPTH, 1, n_col))


IN_TM = 2048
IN_TN = 1024
NORM_ROWS = 256


def _stream_specs(xs, tm, grid_rank):
    def imap(f):
        return (lambda i: f(i)) if grid_rank == 1 else (lambda i, j: f(i))

    width = xs[0].shape[1]
    if len(xs) == 1:
        return [pl.BlockSpec((tm, width), imap(lambda i: (i, 0)))]
    n_ctx_tiles = N_CTX_TOK // tm
    return [pl.BlockSpec((tm, width), imap(lambda i: (jnp.minimum(i, n_ctx_tiles - 1), 0))),
            pl.BlockSpec((tm, width), imap(lambda i: (jnp.maximum(i - n_ctx_tiles, 0), 0)))]


def _stream_rows(x_refs, tile, tm, rows):
    if len(x_refs) == 1:
        return x_refs[0][rows, :]
    return jnp.where(tile < N_CTX_TOK // tm, x_refs[0][rows, :], x_refs[1][rows, :])


def _modulated_norm_to(h_ref, x_rows, nw_ref, sc_ref, sh_ref, rows):
    for r in range(0, rows, NORM_ROWS):
        x = x_rows(slice(r, r + NORM_ROWS))
        h = _rms(x, nw_ref[...]) * (1.0 + sc_ref[...]) + sh_ref[...]
        h_ref[r:r + NORM_ROWS, :] = h.astype(BF16)


def _inproj_kernel(*refs, n_x, tm):
    x_refs = refs[:n_x]
    nw_ref, sh_ref, sc_ref, w_ref, om_ref, os_ref, h_ref = refs[n_x:]
    i = pl.program_id(0)
    j = pl.program_id(1)

    @pl.when(j == 0)
    def _():
        _modulated_norm_to(h_ref, lambda rows: _stream_rows(x_refs, i, tm, rows),
                           nw_ref, sc_ref, sh_ref, tm)

    @pl.when(j < N_MAIN_TILES)
    def _():
        om_ref[...] = jnp.dot(h_ref[...], w_ref[...], preferred_element_type=F32).astype(BF16)

    @pl.when(j == N_MAIN_TILES)
    def _():
        os_ref[...] = jnp.dot(h_ref[...], w_ref[...], preferred_element_type=F32)


def _in_projection(xs, mod_l, norm_w, w_in_p, layer):
    tm = IN_TM if len(xs) == 1 else IN_TM // 2
    row = functools.partial(_mod_row, tile_rows=tm)
    return pl.pallas_call(
        functools.partial(_inproj_kernel, n_x=len(xs), tm=tm),
        grid=(N_TOK // tm, NP // IN_TN),
        in_specs=_stream_specs(xs, tm, 2) + [
            pl.BlockSpec((1, D_MODEL), lambda i, j: (0, 0)),
            pl.BlockSpec((None, None, 1, D_MODEL), lambda i, j: (row(i), 0, 0, 0)),
            pl.BlockSpec((None, None, 1, D_MODEL), lambda i, j: (row(i), 1, 0, 0)),
            pl.BlockSpec((None, D_MODEL, IN_TN), lambda i, j: (layer, 0, j)),
        ],
        out_specs=[
            pl.BlockSpec((tm, IN_TN), lambda i, j: (i, jnp.minimum(j, N_MAIN_TILES - 1))),
            pl.BlockSpec((tm, IN_TN), lambda i, j: (i, 0)),
        ],
        out_shape=[
            jax.ShapeDtypeStruct((N_TOK, NP_MAIN), BF16),
            jax.ShapeDtypeStruct((N_TOK, IN_TN), F32),
        ],
        scratch_shapes=[pltpu.VMEM((tm, D_MODEL), BF16)],
        compiler_params=_cparams(("arbitrary", "arbitrary")),
        name="in_projection",
    )(*xs, norm_w, mod_l, mod_l, w_in_p)


def _conv3_tile(u, prev_row, next_row, w_ref, rows):
    ridx = lax.broadcasted_iota(jnp.int32, (SUBLANE, 1), 0)
    up = pltpu.roll(u, 1, axis=0)
    up = jnp.concatenate([jnp.where(ridx == 0, prev_row, up[0:SUBLANE]), up[SUBLANE:]], axis=0)
    dn = pltpu.roll(u, rows - 1, axis=0)
    dn = jnp.concatenate([dn[:rows - SUBLANE], jnp.where(ridx == SUBLANE - 1, next_row, dn[rows - SUBLANE:])],
                         axis=0)
    return up * w_ref[0:1, :] + u * w_ref[1:2, :] + dn * w_ref[2:3, :]


SSD_TQ = 256
SSD_CPT = SSD_TQ // CHUNK
HALO = 2 * SUBLANE
N_SEQ = BATCH + DEC_BATCH
BC_WIDTH = 2 * SSM_GROUPS * SSM_STATE


def _tile_neighbours(i, tile_rows):
    n_ctx_tiles = N_CTX_TOK // tile_rows
    tiles_per_lat = DEC_SEQ // tile_rows
    t_in_seq = (i - n_ctx_tiles) % tiles_per_lat
    is_lat = i >= n_ctx_tiles
    has_prev = jnp.logical_and(is_lat, t_in_seq > 0).astype(F32)
    has_next = jnp.logical_and(is_lat, t_in_seq < tiles_per_lat - 1).astype(F32)
    return has_prev, has_next


def _ssd_conv_kernel(x_ref, xp_ref, xn_ref, bc_ref, bcp_ref, bcn_ref,
                     cwx_ref, cbx_ref, cwb_ref, cbb_ref, xc_ref, bcc_ref):
    hp, hn = _tile_neighbours(pl.program_id(0), SSD_TQ)

    def conv_silu(u_ref, up_ref, un_ref, w_ref, b_ref):
        prev_row = up_ref[HALO - 1:HALO, :].astype(F32) * hp
        next_row = un_ref[0:1, :].astype(F32) * hn
        return _silu(_conv3_tile(u_ref[...].astype(F32), prev_row, next_row, w_ref, SSD_TQ) + b_ref[...])

    xc_ref[...] = conv_silu(x_ref, xp_ref, xn_ref, cwx_ref, cbx_ref)
    bcc_ref[...] = conv_silu(bc_ref, bcp_ref, bcn_ref, cwb_ref, cbb_ref)


def _ssd_conv(p, conv_wx, conv_bx, conv_wbc, conv_bbc):
    halo_per_tile = SSD_TQ // HALO
    n_halo_blocks = N_TOK // HALO
    cx, cbc = OFF_SX // SSM_INNER, OFF_BC // BC_WIDTH

    def prev_map(col):
        return lambda i: (jnp.maximum(i * halo_per_tile - 1, 0), col)

    def next_map(col):
        return lambda i: (jnp.minimum((i + 1) * halo_per_tile, n_halo_blocks - 1), col)

    def const2(i):
        return (0, 0)

    return pl.pallas_call(
        _ssd_conv_kernel,
        grid=(N_TOK // SSD_TQ,),
        in_specs=[
            pl.BlockSpec((SSD_TQ, SSM_INNER), lambda i: (i, cx)),
            pl.BlockSpec((HALO, SSM_INNER), prev_map(cx)),
            pl.BlockSpec((HALO, SSM_INNER), next_map(cx)),
            pl.BlockSpec((SSD_TQ, BC_WIDTH), lambda i: (i, cbc)),
            pl.BlockSpec((HALO, BC_WIDTH), prev_map(cbc)),
            pl.BlockSpec((HALO, BC_WIDTH), next_map(cbc)),
            pl.BlockSpec((3, SSM_INNER), const2),
            pl.BlockSpec((1, SSM_INNER), const2),
            pl.BlockSpec((3, BC_WIDTH), const2),
            pl.BlockSpec((1, BC_WIDTH), const2),
        ],
        out_specs=[
            pl.BlockSpec((SSD_TQ, SSM_INNER), lambda i: (i, 0)),
            pl.BlockSpec((SSD_TQ, BC_WIDTH), lambda i: (i, 0)),
        ],
        out_shape=[
            jax.ShapeDtypeStruct((N_TOK, SSM_INNER), F32),
            jax.ShapeDtypeStruct((N_TOK, BC_WIDTH), F32),
        ],
        compiler_params=_cparams(("arbitrary",)),
        name="ssd_conv",
    )(p, p, p, p, p, p, conv_wx, conv_bx, conv_wbc, conv_bbc)


def _ssd_tables():
    blk_f, blk_b, seq, first, last = [], [], [], [], []
    for s in range(N_SEQ):
        if s < BATCH:
            base, nt = s * SEQ // SSD_TQ, SEQ // SSD_TQ
        else:
            base, nt = (N_CTX_TOK + (s - BATCH) * DEC_SEQ) // SSD_TQ, DEC_SEQ // SSD_TQ
        for k in range(nt):
            blk_f.append(base + k)
            blk_b.append(base + nt - 1 - k)
            seq.append(s)
            first.append(int(k == 0))
            last.append(int(k == nt - 1))
    return [np.asarray(a, np.int32) for a in (blk_f, blk_b, seq, first, last)]


def _split3(a):
    a1 = a.astype(BF16)
    r1 = a - a1.astype(F32)
    a2 = r1.astype(BF16)
    a3 = (r1 - a2.astype(F32)).astype(BF16)
    return a1, a2, a3


def _dot3(lhs_bf16, a):
    a1, a2, a3 = _split3(a)
    return (jnp.dot(lhs_bf16, a1, preferred_element_type=F32)
            + jnp.dot(lhs_bf16, a2, preferred_element_type=F32)
            + jnp.dot(lhs_bf16, a3, preferred_element_type=F32))


def _softplus(x):
    return jnp.maximum(x, 0.0) + jnp.log1p(jnp.exp(-jnp.abs(x)))


def _ssd_kernel(blkf_t, blkb_t, seq_t, first_t, last_t,
                xf_ref, bcf_ref, dtf_ref, xb_ref, bcb_ref, dtb_ref,
                alog_ref, dtbias_ref, dsk_ref, h0f_ref, h0b_ref,
                yf_ref, yb_ref, finf_ref, finb_ref,
                stf_ref, stb_ref):
    s = pl.program_id(0)
    seq = seq_t[s]
    dirs = ((xf_ref, bcf_ref, dtf_ref, stf_ref, yf_ref), (xb_ref, bcb_ref, dtb_ref, stb_ref, yb_ref))

    @pl.when(jnp.logical_and(first_t[s] == 1, seq < BATCH))
    def _():
        stf_ref[...] = jnp.zeros_like(stf_ref)
        stb_ref[...] = jnp.zeros_like(stb_ref)

    @pl.when(jnp.logical_and(first_t[s] == 1, seq >= BATCH))
    def _():
        stf_ref[...] = h0f_ref[...].T
        stb_ref[...] = h0b_ref[...].T

    ii = lax.broadcasted_iota(jnp.int32, (CHUNK, CHUNK), 0)
    jj = lax.broadcasted_iota(jnp.int32, (CHUNK, CHUNK), 1)
    masks = (jj <= ii, jj >= ii)
    masks_b = tuple(jnp.where(m, 1.0, 0.0).astype(BF16) for m in masks)
    masks_neg = tuple(jnp.where(m, 0.0, NEG_BIG) for m in masks)
    lo = jj < SSM_HEAD_DIM

    e_r = lax.broadcasted_iota(jnp.int32, (LANE, SSM_INNER), 0)
    e_c = lax.broadcasted_iota(jnp.int32, (LANE, SSM_INNER), 1)
    expand = jnp.where(jnp.right_shift(e_c, 6) == e_r, 1.0, 0.0).astype(BF16)

    def chunk_setup(d, c):
        _, _, dt_ref, _, _ = dirs[d]
        rows = slice(c * CHUNK, (c + 1) * CHUNK)
        a_row = -jnp.exp(alog_ref[d])
        dt = _softplus(dt_ref[rows, :] + dtbias_ref[d])
        cum = _dot3(masks_b[d], dt * a_row)
        cum_t = cum.T
        dt_t = dt.T
        end = CHUNK - 1 if d == 0 else 0
        tot_row = cum[end:end + 1, :]
        tot_col = cum_t[:, end:end + 1]
        return dict(
            rows=rows, cum=cum, cum_t=cum_t, dt_t=dt_t,
            w_rows=jnp.exp(tot_col - cum_t) * dt_t,
            sdec=_dot3_rows(jnp.exp(tot_row), expand))

    def group_setup(d, cs, g):
        _, bc_ref, _, _, _ = dirs[d]
        b_g = bc_ref[cs["rows"], g * SSM_STATE:(g + 1) * SSM_STATE]
        c_g = bc_ref[cs["rows"], (SSM_GROUPS + g) * SSM_STATE:(SSM_GROUPS + g + 1) * SSM_STATE].astype(BF16)
        b_gt = b_g.T
        cb = jnp.dot(c_g, b_gt.astype(BF16), preferred_element_type=F32)
        return c_g, b_gt, cb

    def pair_step(d, cs, gs, kp):
        x_ref, _, _, st_ref, y_ref = dirs[d]
        c_g, b_gt, cb = gs
        cum, cum_t, dt_t = cs["cum"], cs["cum_t"], cs["dt_t"]
        h_a, h_b = 2 * kp, 2 * kp + 1
        ls = slice(kp * LANE, (kp + 1) * LANE)
        x_pair = x_ref[cs["rows"], ls]
        rhs = jnp.concatenate([jnp.where(lo, x_pair, 0.0).astype(BF16),
                               jnp.where(lo, 0.0, x_pair).astype(BF16)], axis=0)

        def head_lhs(h):
            col = jnp.broadcast_to(cum[:, h:h + 1], (CHUNK, CHUNK))
            seg = col - cum_t[h:h + 1, :]
            dec = jnp.exp(seg + masks_neg[d])
            w_intra = dec * cb * dt_t[h:h + 1, :]
            w_state = b_gt * cs["w_rows"][h:h + 1, :]
            return w_intra.astype(BF16), w_state.astype(BF16), col

        wi_a, ws_a, col_a = head_lhs(h_a)
        wi_b, ws_b, col_b = head_lhs(h_b)
        lhs = jnp.concatenate([jnp.concatenate([wi_a, wi_b], axis=1),
                               jnp.concatenate([ws_a, ws_b], axis=1)], axis=0)
        both = jnp.dot(lhs, rhs, preferred_element_type=F32)
        y_diag = both[0:CHUNK, :]
        d_state = both[CHUNK:2 * CHUNK, :]

        h_pair = st_ref[:, ls]
        y_off = jnp.dot(c_g, h_pair.astype(BF16), preferred_element_type=F32)
        e_pair = jnp.exp(jnp.where(lo, col_a, col_b))
        y_ref[cs["rows"], ls] = (y_diag + y_off * e_pair + x_pair * dsk_ref[d, :, ls]).astype(y_ref.dtype)
        st_ref[:, ls] = h_pair * cs["sdec"][:, ls] + d_state

    pairs_per_group = SSM_HEADS // SSM_GROUPS // 2
    setups = [(chunk_setup(0, k), chunk_setup(1, SSD_CPT - 1 - k)) for k in range(SSD_CPT)]
    for k in range(SSD_CPT):
        cs = setups[k]
        for g in range(SSM_GROUPS):
            gs = (group_setup(0, cs[0], g), group_setup(1, cs[1], g))
            for kp in range(g * pairs_per_group, (g + 1) * pairs_per_group):
                pair_step(0, cs[0], gs[0], kp)
                pair_step(1, cs[1], gs[1], kp)

    @pl.when(jnp.logical_and(last_t[s] == 1, seq < BATCH))
    def _():
        finf_ref[...] = stf_ref[...].T
        finb_ref[...] = stb_ref[...].T


def _dot3_rows(row, rhs_bf16):
    r8 = jnp.broadcast_to(row, (SUBLANE, row.shape[1]))
    r1, r2, r3 = _split3(r8)
    out = (jnp.dot(r1, rhs_bf16, preferred_element_type=F32)
           + jnp.dot(r2, rhs_bf16, preferred_element_type=F32)
           + jnp.dot(r3, rhs_bf16, preferred_element_type=F32))
    return out[0:1, :]


def _ssd(ps, xc, bcc, layer, alog, dtb, dskip, h0f, h0b):
    tables = [jnp.asarray(t) for t in _ssd_tables()]
    n_steps = int(tables[0].shape[0])
    cdt = (OFF_DT - NP_MAIN) // LANE

    def fwd_tile(col):
        return lambda s, blkf, *_: (blkf[s], col)

    def bwd_tile(col):
        return lambda s, blkf, blkb, *_: (blkb[s], col)

    def tile_specs(tile):
        return [
            pl.BlockSpec((SSD_TQ, SSM_INNER), tile(0)),
            pl.BlockSpec((SSD_TQ, BC_WIDTH), tile(0)),
            pl.BlockSpec((SSD_TQ, LANE), tile(cdt)),
        ]

    def h0_map(s, blkf, blkb, seq, *_):
        return (jnp.maximum(seq[s] - BATCH, 0), layer, 0, 0)

    def fin_map(s, blkf, blkb, seq, *_):
        return (jnp.minimum(seq[s], BATCH - 1), 0, 0)

    def const3(s, *_):
        return (0, 0, 0)

    grid_spec = pltpu.PrefetchScalarGridSpec(
        num_scalar_prefetch=5,
        grid=(n_steps,),
        in_specs=tile_specs(fwd_tile) + tile_specs(bwd_tile) + [
            pl.BlockSpec((2, 1, LANE), const3),
            pl.BlockSpec((2, 1, LANE), const3),
            pl.BlockSpec((2, 1, SSM_INNER), const3),
            pl.BlockSpec((None, None, SSM_INNER, SSM_STATE), h0_map),
            pl.BlockSpec((None, None, SSM_INNER, SSM_STATE), h0_map),
        ],
        out_specs=[
            pl.BlockSpec((SSD_TQ, SSM_INNER), fwd_tile(0)),
            pl.BlockSpec((SSD_TQ, SSM_INNER), bwd_tile(0)),
            pl.BlockSpec((None, SSM_INNER, SSM_STATE), fin_map),
            pl.BlockSpec((None, SSM_INNER, SSM_STATE), fin_map),
        ],
        scratch_shapes=[
            pltpu.VMEM((SSM_STATE, SSM_INNER), F32),
            pltpu.VMEM((SSM_STATE, SSM_INNER), F32),
        ],
    )
    return pl.pallas_call(
        _ssd_kernel,
        grid_spec=grid_spec,
        out_shape=[
            jax.ShapeDtypeStruct((N_TOK, SSM_INNER), BF16),
            jax.ShapeDtypeStruct((N_TOK, SSM_INNER), BF16),
            jax.ShapeDtypeStruct((BATCH, SSM_INNER, SSM_STATE), F32),
            jax.ShapeDtypeStruct((BATCH, SSM_INNER, SSM_STATE), F32),
        ],
        compiler_params=_cparams(("arbitrary",)),
        name="ssd_scan",
    )(*tables, xc, bcc, ps, xc, bcc, ps, alog, dtb, dskip, h0f, h0b)


QK_PAD = 128
N_QROWS = MLA_HEADS * QK_PAD
N_VROWS = MLA_HEADS * V_DIM
NT_DIMS = (((1,), (1,)), ((), ()))
Q_PRESCALE = (1.0 / math.sqrt(QK_DIM)) * math.log2(math.e)


def _mla_prep_kernel(*refs, tm, normalize, with_q):
    if with_q:
        (cq_ref, ckv_ref, kr_ref, krs_ref, cos_ref, sin_ref, cost_ref, sint_ref,
         qnw_ref, kvnw_ref, wq_ref, wk_ref, wv_ref,
         qt_ref, k_ref, vt_ref, ckvn_ref) = refs
    else:
        (ckv_ref, kr_ref, krs_ref, cos_ref, sin_ref, kvnw_ref, wk_ref, wv_ref,
         k_ref, vt_ref) = refs

    ckv = ckv_ref[...]
    if normalize:
        ckv = _rms(ckv, kvnw_ref[...])
        ckvn_ref[...] = ckv
    ckv_b = ckv.astype(BF16)
    kn = jnp.dot(ckv_b, wk_ref[...], preferred_element_type=F32)
    kr = kr_ref[...] * cos_ref[...] + krs_ref[...] * sin_ref[...]
    for h in range(MLA_HEADS):
        hs = slice(h * QK_PAD, (h + 1) * QK_PAD)
        k_ref[:, hs] = (kn[:, hs] + kr).astype(BF16)
    vt_ref[...] = lax.dot_general(wv_ref[...], ckv_b, NT_DIMS, preferred_element_type=F32).astype(BF16)

    if with_q:
        cqn = _rms(cq_ref[...], qnw_ref[...]).astype(BF16)
        qq = lax.dot_general(wq_ref[...], cqn, NT_DIMS, preferred_element_type=F32)
        for h in range(MLA_HEADS):
            q_h = qq[h * QK_PAD:(h + 1) * QK_PAD, :]
            qs_h = qq[N_QROWS + h * QK_PAD:N_QROWS + (h + 1) * QK_PAD, :]
            q_rot = q_h * cost_ref[...] + qs_h * sint_ref[...]
            qt_ref[h * QK_PAD:(h + 1) * QK_PAD, :] = (q_rot * Q_PRESCALE).astype(BF16)


def _mla_prep_tokens(p, row0, n_rows, tm, tab_map, cos, sin, cos_t, sin_t, qnw, kvnw, wq_t, wk, wv_t, layer):
    b0 = row0 // tm
    nt = n_rows // tm
    kernel = functools.partial(_mla_prep_kernel, tm=tm, normalize=True, with_q=True)

    def pcol(width, off):
        return pl.BlockSpec((tm, width), lambda i: (b0 + i, (off - NP_MAIN) // width))

    def full(a):
        return pl.BlockSpec(a.shape, lambda i: (0,) * a.ndim)

    return pl.pallas_call(
        kernel,
        grid=(nt,),
        in_specs=[
            pcol(Q_LORA, OFF_CQ), pcol(KV_LORA, OFF_CKV), pcol(LANE, OFF_KR), pcol(LANE, OFF_KRS),
            pl.BlockSpec((tm, LANE), lambda i: (tab_map(i), 0)),
            pl.BlockSpec((tm, LANE), lambda i: (tab_map(i), 0)),
            pl.BlockSpec((LANE, tm), lambda i: (0, tab_map(i))),
            pl.BlockSpec((LANE, tm), lambda i: (0, tab_map(i))),
            full(qnw), full(kvnw), _layer_spec(wq_t, layer, 1), _layer_spec(wk, layer, 1), _layer_spec(wv_t, layer, 1),
        ],
        out_specs=[
            pl.BlockSpec((N_QROWS, tm), lambda i: (0, i)),
            pl.BlockSpec((None, tm, N_QROWS), lambda i: (i, 0, 0)),
            pl.BlockSpec((None, N_VROWS, tm), lambda i: (i, 0, 0)),
            pl.BlockSpec((tm, KV_LORA), lambda i: (i, 0)),
        ],
        out_shape=[
            jax.ShapeDtypeStruct((N_QROWS, n_rows), BF16),
            jax.ShapeDtypeStruct((nt, tm, N_QROWS), BF16),
            jax.ShapeDtypeStruct((nt, N_VROWS, tm), BF16),
            jax.ShapeDtypeStruct((n_rows, KV_LORA), F32),
        ],
        compiler_params=_cparams(("arbitrary",)),
        name="mla_prep",
    )(p, p, p, p, cos, sin, cos_t, sin_t, qnw, kvnw, wq_t, wk, wv_t)


def _mla_prep_cache(ckv, kr_pad, ones_tab, zeros_tab, kvnw, wk, wv_t, tm, layer):
    n_rows = ckv.shape[0]
    nt = n_rows // tm
    kernel = functools.partial(_mla_prep_kernel, tm=tm, normalize=False, with_q=False)

    def full(a):
        return pl.BlockSpec(a.shape, lambda i: (0,) * a.ndim)

    return pl.pallas_call(
        kernel,
        grid=(nt,),
        in_specs=[
            pl.BlockSpec((tm, KV_LORA), lambda i: (i, 0)),
            pl.BlockSpec((tm, LANE), lambda i: (i, 0)),
            pl.BlockSpec((tm, LANE), lambda i: (i, 0)),
            pl.BlockSpec((tm, LANE), lambda i: (0, 0)),
            pl.BlockSpec((tm, LANE), lambda i: (0, 0)),
            full(kvnw), _layer_spec(wk, layer, 1), _layer_spec(wv_t, layer, 1),
        ],
        out_specs=[
            pl.BlockSpec((None, tm, N_QROWS), lambda i: (i, 0, 0)),
            pl.BlockSpec((None, N_VROWS, tm), lambda i: (i, 0, 0)),
        ],
        out_shape=[
            jax.ShapeDtypeStruct((nt, tm, N_QROWS), BF16),
            jax.ShapeDtypeStruct((nt, N_VROWS, tm), BF16),
        ],
        compiler_params=_cparams(("arbitrary",)),
        name="mla_prep_cache",
    )(ckv, kr_pad, kr_pad, ones_tab, zeros_tab, kvnw, wk, wv_t)


ATT_TQ = 256
ATT_SUB = 256


def _attn_kernel(*refs, n_kt, tq, tk, with_cache):
    if with_cache:
        qt_ref, kp_ref, vtp_ref, k_ref, vt_ref, o_ref, ot_ref, s_ref = refs
    else:
        qt_ref, k_ref, vt_ref, o_ref, ot_ref, s_ref = refs

    def k_tile(kt, h):
        cols = slice(h * QK_PAD, (h + 1) * QK_PAD)
        if with_cache:
            return kp_ref[:, cols] if kt == 0 else k_ref[kt - 1, :, cols]
        return k_ref[kt, :, cols]

    def vt_tile(kt, h):
        rows = slice(h * V_DIM, (h + 1) * V_DIM)
        if with_cache:
            return vtp_ref[rows, :] if kt == 0 else vt_ref[kt - 1, rows, :]
        return vt_ref[kt, rows, :]

    sub = min(ATT_SUB, tk)

    def scores_step(h, kt, m8):
        q_t = qt_ref[h * QK_PAD:(h + 1) * QK_PAD, :]
        k = k_tile(kt, h)
        for r in range(0, tk, sub):
            s = jnp.dot(k[r:r + sub, :], q_t, preferred_element_type=F32)
            s_ref[h % 2, kt, r:r + sub, :] = s
            m8 = jnp.maximum(m8, jnp.max(s.reshape(sub // SUBLANE, SUBLANE, tq), axis=0))
        return m8

    def probs_step(h, kt, m, l8, acc):
        v_t = vt_tile(kt, h)
        for r in range(0, tk, sub):
            pr = jnp.exp2(s_ref[h % 2, kt, r:r + sub, :] - m)
            l8 = l8 + jnp.sum(pr.reshape(sub // SUBLANE, SUBLANE, tq), axis=0)
            acc = acc + jnp.dot(v_t[:, r:r + sub], pr.astype(BF16), preferred_element_type=F32)
        return l8, acc

    m8_init = jnp.full((SUBLANE, tq), NEG_BIG, F32)
    m8 = m8_init
    for kt in range(n_kt):
        m8 = scores_step(0, kt, m8)
    for h in range(MLA_HEADS):
        m = jnp.max(m8, axis=0, keepdims=True)
        l8 = jnp.zeros((SUBLANE, tq), F32)
        acc = jnp.zeros((V_DIM, tq), F32)
        m8 = m8_init
        for kt in range(n_kt):
            l8, acc = probs_step(h, kt, m, l8, acc)
            if h + 1 < MLA_HEADS:
                m8 = scores_step(h + 1, kt, m8)
        ot_ref[h * V_DIM:(h + 1) * V_DIM, :] = acc / jnp.sum(l8, axis=0, keepdims=True)
    o_ref[...] = ot_ref[...].T.astype(o_ref.dtype)


def _attention(qt, k3, vt3, n_batch, lq, tk, cache=None):
    tq = min(ATT_TQ, lq)
    nq = lq // tq
    n_new = k3.shape[0] // n_batch
    k4 = k3.reshape(n_batch, n_new, tk, N_QROWS)
    v4 = vt3.reshape(n_batch, n_new, N_VROWS, tk)
    n_kt = n_new + (0 if cache is None else 1)
    kernel = functools.partial(_attn_kernel, n_kt=n_kt, tq=tq, tk=tk, with_cache=cache is not None)
    cache_specs = [] if cache is None else [
        pl.BlockSpec((None, tk, N_QROWS), lambda b, i: (b, 0, 0)),
        pl.BlockSpec((None, N_VROWS, tk), lambda b, i: (b, 0, 0)),
    ]
    return pl.pallas_call(
        kernel,
        grid=(n_batch, nq),
        in_specs=[pl.BlockSpec((N_QROWS, tq), lambda b, i: (0, b * nq + i))] + cache_specs + [
            pl.BlockSpec((None, n_new, tk, N_QROWS), lambda b, i: (b, 0, 0, 0)),
            pl.BlockSpec((None, n_new, N_VROWS, tk), lambda b, i: (b, 0, 0, 0)),
        ],
        out_specs=pl.BlockSpec((tq, N_VROWS), lambda b, i: (b * nq + i, 0)),
        out_shape=jax.ShapeDtypeStruct((n_batch * lq, N_VROWS), BF16),
        scratch_shapes=[pltpu.VMEM((N_VROWS, tq), F32), pltpu.VMEM((2, n_kt, tk, tq), F32)],
        compiler_params=_cparams(("arbitrary", "arbitrary")),
        name="attention",
    )(qt, *(() if cache is None else cache), k4, v4)


MG_TM = 256


def _merge_kernel(*refs, n_x):
    x_refs = refs[:n_x]
    zc_refs = refs[n_x:n_x + 2]
    (ax_ref, axp_ref, axn_ref, ac_ref, acp_ref, acn_ref, ab_ref,
     g_ref, z_ref, yf_ref, yb_ref, gm_ref,
     cw_ref, wa_ref, nw_ref, wb_ref, wc_ref, wo_ref, o_ref) = refs[n_x + 2:]
    i = pl.program_id(0)
    hp, hn = _tile_neighbours(i, MG_TM)
    all_rows = slice(0, MG_TM)

    def f32(v):
        return v.astype(F32)

    u = f32(ac_ref[...]) * f32(ax_ref[...])
    u_prev = f32(acp_ref[HALO - 1:HALO, :]) * f32(axp_ref[HALO - 1:HALO, :]) * hp
    u_next = f32(acn_ref[0:1, :]) * f32(axn_ref[0:1, :]) * hn
    za = f32(ab_ref[...]) * _conv3_tile(u, u_prev, u_next, cw_ref, MG_TM)
    y_a = jnp.dot(za.astype(BF16), wa_ref[...], preferred_element_type=F32)

    yb = (f32(yf_ref[...]) + f32(yb_ref[...])) * _silu(f32(z_ref[...]))
    zb = _rms(yb, nw_ref[...])
    y_b = jnp.dot(zb.astype(BF16), wb_ref[...], preferred_element_type=F32)

    y_c = jnp.dot(_stream_rows(zc_refs, i, MG_TM, all_rows), wc_ref[...], preferred_element_type=F32)

    merged = (_sigmoid(f32(g_ref[:, 0:D_MODEL])) * y_a
              + _sigmoid(f32(g_ref[:, D_MODEL:2 * D_MODEL])) * y_b
              + _sigmoid(f32(g_ref[:, 2 * D_MODEL:3 * D_MODEL])) * y_c)
    o = jnp.dot(merged.astype(BF16), wo_ref[...], preferred_element_type=F32)
    o_ref[...] = _stream_rows(x_refs, i, MG_TM, all_rows) + gm_ref[...] * o


def _merge(p, y_f, y_b, zcs, xs, mod_l, conv_w, wa, nw, wb, wc, wo, layer):
    halo_per_tile = MG_TM // HALO
    n_halo_blocks = N_TOK // HALO
    row = functools.partial(_mod_row, tile_rows=MG_TM)

    def pcol(width, off):
        return pl.BlockSpec((MG_TM, width), lambda i: (i, off // width))

    def pprev(off):
        return pl.BlockSpec((HALO, A_WIDTH), lambda i: (jnp.maximum(i * halo_per_tile - 1, 0), off // A_WIDTH))

    def pnext(off):
        return pl.BlockSpec((HALO, A_WIDTH),
                            lambda i: (jnp.minimum((i + 1) * halo_per_tile, n_halo_blocks - 1), off // A_WIDTH))

    def full(a):
        return pl.BlockSpec(a.shape, lambda i: (0,) * a.ndim)

    return pl.pallas_call(
        functools.partial(_merge_kernel, n_x=len(xs)),
        grid=(N_TOK // MG_TM,),
        in_specs=_stream_specs(xs, MG_TM, 1) + _stream_specs(zcs, MG_TM, 1) + [
            pcol(A_WIDTH, OFF_AX), pprev(OFF_AX), pnext(OFF_AX),
            pcol(A_WIDTH, OFF_AC), pprev(OFF_AC), pnext(OFF_AC),
            pcol(A_WIDTH, OFF_AB),
            pcol(3 * D_MODEL, OFF_G),
            pcol(SSM_INNER, OFF_Z),
            pl.BlockSpec((MG_TM, SSM_INNER), lambda i: (i, 0)),
            pl.BlockSpec((MG_TM, SSM_INNER), lambda i: (i, 0)),
            pl.BlockSpec((None, None, 1, D_MODEL), lambda i: (row(i), 2, 0, 0)),
            full(conv_w), _layer_spec(wa, layer, 1), full(nw), _layer_spec(wb, layer, 1),
            _layer_spec(wc, layer, 1), _layer_spec(wo, layer, 1),
        ],
        out_specs=pl.BlockSpec((MG_TM, D_MODEL), lambda i: (i, 0)),
        out_shape=jax.ShapeDtypeStruct((N_TOK, D_MODEL), F32),
        compiler_params=_cparams(("arbitrary",)),
        name="merge",
    )(*xs, *zcs, p, p, p, p, p, p, p, p, p, y_f, y_b, mod_l, conv_w, wa, nw, wb, wc, wo)


FF_TM = 512
FF_CHUNK = 256


def _ffn_kernel(x_ref, nw_ref, sh_ref, sc_ref, gm_ref, w1_ref, w3_ref, w2_ref, o_ref, h_ref, g_ref):
    _modulated_norm_to(h_ref, lambda rows: x_ref[rows, :], nw_ref, sc_ref, sh_ref, FF_TM)
    h = h_ref[...]
    for c in range(0, FF_DIM, FF_CHUNK):
        a = jnp.dot(h, w1_ref[:, c:c + FF_CHUNK], preferred_element_type=F32)
        b = jnp.dot(h, w3_ref[:, c:c + FF_CHUNK], preferred_element_type=F32)
        g_ref[:, c:c + FF_CHUNK] = (_silu(a) * b).astype(BF16)
    ff = jnp.dot(g_ref[...], w2_ref[...], preferred_element_type=F32)
    o_ref[...] = x_ref[...] + gm_ref[...] * ff


def _ffn(x, mod_l, norm_w, w1, w3, w2, layer):
    row = functools.partial(_mod_row, tile_rows=FF_TM)
    return pl.pallas_call(
        _ffn_kernel,
        grid=(N_TOK // FF_TM,),
        in_specs=[
            pl.BlockSpec((FF_TM, D_MODEL), lambda i: (i, 0)),
            pl.BlockSpec((1, D_MODEL), lambda i: (0, 0)),
            pl.BlockSpec((None, None, 1, D_MODEL), lambda i: (row(i), 3, 0, 0)),
            pl.BlockSpec((None, None, 1, D_MODEL), lambda i: (row(i), 4, 0, 0)),
            pl.BlockSpec((None, None, 1, D_MODEL), lambda i: (row(i), 5, 0, 0)),
            _layer_spec(w1, layer, 1), _layer_spec(w3, layer, 1), _layer_spec(w2, layer, 1),
        ],
        out_specs=pl.BlockSpec((FF_TM, D_MODEL), lambda i: (i, 0)),
        out_shape=jax.ShapeDtypeStruct((N_TOK, D_MODEL), F32),
        scratch_shapes=[pltpu.VMEM((FF_TM, D_MODEL), BF16), pltpu.VMEM((FF_TM, FF_DIM), BF16)],
        compiler_params=_cparams(("arbitrary",)),
        name="ffn",
    )(x, norm_w, mod_l, mod_l, mod_l, w1, w3, w2)


FN_TM = 1024


def _final_norm_kernel(x_ref, w_ref, o_ref):
    o_ref[...] = _rms(x_ref[...], w_ref[...])


def _final_norm(x, w, row0, n_rows):
    b0 = row0 // FN_TM
    return pl.pallas_call(
        _final_norm_kernel,
        grid=(n_rows // FN_TM,),
        in_specs=[pl.BlockSpec((FN_TM, D_MODEL), lambda i: (b0 + i, 0)),
                  pl.BlockSpec((1, D_MODEL), lambda i: (0, 0))],
        out_specs=pl.BlockSpec((FN_TM, D_MODEL), lambda i: (i, 0)),
        out_shape=jax.ShapeDtypeStruct((n_rows, D_MODEL), F32),
        compiler_params=_cparams(("arbitrary",)),
        name="final_norm",
    )(x, w)


def _pad_in_weights(w_in):
    o = 0
    a_x = w_in[..., o:o + 512]; o += 512
    a_b = w_in[..., o:o + 512]; o += 512
    a_c = w_in[..., o:o + 512]; o += 512
    s_z = w_in[..., o:o + 1024]; o += 1024
    s_x = w_in[..., o:o + 1024]; o += 1024
    s_bc = w_in[..., o:o + 512]; o += 512
    s_dt = w_in[..., o:o + 16]; o += 16
    cq = w_in[..., o:o + 256]; o += 256
    ckv = w_in[..., o:o + 256]; o += 256
    kr = w_in[..., o:o + 32]; o += 32
    gates = w_in[..., o:o + 3072]; o += 3072

    def z(n):
        return jnp.zeros(w_in.shape[:-1] + (n,), w_in.dtype)

    kr_sw = jnp.concatenate([kr[..., 16:], kr[..., :16]], axis=-1)
    cols = [gates, s_z, s_x, a_x, a_b, a_c, s_bc, cq, ckv,
            s_dt, z(LANE - 16),
            z(64), kr, z(32),
            z(64), kr_sw, z(32),
            z(NP - OFF_KRS - LANE)]
    out = jnp.concatenate(cols, axis=-1).astype(BF16)
    assert out.shape[-1] == NP
    return out


def _q_weights_t(w_uq):
    w = w_uq.reshape(DEPTH, Q_LORA, MLA_HEADS, QK_DIM)
    nope, x1, x2 = w[..., :NOPE_DIM], w[..., NOPE_DIM:NOPE_DIM + 16], w[..., NOPE_DIM + 16:]
    z32 = jnp.zeros_like(w[..., :32])
    z64 = jnp.zeros_like(nope)
    q = jnp.concatenate([nope, x1, x2, z32], axis=-1).reshape(DEPTH, Q_LORA, N_QROWS)
    qs = jnp.concatenate([z64, x2, x1, z32], axis=-1).reshape(DEPTH, Q_LORA, N_QROWS)
    return jnp.swapaxes(jnp.concatenate([q, qs], axis=-1), 1, 2).astype(BF16)


def _kv_weights(w_ukv):
    w = w_ukv.reshape(DEPTH, KV_LORA, MLA_HEADS, NOPE_DIM + V_DIM)
    kn = jnp.concatenate([w[..., :NOPE_DIM], jnp.zeros_like(w[..., :QK_PAD - NOPE_DIM])], axis=-1)
    wk = kn.reshape(DEPTH, KV_LORA, N_QROWS).astype(BF16)
    wv_t = jnp.swapaxes(w[..., NOPE_DIM:].reshape(DEPTH, KV_LORA, N_VROWS), 1, 2).astype(BF16)
    return wk, wv_t


def _rope_tables(n_tokens, lead_rows):
    n_rows = n_tokens // GRID_W
    row = jnp.repeat(jnp.arange(n_rows, dtype=F32), GRID_W)
    col = jnp.tile(jnp.arange(GRID_W, dtype=F32), n_rows)
    pairs = ROPE_DIM // 4
    inv = ROPE_BASE ** (-jnp.arange(pairs, dtype=F32) / pairs)
    ang = jnp.concatenate([row[:, None] * inv, col[:, None] * inv], axis=-1)
    cos, sin = jnp.cos(ang), jnp.sin(ang)
    ones = jnp.ones((n_tokens, NOPE_DIM), F32)
    z32 = jnp.zeros((n_tokens, 32), F32)
    cos_l = jnp.concatenate([ones, cos, cos, z32], axis=-1)
    sin_l = jnp.concatenate([jnp.zeros_like(ones), -sin, sin, z32], axis=-1)
    ident_c = jnp.concatenate([jnp.ones((lead_rows, NOPE_DIM + ROPE_DIM), F32), jnp.zeros((lead_rows, 32), F32)], -1)
    ident_s = jnp.zeros((lead_rows, LANE), F32)
    return jnp.concatenate([ident_c, cos_l], axis=0), jnp.concatenate([ident_s, sin_l], axis=0)


PREP_TM_CTX = SEQ
PREP_TM_LAT = 512


def kernel(x_prompt, x_sample, c, cache_ckv, cache_krope, state_ssm_fwd, state_ssm_bwd, c_ctx, w_in, a_conv_w, w_a_out, ssm_conv_w, ssm_conv_b, ssm_a_log, ssm_dt_bias, ssm_d, ssm_norm_w, w_b_out, q_norm_w, w_uq, kv_norm_w, w_ukv, w_c_out, w_o, w_ada, b_ada, norm1_w, norm2_w, w_ff1, w_ff3, w_ff2, final_norm_w):
    w_in_p = _pad_in_weights(w_in)
    wq_t = _q_weights_t(w_uq)
    wk, wv_t = _kv_weights(w_ukv)
    wa, wb, wc, wo = (w.astype(BF16) for w in (w_a_out, w_b_out, w_c_out, w_o))
    w1, w3, w2 = (w.astype(BF16) for w in (w_ff1, w_ff3, w_ff2))

    cond = jnp.concatenate([c_ctx[None, :], c, jnp.zeros((N_MOD_ROWS - 1 - DEC_BATCH, D_MODEL), F32)], axis=0)
    mod = _modulation(cond, w_ada, b_ada).reshape(DEPTH, N_MOD_ROWS, 6, 1, D_MODEL)

    conv_wx = ssm_conv_w[..., :SSM_INNER]
    conv_wbc = ssm_conv_w[..., SSM_INNER:]
    conv_bx = ssm_conv_b[:, None, :SSM_INNER]
    conv_bbc = ssm_conv_b[:, None, SSM_INNER:]
    pad_h = ((0, 0), (0, 0), (0, 0), (0, LANE - SSM_HEADS))
    alog = jnp.pad(ssm_a_log[:, :, None, :], pad_h)
    dtb = jnp.pad(ssm_dt_bias[:, :, None, :], pad_h)
    dskip = jnp.repeat(ssm_d, SSM_HEAD_DIM, axis=-1)[:, :, None, :]
    h0f = state_ssm_fwd.reshape(DEC_BATCH, DEPTH, SSM_INNER, SSM_STATE)
    h0b = state_ssm_bwd.reshape(DEC_BATCH, DEPTH, SSM_INNER, SSM_STATE)

    cos_c, sin_c = _rope_tables(DEC_SEQ, PREP_TM_CTX)
    cos_l, sin_l = cos_c[PREP_TM_CTX:], sin_c[PREP_TM_CTX:]
    ident_cos, ident_sin = cos_c[:PREP_TM_CTX], sin_c[:PREP_TM_CTX]
    ones_tab = jnp.concatenate([ident_cos, ident_cos], axis=0)
    zeros_tab = jnp.zeros_like(ones_tab)
    cache_kr_pad = jnp.pad(cache_krope, ((0, 0), (0, 0), (0, 0), (64, 32)))

    xs = (x_prompt.reshape(N_CTX_TOK, D_MODEL), x_sample.reshape(N_LAT_TOK, D_MODEL))

    lat_tiles_per_seq = DEC_SEQ // PREP_TM_LAT
    new_ckv, new_kr, new_f, new_b = [], [], [], []
    for l in range(DEPTH):
        p, ps = _in_projection(xs, mod[l], norm1_w[l][None, :], w_in_p, l)

        xc, bcc = _ssd_conv(p, conv_wx[l], conv_bx[l], conv_wbc[l], conv_bbc[l])
        y_f, y_b, fin_f, fin_b = _ssd(ps, xc, bcc, l, alog[l], dtb[l], dskip[l], h0f, h0b)
        new_f.append(fin_f)
        new_b.append(fin_b)

        qnw, kvnw = q_norm_w[l][None, :], kv_norm_w[l][None, :]
        qt_c, k_c, vt_c, ckv_c = _mla_prep_tokens(
            ps, 0, N_CTX_TOK, PREP_TM_CTX, lambda i: 0, ident_cos, ident_sin, ident_cos.T, ident_sin.T,
            qnw, kvnw, wq_t, wk, wv_t, l)
        qt_l, k_l, vt_l, _ = _mla_prep_tokens(
            ps, N_CTX_TOK, N_LAT_TOK, PREP_TM_LAT, lambda i: i % lat_tiles_per_seq, cos_l, sin_l, cos_l.T, sin_l.T,
            qnw, kvnw, wq_t, wk, wv_t, l)
        k_p, vt_p = _mla_prep_cache(cache_ckv[:, l].reshape(DEC_BATCH * PAST_LEN, KV_LORA),
                                    cache_kr_pad[:, l].reshape(DEC_BATCH * PAST_LEN, LANE),
                                    ones_tab, zeros_tab, kvnw, wk, wv_t, PAST_LEN, l)
        new_ckv.append(ckv_c.reshape(BATCH, SEQ, KV_LORA))
        kr0 = OFF_KR - NP_MAIN + 64
        new_kr.append(ps[:N_CTX_TOK, kr0:kr0 + ROPE_DIM].reshape(BATCH, SEQ, ROPE_DIM))

        zc_c = _attention(qt_c, k_c, vt_c, BATCH, SEQ, SEQ)
        zc_l = _attention(qt_l, k_l, vt_l, DEC_BATCH, DEC_SEQ, PREP_TM_LAT, cache=(k_p, vt_p))

        x = _merge(p, y_f, y_b, (zc_c, zc_l), xs, mod[l], a_conv_w[l], wa, ssm_norm_w[l][None, :],
                   wb, wc, wo, l)
        x = _ffn(x, mod[l], norm2_w[l][None, :], w1, w3, w2, l)
        xs = (x,)

    fw = final_norm_w[None, :]
    y_prompt = _final_norm(x, fw, 0, N_CTX_TOK).reshape(BATCH, SEQ, D_MODEL)
    y_sample = _final_norm(x, fw, N_CTX_TOK, N_LAT_TOK).reshape(DEC_BATCH, DEC_SEQ, D_MODEL)
    hshape = (BATCH, DEPTH, SSM_HEADS, SSM_HEAD_DIM, SSM_STATE)
    return (y_prompt, y_sample,
            jnp.stack(new_ckv, axis=1), jnp.stack(new_kr, axis=1),
            jnp.stack(new_f, axis=1).reshape(hshape), jnp.stack(new_b, axis=1).reshape(hshape))
```

```python
import functools
import math

import jax
import jax.numpy as jnp
import numpy as np
from jax import lax
from jax.experimental import pallas as pl
from jax.experimental.pallas import tpu as pltpu

F32 = jnp.float32
BF16 = jnp.bfloat16

D_MODEL = 1024
BATCH = 16
SEQ = 256
DEPTH = 4
DEC_BATCH = 4
DEC_SEQ = 4096
PAST_LEN = 512
GRID_W = 64
EPS = 1e-6
A_WIDTH = 512
SSM_INNER = 1024
SSM_HEAD_DIM = 64
SSM_HEADS = 16
SSM_GROUPS = 2
SSM_STATE = 128
CHUNK = 128
MLA_HEADS = 8
Q_LORA = 256
KV_LORA = 256
NOPE_DIM = 64
ROPE_DIM = 32
V_DIM = 64
QK_DIM = NOPE_DIM + ROPE_DIM
ROPE_BASE = 10000.0
FF_DIM = 2816

N_CTX_TOK = BATCH * SEQ
N_LAT_TOK = DEC_BATCH * DEC_SEQ
N_TOK = N_CTX_TOK + N_LAT_TOK
N_MOD_ROWS = 8

LANE = 128
SUBLANE = 8
VMEM_LIMIT = 56 * 1024 * 1024

OFF_G = 0
OFF_Z = 3072
OFF_SX = 4096
OFF_AX = 5120
OFF_AB = 5632
OFF_AC = 6144
OFF_BC = 6656
NP_MAIN = 7168
OFF_CQ = 7168
OFF_CKV = 7424
OFF_DT = 7680
OFF_KR = 7808
OFF_KRS = 7936
NP = 8192
N_MAIN_TILES = 7

NEG_BIG = -1e30


def _cparams(sem):
    return pltpu.CompilerParams(dimension_semantics=sem, vmem_limit_bytes=VMEM_LIMIT)


def _rms(x, w):
    ms = jnp.mean(x * x, axis=-1, keepdims=True)
    return x * lax.rsqrt(ms + EPS) * w


def _sigmoid(x):
    return 0.5 * jnp.tanh(0.5 * x) + 0.5


def _silu(x):
    h = 0.5 * x
    return h * jnp.tanh(h) + h


def _layer_spec(w, layer, grid_rank):
    zeros = (0,) * (w.ndim - 1)
    imap = (lambda i: (layer,) + zeros) if grid_rank == 1 else (lambda i, j: (layer,) + zeros)
    return pl.BlockSpec((None,) + tuple(w.shape[1:]), imap)


def _mod_row(tile, tile_rows):
    n_ctx_tiles = N_CTX_TOK // tile_rows
    tiles_per_lat = DEC_SEQ // tile_rows
    return jnp.where(tile < n_ctx_tiles, 0, 1 + (tile - n_ctx_tiles) // tiles_per_lat)


MOD_TN = 1536


def _mod_kernel(c_ref, w_ref, b_ref, o_ref):
    c = c_ref[...]
    s = _silu(c).astype(BF16)
    o_ref[...] = jnp.dot(s, w_ref[...].astype(BF16), preferred_element_type=F32) + b_ref[...]


def _modulation(cond, w_ada, b_ada):
    n_col = 6 * D_MODEL
    return pl.pallas_call(
        _mod_kernel,
        grid=(DEPTH, n_col // MOD_TN),
        in_specs=[
            pl.BlockSpec((N_MOD_ROWS, D_MODEL), lambda l, j: (0, 0)),
            pl.BlockSpec((None, D_MODEL, MOD_TN), lambda l, j: (l, 0, j)),
            pl.BlockSpec((None, 1, MOD_TN), lambda l, j: (l, 0, j)),
        ],
        out_specs=pl.BlockSpec((None, N_MOD_ROWS, MOD_TN), lambda l, j: (l, 0, j)),
        out_shape=jax.ShapeDtypeStruct((DEPTH, N_MOD_ROWS, n_col), F32),
        compiler_params=_cparams(("arbitrary", "arbitrary")),
        name="modulation",
    )(cond, w_ada, b_ada.reshape(DEPTH, 1, n_col))


IN_TM = 2048
IN_TN = 1024
NORM_ROWS = 256


def _stream_specs(xs, tm, grid_rank):
    def imap(f):
        return (lambda i: f(i)) if grid_rank == 1 else (lambda i, j: f(i))

    width = xs[0].shape[1]
    if len(xs) == 1:
        return [pl.BlockSpec((tm, width), imap(lambda i: (i, 0)))]
    n_ctx_tiles = N_CTX_TOK // tm
    return [pl.BlockSpec((tm, width), imap(lambda i: (jnp.minimum(i, n_ctx_tiles - 1), 0))),
            pl.BlockSpec((tm, width), imap(lambda i: (jnp.maximum(i - n_ctx_tiles, 0), 0)))]


def _stream_rows(x_refs, tile, tm, rows):
    if len(x_refs) == 1:
        return x_refs[0][rows, :]
    return jnp.where(tile < N_CTX_TOK // tm, x_refs[0][rows, :], x_refs[1][rows, :])


def _modulated_norm_to(h_ref, x_rows, nw_ref, sc_ref, sh_ref, rows):
    for r in range(0, rows, NORM_ROWS):
        x = x_rows(slice(r, r + NORM_ROWS))
        h = _rms(x, nw_ref[...]) * (1.0 + sc_ref[...]) + sh_ref[...]
        h_ref[r:r + NORM_ROWS, :] = h.astype(BF16)


def _inproj_kernel(*refs, n_x, tm):
    x_refs = refs[:n_x]
    nw_ref, sh_ref, sc_ref, w_ref, om_ref, os_ref, h_ref = refs[n_x:]
    i = pl.program_id(0)
    j = pl.program_id(1)

    @pl.when(j == 0)
    def _():
        _modulated_norm_to(h_ref, lambda rows: _stream_rows(x_refs, i, tm, rows),
                           nw_ref, sc_ref, sh_ref, tm)

    @pl.when(j < N_MAIN_TILES)
    def _():
        om_ref[...] = jnp.dot(h_ref[...], w_ref[...], preferred_element_type=F32).astype(BF16)

    @pl.when(j == N_MAIN_TILES)
    def _():
        os_ref[...] = jnp.dot(h_ref[...], w_ref[...], preferred_element_type=F32)


def _in_projection(xs, mod_l, norm_w, w_in_p, layer):
    tm = IN_TM if len(xs) == 1 else IN_TM // 2
    row = functools.partial(_mod_row, tile_rows=tm)
    return pl.pallas_call(
        functools.partial(_inproj_kernel, n_x=len(xs), tm=tm),
        grid=(N_TOK // tm, NP // IN_TN),
        in_specs=_stream_specs(xs, tm, 2) + [
            pl.BlockSpec((1, D_MODEL), lambda i, j: (0, 0)),
            pl.BlockSpec((None, None, 1, D_MODEL), lambda i, j: (row(i), 0, 0, 0)),
            pl.BlockSpec((None, None, 1, D_MODEL), lambda i, j: (row(i), 1, 0, 0)),
            pl.BlockSpec((None, D_MODEL, IN_TN), lambda i, j: (layer, 0, j)),
        ],
        out_specs=[
            pl.BlockSpec((tm, IN_TN), lambda i, j: (i, jnp.minimum(j, N_MAIN_TILES - 1))),
            pl.BlockSpec((tm, IN_TN), lambda i, j: (i, 0)),
        ],
        out_shape=[
            jax.ShapeDtypeStruct((N_TOK, NP_MAIN), BF16),
            jax.ShapeDtypeStruct((N_TOK, IN_TN), F32),
        ],
        scratch_shapes=[pltpu.VMEM((tm, D_MODEL), BF16)],
        compiler_params=_cparams(("arbitrary", "arbitrary")),
        name="in_projection",
    )(*xs, norm_w, mod_l, mod_l, w_in_p)


def _conv3_tile(u, prev_row, next_row, w_ref, rows, inner_masks=None):
    ridx = lax.broadcasted_iota(jnp.int32, (SUBLANE, 1), 0)
    up = pltpu.roll(u, 1, axis=0)
    up = jnp.concatenate([jnp.where(ridx == 0, prev_row, up[0:SUBLANE]), up[SUBLANE:]], axis=0)
    dn = pltpu.roll(u, rows - 1, axis=0)
    dn = jnp.concatenate([dn[:rows - SUBLANE], jnp.where(ridx == SUBLANE - 1, next_row, dn[rows - SUBLANE:])],
                         axis=0)
    if inner_masks is not None:
        up = up * inner_masks[0]
        dn = dn * inner_masks[1]
    return up * w_ref[0:1, :] + u * w_ref[1:2, :] + dn * w_ref[2:3, :]


SSD_TQ = 256
SSD_CPT = SSD_TQ // CHUNK
HALO = 2 * SUBLANE
N_SEQ = BATCH + DEC_BATCH
BC_WIDTH = 2 * SSM_GROUPS * SSM_STATE


def _tile_neighbours(i, tile_rows):
    n_ctx_tiles = N_CTX_TOK // tile_rows
    tiles_per_lat = DEC_SEQ // tile_rows
    t_in_seq = (i - n_ctx_tiles) % tiles_per_lat
    is_lat = i >= n_ctx_tiles
    has_prev = jnp.logical_and(is_lat, t_in_seq > 0).astype(F32)
    has_next = jnp.logical_and(is_lat, t_in_seq < tiles_per_lat - 1).astype(F32)
    return has_prev, has_next


def _ssd_conv_kernel(x_ref, xp_ref, xn_ref, bc_ref, bcp_ref, bcn_ref,
                     cwx_ref, cbx_ref, cwb_ref, cbb_ref, xc_ref, bcc_ref):
    hp, hn = _tile_neighbours(pl.program_id(0), SSD_TQ)

    def conv_silu(u_ref, up_ref, un_ref, w_ref, b_ref):
        prev_row = up_ref[HALO - 1:HALO, :].astype(F32) * hp
        next_row = un_ref[0:1, :].astype(F32) * hn
        return _silu(_conv3_tile(u_ref[...].astype(F32), prev_row, next_row, w_ref, SSD_TQ) + b_ref[...])

    xc_ref[...] = conv_silu(x_ref, xp_ref, xn_ref, cwx_ref, cbx_ref)
    bcc_ref[...] = conv_silu(bc_ref, bcp_ref, bcn_ref, cwb_ref, cbb_ref)


def _ssd_conv(p, conv_wx, conv_bx, conv_wbc, conv_bbc):
    halo_per_tile = SSD_TQ // HALO
    n_halo_blocks = N_TOK // HALO
    cx, cbc = OFF_SX // SSM_INNER, OFF_BC // BC_WIDTH

    def prev_map(col):
        return lambda i: (jnp.maximum(i * halo_per_tile - 1, 0), col)

    def next_map(col):
        return lambda i: (jnp.minimum((i + 1) * halo_per_tile, n_halo_blocks - 1), col)

    def const2(i):
        return (0, 0)

    return pl.pallas_call(
        _ssd_conv_kernel,
        grid=(N_TOK // SSD_TQ,),
        in_specs=[
            pl.BlockSpec((SSD_TQ, SSM_INNER), lambda i: (i, cx)),
            pl.BlockSpec((HALO, SSM_INNER), prev_map(cx)),
            pl.BlockSpec((HALO, SSM_INNER), next_map(cx)),
            pl.BlockSpec((SSD_TQ, BC_WIDTH), lambda i: (i, cbc)),
            pl.BlockSpec((HALO, BC_WIDTH), prev_map(cbc)),
            pl.BlockSpec((HALO, BC_WIDTH), next_map(cbc)),
            pl.BlockSpec((3, SSM_INNER), const2),
            pl.BlockSpec((1, SSM_INNER), const2),
            pl.BlockSpec((3, BC_WIDTH), const2),
            pl.BlockSpec((1, BC_WIDTH), const2),
        ],
        out_specs=[
            pl.BlockSpec((SSD_TQ, SSM_INNER), lambda i: (i, 0)),
            pl.BlockSpec((SSD_TQ, BC_WIDTH), lambda i: (i, 0)),
        ],
        out_shape=[
            jax.ShapeDtypeStruct((N_TOK, SSM_INNER), F32),
            jax.ShapeDtypeStruct((N_TOK, BC_WIDTH), F32),
        ],
        compiler_params=_cparams(("arbitrary",)),
        name="ssd_conv",
    )(p, p, p, p, p, p, conv_wx, conv_bx, conv_wbc, conv_bbc)


def _ssd_tables():
    blk_f, blk_b, seq, first, last = [], [], [], [], []
    for s in range(N_SEQ):
        if s < BATCH:
            base, nt = s * SEQ // SSD_TQ, SEQ // SSD_TQ
        else:
            base, nt = (N_CTX_TOK + (s - BATCH) * DEC_SEQ) // SSD_TQ, DEC_SEQ // SSD_TQ
        for k in range(nt):
            blk_f.append(base + k)
            blk_b.append(base + nt - 1 - k)
            seq.append(s)
            first.append(int(k == 0))
            last.append(int(k == nt - 1))
    return [np.asarray(a, np.int32) for a in (blk_f, blk_b, seq, first, last)]


def _split3(a):
    a1 = a.astype(BF16)
    r1 = a - a1.astype(F32)
    a2 = r1.astype(BF16)
    a3 = (r1 - a2.astype(F32)).astype(BF16)
    return a1, a2, a3


def _dot3(lhs_bf16, a):
    a1, a2, a3 = _split3(a)
    return (jnp.dot(lhs_bf16, a1, preferred_element_type=F32)
            + jnp.dot(lhs_bf16, a2, preferred_element_type=F32)
            + jnp.dot(lhs_bf16, a3, preferred_element_type=F32))


def _softplus(x):
    return jnp.maximum(x, 0.0) + jnp.log1p(jnp.exp(-jnp.abs(x)))


def _ssd_kernel(blkf_t, blkb_t, seq_t, first_t, last_t,
                xf_ref, bcf_ref, dtf_ref, xb_ref, bcb_ref, dtb_ref,
                alog_ref, dtbias_ref, dsk_ref, h0f_ref, h0b_ref,
                yf_ref, yb_ref, finf_ref, finb_ref,
                stf_ref, stb_ref):
    s = pl.program_id(0)
    seq = seq_t[s]
    dirs = ((xf_ref, bcf_ref, dtf_ref, stf_ref, yf_ref), (xb_ref, bcb_ref, dtb_ref, stb_ref, yb_ref))

    @pl.when(jnp.logical_and(first_t[s] == 1, seq < BATCH))
    def _():
        stf_ref[...] = jnp.zeros_like(stf_ref)
        stb_ref[...] = jnp.zeros_like(stb_ref)

    @pl.when(jnp.logical_and(first_t[s] == 1, seq >= BATCH))
    def _():
        stf_ref[...] = h0f_ref[...].T
        stb_ref[...] = h0b_ref[...].T

    ii = lax.broadcasted_iota(jnp.int32, (CHUNK, CHUNK), 0)
    jj = lax.broadcasted_iota(jnp.int32, (CHUNK, CHUNK), 1)
    masks = (jj <= ii, jj >= ii)
    masks_b = tuple(jnp.where(m, 1.0, 0.0).astype(BF16) for m in masks)
    masks_neg = tuple(jnp.where(m, 0.0, NEG_BIG) for m in masks)
    lo = jj < SSM_HEAD_DIM

    e_r = lax.broadcasted_iota(jnp.int32, (LANE, SSM_INNER), 0)
    e_c = lax.broadcasted_iota(jnp.int32, (LANE, SSM_INNER), 1)
    expand = jnp.where(jnp.right_shift(e_c, 6) == e_r, 1.0, 0.0).astype(BF16)

    def chunk_setup(d, c):
        _, _, dt_ref, _, _ = dirs[d]
        rows = slice(c * CHUNK, (c + 1) * CHUNK)
        a_row = -jnp.exp(alog_ref[d])
        dt = _softplus(dt_ref[rows, :] + dtbias_ref[d])
        cum = _dot3(masks_b[d], dt * a_row)
        cum_t = cum.T
        dt_t = dt.T
        end = CHUNK - 1 if d == 0 else 0
        tot_row = cum[end:end + 1, :]
        tot_col = cum_t[:, end:end + 1]
        return dict(
            rows=rows, cum=cum, cum_t=cum_t, dt_t=dt_t,
            w_rows=jnp.exp(tot_col - cum_t) * dt_t,
            sdec=_dot3_rows(jnp.exp(tot_row), expand))

    def group_setup(d, cs, g):
        _, bc_ref, _, _, _ = dirs[d]
        b_g = bc_ref[cs["rows"], g * SSM_STATE:(g + 1) * SSM_STATE]
        c_g = bc_ref[cs["rows"], (SSM_GROUPS + g) * SSM_STATE:(SSM_GROUPS + g + 1) * SSM_STATE].astype(BF16)
        b_gt = b_g.T
        cb = jnp.dot(c_g, b_gt.astype(BF16), preferred_element_type=F32)
        return c_g, b_gt, cb

    def pair_step(d, cs, gs, kp):
        x_ref, _, _, st_ref, y_ref = dirs[d]
        c_g, b_gt, cb = gs
        cum, cum_t, dt_t = cs["cum"], cs["cum_t"], cs["dt_t"]
        h_a, h_b = 2 * kp, 2 * kp + 1
        ls = slice(kp * LANE, (kp + 1) * LANE)
        x_pair = x_ref[cs["rows"], ls]
        rhs = jnp.concatenate([jnp.where(lo, x_pair, 0.0).astype(BF16),
                               jnp.where(lo, 0.0, x_pair).astype(BF16)], axis=0)

        def head_lhs(h):
            col = jnp.broadcast_to(cum[:, h:h + 1], (CHUNK, CHUNK))
            seg = col - cum_t[h:h + 1, :]
            dec = jnp.exp(seg + masks_neg[d])
            w_intra = dec * cb * dt_t[h:h + 1, :]
            w_state = b_gt * cs["w_rows"][h:h + 1, :]
            return w_intra.astype(BF16), w_state.astype(BF16), col

        wi_a, ws_a, col_a = head_lhs(h_a)
        wi_b, ws_b, col_b = head_lhs(h_b)
        lhs = jnp.concatenate([jnp.concatenate([wi_a, wi_b], axis=1),
                               jnp.concatenate([ws_a, ws_b], axis=1)], axis=0)
        both = jnp.dot(lhs, rhs, preferred_element_type=F32)
        y_diag = both[0:CHUNK, :]
        d_state = both[CHUNK:2 * CHUNK, :]

        h_pair = st_ref[:, ls]
        y_off = jnp.dot(c_g, h_pair.astype(BF16), preferred_element_type=F32)
        e_pair = jnp.exp(jnp.where(lo, col_a, col_b))
        y_ref[cs["rows"], ls] = (y_diag + y_off * e_pair + x_pair * dsk_ref[d, :, ls]).astype(y_ref.dtype)
        st_ref[:, ls] = h_pair * cs["sdec"][:, ls] + d_state

    pairs_per_group = SSM_HEADS // SSM_GROUPS // 2
    setups = [(chunk_setup(0, k), chunk_setup(1, SSD_CPT - 1 - k)) for k in range(SSD_CPT)]
    for k in range(SSD_CPT):
        cs = setups[k]
        for g in range(SSM_GROUPS):
            gs = (group_setup(0, cs[0], g), group_setup(1, cs[1], g))
            for kp in range(g * pairs_per_group, (g + 1) * pairs_per_group):
                pair_step(0, cs[0], gs[0], kp)
                pair_step(1, cs[1], gs[1], kp)

    @pl.when(jnp.logical_and(last_t[s] == 1, seq < BATCH))
    def _():
        finf_ref[...] = stf_ref[...].T
        finb_ref[...] = stb_ref[...].T


def _dot3_rows(row, rhs_bf16):
    r8 = jnp.broadcast_to(row, (SUBLANE, row.shape[1]))
    r1, r2, r3 = _split3(r8)
    out = (jnp.dot(r1, rhs_bf16, preferred_element_type=F32)
           + jnp.dot(r2, rhs_bf16, preferred_element_type=F32)
           + jnp.dot(r3, rhs_bf16, preferred_element_type=F32))
    return out[0:1, :]


def _ssd(ps, xc, bcc, layer, alog, dtb, dskip, h0f, h0b):
    tables = [jnp.asarray(t) for t in _ssd_tables()]
    n_steps = int(tables[0].shape[0])
    cdt = (OFF_DT - NP_MAIN) // LANE

    def fwd_tile(col):
        return lambda s, blkf, *_: (blkf[s], col)

    def bwd_tile(col):
        return lambda s, blkf, blkb, *_: (blkb[s], col)

    def tile_specs(tile):
        return [
            pl.BlockSpec((SSD_TQ, SSM_INNER), tile(0)),
            pl.BlockSpec((SSD_TQ, BC_WIDTH), tile(0)),
            pl.BlockSpec((SSD_TQ, LANE), tile(cdt)),
        ]

    def h0_map(s, blkf, blkb, seq, *_):
        return (jnp.maximum(seq[s] - BATCH, 0), layer, 0, 0)

    def fin_map(s, blkf, blkb, seq, *_):
        return (jnp.minimum(seq[s], BATCH - 1), 0, 0)

    def const3(s, *_):
        return (0, 0, 0)

    grid_spec = pltpu.PrefetchScalarGridSpec(
        num_scalar_prefetch=5,
        grid=(n_steps,),
        in_specs=tile_specs(fwd_tile) + tile_specs(bwd_tile) + [
            pl.BlockSpec((2, 1, LANE), const3),
            pl.BlockSpec((2, 1, LANE), const3),
            pl.BlockSpec((2, 1, SSM_INNER), const3),
            pl.BlockSpec((None, None, SSM_INNER, SSM_STATE), h0_map),
            pl.BlockSpec((None, None, SSM_INNER, SSM_STATE), h0_map),
        ],
        out_specs=[
            pl.BlockSpec((SSD_TQ, SSM_INNER), fwd_tile(0)),
            pl.BlockSpec((SSD_TQ, SSM_INNER), bwd_tile(0)),
            pl.BlockSpec((None, SSM_INNER, SSM_STATE), fin_map),
            pl.BlockSpec((None, SSM_INNER, SSM_STATE), fin_map),
        ],
        scratch_shapes=[
            pltpu.VMEM((SSM_STATE, SSM_INNER), F32),
            pltpu.VMEM((SSM_STATE, SSM_INNER), F32),
        ],
    )
    return pl.pallas_call(
        _ssd_kernel,
        grid_spec=grid_spec,
        out_shape=[
            jax.ShapeDtypeStruct((N_TOK, SSM_INNER), BF16),
            jax.ShapeDtypeStruct((N_TOK, SSM_INNER), BF16),
            jax.ShapeDtypeStruct((BATCH, SSM_INNER, SSM_STATE), F32),
            jax.ShapeDtypeStruct((BATCH, SSM_INNER, SSM_STATE), F32),
        ],
        compiler_params=_cparams(("arbitrary",)),
        name="ssd_scan",
    )(*tables, xc, bcc, ps, xc, bcc, ps, alog, dtb, dskip, h0f, h0b)


QK_PAD = 128
N_QROWS = MLA_HEADS * QK_PAD
N_VROWS = MLA_HEADS * V_DIM
NT_DIMS = (((1,), (1,)), ((), ()))
Q_PRESCALE = (1.0 / math.sqrt(QK_DIM)) * math.log2(math.e)


def _mla_prep_kernel(*refs, tm, normalize, with_q, keep_ckv):
    if with_q:
        (cq_ref, ckv_ref, kr_ref, krs_ref, cos_ref, sin_ref, cost_ref, sint_ref,
         qnw_ref, kvnw_ref, wq_ref, wk_ref, wv_ref, qt_ref, k_ref, vt_ref) = refs[:16]
    else:
        (ckv_ref, kr_ref, krs_ref, cos_ref, sin_ref, kvnw_ref, wk_ref, wv_ref,
         k_ref, vt_ref) = refs

    ckv = ckv_ref[...]
    if normalize:
        ckv = _rms(ckv, kvnw_ref[...])
    if keep_ckv:
        refs[16][...] = ckv
    ckv_b = ckv.astype(BF16)
    kn = jnp.dot(ckv_b, wk_ref[...], preferred_element_type=F32)
    kr = kr_ref[...] * cos_ref[...] + krs_ref[...] * sin_ref[...]
    for h in range(MLA_HEADS):
        hs = slice(h * QK_PAD, (h + 1) * QK_PAD)
        k_ref[:, hs] = (kn[:, hs] + kr).astype(BF16)
    vt_ref[...] = lax.dot_general(wv_ref[...], ckv_b, NT_DIMS, preferred_element_type=F32).astype(BF16)

    if with_q:
        cqn = _rms(cq_ref[...], qnw_ref[...]).astype(BF16)
        qq = lax.dot_general(wq_ref[...], cqn, NT_DIMS, preferred_element_type=F32)
        for h in range(MLA_HEADS):
            q_h = qq[h * QK_PAD:(h + 1) * QK_PAD, :]
            qs_h = qq[N_QROWS + h * QK_PAD:N_QROWS + (h + 1) * QK_PAD, :]
            q_rot = q_h * cost_ref[...] + qs_h * sint_ref[...]
            qt_ref[h * QK_PAD:(h + 1) * QK_PAD, :] = (q_rot * Q_PRESCALE).astype(BF16)


def _mla_prep_tokens(p, row0, n_rows, tm, tab_map, cos, sin, cos_t, sin_t, qnw, kvnw, wq_t, wk, wv_t, layer,
                     keep_ckv):
    b0 = row0 // tm
    nt = n_rows // tm
    kernel = functools.partial(_mla_prep_kernel, tm=tm, normalize=True, with_q=True, keep_ckv=keep_ckv)

    def pcol(width, off):
        return pl.BlockSpec((tm, width), lambda i: (b0 + i, (off - NP_MAIN) // width))

    def full(a):
        return pl.BlockSpec(a.shape, lambda i: (0,) * a.ndim)

    return pl.pallas_call(
        kernel,
        grid=(nt,),
        in_specs=[
            pcol(Q_LORA, OFF_CQ), pcol(KV_LORA, OFF_CKV), pcol(LANE, OFF_KR), pcol(LANE, OFF_KRS),
            pl.BlockSpec((tm, LANE), lambda i: (tab_map(i), 0)),
            pl.BlockSpec((tm, LANE), lambda i: (tab_map(i), 0)),
            pl.BlockSpec((LANE, tm), lambda i: (0, tab_map(i))),
            pl.BlockSpec((LANE, tm), lambda i: (0, tab_map(i))),
            full(qnw), full(kvnw), _layer_spec(wq_t, layer, 1), _layer_spec(wk, layer, 1), _layer_spec(wv_t, layer, 1),
        ],
        out_specs=[
            pl.BlockSpec((N_QROWS, tm), lambda i: (0, i)),
            pl.BlockSpec((None, tm, N_QROWS), lambda i: (i, 0, 0)),
            pl.BlockSpec((None, N_VROWS, tm), lambda i: (i, 0, 0)),
        ] + ([pl.BlockSpec((tm, KV_LORA), lambda i: (i, 0))] if keep_ckv else []),
        out_shape=[
            jax.ShapeDtypeStruct((N_QROWS, n_rows), BF16),
            jax.ShapeDtypeStruct((nt, tm, N_QROWS), BF16),
            jax.ShapeDtypeStruct((nt, N_VROWS, tm), BF16),
        ] + ([jax.ShapeDtypeStruct((n_rows, KV_LORA), F32)] if keep_ckv else []),
        compiler_params=_cparams(("arbitrary",)),
        name="mla_prep",
    )(p, p, p, p, cos, sin, cos_t, sin_t, qnw, kvnw, wq_t, wk, wv_t)


def _mla_prep_cache(ckv, kr_pad, ones_tab, zeros_tab, kvnw, wk, wv_t, tm, layer):
    n_rows = ckv.shape[0]
    nt = n_rows // tm
    kernel = functools.partial(_mla_prep_kernel, tm=tm, normalize=False, with_q=False, keep_ckv=False)

    def full(a):
        return pl.BlockSpec(a.shape, lambda i: (0,) * a.ndim)

    return pl.pallas_call(
        kernel,
        grid=(nt,),
        in_specs=[
            pl.BlockSpec((tm, KV_LORA), lambda i: (i, 0)),
            pl.BlockSpec((tm, LANE), lambda i: (i, 0)),
            pl.BlockSpec((tm, LANE), lambda i: (i, 0)),
            pl.BlockSpec((tm, LANE), lambda i: (0, 0)),
            pl.BlockSpec((tm, LANE), lambda i: (0, 0)),
            full(kvnw), _layer_spec(wk, layer, 1), _layer_spec(wv_t, layer, 1),
        ],
        out_specs=[
            pl.BlockSpec((None, tm, N_QROWS), lambda i: (i, 0, 0)),
            pl.BlockSpec((None, N_VROWS, tm), lambda i: (i, 0, 0)),
        ],
        out_shape=[
            jax.ShapeDtypeStruct((nt, tm, N_QROWS), BF16),
            jax.ShapeDtypeStruct((nt, N_VROWS, tm), BF16),
        ],
        compiler_params=_cparams(("arbitrary",)),
        name="mla_prep_cache",
    )(ckv, kr_pad, kr_pad, ones_tab, zeros_tab, kvnw, wk, wv_t)


ATT_TQ = 256
ATT_SUB = 256


def _attn_kernel(*refs, n_kt, tq, tk, with_cache):
    if with_cache:
        qt_ref, kp_ref, vtp_ref, k_ref, vt_ref, o_ref, ot_ref, s_ref = refs
    else:
        qt_ref, k_ref, vt_ref, o_ref, ot_ref, s_ref = refs

    def k_tile(kt, h):
        cols = slice(h * QK_PAD, (h + 1) * QK_PAD)
        if with_cache:
            return kp_ref[:, cols] if kt == 0 else k_ref[kt - 1, :, cols]
        return k_ref[kt, :, cols]

    def vt_tile(kt, h):
        rows = slice(h * V_DIM, (h + 1) * V_DIM)
        if with_cache:
            return vtp_ref[rows, :] if kt == 0 else vt_ref[kt - 1, rows, :]
        return vt_ref[kt, rows, :]

    sub = min(ATT_SUB, tk)

    def scores_step(h, kt, m8):
        q_t = qt_ref[h * QK_PAD:(h + 1) * QK_PAD, :]
        k = k_tile(kt, h)
        for r in range(0, tk, sub):
            s = jnp.dot(k[r:r + sub, :], q_t, preferred_element_type=F32)
            s_ref[h % 2, kt, r:r + sub, :] = s
            m8 = jnp.maximum(m8, jnp.max(s.reshape(sub // SUBLANE, SUBLANE, tq), axis=0))
        return m8

    def probs_step(h, kt, m, l8, acc):
        v_t = vt_tile(kt, h)
        for r in range(0, tk, sub):
            pr = jnp.exp2(s_ref[h % 2, kt, r:r + sub, :] - m)
            l8 = l8 + jnp.sum(pr.reshape(sub // SUBLANE, SUBLANE, tq), axis=0)
            acc = acc + jnp.dot(v_t[:, r:r + sub], pr.astype(BF16), preferred_element_type=F32)
        return l8, acc

    m8_init = jnp.full((SUBLANE, tq), NEG_BIG, F32)
    m8 = m8_init
    for kt in range(n_kt):
        m8 = scores_step(0, kt, m8)
    for h in range(MLA_HEADS):
        m = jnp.max(m8, axis=0, keepdims=True)
        l8 = jnp.zeros((SUBLANE, tq), F32)
        acc = jnp.zeros((V_DIM, tq), F32)
        m8 = m8_init
        for kt in range(n_kt):
            l8, acc = probs_step(h, kt, m, l8, acc)
            if h + 1 < MLA_HEADS:
                m8 = scores_step(h + 1, kt, m8)
        ot_ref[h * V_DIM:(h + 1) * V_DIM, :] = acc / jnp.sum(l8, axis=0, keepdims=True)
    o_ref[...] = ot_ref[...].T.astype(o_ref.dtype)


def _attention(qt, k3, vt3, n_batch, lq, tk, cache=None):
    tq = min(ATT_TQ, lq)
    nq = lq // tq
    n_new = k3.shape[0] // n_batch
    k4 = k3.reshape(n_batch, n_new, tk, N_QROWS)
    v4 = vt3.reshape(n_batch, n_new, N_VROWS, tk)
    n_kt = n_new + (0 if cache is None else 1)
    kernel = functools.partial(_attn_kernel, n_kt=n_kt, tq=tq, tk=tk, with_cache=cache is not None)
    cache_specs = [] if cache is None else [
        pl.BlockSpec((None, tk, N_QROWS), lambda b, i: (b, 0, 0)),
        pl.BlockSpec((None, N_VROWS, tk), lambda b, i: (b, 0, 0)),
    ]
    return pl.pallas_call(
        kernel,
        grid=(n_batch, nq),
        in_specs=[pl.BlockSpec((N_QROWS, tq), lambda b, i: (0, b * nq + i))] + cache_specs + [
            pl.BlockSpec((None, n_new, tk, N_QROWS), lambda b, i: (b, 0, 0, 0)),
            pl.BlockSpec((None, n_new, N_VROWS, tk), lambda b, i: (b, 0, 0, 0)),
        ],
        out_specs=pl.BlockSpec((tq, N_VROWS), lambda b, i: (b * nq + i, 0)),
        out_shape=jax.ShapeDtypeStruct((n_batch * lq, N_VROWS), BF16),
        scratch_shapes=[pltpu.VMEM((N_VROWS, tq), F32), pltpu.VMEM((2, n_kt, tk, tq), F32)],
        compiler_params=_cparams(("arbitrary", "arbitrary")),
        name="attention",
    )(qt, *(() if cache is None else cache), k4, v4)


MG_TM = 512


def _merge_kernel(*refs, n_x):
    x_refs = refs[:n_x]
    zc_refs = refs[n_x:n_x + 2]
    (ax_ref, axp_ref, axn_ref, ac_ref, acp_ref, acn_ref, ab_ref,
     g_ref, z_ref, yf_ref, yb_ref, gm_ref,
     cw_ref, wa_ref, nw_ref, wb_ref, wc_ref, wo_ref, o_ref) = refs[n_x + 2:]
    i = pl.program_id(0)
    hp, hn = _tile_neighbours(i, MG_TM)
    all_rows = slice(0, MG_TM)

    def f32(v):
        return v.astype(F32)

    edge_keep = jnp.where(i < N_CTX_TOK // MG_TM, 0.0, 1.0)
    pos = lax.broadcasted_iota(jnp.int32, (MG_TM, 1), 0) & (SEQ - 1)
    up_mask = jnp.where(pos == 0, edge_keep, 1.0)
    dn_mask = jnp.where(pos == SEQ - 1, edge_keep, 1.0)

    u = f32(ac_ref[...]) * f32(ax_ref[...])
    u_prev = f32(acp_ref[HALO - 1:HALO, :]) * f32(axp_ref[HALO - 1:HALO, :]) * hp
    u_next = f32(acn_ref[0:1, :]) * f32(axn_ref[0:1, :]) * hn
    za = f32(ab_ref[...]) * _conv3_tile(u, u_prev, u_next, cw_ref, MG_TM, (up_mask, dn_mask))
    y_a = jnp.dot(za.astype(BF16), wa_ref[...], preferred_element_type=F32)

    yb = (f32(yf_ref[...]) + f32(yb_ref[...])) * _silu(f32(z_ref[...]))
    zb = _rms(yb, nw_ref[...])
    y_b = jnp.dot(zb.astype(BF16), wb_ref[...], preferred_element_type=F32)

    y_c = jnp.dot(_stream_rows(zc_refs, i, MG_TM, all_rows), wc_ref[...], preferred_element_type=F32)

    merged = (_sigmoid(f32(g_ref[:, 0:D_MODEL])) * y_a
              + _sigmoid(f32(g_ref[:, D_MODEL:2 * D_MODEL])) * y_b
              + _sigmoid(f32(g_ref[:, 2 * D_MODEL:3 * D_MODEL])) * y_c)
    o = jnp.dot(merged.astype(BF16), wo_ref[...], preferred_element_type=F32)
    o_ref[...] = _stream_rows(x_refs, i, MG_TM, all_rows) + gm_ref[...] * o


def _merge(p, y_f, y_b, zcs, xs, mod_l, conv_w, wa, nw, wb, wc, wo, layer):
    halo_per_tile = MG_TM // HALO
    n_halo_blocks = N_TOK // HALO
    row = functools.partial(_mod_row, tile_rows=MG_TM)

    def pcol(width, off):
        return pl.BlockSpec((MG_TM, width), lambda i: (i, off // width))

    def pprev(off):
        return pl.BlockSpec((HALO, A_WIDTH), lambda i: (jnp.maximum(i * halo_per_tile - 1, 0), off // A_WIDTH))

    def pnext(off):
        return pl.BlockSpec((HALO, A_WIDTH),
                            lambda i: (jnp.minimum((i + 1) * halo_per_tile, n_halo_blocks - 1), off // A_WIDTH))

    def full(a):
        return pl.BlockSpec(a.shape, lambda i: (0,) * a.ndim)

    return pl.pallas_call(
        functools.partial(_merge_kernel, n_x=len(xs)),
        grid=(N_TOK // MG_TM,),
        in_specs=_stream_specs(xs, MG_TM, 1) + _stream_specs(zcs, MG_TM, 1) + [
            pcol(A_WIDTH, OFF_AX), pprev(OFF_AX), pnext(OFF_AX),
            pcol(A_WIDTH, OFF_AC), pprev(OFF_AC), pnext(OFF_AC),
            pcol(A_WIDTH, OFF_AB),
            pcol(3 * D_MODEL, OFF_G),
            pcol(SSM_INNER, OFF_Z),
            pl.BlockSpec((MG_TM, SSM_INNER), lambda i: (i, 0)),
            pl.BlockSpec((MG_TM, SSM_INNER), lambda i: (i, 0)),
            pl.BlockSpec((None, None, 1, D_MODEL), lambda i: (row(i), 2, 0, 0)),
            full(conv_w), _layer_spec(wa, layer, 1), full(nw), _layer_spec(wb, layer, 1),
            _layer_spec(wc, layer, 1), _layer_spec(wo, layer, 1),
        ],
        out_specs=pl.BlockSpec((MG_TM, D_MODEL), lambda i: (i, 0)),
        out_shape=jax.ShapeDtypeStruct((N_TOK, D_MODEL), F32),
        compiler_params=_cparams(("arbitrary",)),
        name="merge",
    )(*xs, *zcs, p, p, p, p, p, p, p, p, p, y_f, y_b, mod_l, conv_w, wa, nw, wb, wc, wo)


FF_TM = 512
FF_CHUNK = 256


def _ffn_kernel(x_ref, nw_ref, sh_ref, sc_ref, gm_ref, w1_ref, w3_ref, w2_ref, o_ref, h_ref, g_ref):
    _modulated_norm_to(h_ref, lambda rows: x_ref[rows, :], nw_ref, sc_ref, sh_ref, FF_TM)
    h = h_ref[...]
    for c in range(0, FF_DIM, FF_CHUNK):
        a = jnp.dot(h, w1_ref[:, c:c + FF_CHUNK], preferred_element_type=F32)
        b = jnp.dot(h, w3_ref[:, c:c + FF_CHUNK], preferred_element_type=F32)
        g_ref[:, c:c + FF_CHUNK] = (_silu(a) * b).astype(BF16)
    ff = jnp.dot(g_ref[...], w2_ref[...], preferred_element_type=F32)
    o_ref[...] = x_ref[...] + gm_ref[...] * ff


def _ffn(x, mod_l, norm_w, w1, w3, w2, layer):
    row = functools.partial(_mod_row, tile_rows=FF_TM)
    return pl.pallas_call(
        _ffn_kernel,
        grid=(N_TOK // FF_TM,),
        in_specs=[
            pl.BlockSpec((FF_TM, D_MODEL), lambda i: (i, 0)),
            pl.BlockSpec((1, D_MODEL), lambda i: (0, 0)),
            pl.BlockSpec((None, None, 1, D_MODEL), lambda i: (row(i), 3, 0, 0)),
            pl.BlockSpec((None, None, 1, D_MODEL), lambda i: (row(i), 4, 0, 0)),
            pl.BlockSpec((None, None, 1, D_MODEL), lambda i: (row(i), 5, 0, 0)),
            _layer_spec(w1, layer, 1), _layer_spec(w3, layer, 1), _layer_spec(w2, layer, 1),
        ],
        out_specs=pl.BlockSpec((FF_TM, D_MODEL), lambda i: (i, 0)),
        out_shape=jax.ShapeDtypeStruct((N_TOK, D_MODEL), F32),
        scratch_shapes=[pltpu.VMEM((FF_TM, D_MODEL), BF16), pltpu.VMEM((FF_TM, FF_DIM), BF16)],
        compiler_params=_cparams(("arbitrary",)),
        name="ffn",
    )(x, norm_w, mod_l, mod_l, mod_l, w1, w3, w2)


FN_TM = 1024


def _final_norm_kernel(x_ref, w_ref, o_ref):
    o_ref[...] = _rms(x_ref[...], w_ref[...])


def _final_norm(x, w, row0, n_rows):
    b0 = row0 // FN_TM
    return pl.pallas_call(
        _final_norm_kernel,
        grid=(n_rows // FN_TM,),
        in_specs=[pl.BlockSpec((FN_TM, D_MODEL), lambda i: (b0 + i, 0)),
                  pl.BlockSpec((1, D_MODEL), lambda i: (0, 0))],
        out_specs=pl.BlockSpec((FN_TM, D_MODEL), lambda i: (i, 0)),
        out_shape=jax.ShapeDtypeStruct((n_rows, D_MODEL), F32),
        compiler_params=_cparams(("arbitrary",)),
        name="final_norm",
    )(x, w)


def _pad_in_weights(w_in):
    o = 0
    a_x = w_in[..., o:o + 512]; o += 512
    a_b = w_in[..., o:o + 512]; o += 512
    a_c = w_in[..., o:o + 512]; o += 512
    s_z = w_in[..., o:o + 1024]; o += 1024
    s_x = w_in[..., o:o + 1024]; o += 1024
    s_bc = w_in[..., o:o + 512]; o += 512
    s_dt = w_in[..., o:o + 16]; o += 16
    cq = w_in[..., o:o + 256]; o += 256
    ckv = w_in[..., o:o + 256]; o += 256
    kr = w_in[..., o:o + 32]; o += 32
    gates = w_in[..., o:o + 3072]; o += 3072

    def z(n):
        return jnp.zeros(w_in.shape[:-1] + (n,), w_in.dtype)

    kr_sw = jnp.concatenate([kr[..., 16:], kr[..., :16]], axis=-1)
    cols = [gates, s_z, s_x, a_x, a_b, a_c, s_bc, cq, ckv,
            s_dt, z(LANE - 16),
            z(64), kr, z(32),
            z(64), kr_sw, z(32),
            z(NP - OFF_KRS - LANE)]
    out = jnp.concatenate(cols, axis=-1).astype(BF16)
    assert out.shape[-1] == NP
    return out


def _q_weights_t(w_uq):
    w = w_uq.reshape(DEPTH, Q_LORA, MLA_HEADS, QK_DIM)
    nope, x1, x2 = w[..., :NOPE_DIM], w[..., NOPE_DIM:NOPE_DIM + 16], w[..., NOPE_DIM + 16:]
    z32 = jnp.zeros_like(w[..., :32])
    z64 = jnp.zeros_like(nope)
    q = jnp.concatenate([nope, x1, x2, z32], axis=-1).reshape(DEPTH, Q_LORA, N_QROWS)
    qs = jnp.concatenate([z64, x2, x1, z32], axis=-1).reshape(DEPTH, Q_LORA, N_QROWS)
    return jnp.swapaxes(jnp.concatenate([q, qs], axis=-1), 1, 2).astype(BF16)


def _kv_weights(w_ukv):
    w = w_ukv.reshape(DEPTH, KV_LORA, MLA_HEADS, NOPE_DIM + V_DIM)
    kn = jnp.concatenate([w[..., :NOPE_DIM], jnp.zeros_like(w[..., :QK_PAD - NOPE_DIM])], axis=-1)
    wk = kn.reshape(DEPTH, KV_LORA, N_QROWS).astype(BF16)
    wv_t = jnp.swapaxes(w[..., NOPE_DIM:].reshape(DEPTH, KV_LORA, N_VROWS), 1, 2).astype(BF16)
    return wk, wv_t


def _rope_tables(n_tokens, lead_rows):
    n_rows = n_tokens // GRID_W
    row = jnp.repeat(jnp.arange(n_rows, dtype=F32), GRID_W)
    col = jnp.tile(jnp.arange(GRID_W, dtype=F32), n_rows)
    pairs = ROPE_DIM // 4
    inv = ROPE_BASE ** (-jnp.arange(pairs, dtype=F32) / pairs)
    ang = jnp.concatenate([row[:, None] * inv, col[:, None] * inv], axis=-1)
    cos, sin = jnp.cos(ang), jnp.sin(ang)
    ones = jnp.ones((n_tokens, NOPE_DIM), F32)
    z32 = jnp.zeros((n_tokens, 32), F32)
    cos_l = jnp.concatenate([ones, cos, cos, z32], axis=-1)
    sin_l = jnp.concatenate([jnp.zeros_like(ones), -sin, sin, z32], axis=-1)
    ident_c = jnp.concatenate([jnp.ones((lead_rows, NOPE_DIM + ROPE_DIM), F32), jnp.zeros((lead_rows, 32), F32)], -1)
    ident_s = jnp.zeros((lead_rows, LANE), F32)
    return jnp.concatenate([ident_c, cos_l], axis=0), jnp.concatenate([ident_s, sin_l], axis=0)


PREP_TM_CTX = SEQ
PREP_TM_LAT = 512


def kernel(x_prompt, x_sample, c, cache_ckv, cache_krope, state_ssm_fwd, state_ssm_bwd, c_ctx, w_in, a_conv_w, w_a_out, ssm_conv_w, ssm_conv_b, ssm_a_log, ssm_dt_bias, ssm_d, ssm_norm_w, w_b_out, q_norm_w, w_uq, kv_norm_w, w_ukv, w_c_out, w_o, w_ada, b_ada, norm1_w, norm2_w, w_ff1, w_ff3, w_ff2, final_norm_w):
    w_in_p = _pad_in_weights(w_in)
    wq_t = _q_weights_t(w_uq)
    wk, wv_t = _kv_weights(w_ukv)
    wa, wb, wc, wo = (w.astype(BF16) for w in (w_a_out, w_b_out, w_c_out, w_o))
    w1, w3, w2 = (w.astype(BF16) for w in (w_ff1, w_ff3, w_ff2))

    cond = jnp.concatenate([c_ctx[None, :], c, jnp.zeros((N_MOD_ROWS - 1 - DEC_BATCH, D_MODEL), F32)], axis=0)
    mod = _modulation(cond, w_ada, b_ada).reshape(DEPTH, N_MOD_ROWS, 6, 1, D_MODEL)

    conv_wx = ssm_conv_w[..., :SSM_INNER]
    conv_wbc = ssm_conv_w[..., SSM_INNER:]
    conv_bx = ssm_conv_b[:, None, :SSM_INNER]
    conv_bbc = ssm_conv_b[:, None, SSM_INNER:]
    pad_h = ((0, 0), (0, 0), (0, 0), (0, LANE - SSM_HEADS))
    alog = jnp.pad(ssm_a_log[:, :, None, :], pad_h)
    dtb = jnp.pad(ssm_dt_bias[:, :, None, :], pad_h)
    dskip = jnp.repeat(ssm_d, SSM_HEAD_DIM, axis=-1)[:, :, None, :]
    h0f = state_ssm_fwd.reshape(DEC_BATCH, DEPTH, SSM_INNER, SSM_STATE)
    h0b = state_ssm_bwd.reshape(DEC_BATCH, DEPTH, SSM_INNER, SSM_STATE)

    cos_c, sin_c = _rope_tables(DEC_SEQ, PREP_TM_CTX)
    cos_l, sin_l = cos_c[PREP_TM_CTX:], sin_c[PREP_TM_CTX:]
    ident_cos, ident_sin = cos_c[:PREP_TM_CTX], sin_c[:PREP_TM_CTX]
    ones_tab = jnp.concatenate([ident_cos, ident_cos], axis=0)
    zeros_tab = jnp.zeros_like(ones_tab)
    cache_kr_pad = jnp.pad(cache_krope, ((0, 0), (0, 0), (0, 0), (64, 32)))

    xs = (x_prompt.reshape(N_CTX_TOK, D_MODEL), x_sample.reshape(N_LAT_TOK, D_MODEL))

    lat_tiles_per_seq = DEC_SEQ // PREP_TM_LAT
    new_ckv, new_kr, new_f, new_b = [], [], [], []
    for l in range(DEPTH):
        p, ps = _in_projection(xs, mod[l], norm1_w[l][None, :], w_in_p, l)

        xc, bcc = _ssd_conv(p, conv_wx[l], conv_bx[l], conv_wbc[l], conv_bbc[l])
        y_f, y_b, fin_f, fin_b = _ssd(ps, xc, bcc, l, alog[l], dtb[l], dskip[l], h0f, h0b)
        new_f.append(fin_f)
        new_b.append(fin_b)

        qnw, kvnw = q_norm_w[l][None, :], kv_norm_w[l][None, :]
        qt_c, k_c, vt_c, ckv_c = _mla_prep_tokens(
            ps, 0, N_CTX_TOK, PREP_TM_CTX, lambda i: 0, ident_cos, ident_sin, ident_cos.T, ident_sin.T,
            qnw, kvnw, wq_t, wk, wv_t, l, keep_ckv=True)
        qt_l, k_l, vt_l = _mla_prep_tokens(
            ps, N_CTX_TOK, N_LAT_TOK, PREP_TM_LAT, lambda i: i % lat_tiles_per_seq, cos_l, sin_l, cos_l.T, sin_l.T,
            qnw, kvnw, wq_t, wk, wv_t, l, keep_ckv=False)
        k_p, vt_p = _mla_prep_cache(cache_ckv[:, l].reshape(DEC_BATCH * PAST_LEN, KV_LORA),
                                    cache_kr_pad[:, l].reshape(DEC_BATCH * PAST_LEN, LANE),
                                    ones_tab, zeros_tab, kvnw, wk, wv_t, PAST_LEN, l)
        new_ckv.append(ckv_c.reshape(BATCH, SEQ, KV_LORA))
        kr0 = OFF_KR - NP_MAIN + 64
        new_kr.append(ps[:N_CTX_TOK, kr0:kr0 + ROPE_DIM].reshape(BATCH, SEQ, ROPE_DIM))

        zc_c = _attention(qt_c, k_c, vt_c, BATCH, SEQ, SEQ)
        zc_l = _attention(qt_l, k_l, vt_l, DEC_BATCH, DEC_SEQ, PREP_TM_LAT, cache=(k_p, vt_p))

        x = _merge(p, y_f, y_b, (zc_c, zc_l), xs, mod[l], a_conv_w[l], wa, ssm_norm_w[l][None, :],
                   wb, wc, wo, l)
        x = _ffn(x, mod[l], norm2_w[l][None, :], w1, w3, w2, l)
        xs = (x,)

    fw = final_norm_w[None, :]
    y_prompt = _final_norm(x, fw, 0, N_CTX_TOK).reshape(BATCH, SEQ, D_MODEL)
    y_sample = _final_norm(x, fw, N_CTX_TOK, N_LAT_TOK).reshape(DEC_BATCH, DEC_SEQ, D_MODEL)
    hshape = (BATCH, DEPTH, SSM_HEADS, SSM_HEAD_DIM, SSM_STATE)
    return (y_prompt, y_sample,
            jnp.stack(new_ckv, axis=1), jnp.stack(new_kr, axis=1),
            jnp.stack(new_f, axis=1).reshape(hshape), jnp.stack(new_b, axis=1).reshape(hshape))
```

```python
import functools
import math

import jax
import jax.numpy as jnp
import numpy as np
from jax import lax
from jax.experimental import pallas as pl
from jax.experimental.pallas import tpu as pltpu

F32 = jnp.float32
BF16 = jnp.bfloat16

D_MODEL = 1024
BATCH = 16
SEQ = 256
DEPTH = 4
DEC_BATCH = 4
DEC_SEQ = 4096
PAST_LEN = 512
GRID_W = 64
EPS = 1e-6
A_WIDTH = 512
SSM_INNER = 1024
SSM_HEAD_DIM = 64
SSM_HEADS = 16
SSM_GROUPS = 2
SSM_STATE = 128
CHUNK = 128
MLA_HEADS = 8
Q_LORA = 256
KV_LORA = 256
NOPE_DIM = 64
ROPE_DIM = 32
V_DIM = 64
QK_DIM = NOPE_DIM + ROPE_DIM
ROPE_BASE = 10000.0
FF_DIM = 2816

N_CTX_TOK = BATCH * SEQ
N_LAT_TOK = DEC_BATCH * DEC_SEQ
N_TOK = N_CTX_TOK + N_LAT_TOK
N_MOD_ROWS = 8

LANE = 128
SUBLANE = 8
VMEM_LIMIT = 56 * 1024 * 1024

OFF_G = 0
OFF_Z = 3072
OFF_SX = 4096
OFF_AX = 5120
OFF_AB = 5632
OFF_AC = 6144
OFF_BC = 6656
NP_MAIN = 7168
OFF_CQ = 7168
OFF_CKV = 7424
OFF_DT = 7680
OFF_KR = 7808
OFF_KRS = 7936
NP = 8192
N_MAIN_TILES = 7

NEG_BIG = -1e30


def _cparams(sem):
    return pltpu.CompilerParams(dimension_semantics=sem, vmem_limit_bytes=VMEM_LIMIT)


def _rms(x, w):
    ms = jnp.mean(x * x, axis=-1, keepdims=True)
    return x * lax.rsqrt(ms + EPS) * w


def _sigmoid(x):
    return 0.5 * jnp.tanh(0.5 * x) + 0.5


def _silu(x):
    h = 0.5 * x
    return h * jnp.tanh(h) + h


def _layer_spec(w, layer, grid_rank):
    zeros = (0,) * (w.ndim - 1)
    imap = (lambda i: (layer,) + zeros) if grid_rank == 1 else (lambda i, j: (layer,) + zeros)
    return pl.BlockSpec((None,) + tuple(w.shape[1:]), imap)


def _mod_row(tile, tile_rows):
    n_ctx_tiles = N_CTX_TOK // tile_rows
    tiles_per_lat = DEC_SEQ // tile_rows
    return jnp.where(tile < n_ctx_tiles, 0, 1 + (tile - n_ctx_tiles) // tiles_per_lat)


MOD_TN = 1536


def _mod_kernel(c_ref, w_ref, b_ref, o_ref):
    c = c_ref[...]
    s = _silu(c).astype(BF16)
    o_ref[...] = jnp.dot(s, w_ref[...].astype(BF16), preferred_element_type=F32) + b_ref[...]


def _modulation(cond, w_ada, b_ada):
    n_col = 6 * D_MODEL
    return pl.pallas_call(
        _mod_kernel,
        grid=(DEPTH, n_col // MOD_TN),
        in_specs=[
            pl.BlockSpec((N_MOD_ROWS, D_MODEL), lambda l, j: (0, 0)),
            pl.BlockSpec((None, D_MODEL, MOD_TN), lambda l, j: (l, 0, j)),
            pl.BlockSpec((None, 1, MOD_TN), lambda l, j: (l, 0, j)),
        ],
        out_specs=pl.BlockSpec((None, N_MOD_ROWS, MOD_TN), lambda l, j: (l, 0, j)),
        out_shape=jax.ShapeDtypeStruct((DEPTH, N_MOD_ROWS, n_col), F32),
        compiler_params=_cparams(("arbitrary", "arbitrary")),
        name="modulation",
    )(cond, w_ada, b_ada.reshape(DEPTH, 1, n_col))


IN_TM = 2048
IN_TN = 1024
NORM_ROWS = 256


def _stream_specs(xs, tm, grid_rank):
    def imap(f):
        return (lambda i: f(i)) if grid_rank == 1 else (lambda i, j: f(i))

    width = xs[0].shape[1]
    if len(xs) == 1:
        return [pl.BlockSpec((tm, width), imap(lambda i: (i, 0)))]
    n_ctx_tiles = N_CTX_TOK // tm
    return [pl.BlockSpec((tm, width), imap(lambda i: (jnp.minimum(i, n_ctx_tiles - 1), 0))),
            pl.BlockSpec((tm, width), imap(lambda i: (jnp.maximum(i - n_ctx_tiles, 0), 0)))]


def _stream_rows(x_refs, tile, tm, rows):
    if len(x_refs) == 1:
        return x_refs[0][rows, :]
    return jnp.where(tile < N_CTX_TOK // tm, x_refs[0][rows, :], x_refs[1][rows, :])


def _modulated_norm_to(h_ref, x_rows, nw_ref, sc_ref, sh_ref, rows):
    for r in range(0, rows, NORM_ROWS):
        x = x_rows(slice(r, r + NORM_ROWS))
        h = _rms(x, nw_ref[...]) * (1.0 + sc_ref[...]) + sh_ref[...]
        h_ref[r:r + NORM_ROWS, :] = h.astype(BF16)


def _inproj_kernel(*refs, n_x, tm):
    x_refs = refs[:n_x]
    nw_ref, sh_ref, sc_ref, w_ref, om_ref, os_ref, h_ref = refs[n_x:]
    i = pl.program_id(0)
    j = pl.program_id(1)

    @pl.when(j == 0)
    def _():
        _modulated_norm_to(h_ref, lambda rows: _stream_rows(x_refs, i, tm, rows),
                           nw_ref, sc_ref, sh_ref, tm)

    @pl.when(j < N_MAIN_TILES)
    def _():
        om_ref[...] = jnp.dot(h_ref[...], w_ref[...], preferred_element_type=F32).astype(BF16)

    @pl.when(j == N_MAIN_TILES)
    def _():
        os_ref[...] = jnp.dot(h_ref[...], w_ref[...], preferred_element_type=F32)


def _in_projection(xs, mod_l, norm_w, w_in_p, layer):
    tm = IN_TM if len(xs) == 1 else IN_TM // 2
    row = functools.partial(_mod_row, tile_rows=tm)
    return pl.pallas_call(
        functools.partial(_inproj_kernel, n_x=len(xs), tm=tm),
        grid=(N_TOK // tm, NP // IN_TN),
        in_specs=_stream_specs(xs, tm, 2) + [
            pl.BlockSpec((1, D_MODEL), lambda i, j: (0, 0)),
            pl.BlockSpec((None, None, 1, D_MODEL), lambda i, j: (row(i), 0, 0, 0)),
            pl.BlockSpec((None, None, 1, D_MODEL), lambda i, j: (row(i), 1, 0, 0)),
            pl.BlockSpec((None, D_MODEL, IN_TN), lambda i, j: (layer, 0, j)),
        ],
        out_specs=[
            pl.BlockSpec((tm, IN_TN), lambda i, j: (i, jnp.minimum(j, N_MAIN_TILES - 1))),
            pl.BlockSpec((tm, IN_TN), lambda i, j: (i, 0)),
        ],
        out_shape=[
            jax.ShapeDtypeStruct((N_TOK, NP_MAIN), BF16),
            jax.ShapeDtypeStruct((N_TOK, IN_TN), F32),
        ],
        scratch_shapes=[pltpu.VMEM((tm, D_MODEL), BF16)],
        compiler_params=_cparams(("arbitrary", "arbitrary")),
        name="in_projection",
    )(*xs, norm_w, mod_l, mod_l, w_in_p)


def _conv3_tile(u, prev_row, next_row, w_ref, rows, inner_masks=None):
    ridx = lax.broadcasted_iota(jnp.int32, (SUBLANE, 1), 0)
    up = pltpu.roll(u, 1, axis=0)
    up = jnp.concatenate([jnp.where(ridx == 0, prev_row, up[0:SUBLANE]), up[SUBLANE:]], axis=0)
    dn = pltpu.roll(u, rows - 1, axis=0)
    dn = jnp.concatenate([dn[:rows - SUBLANE], jnp.where(ridx == SUBLANE - 1, next_row, dn[rows - SUBLANE:])],
                         axis=0)
    if inner_masks is not None:
        up = up * inner_masks[0]
        dn = dn * inner_masks[1]
    return up * w_ref[0:1, :] + u * w_ref[1:2, :] + dn * w_ref[2:3, :]


CONV_TQ = 512
SSD_TQ = 256
SSD_CPT = SSD_TQ // CHUNK
HALO = 2 * SUBLANE
N_SEQ = BATCH + DEC_BATCH
BC_WIDTH = 2 * SSM_GROUPS * SSM_STATE


def _tile_neighbours(i, tile_rows):
    n_ctx_tiles = N_CTX_TOK // tile_rows
    tiles_per_lat = DEC_SEQ // tile_rows
    t_in_seq = (i - n_ctx_tiles) % tiles_per_lat
    is_lat = i >= n_ctx_tiles
    has_prev = jnp.logical_and(is_lat, t_in_seq > 0).astype(F32)
    has_next = jnp.logical_and(is_lat, t_in_seq < tiles_per_lat - 1).astype(F32)
    return has_prev, has_next


def _context_edge_masks(tile, tile_rows):
    edge_keep = jnp.where(tile < N_CTX_TOK // tile_rows, 0.0, 1.0)
    pos = lax.broadcasted_iota(jnp.int32, (tile_rows, 1), 0) & (SEQ - 1)
    return jnp.where(pos == 0, edge_keep, 1.0), jnp.where(pos == SEQ - 1, edge_keep, 1.0)


def _ssd_conv_kernel(x_ref, xp_ref, xn_ref, bc_ref, bcp_ref, bcn_ref,
                     cwx_ref, cbx_ref, cwb_ref, cbb_ref, xc_ref, bcc_ref):
    i = pl.program_id(0)
    hp, hn = _tile_neighbours(i, CONV_TQ)
    masks = _context_edge_masks(i, CONV_TQ)

    def conv_silu(u_ref, up_ref, un_ref, w_ref, b_ref):
        prev_row = up_ref[HALO - 1:HALO, :].astype(F32) * hp
        next_row = un_ref[0:1, :].astype(F32) * hn
        conv = _conv3_tile(u_ref[...].astype(F32), prev_row, next_row, w_ref, CONV_TQ, masks)
        return _silu(conv + b_ref[...])

    xc_ref[...] = conv_silu(x_ref, xp_ref, xn_ref, cwx_ref, cbx_ref)
    bcc_ref[...] = conv_silu(bc_ref, bcp_ref, bcn_ref, cwb_ref, cbb_ref)


def _ssd_conv(p, conv_wx, conv_bx, conv_wbc, conv_bbc):
    halo_per_tile = CONV_TQ // HALO
    n_halo_blocks = N_TOK // HALO
    cx, cbc = OFF_SX // SSM_INNER, OFF_BC // BC_WIDTH

    def prev_map(col):
        return lambda i: (jnp.maximum(i * halo_per_tile - 1, 0), col)

    def next_map(col):
        return lambda i: (jnp.minimum((i + 1) * halo_per_tile, n_halo_blocks - 1), col)

    def const2(i):
        return (0, 0)

    return pl.pallas_call(
        _ssd_conv_kernel,
        grid=(N_TOK // CONV_TQ,),
        in_specs=[
            pl.BlockSpec((CONV_TQ, SSM_INNER), lambda i: (i, cx)),
            pl.BlockSpec((HALO, SSM_INNER), prev_map(cx)),
            pl.BlockSpec((HALO, SSM_INNER), next_map(cx)),
            pl.BlockSpec((CONV_TQ, BC_WIDTH), lambda i: (i, cbc)),
            pl.BlockSpec((HALO, BC_WIDTH), prev_map(cbc)),
            pl.BlockSpec((HALO, BC_WIDTH), next_map(cbc)),
            pl.BlockSpec((3, SSM_INNER), const2),
            pl.BlockSpec((1, SSM_INNER), const2),
            pl.BlockSpec((3, BC_WIDTH), const2),
            pl.BlockSpec((1, BC_WIDTH), const2),
        ],
        out_specs=[
            pl.BlockSpec((CONV_TQ, SSM_INNER), lambda i: (i, 0)),
            pl.BlockSpec((CONV_TQ, BC_WIDTH), lambda i: (i, 0)),
        ],
        out_shape=[
            jax.ShapeDtypeStruct((N_TOK, SSM_INNER), F32),
            jax.ShapeDtypeStruct((N_TOK, BC_WIDTH), F32),
        ],
        compiler_params=_cparams(("arbitrary",)),
        name="ssd_conv",
    )(p, p, p, p, p, p, conv_wx, conv_bx, conv_wbc, conv_bbc)


def _ssd_tables():
    blk_f, blk_b, seq, first, last = [], [], [], [], []
    for s in range(N_SEQ):
        if s < BATCH:
            base, nt = s * SEQ // SSD_TQ, SEQ // SSD_TQ
        else:
            base, nt = (N_CTX_TOK + (s - BATCH) * DEC_SEQ) // SSD_TQ, DEC_SEQ // SSD_TQ
        for k in range(nt):
            blk_f.append(base + k)
            blk_b.append(base + nt - 1 - k)
            seq.append(s)
            first.append(int(k == 0))
            last.append(int(k == nt - 1))
    return [np.asarray(a, np.int32) for a in (blk_f, blk_b, seq, first, last)]


def _split3(a):
    a1 = a.astype(BF16)
    r1 = a - a1.astype(F32)
    a2 = r1.astype(BF16)
    a3 = (r1 - a2.astype(F32)).astype(BF16)
    return a1, a2, a3


def _dot3(lhs_bf16, a):
    a1, a2, a3 = _split3(a)
    return (jnp.dot(lhs_bf16, a1, preferred_element_type=F32)
            + jnp.dot(lhs_bf16, a2, preferred_element_type=F32)
            + jnp.dot(lhs_bf16, a3, preferred_element_type=F32))


def _softplus(x):
    return jnp.maximum(x, 0.0) + jnp.log1p(jnp.exp(-jnp.abs(x)))


def _ssd_kernel(blkf_t, blkb_t, seq_t, first_t, last_t,
                xf_ref, bcf_ref, dtf_ref, xb_ref, bcb_ref, dtb_ref,
                alog_ref, dtbias_ref, dsk_ref, h0f_ref, h0b_ref,
                yf_ref, yb_ref, finf_ref, finb_ref,
                stf_ref, stb_ref):
    s = pl.program_id(0)
    seq = seq_t[s]
    dirs = ((xf_ref, bcf_ref, dtf_ref, stf_ref, yf_ref), (xb_ref, bcb_ref, dtb_ref, stb_ref, yb_ref))

    @pl.when(jnp.logical_and(first_t[s] == 1, seq < BATCH))
    def _():
        stf_ref[...] = jnp.zeros_like(stf_ref)
        stb_ref[...] = jnp.zeros_like(stb_ref)

    @pl.when(jnp.logical_and(first_t[s] == 1, seq >= BATCH))
    def _():
        stf_ref[...] = h0f_ref[...].T
        stb_ref[...] = h0b_ref[...].T

    ii = lax.broadcasted_iota(jnp.int32, (CHUNK, CHUNK), 0)
    jj = lax.broadcasted_iota(jnp.int32, (CHUNK, CHUNK), 1)
    masks = (jj <= ii, jj >= ii)
    masks_b = tuple(jnp.where(m, 1.0, 0.0).astype(BF16) for m in masks)
    masks_neg = tuple(jnp.where(m, 0.0, NEG_BIG) for m in masks)
    lo = jj < SSM_HEAD_DIM

    e_r = lax.broadcasted_iota(jnp.int32, (LANE, SSM_INNER), 0)
    e_c = lax.broadcasted_iota(jnp.int32, (LANE, SSM_INNER), 1)
    expand = jnp.where(jnp.right_shift(e_c, 6) == e_r, 1.0, 0.0).astype(BF16)

    def chunk_setup(d, c):
        _, _, dt_ref, _, _ = dirs[d]
        rows = slice(c * CHUNK, (c + 1) * CHUNK)
        a_row = -jnp.exp(alog_ref[d])
        dt = _softplus(dt_ref[rows, :] + dtbias_ref[d])
        cum = _dot3(masks_b[d], dt * a_row)
        cum_t = cum.T
        dt_t = dt.T
        end = CHUNK - 1 if d == 0 else 0
        tot_row = cum[end:end + 1, :]
        tot_col = cum_t[:, end:end + 1]
        return dict(
            rows=rows, cum=cum, cum_t=cum_t, dt_t=dt_t,
            w_rows=jnp.exp(tot_col - cum_t) * dt_t,
            sdec=_dot3_rows(jnp.exp(tot_row), expand))

    def group_setup(d, cs, g):
        _, bc_ref, _, _, _ = dirs[d]
        b_g = bc_ref[cs["rows"], g * SSM_STATE:(g + 1) * SSM_STATE]
        c_g = bc_ref[cs["rows"], (SSM_GROUPS + g) * SSM_STATE:(SSM_GROUPS + g + 1) * SSM_STATE].astype(BF16)
        b_gt = b_g.T
        cb = jnp.dot(c_g, b_gt.astype(BF16), preferred_element_type=F32)
        return c_g, b_gt, cb

    def pair_step(d, cs, gs, kp):
        x_ref, _, _, st_ref, y_ref = dirs[d]
        c_g, b_gt, cb = gs
        cum, cum_t, dt_t = cs["cum"], cs["cum_t"], cs["dt_t"]
        h_a, h_b = 2 * kp, 2 * kp + 1
        ls = slice(kp * LANE, (kp + 1) * LANE)
        x_pair = x_ref[cs["rows"], ls]
        rhs = jnp.concatenate([jnp.where(lo, x_pair, 0.0).astype(BF16),
                               jnp.where(lo, 0.0, x_pair).astype(BF16)], axis=0)

        def head_lhs(h):
            col = jnp.broadcast_to(cum[:, h:h + 1], (CHUNK, CHUNK))
            seg = col - cum_t[h:h + 1, :]
            dec = jnp.exp(seg + masks_neg[d])
            w_intra = dec * cb * dt_t[h:h + 1, :]
            w_state = b_gt * cs["w_rows"][h:h + 1, :]
            return w_intra.astype(BF16), w_state.astype(BF16), col

        wi_a, ws_a, col_a = head_lhs(h_a)
        wi_b, ws_b, col_b = head_lhs(h_b)
        lhs = jnp.concatenate([jnp.concatenate([wi_a, wi_b], axis=1),
                               jnp.concatenate([ws_a, ws_b], axis=1)], axis=0)
        both = jnp.dot(lhs, rhs, preferred_element_type=F32)
        y_diag = both[0:CHUNK, :]
        d_state = both[CHUNK:2 * CHUNK, :]

        h_pair = st_ref[:, ls]
        y_off = jnp.dot(c_g, h_pair.astype(BF16), preferred_element_type=F32)
        e_pair = jnp.exp(jnp.where(lo, col_a, col_b))
        y_ref[cs["rows"], ls] = (y_diag + y_off * e_pair + x_pair * dsk_ref[d, :, ls]).astype(y_ref.dtype)
        st_ref[:, ls] = h_pair * cs["sdec"][:, ls] + d_state

    pairs_per_group = SSM_HEADS // SSM_GROUPS // 2
    setups = [(chunk_setup(0, k), chunk_setup(1, SSD_CPT - 1 - k)) for k in range(SSD_CPT)]
    for k in range(SSD_CPT):
        cs = setups[k]
        for g in range(SSM_GROUPS):
            gs = (group_setup(0, cs[0], g), group_setup(1, cs[1], g))
            for kp in range(g * pairs_per_group, (g + 1) * pairs_per_group):
                pair_step(0, cs[0], gs[0], kp)
                pair_step(1, cs[1], gs[1], kp)

    @pl.when(jnp.logical_and(last_t[s] == 1, seq < BATCH))
    def _():
        finf_ref[...] = stf_ref[...].T
        finb_ref[...] = stb_ref[...].T


def _dot3_rows(row, rhs_bf16):
    r8 = jnp.broadcast_to(row, (SUBLANE, row.shape[1]))
    r1, r2, r3 = _split3(r8)
    out = (jnp.dot(r1, rhs_bf16, preferred_element_type=F32)
           + jnp.dot(r2, rhs_bf16, preferred_element_type=F32)
           + jnp.dot(r3, rhs_bf16, preferred_element_type=F32))
    return out[0:1, :]


def _ssd(ps, xc, bcc, layer, alog, dtb, dskip, h0f, h0b):
    tables = [jnp.asarray(t) for t in _ssd_tables()]
    n_steps = int(tables[0].shape[0])
    cdt = (OFF_DT - NP_MAIN) // LANE

    def fwd_tile(col):
        return lambda s, blkf, *_: (blkf[s], col)

    def bwd_tile(col):
        return lambda s, blkf, blkb, *_: (blkb[s], col)

    def tile_specs(tile):
        return [
            pl.BlockSpec((SSD_TQ, SSM_INNER), tile(0)),
            pl.BlockSpec((SSD_TQ, BC_WIDTH), tile(0)),
            pl.BlockSpec((SSD_TQ, LANE), tile(cdt)),
        ]

    def h0_map(s, blkf, blkb, seq, *_):
        return (jnp.maximum(seq[s] - BATCH, 0), layer, 0, 0)

    def fin_map(s, blkf, blkb, seq, *_):
        return (jnp.minimum(seq[s], BATCH - 1), 0, 0)

    def const3(s, *_):
        return (0, 0, 0)

    grid_spec = pltpu.PrefetchScalarGridSpec(
        num_scalar_prefetch=5,
        grid=(n_steps,),
        in_specs=tile_specs(fwd_tile) + tile_specs(bwd_tile) + [
            pl.BlockSpec((2, 1, LANE), const3),
            pl.BlockSpec((2, 1, LANE), const3),
            pl.BlockSpec((2, 1, SSM_INNER), const3),
            pl.BlockSpec((None, None, SSM_INNER, SSM_STATE), h0_map),
            pl.BlockSpec((None, None, SSM_INNER, SSM_STATE), h0_map),
        ],
        out_specs=[
            pl.BlockSpec((SSD_TQ, SSM_INNER), fwd_tile(0)),
            pl.BlockSpec((SSD_TQ, SSM_INNER), bwd_tile(0)),
            pl.BlockSpec((None, SSM_INNER, SSM_STATE), fin_map),
            pl.BlockSpec((None, SSM_INNER, SSM_STATE), fin_map),
        ],
        scratch_shapes=[
            pltpu.VMEM((SSM_STATE, SSM_INNER), F32),
            pltpu.VMEM((SSM_STATE, SSM_INNER), F32),
        ],
    )
    return pl.pallas_call(
        _ssd_kernel,
        grid_spec=grid_spec,
        out_shape=[
            jax.ShapeDtypeStruct((N_TOK, SSM_INNER), BF16),
            jax.ShapeDtypeStruct((N_TOK, SSM_INNER), BF16),
            jax.ShapeDtypeStruct((BATCH, SSM_INNER, SSM_STATE), F32),
            jax.ShapeDtypeStruct((BATCH, SSM_INNER, SSM_STATE), F32),
        ],
        compiler_params=_cparams(("arbitrary",)),
        name="ssd_scan",
    )(*tables, xc, bcc, ps, xc, bcc, ps, alog, dtb, dskip, h0f, h0b)


QK_PAD = 128
N_QROWS = MLA_HEADS * QK_PAD
N_VROWS = MLA_HEADS * V_DIM
NT_DIMS = (((1,), (1,)), ((), ()))
Q_PRESCALE = (1.0 / math.sqrt(QK_DIM)) * math.log2(math.e)


def _mla_prep_kernel(*refs, tm, normalize, with_q, keep_ckv):
    if with_q:
        (cq_ref, ckv_ref, kr_ref, krs_ref, cos_ref, sin_ref, cost_ref, sint_ref,
         qnw_ref, kvnw_ref, wq_ref, wk_ref, wv_ref, qt_ref, k_ref, vt_ref) = refs[:16]
    else:
        (ckv_ref, kr_ref, krs_ref, cos_ref, sin_ref, kvnw_ref, wk_ref, wv_ref,
         k_ref, vt_ref) = refs

    ckv = ckv_ref[...]
    if normalize:
        ckv = _rms(ckv, kvnw_ref[...])
    if keep_ckv:
        refs[16][...] = ckv
    ckv_b = ckv.astype(BF16)
    kn = jnp.dot(ckv_b, wk_ref[...], preferred_element_type=F32)
    kr = kr_ref[...] * cos_ref[...] + krs_ref[...] * sin_ref[...]
    for h in range(MLA_HEADS):
        hs = slice(h * QK_PAD, (h + 1) * QK_PAD)
        k_ref[:, hs] = (kn[:, hs] + kr).astype(BF16)
    vt_ref[...] = lax.dot_general(wv_ref[...], ckv_b, NT_DIMS, preferred_element_type=F32).astype(BF16)

    if with_q:
        cqn = _rms(cq_ref[...], qnw_ref[...]).astype(BF16)
        qq = lax.dot_general(wq_ref[...], cqn, NT_DIMS, preferred_element_type=F32)
        for h in range(MLA_HEADS):
            q_h = qq[h * QK_PAD:(h + 1) * QK_PAD, :]
            qs_h = qq[N_QROWS + h * QK_PAD:N_QROWS + (h + 1) * QK_PAD, :]
            q_rot = q_h * cost_ref[...] + qs_h * sint_ref[...]
            qt_ref[h * QK_PAD:(h + 1) * QK_PAD, :] = (q_rot * Q_PRESCALE).astype(BF16)


def _mla_prep_tokens(p, row0, n_rows, tm, tab_map, cos, sin, cos_t, sin_t, qnw, kvnw, wq_t, wk, wv_t, layer,
                     keep_ckv):
    b0 = row0 // tm
    nt = n_rows // tm
    kernel = functools.partial(_mla_prep_kernel, tm=tm, normalize=True, with_q=True, keep_ckv=keep_ckv)

    def pcol(width, off):
        return pl.BlockSpec((tm, width), lambda i: (b0 + i, (off - NP_MAIN) // width))

    def full(a):
        return pl.BlockSpec(a.shape, lambda i: (0,) * a.ndim)

    return pl.pallas_call(
        kernel,
        grid=(nt,),
        in_specs=[
            pcol(Q_LORA, OFF_CQ), pcol(KV_LORA, OFF_CKV), pcol(LANE, OFF_KR), pcol(LANE, OFF_KRS),
            pl.BlockSpec((tm, LANE), lambda i: (tab_map(i), 0)),
            pl.BlockSpec((tm, LANE), lambda i: (tab_map(i), 0)),
            pl.BlockSpec((LANE, tm), lambda i: (0, tab_map(i))),
            pl.BlockSpec((LANE, tm), lambda i: (0, tab_map(i))),
            full(qnw), full(kvnw), _layer_spec(wq_t, layer, 1), _layer_spec(wk, layer, 1), _layer_spec(wv_t, layer, 1),
        ],
        out_specs=[
            pl.BlockSpec((N_QROWS, tm), lambda i: (0, i)),
            pl.BlockSpec((None, tm, N_QROWS), lambda i: (i, 0, 0)),
            pl.BlockSpec((None, N_VROWS, tm), lambda i: (i, 0, 0)),
        ] + ([pl.BlockSpec((tm, KV_LORA), lambda i: (i, 0))] if keep_ckv else []),
        out_shape=[
            jax.ShapeDtypeStruct((N_QROWS, n_rows), BF16),
            jax.ShapeDtypeStruct((nt, tm, N_QROWS), BF16),
            jax.ShapeDtypeStruct((nt, N_VROWS, tm), BF16),
        ] + ([jax.ShapeDtypeStruct((n_rows, KV_LORA), F32)] if keep_ckv else []),
        compiler_params=_cparams(("arbitrary",)),
        name="mla_prep",
    )(p, p, p, p, cos, sin, cos_t, sin_t, qnw, kvnw, wq_t, wk, wv_t)


def _mla_prep_cache(ckv, kr_pad, ones_tab, zeros_tab, kvnw, wk, wv_t, tm, layer):
    n_rows = ckv.shape[0]
    nt = n_rows // tm
    kernel = functools.partial(_mla_prep_kernel, tm=tm, normalize=False, with_q=False, keep_ckv=False)

    def full(a):
        return pl.BlockSpec(a.shape, lambda i: (0,) * a.ndim)

    return pl.pallas_call(
        kernel,
        grid=(nt,),
        in_specs=[
            pl.BlockSpec((tm, KV_LORA), lambda i: (i, 0)),
            pl.BlockSpec((tm, LANE), lambda i: (i, 0)),
            pl.BlockSpec((tm, LANE), lambda i: (i, 0)),
            pl.BlockSpec((tm, LANE), lambda i: (0, 0)),
            pl.BlockSpec((tm, LANE), lambda i: (0, 0)),
            full(kvnw), _layer_spec(wk, layer, 1), _layer_spec(wv_t, layer, 1),
        ],
        out_specs=[
            pl.BlockSpec((None, tm, N_QROWS), lambda i: (i, 0, 0)),
            pl.BlockSpec((None, N_VROWS, tm), lambda i: (i, 0, 0)),
        ],
        out_shape=[
            jax.ShapeDtypeStruct((nt, tm, N_QROWS), BF16),
            jax.ShapeDtypeStruct((nt, N_VROWS, tm), BF16),
        ],
        compiler_params=_cparams(("arbitrary",)),
        name="mla_prep_cache",
    )(ckv, kr_pad, kr_pad, ones_tab, zeros_tab, kvnw, wk, wv_t)


ATT_TQ = 512
ATT_SUB = 256


def _attn_kernel(*refs, n_kt, tq, tk, with_cache):
    if with_cache:
        qt_ref, kp_ref, vtp_ref, k_ref, vt_ref, o_ref, ot_ref, s_ref = refs
    else:
        qt_ref, k_ref, vt_ref, o_ref, ot_ref, s_ref = refs

    def k_tile(kt, h):
        cols = slice(h * QK_PAD, (h + 1) * QK_PAD)
        if with_cache:
            return kp_ref[:, cols] if kt == 0 else k_ref[kt - 1, :, cols]
        return k_ref[kt, :, cols]

    def vt_tile(kt, h):
        rows = slice(h * V_DIM, (h + 1) * V_DIM)
        if with_cache:
            return vtp_ref[rows, :] if kt == 0 else vt_ref[kt - 1, rows, :]
        return vt_ref[kt, rows, :]

    sub = min(ATT_SUB, tk)

    def scores_step(h, kt, m8):
        q_t = qt_ref[h * QK_PAD:(h + 1) * QK_PAD, :]
        k = k_tile(kt, h)
        for r in range(0, tk, sub):
            s = jnp.dot(k[r:r + sub, :], q_t, preferred_element_type=F32)
            s_ref[h % 2, kt, r:r + sub, :] = s
            m8 = jnp.maximum(m8, jnp.max(s.reshape(sub // SUBLANE, SUBLANE, tq), axis=0))
        return m8

    def probs_step(h, kt, m, l8, acc):
        v_t = vt_tile(kt, h)
        for r in range(0, tk, sub):
            pr = jnp.exp2(s_ref[h % 2, kt, r:r + sub, :] - m)
            l8 = l8 + jnp.sum(pr.reshape(sub // SUBLANE, SUBLANE, tq), axis=0)
            acc = acc + jnp.dot(v_t[:, r:r + sub], pr.astype(BF16), preferred_element_type=F32)
        return l8, acc

    m8_init = jnp.full((SUBLANE, tq), NEG_BIG, F32)
    m8 = m8_init
    for kt in range(n_kt):
        m8 = scores_step(0, kt, m8)
    for h in range(MLA_HEADS):
        m = jnp.max(m8, axis=0, keepdims=True)
        l8 = jnp.zeros((SUBLANE, tq), F32)
        acc = jnp.zeros((V_DIM, tq), F32)
        m8 = m8_init
        for kt in range(n_kt):
            l8, acc = probs_step(h, kt, m, l8, acc)
            if h + 1 < MLA_HEADS:
                m8 = scores_step(h + 1, kt, m8)
        ot_ref[h * V_DIM:(h + 1) * V_DIM, :] = acc / jnp.sum(l8, axis=0, keepdims=True)
    o_ref[...] = ot_ref[...].T.astype(o_ref.dtype)


def _attention(qt, k3, vt3, n_batch, lq, tk, cache=None):
    tq = min(ATT_TQ, lq)
    nq = lq // tq
    n_new = k3.shape[0] // n_batch
    k4 = k3.reshape(n_batch, n_new, tk, N_QROWS)
    v4 = vt3.reshape(n_batch, n_new, N_VROWS, tk)
    n_kt = n_new + (0 if cache is None else 1)
    kernel = functools.partial(_attn_kernel, n_kt=n_kt, tq=tq, tk=tk, with_cache=cache is not None)
    cache_specs = [] if cache is None else [
        pl.BlockSpec((None, tk, N_QROWS), lambda b, i: (b, 0, 0)),
        pl.BlockSpec((None, N_VROWS, tk), lambda b, i: (b, 0, 0)),
    ]
    return pl.pallas_call(
        kernel,
        grid=(n_batch, nq),
        in_specs=[pl.BlockSpec((N_QROWS, tq), lambda b, i: (0, b * nq + i))] + cache_specs + [
            pl.BlockSpec((None, n_new, tk, N_QROWS), lambda b, i: (b, 0, 0, 0)),
            pl.BlockSpec((None, n_new, N_VROWS, tk), lambda b, i: (b, 0, 0, 0)),
        ],
        out_specs=pl.BlockSpec((tq, N_VROWS), lambda b, i: (b * nq + i, 0)),
        out_shape=jax.ShapeDtypeStruct((n_batch * lq, N_VROWS), BF16),
        scratch_shapes=[pltpu.VMEM((N_VROWS, tq), F32), pltpu.VMEM((2, n_kt, tk, tq), F32)],
        compiler_params=_cparams(("arbitrary", "arbitrary")),
        name="attention",
    )(qt, *(() if cache is None else cache), k4, v4)


MG_TM = 512


def _merge_kernel(*refs, n_x):
    x_refs = refs[:n_x]
    zc_refs = refs[n_x:n_x + 2]
    (ax_ref, axp_ref, axn_ref, ac_ref, acp_ref, acn_ref, ab_ref,
     g_ref, z_ref, yf_ref, yb_ref, gm_ref,
     cw_ref, wa_ref, nw_ref, wb_ref, wc_ref, wo_ref, o_ref) = refs[n_x + 2:]
    i = pl.program_id(0)
    hp, hn = _tile_neighbours(i, MG_TM)
    all_rows = slice(0, MG_TM)

    def f32(v):
        return v.astype(F32)

    u = f32(ac_ref[...]) * f32(ax_ref[...])
    u_prev = f32(acp_ref[HALO - 1:HALO, :]) * f32(axp_ref[HALO - 1:HALO, :]) * hp
    u_next = f32(acn_ref[0:1, :]) * f32(axn_ref[0:1, :]) * hn
    za = f32(ab_ref[...]) * _conv3_tile(u, u_prev, u_next, cw_ref, MG_TM, _context_edge_masks(i, MG_TM))
    y_a = jnp.dot(za.astype(BF16), wa_ref[...], preferred_element_type=F32)

    yb = (f32(yf_ref[...]) + f32(yb_ref[...])) * _silu(f32(z_ref[...]))
    zb = _rms(yb, nw_ref[...])
    y_b = jnp.dot(zb.astype(BF16), wb_ref[...], preferred_element_type=F32)

    y_c = jnp.dot(_stream_rows(zc_refs, i, MG_TM, all_rows), wc_ref[...], preferred_element_type=F32)

    merged = (_sigmoid(f32(g_ref[:, 0:D_MODEL])) * y_a
              + _sigmoid(f32(g_ref[:, D_MODEL:2 * D_MODEL])) * y_b
              + _sigmoid(f32(g_ref[:, 2 * D_MODEL:3 * D_MODEL])) * y_c)
    o = jnp.dot(merged.astype(BF16), wo_ref[...], preferred_element_type=F32)
    o_ref[...] = _stream_rows(x_refs, i, MG_TM, all_rows) + gm_ref[...] * o


def _merge(p, y_f, y_b, zcs, xs, mod_l, conv_w, wa, nw, wb, wc, wo, layer):
    halo_per_tile = MG_TM // HALO
    n_halo_blocks = N_TOK // HALO
    row = functools.partial(_mod_row, tile_rows=MG_TM)

    def pcol(width, off):
        return pl.BlockSpec((MG_TM, width), lambda i: (i, off // width))

    def pprev(off):
        return pl.BlockSpec((HALO, A_WIDTH), lambda i: (jnp.maximum(i * halo_per_tile - 1, 0), off // A_WIDTH))

    def pnext(off):
        return pl.BlockSpec((HALO, A_WIDTH),
                            lambda i: (jnp.minimum((i + 1) * halo_per_tile, n_halo_blocks - 1), off // A_WIDTH))

    def full(a):
        return pl.BlockSpec(a.shape, lambda i: (0,) * a.ndim)

    return pl.pallas_call(
        functools.partial(_merge_kernel, n_x=len(xs)),
        grid=(N_TOK // MG_TM,),
        in_specs=_stream_specs(xs, MG_TM, 1) + _stream_specs(zcs, MG_TM, 1) + [
            pcol(A_WIDTH, OFF_AX), pprev(OFF_AX), pnext(OFF_AX),
            pcol(A_WIDTH, OFF_AC), pprev(OFF_AC), pnext(OFF_AC),
            pcol(A_WIDTH, OFF_AB),
            pcol(3 * D_MODEL, OFF_G),
            pcol(SSM_INNER, OFF_Z),
            pl.BlockSpec((MG_TM, SSM_INNER), lambda i: (i, 0)),
            pl.BlockSpec((MG_TM, SSM_INNER), lambda i: (i, 0)),
            pl.BlockSpec((None, None, 1, D_MODEL), lambda i: (row(i), 2, 0, 0)),
            full(conv_w), _layer_spec(wa, layer, 1), full(nw), _layer_spec(wb, layer, 1),
            _layer_spec(wc, layer, 1), _layer_spec(wo, layer, 1),
        ],
        out_specs=pl.BlockSpec((MG_TM, D_MODEL), lambda i: (i, 0)),
        out_shape=jax.ShapeDtypeStruct((N_TOK, D_MODEL), F32),
        compiler_params=_cparams(("arbitrary",)),
        name="merge",
    )(*xs, *zcs, p, p, p, p, p, p, p, p, p, y_f, y_b, mod_l, conv_w, wa, nw, wb, wc, wo)


FF_TM = 512
FF_CHUNK = 256


def _ffn_kernel(x_ref, nw_ref, sh_ref, sc_ref, gm_ref, w1_ref, w3_ref, w2_ref, o_ref, h_ref, g_ref):
    _modulated_norm_to(h_ref, lambda rows: x_ref[rows, :], nw_ref, sc_ref, sh_ref, FF_TM)
    h = h_ref[...]
    for c in range(0, FF_DIM, FF_CHUNK):
        a = jnp.dot(h, w1_ref[:, c:c + FF_CHUNK], preferred_element_type=F32)
        b = jnp.dot(h, w3_ref[:, c:c + FF_CHUNK], preferred_element_type=F32)
        g_ref[:, c:c + FF_CHUNK] = (_silu(a) * b).astype(BF16)
    ff = jnp.dot(g_ref[...], w2_ref[...], preferred_element_type=F32)
    o_ref[...] = x_ref[...] + gm_ref[...] * ff


def _ffn(x, mod_l, norm_w, w1, w3, w2, layer):
    row = functools.partial(_mod_row, tile_rows=FF_TM)
    return pl.pallas_call(
        _ffn_kernel,
        grid=(N_TOK // FF_TM,),
        in_specs=[
            pl.BlockSpec((FF_TM, D_MODEL), lambda i: (i, 0)),
            pl.BlockSpec((1, D_MODEL), lambda i: (0, 0)),
            pl.BlockSpec((None, None, 1, D_MODEL), lambda i: (row(i), 3, 0, 0)),
            pl.BlockSpec((None, None, 1, D_MODEL), lambda i: (row(i), 4, 0, 0)),
            pl.BlockSpec((None, None, 1, D_MODEL), lambda i: (row(i), 5, 0, 0)),
            _layer_spec(w1, layer, 1), _layer_spec(w3, layer, 1), _layer_spec(w2, layer, 1),
        ],
        out_specs=pl.BlockSpec((FF_TM, D_MODEL), lambda i: (i, 0)),
        out_shape=jax.ShapeDtypeStruct((N_TOK, D_MODEL), F32),
        scratch_shapes=[pltpu.VMEM((FF_TM, D_MODEL), BF16), pltpu.VMEM((FF_TM, FF_DIM), BF16)],
        compiler_params=_cparams(("arbitrary",)),
        name="ffn",
    )(x, norm_w, mod_l, mod_l, mod_l, w1, w3, w2)


FN_TM = 1024


def _final_norm_kernel(x_ref, w_ref, o_ref):
    o_ref[...] = _rms(x_ref[...], w_ref[...])


def _final_norm(x, w, row0, n_rows):
    b0 = row0 // FN_TM
    return pl.pallas_call(
        _final_norm_kernel,
        grid=(n_rows // FN_TM,),
        in_specs=[pl.BlockSpec((FN_TM, D_MODEL), lambda i: (b0 + i, 0)),
                  pl.BlockSpec((1, D_MODEL), lambda i: (0, 0))],
        out_specs=pl.BlockSpec((FN_TM, D_MODEL), lambda i: (i, 0)),
        out_shape=jax.ShapeDtypeStruct((n_rows, D_MODEL), F32),
        compiler_params=_cparams(("arbitrary",)),
        name="final_norm",
    )(x, w)


def _pad_in_weights(w_in):
    o = 0
    a_x = w_in[..., o:o + 512]; o += 512
    a_b = w_in[..., o:o + 512]; o += 512
    a_c = w_in[..., o:o + 512]; o += 512
    s_z = w_in[..., o:o + 1024]; o += 1024
    s_x = w_in[..., o:o + 1024]; o += 1024
    s_bc = w_in[..., o:o + 512]; o += 512
    s_dt = w_in[..., o:o + 16]; o += 16
    cq = w_in[..., o:o + 256]; o += 256
    ckv = w_in[..., o:o + 256]; o += 256
    kr = w_in[..., o:o + 32]; o += 32
    gates = w_in[..., o:o + 3072]; o += 3072

    def z(n):
        return jnp.zeros(w_in.shape[:-1] + (n,), w_in.dtype)

    kr_sw = jnp.concatenate([kr[..., 16:], kr[..., :16]], axis=-1)
    cols = [gates, s_z, s_x, a_x, a_b, a_c, s_bc, cq, ckv,
            s_dt, z(LANE - 16),
            z(64), kr, z(32),
            z(64), kr_sw, z(32),
            z(NP - OFF_KRS - LANE)]
    out = jnp.concatenate(cols, axis=-1).astype(BF16)
    assert out.shape[-1] == NP
    return out


def _q_weights_t(w_uq):
    w = w_uq.reshape(DEPTH, Q_LORA, MLA_HEADS, QK_DIM)
    nope, x1, x2 = w[..., :NOPE_DIM], w[..., NOPE_DIM:NOPE_DIM + 16], w[..., NOPE_DIM + 16:]
    z32 = jnp.zeros_like(w[..., :32])
    z64 = jnp.zeros_like(nope)
    q = jnp.concatenate([nope, x1, x2, z32], axis=-1).reshape(DEPTH, Q_LORA, N_QROWS)
    qs = jnp.concatenate([z64, x2, x1, z32], axis=-1).reshape(DEPTH, Q_LORA, N_QROWS)
    return jnp.swapaxes(jnp.concatenate([q, qs], axis=-1), 1, 2).astype(BF16)


def _kv_weights(w_ukv):
    w = w_ukv.reshape(DEPTH, KV_LORA, MLA_HEADS, NOPE_DIM + V_DIM)
    kn = jnp.concatenate([w[..., :NOPE_DIM], jnp.zeros_like(w[..., :QK_PAD - NOPE_DIM])], axis=-1)
    wk = kn.reshape(DEPTH, KV_LORA, N_QROWS).astype(BF16)
    wv_t = jnp.swapaxes(w[..., NOPE_DIM:].reshape(DEPTH, KV_LORA, N_VROWS), 1, 2).astype(BF16)
    return wk, wv_t


def _rope_tables(n_tokens, lead_rows):
    n_rows = n_tokens // GRID_W
    row = jnp.repeat(jnp.arange(n_rows, dtype=F32), GRID_W)
    col = jnp.tile(jnp.arange(GRID_W, dtype=F32), n_rows)
    pairs = ROPE_DIM // 4
    inv = ROPE_BASE ** (-jnp.arange(pairs, dtype=F32) / pairs)
    ang = jnp.concatenate([row[:, None] * inv, col[:, None] * inv], axis=-1)
    cos, sin = jnp.cos(ang), jnp.sin(ang)
    ones = jnp.ones((n_tokens, NOPE_DIM), F32)
    z32 = jnp.zeros((n_tokens, 32), F32)
    cos_l = jnp.concatenate([ones, cos, cos, z32], axis=-1)
    sin_l = jnp.concatenate([jnp.zeros_like(ones), -sin, sin, z32], axis=-1)
    ident_c = jnp.concatenate([jnp.ones((lead_rows, NOPE_DIM + ROPE_DIM), F32), jnp.zeros((lead_rows, 32), F32)], -1)
    ident_s = jnp.zeros((lead_rows, LANE), F32)
    return jnp.concatenate([ident_c, cos_l], axis=0), jnp.concatenate([ident_s, sin_l], axis=0)


PREP_TM_CTX = SEQ
PREP_TM_LAT = 512


def kernel(x_prompt, x_sample, c, cache_ckv, cache_krope, state_ssm_fwd, state_ssm_bwd, c_ctx, w_in, a_conv_w, w_a_out, ssm_conv_w, ssm_conv_b, ssm_a_log, ssm_dt_bias, ssm_d, ssm_norm_w, w_b_out, q_norm_w, w_uq, kv_norm_w, w_ukv, w_c_out, w_o, w_ada, b_ada, norm1_w, norm2_w, w_ff1, w_ff3, w_ff2, final_norm_w):
    w_in_p = _pad_in_weights(w_in)
    wq_t = _q_weights_t(w_uq)
    wk, wv_t = _kv_weights(w_ukv)
    wa, wb, wc, wo = (w.astype(BF16) for w in (w_a_out, w_b_out, w_c_out, w_o))
    w1, w3, w2 = (w.astype(BF16) for w in (w_ff1, w_ff3, w_ff2))

    cond = jnp.concatenate([c_ctx[None, :], c, jnp.zeros((N_MOD_ROWS - 1 - DEC_BATCH, D_MODEL), F32)], axis=0)
    mod = _modulation(cond, w_ada, b_ada).reshape(DEPTH, N_MOD_ROWS, 6, 1, D_MODEL)

    conv_wx = ssm_conv_w[..., :SSM_INNER]
    conv_wbc = ssm_conv_w[..., SSM_INNER:]
    conv_bx = ssm_conv_b[:, None, :SSM_INNER]
    conv_bbc = ssm_conv_b[:, None, SSM_INNER:]
    pad_h = ((0, 0), (0, 0), (0, 0), (0, LANE - SSM_HEADS))
    alog = jnp.pad(ssm_a_log[:, :, None, :], pad_h)
    dtb = jnp.pad(ssm_dt_bias[:, :, None, :], pad_h)
    dskip = jnp.repeat(ssm_d, SSM_HEAD_DIM, axis=-1)[:, :, None, :]
    h0f = state_ssm_fwd.reshape(DEC_BATCH, DEPTH, SSM_INNER, SSM_STATE)
    h0b = state_ssm_bwd.reshape(DEC_BATCH, DEPTH, SSM_INNER, SSM_STATE)

    cos_c, sin_c = _rope_tables(DEC_SEQ, PREP_TM_CTX)
    cos_l, sin_l = cos_c[PREP_TM_CTX:], sin_c[PREP_TM_CTX:]
    ident_cos, ident_sin = cos_c[:PREP_TM_CTX], sin_c[:PREP_TM_CTX]
    ones_tab = jnp.concatenate([ident_cos, ident_cos], axis=0)
    zeros_tab = jnp.zeros_like(ones_tab)
    cache_kr_pad = jnp.pad(cache_krope, ((0, 0), (0, 0), (0, 0), (64, 32)))

    xs = (x_prompt.reshape(N_CTX_TOK, D_MODEL), x_sample.reshape(N_LAT_TOK, D_MODEL))

    lat_tiles_per_seq = DEC_SEQ // PREP_TM_LAT
    new_ckv, new_kr, new_f, new_b = [], [], [], []
    for l in range(DEPTH):
        p, ps = _in_projection(xs, mod[l], norm1_w[l][None, :], w_in_p, l)

        xc, bcc = _ssd_conv(p, conv_wx[l], conv_bx[l], conv_wbc[l], conv_bbc[l])
        y_f, y_b, fin_f, fin_b = _ssd(ps, xc, bcc, l, alog[l], dtb[l], dskip[l], h0f, h0b)
        new_f.append(fin_f)
        new_b.append(fin_b)

        qnw, kvnw = q_norm_w[l][None, :], kv_norm_w[l][None, :]
        qt_c, k_c, vt_c, ckv_c = _mla_prep_tokens(
            ps, 0, N_CTX_TOK, PREP_TM_CTX, lambda i: 0, ident_cos, ident_sin, ident_cos.T, ident_sin.T,
            qnw, kvnw, wq_t, wk, wv_t, l, keep_ckv=True)
        qt_l, k_l, vt_l = _mla_prep_tokens(
            ps, N_CTX_TOK, N_LAT_TOK, PREP_TM_LAT, lambda i: i % lat_tiles_per_seq, cos_l, sin_l, cos_l.T, sin_l.T,
            qnw, kvnw, wq_t, wk, wv_t, l, keep_ckv=False)
        k_p, vt_p = _mla_prep_cache(cache_ckv[:, l].reshape(DEC_BATCH * PAST_LEN, KV_LORA),
                                    cache_kr_pad[:, l].reshape(DEC_BATCH * PAST_LEN, LANE),
                                    ones_tab, zeros_tab, kvnw, wk, wv_t, PAST_LEN, l)
        new_ckv.append(ckv_c.reshape(BATCH, SEQ, KV_LORA))
        kr0 = OFF_KR - NP_MAIN + 64
        new_kr.append(ps[:N_CTX_TOK, kr0:kr0 + ROPE_DIM].reshape(BATCH, SEQ, ROPE_DIM))

        zc_c = _attention(qt_c, k_c, vt_c, BATCH, SEQ, SEQ)
        zc_l = _attention(qt_l, k_l, vt_l, DEC_BATCH, DEC_SEQ, PREP_TM_LAT, cache=(k_p, vt_p))

        x = _merge(p, y_f, y_b, (zc_c, zc_l), xs, mod[l], a_conv_w[l], wa, ssm_norm_w[l][None, :],
                   wb, wc, wo, l)
        x = _ffn(x, mod[l], norm2_w[l][None, :], w1, w3, w2, l)
        xs = (x,)

    fw = final_norm_w[None, :]
    y_prompt = _final_norm(x, fw, 0, N_CTX_TOK).reshape(BATCH, SEQ, D_MODEL)
    y_sample = _final_norm(x, fw, N_CTX_TOK, N_LAT_TOK).reshape(DEC_BATCH, DEC_SEQ, D_MODEL)
    hshape = (BATCH, DEPTH, SSM_HEADS, SSM_HEAD_DIM, SSM_STATE)
    return (y_prompt, y_sample,
            jnp.stack(new_ckv, axis=1), jnp.stack(new_kr, axis=1),
            jnp.stack(new_f, axis=1).reshape(hshape), jnp.stack(new_b, axis=1).reshape(hshape))
```

```python
import functools
import math

import jax
import jax.numpy as jnp
import numpy as np
from jax import lax
from jax.experimental import pallas as pl
from jax.experimental.pallas import tpu as pltpu

F32 = jnp.float32
BF16 = jnp.bfloat16

D_MODEL = 1024
BATCH = 16
SEQ = 256
DEPTH = 4
DEC_BATCH = 4
DEC_SEQ = 4096
PAST_LEN = 512
GRID_W = 64
EPS = 1e-6
A_WIDTH = 512
SSM_INNER = 1024
SSM_HEAD_DIM = 64
SSM_HEADS = 16
SSM_GROUPS = 2
SSM_STATE = 128
CHUNK = 128
MLA_HEADS = 8
Q_LORA = 256
KV_LORA = 256
NOPE_DIM = 64
ROPE_DIM = 32
V_DIM = 64
QK_DIM = NOPE_DIM + ROPE_DIM
ROPE_BASE = 10000.0
FF_DIM = 2816

N_CTX_TOK = BATCH * SEQ
N_LAT_TOK = DEC_BATCH * DEC_SEQ
N_TOK = N_CTX_TOK + N_LAT_TOK
N_MOD_ROWS = 8

LANE = 128
SUBLANE = 8
VMEM_LIMIT = 56 * 1024 * 1024

OFF_G = 0
OFF_Z = 3072
OFF_SX = 4096
OFF_AX = 5120
OFF_AB = 5632
OFF_AC = 6144
OFF_BC = 6656
NP_MAIN = 7168
OFF_CQ = 7168
OFF_CKV = 7424
OFF_DT = 7680
OFF_KR = 7808
OFF_KRS = 7936
NP = 8192
N_MAIN_TILES = 7

NEG_BIG = -1e30


def _cparams(sem):
    return pltpu.CompilerParams(dimension_semantics=sem, vmem_limit_bytes=VMEM_LIMIT)


def _rms(x, w):
    ms = jnp.mean(x * x, axis=-1, keepdims=True)
    return x * lax.rsqrt(ms + EPS) * w


def _sigmoid(x):
    return 0.5 * jnp.tanh(0.5 * x) + 0.5


def _silu(x):
    h = 0.5 * x
    return h * jnp.tanh(h) + h


def _layer_spec(w, layer, grid_rank):
    zeros = (0,) * (w.ndim - 1)
    imap = (lambda i: (layer,) + zeros) if grid_rank == 1 else (lambda i, j: (layer,) + zeros)
    return pl.BlockSpec((None,) + tuple(w.shape[1:]), imap)


def _mod_row(tile, tile_rows):
    n_ctx_tiles = N_CTX_TOK // tile_rows
    tiles_per_lat = DEC_SEQ // tile_rows
    return jnp.where(tile < n_ctx_tiles, 0, 1 + (tile - n_ctx_tiles) // tiles_per_lat)


MOD_TN = 1536


def _mod_kernel(c_ref, w_ref, b_ref, o_ref):
    c = c_ref[...]
    s = _silu(c).astype(BF16)
    o_ref[...] = jnp.dot(s, w_ref[...].astype(BF16), preferred_element_type=F32) + b_ref[...]


def _modulation(cond, w_ada, b_ada):
    n_col = 6 * D_MODEL
    return pl.pallas_call(
        _mod_kernel,
        grid=(DEPTH, n_col // MOD_TN),
        in_specs=[
            pl.BlockSpec((N_MOD_ROWS, D_MODEL), lambda l, j: (0, 0)),
            pl.BlockSpec((None, D_MODEL, MOD_TN), lambda l, j: (l, 0, j)),
            pl.BlockSpec((None, 1, MOD_TN), lambda l, j: (l, 0, j)),
        ],
        out_specs=pl.BlockSpec((None, N_MOD_ROWS, MOD_TN), lambda l, j: (l, 0, j)),
        out_shape=jax.ShapeDtypeStruct((DEPTH, N_MOD_ROWS, n_col), F32),
        compiler_params=_cparams(("arbitrary", "arbitrary")),
        name="modulation",
    )(cond, w_ada, b_ada.reshape(DEPTH, 1, n_col))


IN_TM = 2048
IN_TN = 1024
NORM_ROWS = 256


def _stream_specs(xs, tm, grid_rank):
    def imap(f):
        return (lambda i: f(i)) if grid_rank == 1 else (lambda i, j: f(i))

    width = xs[0].shape[1]
    if len(xs) == 1:
        return [pl.BlockSpec((tm, width), imap(lambda i: (i, 0)))]
    n_ctx_tiles = N_CTX_TOK // tm
    return [pl.BlockSpec((tm, width), imap(lambda i: (jnp.minimum(i, n_ctx_tiles - 1), 0))),
            pl.BlockSpec((tm, width), imap(lambda i: (jnp.maximum(i - n_ctx_tiles, 0), 0)))]


def _stream_rows(x_refs, tile, tm, rows):
    if len(x_refs) == 1:
        return x_refs[0][rows, :]
    return jnp.where(tile < N_CTX_TOK // tm, x_refs[0][rows, :], x_refs[1][rows, :])


def _modulated_norm_to(h_ref, x_rows, nw_ref, sc_ref, sh_ref, rows):
    for r in range(0, rows, NORM_ROWS):
        x = x_rows(slice(r, r + NORM_ROWS))
        h = _rms(x, nw_ref[...]) * (1.0 + sc_ref[...]) + sh_ref[...]
        h_ref[r:r + NORM_ROWS, :] = h.astype(BF16)


def _inproj_kernel(*refs, n_x, tm):
    x_refs = refs[:n_x]
    nw_ref, sh_ref, sc_ref, w_ref, om_ref, os_ref, h_ref = refs[n_x:]
    i = pl.program_id(0)
    j = pl.program_id(1)

    @pl.when(j == 0)
    def _():
        _modulated_norm_to(h_ref, lambda rows: _stream_rows(x_refs, i, tm, rows),
                           nw_ref, sc_ref, sh_ref, tm)

    @pl.when(j < N_MAIN_TILES)
    def _():
        om_ref[...] = jnp.dot(h_ref[...], w_ref[...], preferred_element_type=F32).astype(BF16)

    @pl.when(j == N_MAIN_TILES)
    def _():
        os_ref[...] = jnp.dot(h_ref[...], w_ref[...], preferred_element_type=F32)


def _in_projection(xs, mod_l, norm_w, w_in_p, layer):
    tm = IN_TM if len(xs) == 1 else IN_TM // 2
    row = functools.partial(_mod_row, tile_rows=tm)
    return pl.pallas_call(
        functools.partial(_inproj_kernel, n_x=len(xs), tm=tm),
        grid=(N_TOK // tm, NP // IN_TN),
        in_specs=_stream_specs(xs, tm, 2) + [
            pl.BlockSpec((1, D_MODEL), lambda i, j: (0, 0)),
            pl.BlockSpec((None, None, 1, D_MODEL), lambda i, j: (row(i), 0, 0, 0)),
            pl.BlockSpec((None, None, 1, D_MODEL), lambda i, j: (row(i), 1, 0, 0)),
            pl.BlockSpec((None, D_MODEL, IN_TN), lambda i, j: (layer, 0, j)),
        ],
        out_specs=[
            pl.BlockSpec((tm, IN_TN), lambda i, j: (i, jnp.minimum(j, N_MAIN_TILES - 1))),
            pl.BlockSpec((tm, IN_TN), lambda i, j: (i, 0)),
        ],
        out_shape=[
            jax.ShapeDtypeStruct((N_TOK, NP_MAIN), BF16),
            jax.ShapeDtypeStruct((N_TOK, IN_TN), F32),
        ],
        scratch_shapes=[pltpu.VMEM((tm, D_MODEL), BF16)],
        compiler_params=_cparams(("arbitrary", "arbitrary")),
        name="in_projection",
    )(*xs, norm_w, mod_l, mod_l, w_in_p)


def _conv3_tile(u, prev_row, next_row, w_ref, rows, inner_masks=None):
    ridx = lax.broadcasted_iota(jnp.int32, (SUBLANE, 1), 0)
    up = pltpu.roll(u, 1, axis=0)
    up = jnp.concatenate([jnp.where(ridx == 0, prev_row, up[0:SUBLANE]), up[SUBLANE:]], axis=0)
    dn = pltpu.roll(u, rows - 1, axis=0)
    dn = jnp.concatenate([dn[:rows - SUBLANE], jnp.where(ridx == SUBLANE - 1, next_row, dn[rows - SUBLANE:])],
                         axis=0)
    if inner_masks is not None:
        up = up * inner_masks[0]
        dn = dn * inner_masks[1]
    return up * w_ref[0:1, :] + u * w_ref[1:2, :] + dn * w_ref[2:3, :]


CONV_TQ = 512
SSD_TQ_LAT = 512
HALO = 2 * SUBLANE
BC_WIDTH = 2 * SSM_GROUPS * SSM_STATE


def _tile_neighbours(i, tile_rows):
    n_ctx_tiles = N_CTX_TOK // tile_rows
    tiles_per_lat = DEC_SEQ // tile_rows
    t_in_seq = (i - n_ctx_tiles) % tiles_per_lat
    is_lat = i >= n_ctx_tiles
    has_prev = jnp.logical_and(is_lat, t_in_seq > 0).astype(F32)
    has_next = jnp.logical_and(is_lat, t_in_seq < tiles_per_lat - 1).astype(F32)
    return has_prev, has_next


def _context_edge_masks(tile, tile_rows):
    edge_keep = jnp.where(tile < N_CTX_TOK // tile_rows, 0.0, 1.0)
    pos = lax.broadcasted_iota(jnp.int32, (tile_rows, 1), 0) & (SEQ - 1)
    return jnp.where(pos == 0, edge_keep, 1.0), jnp.where(pos == SEQ - 1, edge_keep, 1.0)


def _ssd_conv_kernel(x_ref, xp_ref, xn_ref, bc_ref, bcp_ref, bcn_ref,
                     cwx_ref, cbx_ref, cwb_ref, cbb_ref, xc_ref, bcc_ref):
    i = pl.program_id(0)
    hp, hn = _tile_neighbours(i, CONV_TQ)
    masks = _context_edge_masks(i, CONV_TQ)

    def conv_silu(u_ref, up_ref, un_ref, w_ref, b_ref):
        prev_row = up_ref[HALO - 1:HALO, :].astype(F32) * hp
        next_row = un_ref[0:1, :].astype(F32) * hn
        conv = _conv3_tile(u_ref[...].astype(F32), prev_row, next_row, w_ref, CONV_TQ, masks)
        return _silu(conv + b_ref[...])

    xc_ref[...] = conv_silu(x_ref, xp_ref, xn_ref, cwx_ref, cbx_ref)
    bcc_ref[...] = conv_silu(bc_ref, bcp_ref, bcn_ref, cwb_ref, cbb_ref)


def _ssd_conv(p, conv_wx, conv_bx, conv_wbc, conv_bbc):
    halo_per_tile = CONV_TQ // HALO
    n_halo_blocks = N_TOK // HALO
    cx, cbc = OFF_SX // SSM_INNER, OFF_BC // BC_WIDTH

    def prev_map(col):
        return lambda i: (jnp.maximum(i * halo_per_tile - 1, 0), col)

    def next_map(col):
        return lambda i: (jnp.minimum((i + 1) * halo_per_tile, n_halo_blocks - 1), col)

    def const2(i):
        return (0, 0)

    return pl.pallas_call(
        _ssd_conv_kernel,
        grid=(N_TOK // CONV_TQ,),
        in_specs=[
            pl.BlockSpec((CONV_TQ, SSM_INNER), lambda i: (i, cx)),
            pl.BlockSpec((HALO, SSM_INNER), prev_map(cx)),
            pl.BlockSpec((HALO, SSM_INNER), next_map(cx)),
            pl.BlockSpec((CONV_TQ, BC_WIDTH), lambda i: (i, cbc)),
            pl.BlockSpec((HALO, BC_WIDTH), prev_map(cbc)),
            pl.BlockSpec((HALO, BC_WIDTH), next_map(cbc)),
            pl.BlockSpec((3, SSM_INNER), const2),
            pl.BlockSpec((1, SSM_INNER), const2),
            pl.BlockSpec((3, BC_WIDTH), const2),
            pl.BlockSpec((1, BC_WIDTH), const2),
        ],
        out_specs=[
            pl.BlockSpec((CONV_TQ, SSM_INNER), lambda i: (i, 0)),
            pl.BlockSpec((CONV_TQ, BC_WIDTH), lambda i: (i, 0)),
        ],
        out_shape=[
            jax.ShapeDtypeStruct((N_TOK, SSM_INNER), F32),
            jax.ShapeDtypeStruct((N_TOK, BC_WIDTH), F32),
        ],
        compiler_params=_cparams(("arbitrary",)),
        name="ssd_conv",
    )(p, p, p, p, p, p, conv_wx, conv_bx, conv_wbc, conv_bbc)


def _ssd_tables(row0, n_seq, seq_len, tq):
    blk_f, blk_b, seq, first = [], [], [], []
    nt = seq_len // tq
    for s in range(n_seq):
        base = (row0 + s * seq_len) // tq
        for k in range(nt):
            blk_f.append(base + k)
            blk_b.append(base + nt - 1 - k)
            seq.append(s)
            first.append(int(k == 0))
    return [np.asarray(a, np.int32) for a in (blk_f, blk_b, seq, first)]


def _split3(a):
    a1 = a.astype(BF16)
    r1 = a - a1.astype(F32)
    a2 = r1.astype(BF16)
    a3 = (r1 - a2.astype(F32)).astype(BF16)
    return a1, a2, a3


def _dot3(lhs_bf16, a):
    a1, a2, a3 = _split3(a)
    return (jnp.dot(lhs_bf16, a1, preferred_element_type=F32)
            + jnp.dot(lhs_bf16, a2, preferred_element_type=F32)
            + jnp.dot(lhs_bf16, a3, preferred_element_type=F32))


def _softplus(x):
    return jnp.maximum(x, 0.0) + jnp.log1p(jnp.exp(-jnp.abs(x)))


def _ssd_kernel(*refs, n_chunks, context):
    blkf_t, blkb_t, seq_t, first_t = refs[:4]
    xf_ref, bcf_ref, dtf_ref, xb_ref, bcb_ref, dtb_ref, alog_ref, dtbias_ref, dsk_ref = refs[4:13]
    if context:
        yf_ref, yb_ref, finf_ref, finb_ref, stf_ref, stb_ref = refs[13:]
    else:
        h0f_ref, h0b_ref, yf_ref, yb_ref, stf_ref, stb_ref = refs[13:]
    s = pl.program_id(0)
    dirs = ((xf_ref, bcf_ref, dtf_ref, stf_ref, yf_ref), (xb_ref, bcb_ref, dtb_ref, stb_ref, yb_ref))

    if context:
        stf_ref[...] = jnp.zeros_like(stf_ref)
        stb_ref[...] = jnp.zeros_like(stb_ref)
    else:
        @pl.when(first_t[s] == 1)
        def _():
            stf_ref[...] = h0f_ref[...].T
            stb_ref[...] = h0b_ref[...].T

    ii = lax.broadcasted_iota(jnp.int32, (CHUNK, CHUNK), 0)
    jj = lax.broadcasted_iota(jnp.int32, (CHUNK, CHUNK), 1)
    masks = (jj <= ii, jj >= ii)
    masks_b = tuple(jnp.where(m, 1.0, 0.0).astype(BF16) for m in masks)
    masks_neg = tuple(jnp.where(m, 0.0, NEG_BIG) for m in masks)
    lo = jj < SSM_HEAD_DIM

    e_r = lax.broadcasted_iota(jnp.int32, (LANE, SSM_INNER), 0)
    e_c = lax.broadcasted_iota(jnp.int32, (LANE, SSM_INNER), 1)
    expand = jnp.where(jnp.right_shift(e_c, 6) == e_r, 1.0, 0.0).astype(BF16)

    def chunk_setup(d, c):
        _, _, dt_ref, _, _ = dirs[d]
        rows = slice(c * CHUNK, (c + 1) * CHUNK)
        a_row = -jnp.exp(alog_ref[d])
        dt = _softplus(dt_ref[rows, :] + dtbias_ref[d])
        cum = _dot3(masks_b[d], dt * a_row)
        cum_t = cum.T
        dt_t = dt.T
        end = CHUNK - 1 if d == 0 else 0
        tot_row = cum[end:end + 1, :]
        tot_col = cum_t[:, end:end + 1]
        return dict(
            rows=rows, cum=cum, cum_t=cum_t, dt_t=dt_t,
            w_rows=jnp.exp(tot_col - cum_t) * dt_t,
            sdec=_dot3_rows(jnp.exp(tot_row), expand))

    def group_setup(d, cs, g):
        _, bc_ref, _, _, _ = dirs[d]
        b_g = bc_ref[cs["rows"], g * SSM_STATE:(g + 1) * SSM_STATE]
        c_g = bc_ref[cs["rows"], (SSM_GROUPS + g) * SSM_STATE:(SSM_GROUPS + g + 1) * SSM_STATE].astype(BF16)
        b_gt = b_g.T
        cb = jnp.dot(c_g, b_gt.astype(BF16), preferred_element_type=F32)
        return c_g, b_gt, cb

    def pair_step(d, cs, gs, kp):
        x_ref, _, _, st_ref, y_ref = dirs[d]
        c_g, b_gt, cb = gs
        cum, cum_t, dt_t = cs["cum"], cs["cum_t"], cs["dt_t"]
        h_a, h_b = 2 * kp, 2 * kp + 1
        ls = slice(kp * LANE, (kp + 1) * LANE)
        x_pair = x_ref[cs["rows"], ls]
        rhs = jnp.concatenate([jnp.where(lo, x_pair, 0.0).astype(BF16),
                               jnp.where(lo, 0.0, x_pair).astype(BF16)], axis=0)

        def head_lhs(h):
            col = jnp.broadcast_to(cum[:, h:h + 1], (CHUNK, CHUNK))
            seg = col - cum_t[h:h + 1, :]
            dec = jnp.exp(seg + masks_neg[d])
            w_intra = dec * cb * dt_t[h:h + 1, :]
            w_state = b_gt * cs["w_rows"][h:h + 1, :]
            return w_intra.astype(BF16), w_state.astype(BF16), col

        wi_a, ws_a, col_a = head_lhs(h_a)
        wi_b, ws_b, col_b = head_lhs(h_b)
        lhs = jnp.concatenate([jnp.concatenate([wi_a, wi_b], axis=1),
                               jnp.concatenate([ws_a, ws_b], axis=1)], axis=0)
        both = jnp.dot(lhs, rhs, preferred_element_type=F32)
        y_diag = both[0:CHUNK, :]
        d_state = both[CHUNK:2 * CHUNK, :]

        h_pair = st_ref[:, ls]
        y_off = jnp.dot(c_g, h_pair.astype(BF16), preferred_element_type=F32)
        e_pair = jnp.exp(jnp.where(lo, col_a, col_b))
        y_ref[cs["rows"], ls] = (y_diag + y_off * e_pair + x_pair * dsk_ref[d, :, ls]).astype(y_ref.dtype)
        st_ref[:, ls] = h_pair * cs["sdec"][:, ls] + d_state

    pairs_per_group = SSM_HEADS // SSM_GROUPS // 2
    setups = [(chunk_setup(0, k), chunk_setup(1, n_chunks - 1 - k)) for k in range(n_chunks)]
    for k in range(n_chunks):
        cs = setups[k]
        for g in range(SSM_GROUPS):
            gs = (group_setup(0, cs[0], g), group_setup(1, cs[1], g))
            for kp in range(g * pairs_per_group, (g + 1) * pairs_per_group):
                pair_step(0, cs[0], gs[0], kp)
                pair_step(1, cs[1], gs[1], kp)

    if context:
        finf_ref[...] = stf_ref[...].T
        finb_ref[...] = stb_ref[...].T


def _dot3_rows(row, rhs_bf16):
    r8 = jnp.broadcast_to(row, (SUBLANE, row.shape[1]))
    r1, r2, r3 = _split3(r8)
    out = (jnp.dot(r1, rhs_bf16, preferred_element_type=F32)
           + jnp.dot(r2, rhs_bf16, preferred_element_type=F32)
           + jnp.dot(r3, rhs_bf16, preferred_element_type=F32))
    return out[0:1, :]


def _ssd(ps, xc, bcc, alog, dtb, dskip, *, row0, n_seq, seq_len, tq, h0=None):
    context = h0 is None
    assert not context or seq_len == tq
    tables = [jnp.asarray(t) for t in _ssd_tables(row0, n_seq, seq_len, tq)]
    n_steps = int(tables[0].shape[0])
    n_rows = n_seq * seq_len
    blk0 = row0 // tq
    cdt = (OFF_DT - NP_MAIN) // LANE

    def fwd_tile(col, off=0):
        return lambda s, blkf, *_: (blkf[s] - off, col)

    def bwd_tile(col, off=0):
        return lambda s, blkf, blkb, *_: (blkb[s] - off, col)

    def tile_specs(tile):
        return [
            pl.BlockSpec((tq, SSM_INNER), tile(0)),
            pl.BlockSpec((tq, BC_WIDTH), tile(0)),
            pl.BlockSpec((tq, LANE), tile(cdt)),
        ]

    def const3(s, *_):
        return (0, 0, 0)

    def seq_map(s, blkf, blkb, seq, *_):
        return (seq[s], 0, 0)

    in_specs = tile_specs(fwd_tile) + tile_specs(bwd_tile) + [
        pl.BlockSpec((2, 1, LANE), const3),
        pl.BlockSpec((2, 1, LANE), const3),
        pl.BlockSpec((2, 1, SSM_INNER), const3),
    ]
    out_specs = [pl.BlockSpec((tq, SSM_INNER), fwd_tile(0, blk0)), pl.BlockSpec((tq, SSM_INNER), bwd_tile(0, blk0))]
    out_shape = [jax.ShapeDtypeStruct((n_rows, SSM_INNER), BF16), jax.ShapeDtypeStruct((n_rows, SSM_INNER), BF16)]
    operands = [xc, bcc, ps, xc, bcc, ps, alog, dtb, dskip]
    if context:
        out_specs += [pl.BlockSpec((None, SSM_INNER, SSM_STATE), seq_map)] * 2
        out_shape += [jax.ShapeDtypeStruct((n_seq, SSM_INNER, SSM_STATE), F32)] * 2
    else:
        h0f, h0b, layer = h0

        def h0_map(s, blkf, blkb, seq, *_):
            return (seq[s], layer, 0, 0)

        in_specs += [pl.BlockSpec((None, None, SSM_INNER, SSM_STATE), h0_map)] * 2
        operands += [h0f, h0b]

    grid_spec = pltpu.PrefetchScalarGridSpec(
        num_scalar_prefetch=4,
        grid=(n_steps,),
        in_specs=in_specs,
        out_specs=out_specs,
        scratch_shapes=[
            pltpu.VMEM((SSM_STATE, SSM_INNER), F32),
            pltpu.VMEM((SSM_STATE, SSM_INNER), F32),
        ],
    )
    return pl.pallas_call(
        functools.partial(_ssd_kernel, n_chunks=tq // CHUNK, context=context),
        grid_spec=grid_spec,
        out_shape=out_shape,
        compiler_params=_cparams(("arbitrary",)),
        name="ssd_scan",
    )(*tables, *operands)


QK_PAD = 128
N_QROWS = MLA_HEADS * QK_PAD
N_VROWS = MLA_HEADS * V_DIM
NT_DIMS = (((1,), (1,)), ((), ()))
Q_PRESCALE = (1.0 / math.sqrt(QK_DIM)) * math.log2(math.e)


def _mla_prep_kernel(*refs, tm, normalize, with_q, keep_ckv):
    if with_q:
        (cq_ref, ckv_ref, kr_ref, krs_ref, cos_ref, sin_ref, cost_ref, sint_ref,
         qnw_ref, kvnw_ref, wq_ref, wk_ref, wv_ref, qt_ref, k_ref, vt_ref) = refs[:16]
    else:
        (ckv_ref, kr_ref, krs_ref, cos_ref, sin_ref, kvnw_ref, wk_ref, wv_ref,
         k_ref, vt_ref) = refs

    ckv = ckv_ref[...]
    if normalize:
        ckv = _rms(ckv, kvnw_ref[...])
    if keep_ckv:
        refs[16][...] = ckv
    ckv_b = ckv.astype(BF16)
    kn = jnp.dot(ckv_b, wk_ref[...], preferred_element_type=F32)
    kr = kr_ref[...] * cos_ref[...] + krs_ref[...] * sin_ref[...]
    for h in range(MLA_HEADS):
        hs = slice(h * QK_PAD, (h + 1) * QK_PAD)
        k_ref[:, hs] = (kn[:, hs] + kr).astype(BF16)
    vt_ref[...] = lax.dot_general(wv_ref[...], ckv_b, NT_DIMS, preferred_element_type=F32).astype(BF16)

    if with_q:
        cqn = _rms(cq_ref[...], qnw_ref[...]).astype(BF16)
        qq = lax.dot_general(wq_ref[...], cqn, NT_DIMS, preferred_element_type=F32)
        for h in range(MLA_HEADS):
            q_h = qq[h * QK_PAD:(h + 1) * QK_PAD, :]
            qs_h = qq[N_QROWS + h * QK_PAD:N_QROWS + (h + 1) * QK_PAD, :]
            q_rot = q_h * cost_ref[...] + qs_h * sint_ref[...]
            qt_ref[h * QK_PAD:(h + 1) * QK_PAD, :] = (q_rot * Q_PRESCALE).astype(BF16)


def _mla_prep_tokens(p, row0, n_rows, tm, tab_map, cos, sin, cos_t, sin_t, qnw, kvnw, wq_t, wk, wv_t, layer,
                     keep_ckv):
    b0 = row0 // tm
    nt = n_rows // tm
    kernel = functools.partial(_mla_prep_kernel, tm=tm, normalize=True, with_q=True, keep_ckv=keep_ckv)

    def pcol(width, off):
        return pl.BlockSpec((tm, width), lambda i: (b0 + i, (off - NP_MAIN) // width))

    def full(a):
        return pl.BlockSpec(a.shape, lambda i: (0,) * a.ndim)

    return pl.pallas_call(
        kernel,
        grid=(nt,),
        in_specs=[
            pcol(Q_LORA, OFF_CQ), pcol(KV_LORA, OFF_CKV), pcol(LANE, OFF_KR), pcol(LANE, OFF_KRS),
            pl.BlockSpec((tm, LANE), lambda i: (tab_map(i), 0)),
            pl.BlockSpec((tm, LANE), lambda i: (tab_map(i), 0)),
            pl.BlockSpec((LANE, tm), lambda i: (0, tab_map(i))),
            pl.BlockSpec((LANE, tm), lambda i: (0, tab_map(i))),
            full(qnw), full(kvnw), _layer_spec(wq_t, layer, 1), _layer_spec(wk, layer, 1), _layer_spec(wv_t, layer, 1),
        ],
        out_specs=[
            pl.BlockSpec((N_QROWS, tm), lambda i: (0, i)),
            pl.BlockSpec((None, tm, N_QROWS), lambda i: (i, 0, 0)),
            pl.BlockSpec((None, N_VROWS, tm), lambda i: (i, 0, 0)),
        ] + ([pl.BlockSpec((tm, KV_LORA), lambda i: (i, 0))] if keep_ckv else []),
        out_shape=[
            jax.ShapeDtypeStruct((N_QROWS, n_rows), BF16),
            jax.ShapeDtypeStruct((nt, tm, N_QROWS), BF16),
            jax.ShapeDtypeStruct((nt, N_VROWS, tm), BF16),
        ] + ([jax.ShapeDtypeStruct((n_rows, KV_LORA), F32)] if keep_ckv else []),
        compiler_params=_cparams(("arbitrary",)),
        name="mla_prep",
    )(p, p, p, p, cos, sin, cos_t, sin_t, qnw, kvnw, wq_t, wk, wv_t)


def _mla_prep_cache(ckv, kr_pad, ones_tab, zeros_tab, kvnw, wk, wv_t, tm, layer):
    n_rows = ckv.shape[0]
    nt = n_rows // tm
    kernel = functools.partial(_mla_prep_kernel, tm=tm, normalize=False, with_q=False, keep_ckv=False)

    def full(a):
        return pl.BlockSpec(a.shape, lambda i: (0,) * a.ndim)

    return pl.pallas_call(
        kernel,
        grid=(nt,),
        in_specs=[
            pl.BlockSpec((tm, KV_LORA), lambda i: (i, 0)),
            pl.BlockSpec((tm, LANE), lambda i: (i, 0)),
            pl.BlockSpec((tm, LANE), lambda i: (i, 0)),
            pl.BlockSpec((tm, LANE), lambda i: (0, 0)),
            pl.BlockSpec((tm, LANE), lambda i: (0, 0)),
            full(kvnw), _layer_spec(wk, layer, 1), _layer_spec(wv_t, layer, 1),
        ],
        out_specs=[
            pl.BlockSpec((None, tm, N_QROWS), lambda i: (i, 0, 0)),
            pl.BlockSpec((None, N_VROWS, tm), lambda i: (i, 0, 0)),
        ],
        out_shape=[
            jax.ShapeDtypeStruct((nt, tm, N_QROWS), BF16),
            jax.ShapeDtypeStruct((nt, N_VROWS, tm), BF16),
        ],
        compiler_params=_cparams(("arbitrary",)),
        name="mla_prep_cache",
    )(ckv, kr_pad, kr_pad, ones_tab, zeros_tab, kvnw, wk, wv_t)


ATT_TQ = 512
ATT_SUB = 256


def _attn_kernel(*refs, n_kt, tq, tk, with_cache):
    if with_cache:
        qt_ref, kp_ref, vtp_ref, k_ref, vt_ref, o_ref, ot_ref, s_ref = refs
    else:
        qt_ref, k_ref, vt_ref, o_ref, ot_ref, s_ref = refs

    def k_tile(kt, h):
        cols = slice(h * QK_PAD, (h + 1) * QK_PAD)
        if with_cache:
            return kp_ref[:, cols] if kt == 0 else k_ref[kt - 1, :, cols]
        return k_ref[kt, :, cols]

    def vt_tile(kt, h):
        rows = slice(h * V_DIM, (h + 1) * V_DIM)
        if with_cache:
            return vtp_ref[rows, :] if kt == 0 else vt_ref[kt - 1, rows, :]
        return vt_ref[kt, rows, :]

    sub = min(ATT_SUB, tk)

    def scores_step(h, kt, m8):
        q_t = qt_ref[h * QK_PAD:(h + 1) * QK_PAD, :]
        k = k_tile(kt, h)
        for r in range(0, tk, sub):
            s = jnp.dot(k[r:r + sub, :], q_t, preferred_element_type=F32)
            s_ref[h % 2, kt, r:r + sub, :] = s
            m8 = jnp.maximum(m8, jnp.max(s.reshape(sub // SUBLANE, SUBLANE, tq), axis=0))
        return m8

    def probs_step(h, kt, m, l8, acc):
        v_t = vt_tile(kt, h)
        for r in range(0, tk, sub):
            pr = jnp.exp2(s_ref[h % 2, kt, r:r + sub, :] - m)
            l8 = l8 + jnp.sum(pr.reshape(sub // SUBLANE, SUBLANE, tq), axis=0)
            acc = acc + jnp.dot(v_t[:, r:r + sub], pr.astype(BF16), preferred_element_type=F32)
        return l8, acc

    m8_init = jnp.full((SUBLANE, tq), NEG_BIG, F32)
    m8 = m8_init
    for kt in range(n_kt):
        m8 = scores_step(0, kt, m8)
    for h in range(MLA_HEADS):
        m = jnp.max(m8, axis=0, keepdims=True)
        l8 = jnp.zeros((SUBLANE, tq), F32)
        acc = jnp.zeros((V_DIM, tq), F32)
        m8 = m8_init
        for kt in range(n_kt):
            l8, acc = probs_step(h, kt, m, l8, acc)
            if h + 1 < MLA_HEADS:
                m8 = scores_step(h + 1, kt, m8)
        ot_ref[h * V_DIM:(h + 1) * V_DIM, :] = acc / jnp.sum(l8, axis=0, keepdims=True)
    o_ref[...] = ot_ref[...].T.astype(o_ref.dtype)


def _attention(qt, k3, vt3, n_batch, lq, tk, cache=None):
    tq = min(ATT_TQ, lq)
    nq = lq // tq
    n_new = k3.shape[0] // n_batch
    k4 = k3.reshape(n_batch, n_new, tk, N_QROWS)
    v4 = vt3.reshape(n_batch, n_new, N_VROWS, tk)
    n_kt = n_new + (0 if cache is None else 1)
    kernel = functools.partial(_attn_kernel, n_kt=n_kt, tq=tq, tk=tk, with_cache=cache is not None)
    cache_specs = [] if cache is None else [
        pl.BlockSpec((None, tk, N_QROWS), lambda b, i: (b, 0, 0)),
        pl.BlockSpec((None, N_VROWS, tk), lambda b, i: (b, 0, 0)),
    ]
    return pl.pallas_call(
        kernel,
        grid=(n_batch, nq),
        in_specs=[pl.BlockSpec((N_QROWS, tq), lambda b, i: (0, b * nq + i))] + cache_specs + [
            pl.BlockSpec((None, n_new, tk, N_QROWS), lambda b, i: (b, 0, 0, 0)),
            pl.BlockSpec((None, n_new, N_VROWS, tk), lambda b, i: (b, 0, 0, 0)),
        ],
        out_specs=pl.BlockSpec((tq, N_VROWS), lambda b, i: (b * nq + i, 0)),
        out_shape=jax.ShapeDtypeStruct((n_batch * lq, N_VROWS), BF16),
        scratch_shapes=[pltpu.VMEM((N_VROWS, tq), F32), pltpu.VMEM((2, n_kt, tk, tq), F32)],
        compiler_params=_cparams(("arbitrary", "arbitrary")),
        name="attention",
    )(qt, *(() if cache is None else cache), k4, v4)


MG_TM = 512


def _merge_kernel(*refs, n_x):
    x_refs = refs[:n_x]
    zc_refs, yf_refs, yb_refs = refs[n_x:n_x + 2], refs[n_x + 2:n_x + 4], refs[n_x + 4:n_x + 6]
    (ax_ref, axp_ref, axn_ref, ac_ref, acp_ref, acn_ref, ab_ref,
     g_ref, z_ref, gm_ref,
     cw_ref, wa_ref, nw_ref, wb_ref, wc_ref, wo_ref, o_ref) = refs[n_x + 6:]
    i = pl.program_id(0)
    hp, hn = _tile_neighbours(i, MG_TM)
    all_rows = slice(0, MG_TM)

    def f32(v):
        return v.astype(F32)

    u = f32(ac_ref[...]) * f32(ax_ref[...])
    u_prev = f32(acp_ref[HALO - 1:HALO, :]) * f32(axp_ref[HALO - 1:HALO, :]) * hp
    u_next = f32(acn_ref[0:1, :]) * f32(axn_ref[0:1, :]) * hn
    za = f32(ab_ref[...]) * _conv3_tile(u, u_prev, u_next, cw_ref, MG_TM, _context_edge_masks(i, MG_TM))
    y_a = jnp.dot(za.astype(BF16), wa_ref[...], preferred_element_type=F32)

    y_scan = f32(_stream_rows(yf_refs, i, MG_TM, all_rows)) + f32(_stream_rows(yb_refs, i, MG_TM, all_rows))
    yb = y_scan * _silu(f32(z_ref[...]))
    zb = _rms(yb, nw_ref[...])
    y_b = jnp.dot(zb.astype(BF16), wb_ref[...], preferred_element_type=F32)

    y_c = jnp.dot(_stream_rows(zc_refs, i, MG_TM, all_rows), wc_ref[...], preferred_element_type=F32)

    merged = (_sigmoid(f32(g_ref[:, 0:D_MODEL])) * y_a
              + _sigmoid(f32(g_ref[:, D_MODEL:2 * D_MODEL])) * y_b
              + _sigmoid(f32(g_ref[:, 2 * D_MODEL:3 * D_MODEL])) * y_c)
    o = jnp.dot(merged.astype(BF16), wo_ref[...], preferred_element_type=F32)
    o_ref[...] = _stream_rows(x_refs, i, MG_TM, all_rows) + gm_ref[...] * o


def _merge(p, yfs, ybs, zcs, xs, mod_l, conv_w, wa, nw, wb, wc, wo, layer):
    halo_per_tile = MG_TM // HALO
    n_halo_blocks = N_TOK // HALO
    row = functools.partial(_mod_row, tile_rows=MG_TM)

    def pcol(width, off):
        return pl.BlockSpec((MG_TM, width), lambda i: (i, off // width))

    def pprev(off):
        return pl.BlockSpec((HALO, A_WIDTH), lambda i: (jnp.maximum(i * halo_per_tile - 1, 0), off // A_WIDTH))

    def pnext(off):
        return pl.BlockSpec((HALO, A_WIDTH),
                            lambda i: (jnp.minimum((i + 1) * halo_per_tile, n_halo_blocks - 1), off // A_WIDTH))

    def full(a):
        return pl.BlockSpec(a.shape, lambda i: (0,) * a.ndim)

    return pl.pallas_call(
        functools.partial(_merge_kernel, n_x=len(xs)),
        grid=(N_TOK // MG_TM,),
        in_specs=(_stream_specs(xs, MG_TM, 1) + _stream_specs(zcs, MG_TM, 1)
                  + _stream_specs(yfs, MG_TM, 1) + _stream_specs(ybs, MG_TM, 1)) + [
            pcol(A_WIDTH, OFF_AX), pprev(OFF_AX), pnext(OFF_AX),
            pcol(A_WIDTH, OFF_AC), pprev(OFF_AC), pnext(OFF_AC),
            pcol(A_WIDTH, OFF_AB),
            pcol(3 * D_MODEL, OFF_G),
            pcol(SSM_INNER, OFF_Z),
            pl.BlockSpec((None, None, 1, D_MODEL), lambda i: (row(i), 2, 0, 0)),
            full(conv_w), _layer_spec(wa, layer, 1), full(nw), _layer_spec(wb, layer, 1),
            _layer_spec(wc, layer, 1), _layer_spec(wo, layer, 1),
        ],
        out_specs=pl.BlockSpec((MG_TM, D_MODEL), lambda i: (i, 0)),
        out_shape=jax.ShapeDtypeStruct((N_TOK, D_MODEL), F32),
        compiler_params=_cparams(("arbitrary",)),
        name="merge",
    )(*xs, *zcs, *yfs, *ybs, p, p, p, p, p, p, p, p, p, mod_l, conv_w, wa, nw, wb, wc, wo)


FF_TM = 512
FF_CHUNK = 256


def _ffn_kernel(x_ref, nw_ref, sh_ref, sc_ref, gm_ref, w1_ref, w3_ref, w2_ref, o_ref, h_ref, g_ref):
    _modulated_norm_to(h_ref, lambda rows: x_ref[rows, :], nw_ref, sc_ref, sh_ref, FF_TM)
    h = h_ref[...]
    for c in range(0, FF_DIM, FF_CHUNK):
        a = jnp.dot(h, w1_ref[:, c:c + FF_CHUNK], preferred_element_type=F32)
        b = jnp.dot(h, w3_ref[:, c:c + FF_CHUNK], preferred_element_type=F32)
        g_ref[:, c:c + FF_CHUNK] = (_silu(a) * b).astype(BF16)
    ff = jnp.dot(g_ref[...], w2_ref[...], preferred_element_type=F32)
    o_ref[...] = x_ref[...] + gm_ref[...] * ff


def _ffn(x, mod_l, norm_w, w1, w3, w2, layer):
    row = functools.partial(_mod_row, tile_rows=FF_TM)
    return pl.pallas_call(
        _ffn_kernel,
        grid=(N_TOK // FF_TM,),
        in_specs=[
            pl.BlockSpec((FF_TM, D_MODEL), lambda i: (i, 0)),
            pl.BlockSpec((1, D_MODEL), lambda i: (0, 0)),
            pl.BlockSpec((None, None, 1, D_MODEL), lambda i: (row(i), 3, 0, 0)),
            pl.BlockSpec((None, None, 1, D_MODEL), lambda i: (row(i), 4, 0, 0)),
            pl.BlockSpec((None, None, 1, D_MODEL), lambda i: (row(i), 5, 0, 0)),
            _layer_spec(w1, layer, 1), _layer_spec(w3, layer, 1), _layer_spec(w2, layer, 1),
        ],
        out_specs=pl.BlockSpec((FF_TM, D_MODEL), lambda i: (i, 0)),
        out_shape=jax.ShapeDtypeStruct((N_TOK, D_MODEL), F32),
        scratch_shapes=[pltpu.VMEM((FF_TM, D_MODEL), BF16), pltpu.VMEM((FF_TM, FF_DIM), BF16)],
        compiler_params=_cparams(("arbitrary",)),
        name="ffn",
    )(x, norm_w, mod_l, mod_l, mod_l, w1, w3, w2)


FN_TM = 1024


def _final_norm_kernel(x_ref, w_ref, o_ref):
    o_ref[...] = _rms(x_ref[...], w_ref[...])


def _final_norm(x, w, row0, n_rows):
    b0 = row0 // FN_TM
    return pl.pallas_call(
        _final_norm_kernel,
        grid=(n_rows // FN_TM,),
        in_specs=[pl.BlockSpec((FN_TM, D_MODEL), lambda i: (b0 + i, 0)),
                  pl.BlockSpec((1, D_MODEL), lambda i: (0, 0))],
        out_specs=pl.BlockSpec((FN_TM, D_MODEL), lambda i: (i, 0)),
        out_shape=jax.ShapeDtypeStruct((n_rows, D_MODEL), F32),
        compiler_params=_cparams(("arbitrary",)),
        name="final_norm",
    )(x, w)


def _pad_in_weights(w_in):
    o = 0
    a_x = w_in[..., o:o + 512]; o += 512
    a_b = w_in[..., o:o + 512]; o += 512
    a_c = w_in[..., o:o + 512]; o += 512
    s_z = w_in[..., o:o + 1024]; o += 1024
    s_x = w_in[..., o:o + 1024]; o += 1024
    s_bc = w_in[..., o:o + 512]; o += 512
    s_dt = w_in[..., o:o + 16]; o += 16
    cq = w_in[..., o:o + 256]; o += 256
    ckv = w_in[..., o:o + 256]; o += 256
    kr = w_in[..., o:o + 32]; o += 32
    gates = w_in[..., o:o + 3072]; o += 3072

    def z(n):
        return jnp.zeros(w_in.shape[:-1] + (n,), w_in.dtype)

    kr_sw = jnp.concatenate([kr[..., 16:], kr[..., :16]], axis=-1)
    cols = [gates, s_z, s_x, a_x, a_b, a_c, s_bc, cq, ckv,
            s_dt, z(LANE - 16),
            z(64), kr, z(32),
            z(64), kr_sw, z(32),
            z(NP - OFF_KRS - LANE)]
    out = jnp.concatenate(cols, axis=-1).astype(BF16)
    assert out.shape[-1] == NP
    return out


def _q_weights_t(w_uq):
    w = w_uq.reshape(DEPTH, Q_LORA, MLA_HEADS, QK_DIM)
    nope, x1, x2 = w[..., :NOPE_DIM], w[..., NOPE_DIM:NOPE_DIM + 16], w[..., NOPE_DIM + 16:]
    z32 = jnp.zeros_like(w[..., :32])
    z64 = jnp.zeros_like(nope)
    q = jnp.concatenate([nope, x1, x2, z32], axis=-1).reshape(DEPTH, Q_LORA, N_QROWS)
    qs = jnp.concatenate([z64, x2, x1, z32], axis=-1).reshape(DEPTH, Q_LORA, N_QROWS)
    return jnp.swapaxes(jnp.concatenate([q, qs], axis=-1), 1, 2).astype(BF16)


def _kv_weights(w_ukv):
    w = w_ukv.reshape(DEPTH, KV_LORA, MLA_HEADS, NOPE_DIM + V_DIM)
    kn = jnp.concatenate([w[..., :NOPE_DIM], jnp.zeros_like(w[..., :QK_PAD - NOPE_DIM])], axis=-1)
    wk = kn.reshape(DEPTH, KV_LORA, N_QROWS).astype(BF16)
    wv_t = jnp.swapaxes(w[..., NOPE_DIM:].reshape(DEPTH, KV_LORA, N_VROWS), 1, 2).astype(BF16)
    return wk, wv_t


def _rope_tables(n_tokens, lead_rows):
    n_rows = n_tokens // GRID_W
    row = jnp.repeat(jnp.arange(n_rows, dtype=F32), GRID_W)
    col = jnp.tile(jnp.arange(GRID_W, dtype=F32), n_rows)
    pairs = ROPE_DIM // 4
    inv = ROPE_BASE ** (-jnp.arange(pairs, dtype=F32) / pairs)
    ang = jnp.concatenate([row[:, None] * inv, col[:, None] * inv], axis=-1)
    cos, sin = jnp.cos(ang), jnp.sin(ang)
    ones = jnp.ones((n_tokens, NOPE_DIM), F32)
    z32 = jnp.zeros((n_tokens, 32), F32)
    cos_l = jnp.concatenate([ones, cos, cos, z32], axis=-1)
    sin_l = jnp.concatenate([jnp.zeros_like(ones), -sin, sin, z32], axis=-1)
    ident_c = jnp.concatenate([jnp.ones((lead_rows, NOPE_DIM + ROPE_DIM), F32), jnp.zeros((lead_rows, 32), F32)], -1)
    ident_s = jnp.zeros((lead_rows, LANE), F32)
    return jnp.concatenate([ident_c, cos_l], axis=0), jnp.concatenate([ident_s, sin_l], axis=0)


PREP_TM_CTX = SEQ
PREP_TM_LAT = 512


def kernel(x_prompt, x_sample, c, cache_ckv, cache_krope, state_ssm_fwd, state_ssm_bwd, c_ctx, w_in, a_conv_w, w_a_out, ssm_conv_w, ssm_conv_b, ssm_a_log, ssm_dt_bias, ssm_d, ssm_norm_w, w_b_out, q_norm_w, w_uq, kv_norm_w, w_ukv, w_c_out, w_o, w_ada, b_ada, norm1_w, norm2_w, w_ff1, w_ff3, w_ff2, final_norm_w):
    w_in_p = _pad_in_weights(w_in)
    wq_t = _q_weights_t(w_uq)
    wk, wv_t = _kv_weights(w_ukv)
    wa, wb, wc, wo = (w.astype(BF16) for w in (w_a_out, w_b_out, w_c_out, w_o))
    w1, w3, w2 = (w.astype(BF16) for w in (w_ff1, w_ff3, w_ff2))

    cond = jnp.concatenate([c_ctx[None, :], c, jnp.zeros((N_MOD_ROWS - 1 - DEC_BATCH, D_MODEL), F32)], axis=0)
    mod = _modulation(cond, w_ada, b_ada).reshape(DEPTH, N_MOD_ROWS, 6, 1, D_MODEL)

    conv_wx = ssm_conv_w[..., :SSM_INNER]
    conv_wbc = ssm_conv_w[..., SSM_INNER:]
    conv_bx = ssm_conv_b[:, None, :SSM_INNER]
    conv_bbc = ssm_conv_b[:, None, SSM_INNER:]
    pad_h = ((0, 0), (0, 0), (0, 0), (0, LANE - SSM_HEADS))
    alog = jnp.pad(ssm_a_log[:, :, None, :], pad_h)
    dtb = jnp.pad(ssm_dt_bias[:, :, None, :], pad_h)
    dskip = jnp.repeat(ssm_d, SSM_HEAD_DIM, axis=-1)[:, :, None, :]
    h0f = state_ssm_fwd.reshape(DEC_BATCH, DEPTH, SSM_INNER, SSM_STATE)
    h0b = state_ssm_bwd.reshape(DEC_BATCH, DEPTH, SSM_INNER, SSM_STATE)

    cos_c, sin_c = _rope_tables(DEC_SEQ, PREP_TM_CTX)
    cos_l, sin_l = cos_c[PREP_TM_CTX:], sin_c[PREP_TM_CTX:]
    ident_cos, ident_sin = cos_c[:PREP_TM_CTX], sin_c[:PREP_TM_CTX]
    ones_tab = jnp.concatenate([ident_cos, ident_cos], axis=0)
    zeros_tab = jnp.zeros_like(ones_tab)
    cache_kr_pad = jnp.pad(cache_krope, ((0, 0), (0, 0), (0, 0), (64, 32)))

    xs = (x_prompt.reshape(N_CTX_TOK, D_MODEL), x_sample.reshape(N_LAT_TOK, D_MODEL))

    lat_tiles_per_seq = DEC_SEQ // PREP_TM_LAT
    new_ckv, new_kr, new_f, new_b = [], [], [], []
    for l in range(DEPTH):
        p, ps = _in_projection(xs, mod[l], norm1_w[l][None, :], w_in_p, l)

        xc, bcc = _ssd_conv(p, conv_wx[l], conv_bx[l], conv_wbc[l], conv_bbc[l])
        yc_f, yc_b, fin_f, fin_b = _ssd(ps, xc, bcc, alog[l], dtb[l], dskip[l],
                                        row0=0, n_seq=BATCH, seq_len=SEQ, tq=SEQ)
        yl_f, yl_b = _ssd(ps, xc, bcc, alog[l], dtb[l], dskip[l],
                          row0=N_CTX_TOK, n_seq=DEC_BATCH, seq_len=DEC_SEQ, tq=SSD_TQ_LAT, h0=(h0f, h0b, l))
        new_f.append(fin_f)
        new_b.append(fin_b)

        qnw, kvnw = q_norm_w[l][None, :], kv_norm_w[l][None, :]
        qt_c, k_c, vt_c, ckv_c = _mla_prep_tokens(
            ps, 0, N_CTX_TOK, PREP_TM_CTX, lambda i: 0, ident_cos, ident_sin, ident_cos.T, ident_sin.T,
            qnw, kvnw, wq_t, wk, wv_t, l, keep_ckv=True)
        qt_l, k_l, vt_l = _mla_prep_tokens(
            ps, N_CTX_TOK, N_LAT_TOK, PREP_TM_LAT, lambda i: i % lat_tiles_per_seq, cos_l, sin_l, cos_l.T, sin_l.T,
            qnw, kvnw, wq_t, wk, wv_t, l, keep_ckv=False)
        k_p, vt_p = _mla_prep_cache(cache_ckv[:, l].reshape(DEC_BATCH * PAST_LEN, KV_LORA),
                                    cache_kr_pad[:, l].reshape(DEC_BATCH * PAST_LEN, LANE),
                                    ones_tab, zeros_tab, kvnw, wk, wv_t, PAST_LEN, l)
        new_ckv.append(ckv_c.reshape(BATCH, SEQ, KV_LORA))
        kr0 = OFF_KR - NP_MAIN + 64
        new_kr.append(ps[:N_CTX_TOK, kr0:kr0 + ROPE_DIM].reshape(BATCH, SEQ, ROPE_DIM))

        zc_c = _attention(qt_c, k_c, vt_c, BATCH, SEQ, SEQ)
        zc_l = _attention(qt_l, k_l, vt_l, DEC_BATCH, DEC_SEQ, PREP_TM_LAT, cache=(k_p, vt_p))

        x = _merge(p, (yc_f, yl_f), (yc_b, yl_b), (zc_c, zc_l), xs, mod[l], a_conv_w[l], wa, ssm_norm_w[l][None, :],
                   wb, wc, wo, l)
        x = _ffn(x, mod[l], norm2_w[l][None, :], w1, w3, w2, l)
        xs = (x,)

    fw = final_norm_w[None, :]
    y_prompt = _final_norm(x, fw, 0, N_CTX_TOK).reshape(BATCH, SEQ, D_MODEL)
    y_sample = _final_norm(x, fw, N_CTX_TOK, N_LAT_TOK).reshape(DEC_BATCH, DEC_SEQ, D_MODEL)
    hshape = (BATCH, DEPTH, SSM_HEADS, SSM_HEAD_DIM, SSM_STATE)
    return (y_prompt, y_sample,
            jnp.stack(new_ckv, axis=1), jnp.stack(new_kr, axis=1),
            jnp.stack(new_f, axis=1).reshape(hshape), jnp.stack(new_b, axis=1).reshape(hshape))
```

```python
import functools
import math

import jax
import jax.numpy as jnp
import numpy as np
from jax import lax
from jax.experimental import pallas as pl
from jax.experimental.pallas import tpu as pltpu

F32 = jnp.float32
BF16 = jnp.bfloat16

D_MODEL = 1024
BATCH = 16
SEQ = 256
DEPTH = 4
DEC_BATCH = 4
DEC_SEQ = 4096
PAST_LEN = 512
GRID_W = 64
EPS = 1e-6
A_WIDTH = 512
SSM_INNER = 1024
SSM_HEAD_DIM = 64
SSM_HEADS = 16
SSM_GROUPS = 2
SSM_STATE = 128
CHUNK = 128
MLA_HEADS = 8
Q_LORA = 256
KV_LORA = 256
NOPE_DIM = 64
ROPE_DIM = 32
V_DIM = 64
QK_DIM = NOPE_DIM + ROPE_DIM
ROPE_BASE = 10000.0
FF_DIM = 2816

N_CTX_TOK = BATCH * SEQ
N_LAT_TOK = DEC_BATCH * DEC_SEQ
N_TOK = N_CTX_TOK + N_LAT_TOK
N_MOD_ROWS = 8

LANE = 128
SUBLANE = 8
VMEM_LIMIT = 56 * 1024 * 1024

OFF_G = 0
OFF_Z = 3072
OFF_SX = 4096
OFF_AX = 5120
OFF_AB = 5632
OFF_AC = 6144
OFF_BC = 6656
NP_MAIN = 7168
OFF_CQ = 7168
OFF_CKV = 7424
OFF_DT = 7680
OFF_KR = 7808
OFF_KRS = 7936
NP = 8192
N_MAIN_TILES = 7

NEG_BIG = -1e30


def _cparams(sem):
    return pltpu.CompilerParams(dimension_semantics=sem, vmem_limit_bytes=VMEM_LIMIT)


def _rms(x, w):
    ms = jnp.mean(x * x, axis=-1, keepdims=True)
    return x * lax.rsqrt(ms + EPS) * w


def _sigmoid(x):
    return 0.5 * jnp.tanh(0.5 * x) + 0.5


def _silu(x):
    h = 0.5 * x
    return h * jnp.tanh(h) + h


def _layer_spec(w, layer, grid_rank):
    zeros = (0,) * (w.ndim - 1)
    imap = (lambda i: (layer,) + zeros) if grid_rank == 1 else (lambda i, j: (layer,) + zeros)
    return pl.BlockSpec((None,) + tuple(w.shape[1:]), imap)


def _mod_row(tile, tile_rows):
    n_ctx_tiles = N_CTX_TOK // tile_rows
    tiles_per_lat = DEC_SEQ // tile_rows
    return jnp.where(tile < n_ctx_tiles, 0, 1 + (tile - n_ctx_tiles) // tiles_per_lat)


MOD_TN = 1536


def _mod_kernel(c_ref, w_ref, b_ref, o_ref):
    c = c_ref[...]
    s = _silu(c).astype(BF16)
    o_ref[...] = jnp.dot(s, w_ref[...].astype(BF16), preferred_element_type=F32) + b_ref[...]


def _modulation(cond, w_ada, b_ada):
    n_col = 6 * D_MODEL
    return pl.pallas_call(
        _mod_kernel,
        grid=(DEPTH, n_col // MOD_TN),
        in_specs=[
            pl.BlockSpec((N_MOD_ROWS, D_MODEL), lambda l, j: (0, 0)),
            pl.BlockSpec((None, D_MODEL, MOD_TN), lambda l, j: (l, 0, j)),
            pl.BlockSpec((None, 1, MOD_TN), lambda l, j: (l, 0, j)),
        ],
        out_specs=pl.BlockSpec((None, N_MOD_ROWS, MOD_TN), lambda l, j: (l, 0, j)),
        out_shape=jax.ShapeDtypeStruct((DEPTH, N_MOD_ROWS, n_col), F32),
        compiler_params=_cparams(("arbitrary", "arbitrary")),
        name="modulation",
    )(cond, w_ada, b_ada.reshape(DEPTH, 1, n_col))


IN_TM = 2048
IN_TN = 1024
NORM_ROWS = 256


def _stream_specs(xs, tm, grid_rank):
    def imap(f):
        return (lambda i: f(i)) if grid_rank == 1 else (lambda i, j: f(i))

    width = xs[0].shape[1]
    if len(xs) == 1:
        return [pl.BlockSpec((tm, width), imap(lambda i: (i, 0)))]
    n_ctx_tiles = N_CTX_TOK // tm
    return [pl.BlockSpec((tm, width), imap(lambda i: (jnp.minimum(i, n_ctx_tiles - 1), 0))),
            pl.BlockSpec((tm, width), imap(lambda i: (jnp.maximum(i - n_ctx_tiles, 0), 0)))]


def _stream_rows(x_refs, tile, tm, rows):
    if len(x_refs) == 1:
        return x_refs[0][rows, :]
    return jnp.where(tile < N_CTX_TOK // tm, x_refs[0][rows, :], x_refs[1][rows, :])


def _modulated_norm_to(h_ref, x_rows, nw_ref, sc_ref, sh_ref, rows):
    for r in range(0, rows, NORM_ROWS):
        x = x_rows(slice(r, r + NORM_ROWS))
        h = _rms(x, nw_ref[...]) * (1.0 + sc_ref[...]) + sh_ref[...]
        h_ref[r:r + NORM_ROWS, :] = h.astype(BF16)


def _inproj_kernel(*refs, n_x, tm):
    x_refs = refs[:n_x]
    nw_ref, sh_ref, sc_ref, w_ref, om_ref, os_ref, h_ref = refs[n_x:]
    i = pl.program_id(0)
    j = pl.program_id(1)

    @pl.when(j == 0)
    def _():
        _modulated_norm_to(h_ref, lambda rows: _stream_rows(x_refs, i, tm, rows),
                           nw_ref, sc_ref, sh_ref, tm)

    @pl.when(j < N_MAIN_TILES)
    def _():
        om_ref[...] = jnp.dot(h_ref[...], w_ref[...], preferred_element_type=F32).astype(BF16)

    @pl.when(j == N_MAIN_TILES)
    def _():
        os_ref[...] = jnp.dot(h_ref[...], w_ref[...], preferred_element_type=F32)


def _in_projection(xs, mod_l, norm_w, w_in_p, layer):
    tm = IN_TM if len(xs) == 1 else IN_TM // 2
    row = functools.partial(_mod_row, tile_rows=tm)
    return pl.pallas_call(
        functools.partial(_inproj_kernel, n_x=len(xs), tm=tm),
        grid=(N_TOK // tm, NP // IN_TN),
        in_specs=_stream_specs(xs, tm, 2) + [
            pl.BlockSpec((1, D_MODEL), lambda i, j: (0, 0)),
            pl.BlockSpec((None, None, 1, D_MODEL), lambda i, j: (row(i), 0, 0, 0)),
            pl.BlockSpec((None, None, 1, D_MODEL), lambda i, j: (row(i), 1, 0, 0)),
            pl.BlockSpec((None, D_MODEL, IN_TN), lambda i, j: (layer, 0, j)),
        ],
        out_specs=[
            pl.BlockSpec((tm, IN_TN), lambda i, j: (i, jnp.minimum(j, N_MAIN_TILES - 1))),
            pl.BlockSpec((tm, IN_TN), lambda i, j: (i, 0)),
        ],
        out_shape=[
            jax.ShapeDtypeStruct((N_TOK, NP_MAIN), BF16),
            jax.ShapeDtypeStruct((N_TOK, IN_TN), F32),
        ],
        scratch_shapes=[pltpu.VMEM((tm, D_MODEL), BF16)],
        compiler_params=_cparams(("arbitrary", "arbitrary")),
        name="in_projection",
    )(*xs, norm_w, mod_l, mod_l, w_in_p)


def _conv3_tile(u, prev_row, next_row, w_ref, rows, inner_masks=None):
    ridx = lax.broadcasted_iota(jnp.int32, (SUBLANE, 1), 0)
    up = pltpu.roll(u, 1, axis=0)
    up = jnp.concatenate([jnp.where(ridx == 0, prev_row, up[0:SUBLANE]), up[SUBLANE:]], axis=0)
    dn = pltpu.roll(u, rows - 1, axis=0)
    dn = jnp.concatenate([dn[:rows - SUBLANE], jnp.where(ridx == SUBLANE - 1, next_row, dn[rows - SUBLANE:])],
                         axis=0)
    if inner_masks is not None:
        up = up * inner_masks[0]
        dn = dn * inner_masks[1]
    return up * w_ref[0:1, :] + u * w_ref[1:2, :] + dn * w_ref[2:3, :]


CONV_TQ = 1024
SSD_TQ_LAT = 512
HALO = 2 * SUBLANE
BC_WIDTH = 2 * SSM_GROUPS * SSM_STATE


def _tile_neighbours(i, tile_rows):
    n_ctx_tiles = N_CTX_TOK // tile_rows
    tiles_per_lat = DEC_SEQ // tile_rows
    t_in_seq = (i - n_ctx_tiles) % tiles_per_lat
    is_lat = i >= n_ctx_tiles
    has_prev = jnp.logical_and(is_lat, t_in_seq > 0).astype(F32)
    has_next = jnp.logical_and(is_lat, t_in_seq < tiles_per_lat - 1).astype(F32)
    return has_prev, has_next


def _context_edge_masks(tile, tile_rows):
    edge_keep = jnp.where(tile < N_CTX_TOK // tile_rows, 0.0, 1.0)
    pos = lax.broadcasted_iota(jnp.int32, (tile_rows, 1), 0) & (SEQ - 1)
    return jnp.where(pos == 0, edge_keep, 1.0), jnp.where(pos == SEQ - 1, edge_keep, 1.0)


def _ssd_conv_kernel(x_ref, xp_ref, xn_ref, bc_ref, bcp_ref, bcn_ref,
                     cwx_ref, cbx_ref, cwb_ref, cbb_ref, xc_ref, bcc_ref):
    i = pl.program_id(0)
    hp, hn = _tile_neighbours(i, CONV_TQ)
    masks = _context_edge_masks(i, CONV_TQ)

    def conv_silu(u_ref, up_ref, un_ref, w_ref, b_ref):
        prev_row = up_ref[HALO - 1:HALO, :].astype(F32) * hp
        next_row = un_ref[0:1, :].astype(F32) * hn
        conv = _conv3_tile(u_ref[...].astype(F32), prev_row, next_row, w_ref, CONV_TQ, masks)
        return _silu(conv + b_ref[...])

    xc_ref[...] = conv_silu(x_ref, xp_ref, xn_ref, cwx_ref, cbx_ref)
    bcc_ref[...] = conv_silu(bc_ref, bcp_ref, bcn_ref, cwb_ref, cbb_ref)


def _ssd_conv(p, conv_wx, conv_bx, conv_wbc, conv_bbc):
    halo_per_tile = CONV_TQ // HALO
    n_halo_blocks = N_TOK // HALO
    cx, cbc = OFF_SX // SSM_INNER, OFF_BC // BC_WIDTH

    def prev_map(col):
        return lambda i: (jnp.maximum(i * halo_per_tile - 1, 0), col)

    def next_map(col):
        return lambda i: (jnp.minimum((i + 1) * halo_per_tile, n_halo_blocks - 1), col)

    def const2(i):
        return (0, 0)

    return pl.pallas_call(
        _ssd_conv_kernel,
        grid=(N_TOK // CONV_TQ,),
        in_specs=[
            pl.BlockSpec((CONV_TQ, SSM_INNER), lambda i: (i, cx)),
            pl.BlockSpec((HALO, SSM_INNER), prev_map(cx)),
            pl.BlockSpec((HALO, SSM_INNER), next_map(cx)),
            pl.BlockSpec((CONV_TQ, BC_WIDTH), lambda i: (i, cbc)),
            pl.BlockSpec((HALO, BC_WIDTH), prev_map(cbc)),
            pl.BlockSpec((HALO, BC_WIDTH), next_map(cbc)),
            pl.BlockSpec((3, SSM_INNER), const2),
            pl.BlockSpec((1, SSM_INNER), const2),
            pl.BlockSpec((3, BC_WIDTH), const2),
            pl.BlockSpec((1, BC_WIDTH), const2),
        ],
        out_specs=[
            pl.BlockSpec((CONV_TQ, SSM_INNER), lambda i: (i, 0)),
            pl.BlockSpec((CONV_TQ, BC_WIDTH), lambda i: (i, 0)),
        ],
        out_shape=[
            jax.ShapeDtypeStruct((N_TOK, SSM_INNER), F32),
            jax.ShapeDtypeStruct((N_TOK, BC_WIDTH), F32),
        ],
        compiler_params=_cparams(("arbitrary",)),
        name="ssd_conv",
    )(p, p, p, p, p, p, conv_wx, conv_bx, conv_wbc, conv_bbc)


def _ssd_tables(row0, n_seq, seq_len, tq):
    blk_f, blk_b, seq, first = [], [], [], []
    nt = seq_len // tq
    for s in range(n_seq):
        base = (row0 + s * seq_len) // tq
        for k in range(nt):
            blk_f.append(base + k)
            blk_b.append(base + nt - 1 - k)
            seq.append(s)
            first.append(int(k == 0))
    return [np.asarray(a, np.int32) for a in (blk_f, blk_b, seq, first)]


def _split3(a):
    a1 = a.astype(BF16)
    r1 = a - a1.astype(F32)
    a2 = r1.astype(BF16)
    a3 = (r1 - a2.astype(F32)).astype(BF16)
    return a1, a2, a3


def _dot3(lhs_bf16, a):
    a1, a2, a3 = _split3(a)
    return (jnp.dot(lhs_bf16, a1, preferred_element_type=F32)
            + jnp.dot(lhs_bf16, a2, preferred_element_type=F32)
            + jnp.dot(lhs_bf16, a3, preferred_element_type=F32))


def _softplus(x):
    return jnp.maximum(x, 0.0) + jnp.log1p(jnp.exp(-jnp.abs(x)))


def _ssd_kernel(*refs, n_chunks, context):
    blkf_t, blkb_t, seq_t, first_t = refs[:4]
    xf_ref, bcf_ref, dtf_ref, xb_ref, bcb_ref, dtb_ref, alog_ref, dtbias_ref, dsk_ref = refs[4:13]
    if context:
        yf_ref, yb_ref, finf_ref, finb_ref, stf_ref, stb_ref = refs[13:]
    else:
        h0f_ref, h0b_ref, yf_ref, yb_ref, stf_ref, stb_ref = refs[13:]
    s = pl.program_id(0)
    dirs = ((xf_ref, bcf_ref, dtf_ref, stf_ref, yf_ref), (xb_ref, bcb_ref, dtb_ref, stb_ref, yb_ref))

    if context:
        stf_ref[...] = jnp.zeros_like(stf_ref)
        stb_ref[...] = jnp.zeros_like(stb_ref)
    else:
        @pl.when(first_t[s] == 1)
        def _():
            stf_ref[...] = h0f_ref[...].T
            stb_ref[...] = h0b_ref[...].T

    ii = lax.broadcasted_iota(jnp.int32, (CHUNK, CHUNK), 0)
    jj = lax.broadcasted_iota(jnp.int32, (CHUNK, CHUNK), 1)
    masks = (jj <= ii, jj >= ii)
    masks_b = tuple(jnp.where(m, 1.0, 0.0).astype(BF16) for m in masks)
    masks_neg = tuple(jnp.where(m, 0.0, NEG_BIG) for m in masks)
    lo = jj < SSM_HEAD_DIM

    e_r = lax.broadcasted_iota(jnp.int32, (LANE, SSM_INNER), 0)
    e_c = lax.broadcasted_iota(jnp.int32, (LANE, SSM_INNER), 1)
    expand = jnp.where(jnp.right_shift(e_c, 6) == e_r, 1.0, 0.0).astype(BF16)

    def chunk_setup(d, c):
        _, _, dt_ref, _, _ = dirs[d]
        rows = slice(c * CHUNK, (c + 1) * CHUNK)
        a_row = -jnp.exp(alog_ref[d])
        dt = _softplus(dt_ref[rows, :] + dtbias_ref[d])
        cum = _dot3(masks_b[d], dt * a_row)
        cum_t = cum.T
        dt_t = dt.T
        end = CHUNK - 1 if d == 0 else 0
        tot_row = cum[end:end + 1, :]
        tot_col = cum_t[:, end:end + 1]
        return dict(
            rows=rows, cum=cum, cum_t=cum_t, dt_t=dt_t,
            w_rows=jnp.exp(tot_col - cum_t) * dt_t,
            sdec=_dot3_rows(jnp.exp(tot_row), expand))

    def group_setup(d, cs, g):
        _, bc_ref, _, _, _ = dirs[d]
        b_g = bc_ref[cs["rows"], g * SSM_STATE:(g + 1) * SSM_STATE]
        c_g = bc_ref[cs["rows"], (SSM_GROUPS + g) * SSM_STATE:(SSM_GROUPS + g + 1) * SSM_STATE].astype(BF16)
        b_gt = b_g.T
        cb = jnp.dot(c_g, b_gt.astype(BF16), preferred_element_type=F32)
        return c_g, b_gt, cb

    def pair_step(d, cs, gs, kp):
        x_ref, _, _, st_ref, y_ref = dirs[d]
        c_g, b_gt, cb = gs
        cum, cum_t, dt_t = cs["cum"], cs["cum_t"], cs["dt_t"]
        h_a, h_b = 2 * kp, 2 * kp + 1
        ls = slice(kp * LANE, (kp + 1) * LANE)
        x_pair = x_ref[cs["rows"], ls]
        rhs = jnp.concatenate([jnp.where(lo, x_pair, 0.0).astype(BF16),
                               jnp.where(lo, 0.0, x_pair).astype(BF16)], axis=0)

        def head_lhs(h):
            col = jnp.broadcast_to(cum[:, h:h + 1], (CHUNK, CHUNK))
            seg = col - cum_t[h:h + 1, :]
            dec = jnp.exp(seg + masks_neg[d])
            w_intra = dec * cb * dt_t[h:h + 1, :]
            w_state = b_gt * cs["w_rows"][h:h + 1, :]
            return w_intra.astype(BF16), w_state.astype(BF16), col

        wi_a, ws_a, col_a = head_lhs(h_a)
        wi_b, ws_b, col_b = head_lhs(h_b)
        lhs = jnp.concatenate([jnp.concatenate([wi_a, wi_b], axis=1),
                               jnp.concatenate([ws_a, ws_b], axis=1)], axis=0)
        both = jnp.dot(lhs, rhs, preferred_element_type=F32)
        y_diag = both[0:CHUNK, :]
        d_state = both[CHUNK:2 * CHUNK, :]

        h_pair = st_ref[:, ls]
        y_off = jnp.dot(c_g, h_pair.astype(BF16), preferred_element_type=F32)
        e_pair = jnp.exp(jnp.where(lo, col_a, col_b))
        y_ref[cs["rows"], ls] = (y_diag + y_off * e_pair + x_pair * dsk_ref[d, :, ls]).astype(y_ref.dtype)
        st_ref[:, ls] = h_pair * cs["sdec"][:, ls] + d_state

    pairs_per_group = SSM_HEADS // SSM_GROUPS // 2
    setups = [(chunk_setup(0, k), chunk_setup(1, n_chunks - 1 - k)) for k in range(n_chunks)]
    for k in range(n_chunks):
        cs = setups[k]
        for g in range(SSM_GROUPS):
            gs = (group_setup(0, cs[0], g), group_setup(1, cs[1], g))
            for kp in range(g * pairs_per_group, (g + 1) * pairs_per_group):
                pair_step(0, cs[0], gs[0], kp)
                pair_step(1, cs[1], gs[1], kp)

    if context:
        finf_ref[...] = stf_ref[...].T
        finb_ref[...] = stb_ref[...].T


def _dot3_rows(row, rhs_bf16):
    r8 = jnp.broadcast_to(row, (SUBLANE, row.shape[1]))
    r1, r2, r3 = _split3(r8)
    out = (jnp.dot(r1, rhs_bf16, preferred_element_type=F32)
           + jnp.dot(r2, rhs_bf16, preferred_element_type=F32)
           + jnp.dot(r3, rhs_bf16, preferred_element_type=F32))
    return out[0:1, :]


def _ssd(ps, xc, bcc, alog, dtb, dskip, *, row0, n_seq, seq_len, tq, h0=None):
    context = h0 is None
    assert not context or seq_len == tq
    tables = [jnp.asarray(t) for t in _ssd_tables(row0, n_seq, seq_len, tq)]
    n_steps = int(tables[0].shape[0])
    n_rows = n_seq * seq_len
    blk0 = row0 // tq
    cdt = (OFF_DT - NP_MAIN) // LANE

    def fwd_tile(col, off=0):
        return lambda s, blkf, *_: (blkf[s] - off, col)

    def bwd_tile(col, off=0):
        return lambda s, blkf, blkb, *_: (blkb[s] - off, col)

    def tile_specs(tile):
        return [
            pl.BlockSpec((tq, SSM_INNER), tile(0)),
            pl.BlockSpec((tq, BC_WIDTH), tile(0)),
            pl.BlockSpec((tq, LANE), tile(cdt)),
        ]

    def const3(s, *_):
        return (0, 0, 0)

    def seq_map(s, blkf, blkb, seq, *_):
        return (seq[s], 0, 0)

    in_specs = tile_specs(fwd_tile) + tile_specs(bwd_tile) + [
        pl.BlockSpec((2, 1, LANE), const3),
        pl.BlockSpec((2, 1, LANE), const3),
        pl.BlockSpec((2, 1, SSM_INNER), const3),
    ]
    out_specs = [pl.BlockSpec((tq, SSM_INNER), fwd_tile(0, blk0)), pl.BlockSpec((tq, SSM_INNER), bwd_tile(0, blk0))]
    out_shape = [jax.ShapeDtypeStruct((n_rows, SSM_INNER), BF16), jax.ShapeDtypeStruct((n_rows, SSM_INNER), BF16)]
    operands = [xc, bcc, ps, xc, bcc, ps, alog, dtb, dskip]
    if context:
        out_specs += [pl.BlockSpec((None, SSM_INNER, SSM_STATE), seq_map)] * 2
        out_shape += [jax.ShapeDtypeStruct((n_seq, SSM_INNER, SSM_STATE), F32)] * 2
    else:
        h0f, h0b, layer = h0

        def h0_map(s, blkf, blkb, seq, *_):
            return (seq[s], layer, 0, 0)

        in_specs += [pl.BlockSpec((None, None, SSM_INNER, SSM_STATE), h0_map)] * 2
        operands += [h0f, h0b]

    grid_spec = pltpu.PrefetchScalarGridSpec(
        num_scalar_prefetch=4,
        grid=(n_steps,),
        in_specs=in_specs,
        out_specs=out_specs,
        scratch_shapes=[
            pltpu.VMEM((SSM_STATE, SSM_INNER), F32),
            pltpu.VMEM((SSM_STATE, SSM_INNER), F32),
        ],
    )
    return pl.pallas_call(
        functools.partial(_ssd_kernel, n_chunks=tq // CHUNK, context=context),
        grid_spec=grid_spec,
        out_shape=out_shape,
        compiler_params=_cparams(("arbitrary",)),
        name="ssd_scan",
    )(*tables, *operands)


QK_PAD = 128
N_QROWS = MLA_HEADS * QK_PAD
N_VROWS = MLA_HEADS * V_DIM
NT_DIMS = (((1,), (1,)), ((), ()))
Q_PRESCALE = (1.0 / math.sqrt(QK_DIM)) * math.log2(math.e)


def _mla_prep_kernel(*refs, tm, normalize, with_q, keep_ckv):
    if with_q:
        (cq_ref, ckv_ref, kr_ref, krs_ref, cos_ref, sin_ref, cost_ref, sint_ref,
         qnw_ref, kvnw_ref, wq_ref, wk_ref, wv_ref, qt_ref, k_ref, vt_ref) = refs[:16]
    else:
        (ckv_ref, kr_ref, krs_ref, cos_ref, sin_ref, kvnw_ref, wk_ref, wv_ref,
         k_ref, vt_ref) = refs

    ckv = ckv_ref[...]
    if normalize:
        ckv = _rms(ckv, kvnw_ref[...])
    if keep_ckv:
        refs[16][...] = ckv
    ckv_b = ckv.astype(BF16)
    kn = jnp.dot(ckv_b, wk_ref[...], preferred_element_type=F32)
    kr = kr_ref[...] * cos_ref[...] + krs_ref[...] * sin_ref[...]
    for h in range(MLA_HEADS):
        hs = slice(h * QK_PAD, (h + 1) * QK_PAD)
        k_ref[:, hs] = (kn[:, hs] + kr).astype(BF16)
    vt_ref[...] = lax.dot_general(wv_ref[...], ckv_b, NT_DIMS, preferred_element_type=F32).astype(BF16)

    if with_q:
        cqn = _rms(cq_ref[...], qnw_ref[...]).astype(BF16)
        qq = lax.dot_general(wq_ref[...], cqn, NT_DIMS, preferred_element_type=F32)
        for h in range(MLA_HEADS):
            q_h = qq[h * QK_PAD:(h + 1) * QK_PAD, :]
            qs_h = qq[N_QROWS + h * QK_PAD:N_QROWS + (h + 1) * QK_PAD, :]
            q_rot = q_h * cost_ref[...] + qs_h * sint_ref[...]
            qt_ref[h * QK_PAD:(h + 1) * QK_PAD, :] = (q_rot * Q_PRESCALE).astype(BF16)


def _mla_prep_tokens(p, row0, n_seq, seq_len, tm, cos, sin, cos_t, sin_t, qnw, kvnw, wq_t, wk, wv_t, layer,
                     keep_ckv):
    b0 = row0 // tm
    tps = seq_len // tm
    nt = n_seq * tps
    n_rows = n_seq * seq_len
    kernel = functools.partial(_mla_prep_kernel, tm=tm, normalize=True, with_q=True, keep_ckv=keep_ckv)

    def tile(t, b):
        return b * tps + t

    def pcol(width, off):
        return pl.BlockSpec((tm, width), lambda t, b: (b0 + tile(t, b), (off - NP_MAIN) // width))

    def full(a):
        return pl.BlockSpec(a.shape, lambda t, b: (0,) * a.ndim)

    return pl.pallas_call(
        kernel,
        grid=(tps, n_seq),
        in_specs=[
            pcol(Q_LORA, OFF_CQ), pcol(KV_LORA, OFF_CKV), pcol(LANE, OFF_KR), pcol(LANE, OFF_KRS),
            pl.BlockSpec((tm, LANE), lambda t, b: (t, 0)),
            pl.BlockSpec((tm, LANE), lambda t, b: (t, 0)),
            pl.BlockSpec((LANE, tm), lambda t, b: (0, t)),
            pl.BlockSpec((LANE, tm), lambda t, b: (0, t)),
            full(qnw), full(kvnw), _layer_spec(wq_t, layer, 2), _layer_spec(wk, layer, 2), _layer_spec(wv_t, layer, 2),
        ],
        out_specs=[
            pl.BlockSpec((N_QROWS, tm), lambda t, b: (0, tile(t, b))),
            pl.BlockSpec((None, tm, N_QROWS), lambda t, b: (tile(t, b), 0, 0)),
            pl.BlockSpec((None, N_VROWS, tm), lambda t, b: (tile(t, b), 0, 0)),
        ] + ([pl.BlockSpec((tm, KV_LORA), lambda t, b: (tile(t, b), 0))] if keep_ckv else []),
        out_shape=[
            jax.ShapeDtypeStruct((N_QROWS, n_rows), BF16),
            jax.ShapeDtypeStruct((nt, tm, N_QROWS), BF16),
            jax.ShapeDtypeStruct((nt, N_VROWS, tm), BF16),
        ] + ([jax.ShapeDtypeStruct((n_rows, KV_LORA), F32)] if keep_ckv else []),
        compiler_params=_cparams(("arbitrary", "arbitrary")),
        name="mla_prep",
    )(p, p, p, p, cos, sin, cos_t, sin_t, qnw, kvnw, wq_t, wk, wv_t)


def _mla_prep_cache(ckv, kr_pad, ones_tab, zeros_tab, kvnw, wk, wv_t, tm, layer):
    n_rows = ckv.shape[0]
    nt = n_rows // tm
    kernel = functools.partial(_mla_prep_kernel, tm=tm, normalize=False, with_q=False, keep_ckv=False)

    def full(a):
        return pl.BlockSpec(a.shape, lambda i: (0,) * a.ndim)

    return pl.pallas_call(
        kernel,
        grid=(nt,),
        in_specs=[
            pl.BlockSpec((tm, KV_LORA), lambda i: (i, 0)),
            pl.BlockSpec((tm, LANE), lambda i: (i, 0)),
            pl.BlockSpec((tm, LANE), lambda i: (i, 0)),
            pl.BlockSpec((tm, LANE), lambda i: (0, 0)),
            pl.BlockSpec((tm, LANE), lambda i: (0, 0)),
            full(kvnw), _layer_spec(wk, layer, 1), _layer_spec(wv_t, layer, 1),
        ],
        out_specs=[
            pl.BlockSpec((None, tm, N_QROWS), lambda i: (i, 0, 0)),
            pl.BlockSpec((None, N_VROWS, tm), lambda i: (i, 0, 0)),
        ],
        out_shape=[
            jax.ShapeDtypeStruct((nt, tm, N_QROWS), BF16),
            jax.ShapeDtypeStruct((nt, N_VROWS, tm), BF16),
        ],
        compiler_params=_cparams(("arbitrary",)),
        name="mla_prep_cache",
    )(ckv, kr_pad, kr_pad, ones_tab, zeros_tab, kvnw, wk, wv_t)


ATT_TQ = 512
ATT_SUB = 256


def _attn_kernel(*refs, n_kt, tq, tk, with_cache):
    if with_cache:
        qt_ref, kp_ref, vtp_ref, k_ref, vt_ref, o_ref, ot_ref, s_ref = refs
    else:
        qt_ref, k_ref, vt_ref, o_ref, ot_ref, s_ref = refs

    def k_tile(kt, h):
        cols = slice(h * QK_PAD, (h + 1) * QK_PAD)
        if with_cache:
            return kp_ref[:, cols] if kt == 0 else k_ref[kt - 1, :, cols]
        return k_ref[kt, :, cols]

    def vt_tile(kt, h):
        rows = slice(h * V_DIM, (h + 1) * V_DIM)
        if with_cache:
            return vtp_ref[rows, :] if kt == 0 else vt_ref[kt - 1, rows, :]
        return vt_ref[kt, rows, :]

    sub = min(ATT_SUB, tk)

    def scores_step(h, kt, m8):
        q_t = qt_ref[h * QK_PAD:(h + 1) * QK_PAD, :]
        k = k_tile(kt, h)
        for r in range(0, tk, sub):
            s = jnp.dot(k[r:r + sub, :], q_t, preferred_element_type=F32)
            s_ref[h % 2, kt, r:r + sub, :] = s
            m8 = jnp.maximum(m8, jnp.max(s.reshape(sub // SUBLANE, SUBLANE, tq), axis=0))
        return m8

    def probs_step(h, kt, m, l8, acc):
        v_t = vt_tile(kt, h)
        for r in range(0, tk, sub):
            pr = jnp.exp2(s_ref[h % 2, kt, r:r + sub, :] - m)
            l8 = l8 + jnp.sum(pr.reshape(sub // SUBLANE, SUBLANE, tq), axis=0)
            acc = acc + jnp.dot(v_t[:, r:r + sub], pr.astype(BF16), preferred_element_type=F32)
        return l8, acc

    m8_init = jnp.full((SUBLANE, tq), NEG_BIG, F32)
    m8 = m8_init
    for kt in range(n_kt):
        m8 = scores_step(0, kt, m8)
    for h in range(MLA_HEADS):
        m = jnp.max(m8, axis=0, keepdims=True)
        l8 = jnp.zeros((SUBLANE, tq), F32)
        acc = jnp.zeros((V_DIM, tq), F32)
        m8 = m8_init
        for kt in range(n_kt):
            l8, acc = probs_step(h, kt, m, l8, acc)
            if h + 1 < MLA_HEADS:
                m8 = scores_step(h + 1, kt, m8)
        ot_ref[h * V_DIM:(h + 1) * V_DIM, :] = acc / jnp.sum(l8, axis=0, keepdims=True)
    o_ref[...] = ot_ref[...].T.astype(o_ref.dtype)


def _attention(qt, k3, vt3, n_batch, lq, tk, cache=None):
    tq = min(ATT_TQ, lq)
    nq = lq // tq
    n_new = k3.shape[0] // n_batch
    k4 = k3.reshape(n_batch, n_new, tk, N_QROWS)
    v4 = vt3.reshape(n_batch, n_new, N_VROWS, tk)
    n_kt = n_new + (0 if cache is None else 1)
    kernel = functools.partial(_attn_kernel, n_kt=n_kt, tq=tq, tk=tk, with_cache=cache is not None)
    cache_specs = [] if cache is None else [
        pl.BlockSpec((None, tk, N_QROWS), lambda b, i: (b, 0, 0)),
        pl.BlockSpec((None, N_VROWS, tk), lambda b, i: (b, 0, 0)),
    ]
    return pl.pallas_call(
        kernel,
        grid=(n_batch, nq),
        in_specs=[pl.BlockSpec((N_QROWS, tq), lambda b, i: (0, b * nq + i))] + cache_specs + [
            pl.BlockSpec((None, n_new, tk, N_QROWS), lambda b, i: (b, 0, 0, 0)),
            pl.BlockSpec((None, n_new, N_VROWS, tk), lambda b, i: (b, 0, 0, 0)),
        ],
        out_specs=pl.BlockSpec((tq, N_VROWS), lambda b, i: (b * nq + i, 0)),
        out_shape=jax.ShapeDtypeStruct((n_batch * lq, N_VROWS), BF16),
        scratch_shapes=[pltpu.VMEM((N_VROWS, tq), F32), pltpu.VMEM((2, n_kt, tk, tq), F32)],
        compiler_params=_cparams(("arbitrary", "arbitrary")),
        name="attention",
    )(qt, *(() if cache is None else cache), k4, v4)


MG_TM = 512


def _merge_kernel(*refs, n_x):
    x_refs = refs[:n_x]
    zc_refs, yf_refs, yb_refs = refs[n_x:n_x + 2], refs[n_x + 2:n_x + 4], refs[n_x + 4:n_x + 6]
    (ax_ref, axp_ref, axn_ref, ac_ref, acp_ref, acn_ref, ab_ref,
     g_ref, z_ref, gm_ref,
     cw_ref, wa_ref, nw_ref, wb_ref, wc_ref, wo_ref, o_ref) = refs[n_x + 6:]
    i = pl.program_id(0)
    hp, hn = _tile_neighbours(i, MG_TM)
    all_rows = slice(0, MG_TM)

    def f32(v):
        return v.astype(F32)

    u = f32(ac_ref[...]) * f32(ax_ref[...])
    u_prev = f32(acp_ref[HALO - 1:HALO, :]) * f32(axp_ref[HALO - 1:HALO, :]) * hp
    u_next = f32(acn_ref[0:1, :]) * f32(axn_ref[0:1, :]) * hn
    za = f32(ab_ref[...]) * _conv3_tile(u, u_prev, u_next, cw_ref, MG_TM, _context_edge_masks(i, MG_TM))
    y_a = jnp.dot(za.astype(BF16), wa_ref[...], preferred_element_type=F32)

    y_scan = f32(_stream_rows(yf_refs, i, MG_TM, all_rows)) + f32(_stream_rows(yb_refs, i, MG_TM, all_rows))
    yb = y_scan * _silu(f32(z_ref[...]))
    zb = _rms(yb, nw_ref[...])
    y_b = jnp.dot(zb.astype(BF16), wb_ref[...], preferred_element_type=F32)

    y_c = jnp.dot(_stream_rows(zc_refs, i, MG_TM, all_rows), wc_ref[...], preferred_element_type=F32)

    merged = (_sigmoid(f32(g_ref[:, 0:D_MODEL])) * y_a
              + _sigmoid(f32(g_ref[:, D_MODEL:2 * D_MODEL])) * y_b
              + _sigmoid(f32(g_ref[:, 2 * D_MODEL:3 * D_MODEL])) * y_c)
    o = jnp.dot(merged.astype(BF16), wo_ref[...], preferred_element_type=F32)
    o_ref[...] = _stream_rows(x_refs, i, MG_TM, all_rows) + gm_ref[...] * o


def _merge(p, yfs, ybs, zcs, xs, mod_l, conv_w, wa, nw, wb, wc, wo, layer):
    halo_per_tile = MG_TM // HALO
    n_halo_blocks = N_TOK // HALO
    row = functools.partial(_mod_row, tile_rows=MG_TM)

    def pcol(width, off):
        return pl.BlockSpec((MG_TM, width), lambda i: (i, off // width))

    def pprev(off):
        return pl.BlockSpec((HALO, A_WIDTH), lambda i: (jnp.maximum(i * halo_per_tile - 1, 0), off // A_WIDTH))

    def pnext(off):
        return pl.BlockSpec((HALO, A_WIDTH),
                            lambda i: (jnp.minimum((i + 1) * halo_per_tile, n_halo_blocks - 1), off // A_WIDTH))

    def full(a):
        return pl.BlockSpec(a.shape, lambda i: (0,) * a.ndim)

    return pl.pallas_call(
        functools.partial(_merge_kernel, n_x=len(xs)),
        grid=(N_TOK // MG_TM,),
        in_specs=(_stream_specs(xs, MG_TM, 1) + _stream_specs(zcs, MG_TM, 1)
                  + _stream_specs(yfs, MG_TM, 1) + _stream_specs(ybs, MG_TM, 1)) + [
            pcol(A_WIDTH, OFF_AX), pprev(OFF_AX), pnext(OFF_AX),
            pcol(A_WIDTH, OFF_AC), pprev(OFF_AC), pnext(OFF_AC),
            pcol(A_WIDTH, OFF_AB),
            pcol(3 * D_MODEL, OFF_G),
            pcol(SSM_INNER, OFF_Z),
            pl.BlockSpec((None, None, 1, D_MODEL), lambda i: (row(i), 2, 0, 0)),
            full(conv_w), _layer_spec(wa, layer, 1), full(nw), _layer_spec(wb, layer, 1),
            _layer_spec(wc, layer, 1), _layer_spec(wo, layer, 1),
        ],
        out_specs=pl.BlockSpec((MG_TM, D_MODEL), lambda i: (i, 0)),
        out_shape=jax.ShapeDtypeStruct((N_TOK, D_MODEL), F32),
        compiler_params=_cparams(("arbitrary",)),
        name="merge",
    )(*xs, *zcs, *yfs, *ybs, p, p, p, p, p, p, p, p, p, mod_l, conv_w, wa, nw, wb, wc, wo)


FF_TM = 512
FF_CHUNK = 256


def _ffn_kernel(x_ref, nw_ref, sh_ref, sc_ref, gm_ref, w1_ref, w3_ref, w2_ref, o_ref, h_ref, g_ref):
    _modulated_norm_to(h_ref, lambda rows: x_ref[rows, :], nw_ref, sc_ref, sh_ref, FF_TM)
    h = h_ref[...]
    for c in range(0, FF_DIM, FF_CHUNK):
        a = jnp.dot(h, w1_ref[:, c:c + FF_CHUNK], preferred_element_type=F32)
        b = jnp.dot(h, w3_ref[:, c:c + FF_CHUNK], preferred_element_type=F32)
        g_ref[:, c:c + FF_CHUNK] = (_silu(a) * b).astype(BF16)
    ff = jnp.dot(g_ref[...], w2_ref[...], preferred_element_type=F32)
    o_ref[...] = x_ref[...] + gm_ref[...] * ff


def _ffn(x, mod_l, norm_w, w1, w3, w2, layer):
    row = functools.partial(_mod_row, tile_rows=FF_TM)
    return pl.pallas_call(
        _ffn_kernel,
        grid=(N_TOK // FF_TM,),
        in_specs=[
            pl.BlockSpec((FF_TM, D_MODEL), lambda i: (i, 0)),
            pl.BlockSpec((1, D_MODEL), lambda i: (0, 0)),
            pl.BlockSpec((None, None, 1, D_MODEL), lambda i: (row(i), 3, 0, 0)),
            pl.BlockSpec((None, None, 1, D_MODEL), lambda i: (row(i), 4, 0, 0)),
            pl.BlockSpec((None, None, 1, D_MODEL), lambda i: (row(i), 5, 0, 0)),
            _layer_spec(w1, layer, 1), _layer_spec(w3, layer, 1), _layer_spec(w2, layer, 1),
        ],
        out_specs=pl.BlockSpec((FF_TM, D_MODEL), lambda i: (i, 0)),
        out_shape=jax.ShapeDtypeStruct((N_TOK, D_MODEL), F32),
        scratch_shapes=[pltpu.VMEM((FF_TM, D_MODEL), BF16), pltpu.VMEM((FF_TM, FF_DIM), BF16)],
        compiler_params=_cparams(("arbitrary",)),
        name="ffn",
    )(x, norm_w, mod_l, mod_l, mod_l, w1, w3, w2)


FN_TM = 1024


def _final_norm_kernel(x_ref, w_ref, o_ref):
    o_ref[...] = _rms(x_ref[...], w_ref[...])


def _final_norm(x, w, row0, n_rows):
    b0 = row0 // FN_TM
    return pl.pallas_call(
        _final_norm_kernel,
        grid=(n_rows // FN_TM,),
        in_specs=[pl.BlockSpec((FN_TM, D_MODEL), lambda i: (b0 + i, 0)),
                  pl.BlockSpec((1, D_MODEL), lambda i: (0, 0))],
        out_specs=pl.BlockSpec((FN_TM, D_MODEL), lambda i: (i, 0)),
        out_shape=jax.ShapeDtypeStruct((n_rows, D_MODEL), F32),
        compiler_params=_cparams(("arbitrary",)),
        name="final_norm",
    )(x, w)


def _pad_in_weights(w_in):
    o = 0
    a_x = w_in[..., o:o + 512]; o += 512
    a_b = w_in[..., o:o + 512]; o += 512
    a_c = w_in[..., o:o + 512]; o += 512
    s_z = w_in[..., o:o + 1024]; o += 1024
    s_x = w_in[..., o:o + 1024]; o += 1024
    s_bc = w_in[..., o:o + 512]; o += 512
    s_dt = w_in[..., o:o + 16]; o += 16
    cq = w_in[..., o:o + 256]; o += 256
    ckv = w_in[..., o:o + 256]; o += 256
    kr = w_in[..., o:o + 32]; o += 32
    gates = w_in[..., o:o + 3072]; o += 3072

    def z(n):
        return jnp.zeros(w_in.shape[:-1] + (n,), w_in.dtype)

    kr_sw = jnp.concatenate([kr[..., 16:], kr[..., :16]], axis=-1)
    cols = [gates, s_z, s_x, a_x, a_b, a_c, s_bc, cq, ckv,
            s_dt, z(LANE - 16),
            z(64), kr, z(32),
            z(64), kr_sw, z(32),
            z(NP - OFF_KRS - LANE)]
    out = jnp.concatenate(cols, axis=-1).astype(BF16)
    assert out.shape[-1] == NP
    return out


def _q_weights_t(w_uq):
    w = w_uq.reshape(DEPTH, Q_LORA, MLA_HEADS, QK_DIM)
    nope, x1, x2 = w[..., :NOPE_DIM], w[..., NOPE_DIM:NOPE_DIM + 16], w[..., NOPE_DIM + 16:]
    z32 = jnp.zeros_like(w[..., :32])
    z64 = jnp.zeros_like(nope)
    q = jnp.concatenate([nope, x1, x2, z32], axis=-1).reshape(DEPTH, Q_LORA, N_QROWS)
    qs = jnp.concatenate([z64, x2, x1, z32], axis=-1).reshape(DEPTH, Q_LORA, N_QROWS)
    return jnp.swapaxes(jnp.concatenate([q, qs], axis=-1), 1, 2).astype(BF16)


def _kv_weights(w_ukv):
    w = w_ukv.reshape(DEPTH, KV_LORA, MLA_HEADS, NOPE_DIM + V_DIM)
    kn = jnp.concatenate([w[..., :NOPE_DIM], jnp.zeros_like(w[..., :QK_PAD - NOPE_DIM])], axis=-1)
    wk = kn.reshape(DEPTH, KV_LORA, N_QROWS).astype(BF16)
    wv_t = jnp.swapaxes(w[..., NOPE_DIM:].reshape(DEPTH, KV_LORA, N_VROWS), 1, 2).astype(BF16)
    return wk, wv_t


def _rope_tables(n_tokens, lead_rows):
    n_rows = n_tokens // GRID_W
    row = jnp.repeat(jnp.arange(n_rows, dtype=F32), GRID_W)
    col = jnp.tile(jnp.arange(GRID_W, dtype=F32), n_rows)
    pairs = ROPE_DIM // 4
    inv = ROPE_BASE ** (-jnp.arange(pairs, dtype=F32) / pairs)
    ang = jnp.concatenate([row[:, None] * inv, col[:, None] * inv], axis=-1)
    cos, sin = jnp.cos(ang), jnp.sin(ang)
    ones = jnp.ones((n_tokens, NOPE_DIM), F32)
    z32 = jnp.zeros((n_tokens, 32), F32)
    cos_l = jnp.concatenate([ones, cos, cos, z32], axis=-1)
    sin_l = jnp.concatenate([jnp.zeros_like(ones), -sin, sin, z32], axis=-1)
    ident_c = jnp.concatenate([jnp.ones((lead_rows, NOPE_DIM + ROPE_DIM), F32), jnp.zeros((lead_rows, 32), F32)], -1)
    ident_s = jnp.zeros((lead_rows, LANE), F32)
    return jnp.concatenate([ident_c, cos_l], axis=0), jnp.concatenate([ident_s, sin_l], axis=0)


PREP_TM_CTX = SEQ
PREP_TM_LAT = 512


def kernel(x_prompt, x_sample, c, cache_ckv, cache_krope, state_ssm_fwd, state_ssm_bwd, c_ctx, w_in, a_conv_w, w_a_out, ssm_conv_w, ssm_conv_b, ssm_a_log, ssm_dt_bias, ssm_d, ssm_norm_w, w_b_out, q_norm_w, w_uq, kv_norm_w, w_ukv, w_c_out, w_o, w_ada, b_ada, norm1_w, norm2_w, w_ff1, w_ff3, w_ff2, final_norm_w):
    w_in_p = _pad_in_weights(w_in)
    wq_t = _q_weights_t(w_uq)
    wk, wv_t = _kv_weights(w_ukv)
    wa, wb, wc, wo = (w.astype(BF16) for w in (w_a_out, w_b_out, w_c_out, w_o))
    w1, w3, w2 = (w.astype(BF16) for w in (w_ff1, w_ff3, w_ff2))

    cond = jnp.concatenate([c_ctx[None, :], c, jnp.zeros((N_MOD_ROWS - 1 - DEC_BATCH, D_MODEL), F32)], axis=0)
    mod = _modulation(cond, w_ada, b_ada).reshape(DEPTH, N_MOD_ROWS, 6, 1, D_MODEL)

    conv_wx = ssm_conv_w[..., :SSM_INNER]
    conv_wbc = ssm_conv_w[..., SSM_INNER:]
    conv_bx = ssm_conv_b[:, None, :SSM_INNER]
    conv_bbc = ssm_conv_b[:, None, SSM_INNER:]
    pad_h = ((0, 0), (0, 0), (0, 0), (0, LANE - SSM_HEADS))
    alog = jnp.pad(ssm_a_log[:, :, None, :], pad_h)
    dtb = jnp.pad(ssm_dt_bias[:, :, None, :], pad_h)
    dskip = jnp.repeat(ssm_d, SSM_HEAD_DIM, axis=-1)[:, :, None, :]
    h0f = state_ssm_fwd.reshape(DEC_BATCH, DEPTH, SSM_INNER, SSM_STATE)
    h0b = state_ssm_bwd.reshape(DEC_BATCH, DEPTH, SSM_INNER, SSM_STATE)

    cos_c, sin_c = _rope_tables(DEC_SEQ, PREP_TM_CTX)
    cos_l, sin_l = cos_c[PREP_TM_CTX:], sin_c[PREP_TM_CTX:]
    ident_cos, ident_sin = cos_c[:PREP_TM_CTX], sin_c[:PREP_TM_CTX]
    ones_tab = jnp.concatenate([ident_cos, ident_cos], axis=0)
    zeros_tab = jnp.zeros_like(ones_tab)
    cache_kr_pad = jnp.pad(cache_krope, ((0, 0), (0, 0), (0, 0), (64, 32)))

    xs = (x_prompt.reshape(N_CTX_TOK, D_MODEL), x_sample.reshape(N_LAT_TOK, D_MODEL))

    new_ckv, new_kr, new_f, new_b = [], [], [], []
    for l in range(DEPTH):
        p, ps = _in_projection(xs, mod[l], norm1_w[l][None, :], w_in_p, l)

        xc, bcc = _ssd_conv(p, conv_wx[l], conv_bx[l], conv_wbc[l], conv_bbc[l])
        yc_f, yc_b, fin_f, fin_b = _ssd(ps, xc, bcc, alog[l], dtb[l], dskip[l],
                                        row0=0, n_seq=BATCH, seq_len=SEQ, tq=SEQ)
        yl_f, yl_b = _ssd(ps, xc, bcc, alog[l], dtb[l], dskip[l],
                          row0=N_CTX_TOK, n_seq=DEC_BATCH, seq_len=DEC_SEQ, tq=SSD_TQ_LAT, h0=(h0f, h0b, l))
        new_f.append(fin_f)
        new_b.append(fin_b)

        qnw, kvnw = q_norm_w[l][None, :], kv_norm_w[l][None, :]
        qt_c, k_c, vt_c, ckv_c = _mla_prep_tokens(
            ps, 0, BATCH, SEQ, PREP_TM_CTX, ident_cos, ident_sin, ident_cos.T, ident_sin.T,
            qnw, kvnw, wq_t, wk, wv_t, l, keep_ckv=True)
        qt_l, k_l, vt_l = _mla_prep_tokens(
            ps, N_CTX_TOK, DEC_BATCH, DEC_SEQ, PREP_TM_LAT, cos_l, sin_l, cos_l.T, sin_l.T,
            qnw, kvnw, wq_t, wk, wv_t, l, keep_ckv=False)
        k_p, vt_p = _mla_prep_cache(cache_ckv[:, l].reshape(DEC_BATCH * PAST_LEN, KV_LORA),
                                    cache_kr_pad[:, l].reshape(DEC_BATCH * PAST_LEN, LANE),
                                    ones_tab, zeros_tab, kvnw, wk, wv_t, PAST_LEN, l)
        new_ckv.append(ckv_c.reshape(BATCH, SEQ, KV_LORA))
        kr0 = OFF_KR - NP_MAIN + 64
        new_kr.append(ps[:N_CTX_TOK, kr0:kr0 + ROPE_DIM].reshape(BATCH, SEQ, ROPE_DIM))

        zc_c = _attention(qt_c, k_c, vt_c, BATCH, SEQ, SEQ)
        zc_l = _attention(qt_l, k_l, vt_l, DEC_BATCH, DEC_SEQ, PREP_TM_LAT, cache=(k_p, vt_p))

        x = _merge(p, (yc_f, yl_f), (yc_b, yl_b), (zc_c, zc_l), xs, mod[l], a_conv_w[l], wa, ssm_norm_w[l][None, :],
                   wb, wc, wo, l)
        x = _ffn(x, mod[l], norm2_w[l][None, :], w1, w3, w2, l)
        xs = (x,)

    fw = final_norm_w[None, :]
    y_prompt = _final_norm(x, fw, 0, N_CTX_TOK).reshape(BATCH, SEQ, D_MODEL)
    y_sample = _final_norm(x, fw, N_CTX_TOK, N_LAT_TOK).reshape(DEC_BATCH, DEC_SEQ, D_MODEL)
    hshape = (BATCH, DEPTH, SSM_HEADS, SSM_HEAD_DIM, SSM_STATE)
    return (y_prompt, y_sample,
            jnp.stack(new_ckv, axis=1), jnp.stack(new_kr, axis=1),
            jnp.stack(new_f, axis=1).reshape(hshape), jnp.stack(new_b, axis=1).reshape(hshape))
```

```python
import functools
import math

import jax
import jax.numpy as jnp
import numpy as np
from jax import lax
from jax.experimental import pallas as pl
from jax.experimental.pallas import tpu as pltpu

F32 = jnp.float32
BF16 = jnp.bfloat16

D_MODEL = 1024
BATCH = 16
SEQ = 256
DEPTH = 4
DEC_BATCH = 4
DEC_SEQ = 4096
PAST_LEN = 512
GRID_W = 64
EPS = 1e-6
A_WIDTH = 512
SSM_INNER = 1024
SSM_HEAD_DIM = 64
SSM_HEADS = 16
SSM_GROUPS = 2
SSM_STATE = 128
CHUNK = 128
MLA_HEADS = 8
Q_LORA = 256
KV_LORA = 256
NOPE_DIM = 64
ROPE_DIM = 32
V_DIM = 64
QK_DIM = NOPE_DIM + ROPE_DIM
ROPE_BASE = 10000.0
FF_DIM = 2816

N_CTX_TOK = BATCH * SEQ
N_LAT_TOK = DEC_BATCH * DEC_SEQ
N_TOK = N_CTX_TOK + N_LAT_TOK
N_MOD_ROWS = 8

LANE = 128
SUBLANE = 8
VMEM_LIMIT = 56 * 1024 * 1024

OFF_G = 0
OFF_Z = 3072
OFF_SX = 4096
OFF_AX = 5120
OFF_AB = 5632
OFF_AC = 6144
OFF_BC = 6656
NP_MAIN = 7168
OFF_CQ = 7168
OFF_CKV = 7424
OFF_DT = 7680
OFF_KR = 7808
OFF_KRS = 7936
NP = 8192
N_MAIN_TILES = 7

NEG_BIG = -1e30


def _cparams(sem):
    return pltpu.CompilerParams(dimension_semantics=sem, vmem_limit_bytes=VMEM_LIMIT)


def _rms(x, w):
    ms = jnp.mean(x * x, axis=-1, keepdims=True)
    return x * lax.rsqrt(ms + EPS) * w


def _sigmoid(x):
    return 0.5 * jnp.tanh(0.5 * x) + 0.5


def _silu(x):
    h = 0.5 * x
    return h * jnp.tanh(h) + h


def _layer_spec(w, layer, grid_rank):
    zeros = (0,) * (w.ndim - 1)
    imap = (lambda i: (layer,) + zeros) if grid_rank == 1 else (lambda i, j: (layer,) + zeros)
    return pl.BlockSpec((None,) + tuple(w.shape[1:]), imap)


def _mod_row(tile, tile_rows):
    n_ctx_tiles = N_CTX_TOK // tile_rows
    tiles_per_lat = DEC_SEQ // tile_rows
    return jnp.where(tile < n_ctx_tiles, 0, 1 + (tile - n_ctx_tiles) // tiles_per_lat)


MOD_TN = 1536


def _mod_kernel(c_ref, w_ref, b_ref, o_ref):
    c = c_ref[...]
    s = _silu(c).astype(BF16)
    o_ref[...] = jnp.dot(s, w_ref[...].astype(BF16), preferred_element_type=F32) + b_ref[...]


def _modulation(cond, w_ada, b_ada):
    n_col = 6 * D_MODEL
    return pl.pallas_call(
        _mod_kernel,
        grid=(DEPTH, n_col // MOD_TN),
        in_specs=[
            pl.BlockSpec((N_MOD_ROWS, D_MODEL), lambda l, j: (0, 0)),
            pl.BlockSpec((None, D_MODEL, MOD_TN), lambda l, j: (l, 0, j)),
            pl.BlockSpec((None, 1, MOD_TN), lambda l, j: (l, 0, j)),
        ],
        out_specs=pl.BlockSpec((None, N_MOD_ROWS, MOD_TN), lambda l, j: (l, 0, j)),
        out_shape=jax.ShapeDtypeStruct((DEPTH, N_MOD_ROWS, n_col), F32),
        compiler_params=_cparams(("arbitrary", "arbitrary")),
        name="modulation",
    )(cond, w_ada, b_ada.reshape(DEPTH, 1, n_col))


IN_TM = 2048
IN_TN = 1024
NORM_ROWS = 256


def _stream_specs(xs, tm, grid_rank):
    def imap(f):
        return (lambda i: f(i)) if grid_rank == 1 else (lambda i, j: f(i))

    width = xs[0].shape[1]
    if len(xs) == 1:
        return [pl.BlockSpec((tm, width), imap(lambda i: (i, 0)))]
    n_ctx_tiles = N_CTX_TOK // tm
    return [pl.BlockSpec((tm, width), imap(lambda i: (jnp.minimum(i, n_ctx_tiles - 1), 0))),
            pl.BlockSpec((tm, width), imap(lambda i: (jnp.maximum(i - n_ctx_tiles, 0), 0)))]


def _stream_rows(x_refs, tile, tm, rows):
    if len(x_refs) == 1:
        return x_refs[0][rows, :]
    return jnp.where(tile < N_CTX_TOK // tm, x_refs[0][rows, :], x_refs[1][rows, :])


def _modulated_norm_to(h_ref, x_rows, nw_ref, sc_ref, sh_ref, rows):
    for r in range(0, rows, NORM_ROWS):
        x = x_rows(slice(r, r + NORM_ROWS))
        h = _rms(x, nw_ref[...]) * (1.0 + sc_ref[...]) + sh_ref[...]
        h_ref[r:r + NORM_ROWS, :] = h.astype(BF16)


def _inproj_kernel(*refs, n_x, tm):
    x_refs = refs[:n_x]
    nw_ref, sh_ref, sc_ref, w_ref, om_ref, os_ref, h_ref = refs[n_x:]
    i = pl.program_id(0)
    j = pl.program_id(1)

    @pl.when(j == 0)
    def _():
        _modulated_norm_to(h_ref, lambda rows: _stream_rows(x_refs, i, tm, rows),
                           nw_ref, sc_ref, sh_ref, tm)

    @pl.when(j < N_MAIN_TILES)
    def _():
        om_ref[...] = jnp.dot(h_ref[...], w_ref[...], preferred_element_type=F32).astype(BF16)

    @pl.when(j == N_MAIN_TILES)
    def _():
        os_ref[...] = jnp.dot(h_ref[...], w_ref[...], preferred_element_type=F32)


def _in_projection(xs, mod_l, norm_w, w_in_p, layer):
    tm = IN_TM if len(xs) == 1 else IN_TM // 2
    row = functools.partial(_mod_row, tile_rows=tm)
    return pl.pallas_call(
        functools.partial(_inproj_kernel, n_x=len(xs), tm=tm),
        grid=(N_TOK // tm, NP // IN_TN),
        in_specs=_stream_specs(xs, tm, 2) + [
            pl.BlockSpec((1, D_MODEL), lambda i, j: (0, 0)),
            pl.BlockSpec((None, None, 1, D_MODEL), lambda i, j: (row(i), 0, 0, 0)),
            pl.BlockSpec((None, None, 1, D_MODEL), lambda i, j: (row(i), 1, 0, 0)),
            pl.BlockSpec((None, D_MODEL, IN_TN), lambda i, j: (layer, 0, j)),
        ],
        out_specs=[
            pl.BlockSpec((tm, IN_TN), lambda i, j: (i, jnp.minimum(j, N_MAIN_TILES - 1))),
            pl.BlockSpec((tm, IN_TN), lambda i, j: (i, 0)),
        ],
        out_shape=[
            jax.ShapeDtypeStruct((N_TOK, NP_MAIN), BF16),
            jax.ShapeDtypeStruct((N_TOK, IN_TN), F32),
        ],
        scratch_shapes=[pltpu.VMEM((tm, D_MODEL), BF16)],
        compiler_params=_cparams(("arbitrary", "arbitrary")),
        name="in_projection",
    )(*xs, norm_w, mod_l, mod_l, w_in_p)


def _conv3_tile(u, prev_row, next_row, w_ref, rows, inner_masks=None):
    ridx = lax.broadcasted_iota(jnp.int32, (SUBLANE, 1), 0)
    up = pltpu.roll(u, 1, axis=0)
    up = jnp.concatenate([jnp.where(ridx == 0, prev_row, up[0:SUBLANE]), up[SUBLANE:]], axis=0)
    dn = pltpu.roll(u, rows - 1, axis=0)
    dn = jnp.concatenate([dn[:rows - SUBLANE], jnp.where(ridx == SUBLANE - 1, next_row, dn[rows - SUBLANE:])],
                         axis=0)
    if inner_masks is not None:
        up = up * inner_masks[0]
        dn = dn * inner_masks[1]
    return up * w_ref[0:1, :] + u * w_ref[1:2, :] + dn * w_ref[2:3, :]


CONV_TQ = 1024
SSD_TQ_LAT = 512
HALO = 2 * SUBLANE
BC_WIDTH = 2 * SSM_GROUPS * SSM_STATE


def _tile_neighbours(i, tile_rows):
    n_ctx_tiles = N_CTX_TOK // tile_rows
    tiles_per_lat = DEC_SEQ // tile_rows
    t_in_seq = (i - n_ctx_tiles) % tiles_per_lat
    is_lat = i >= n_ctx_tiles
    has_prev = jnp.logical_and(is_lat, t_in_seq > 0).astype(F32)
    has_next = jnp.logical_and(is_lat, t_in_seq < tiles_per_lat - 1).astype(F32)
    return has_prev, has_next


def _context_edge_masks(tile, tile_rows):
    edge_keep = jnp.where(tile < N_CTX_TOK // tile_rows, 0.0, 1.0)
    pos = lax.broadcasted_iota(jnp.int32, (tile_rows, 1), 0) & (SEQ - 1)
    return jnp.where(pos == 0, edge_keep, 1.0), jnp.where(pos == SEQ - 1, edge_keep, 1.0)


def _ssd_conv_kernel(x_ref, xp_ref, xn_ref, bc_ref, bcp_ref, bcn_ref,
                     cwx_ref, cbx_ref, cwb_ref, cbb_ref, xc_ref, bcc_ref):
    i = pl.program_id(0)
    hp, hn = _tile_neighbours(i, CONV_TQ)
    masks = _context_edge_masks(i, CONV_TQ)

    def conv_silu(u_ref, up_ref, un_ref, w_ref, b_ref):
        prev_row = up_ref[HALO - 1:HALO, :].astype(F32) * hp
        next_row = un_ref[0:1, :].astype(F32) * hn
        conv = _conv3_tile(u_ref[...].astype(F32), prev_row, next_row, w_ref, CONV_TQ, masks)
        return _silu(conv + b_ref[...])

    xc_ref[...] = conv_silu(x_ref, xp_ref, xn_ref, cwx_ref, cbx_ref)
    bcc_ref[...] = conv_silu(bc_ref, bcp_ref, bcn_ref, cwb_ref, cbb_ref)


def _ssd_conv(p, conv_wx, conv_bx, conv_wbc, conv_bbc):
    halo_per_tile = CONV_TQ // HALO
    n_halo_blocks = N_TOK // HALO
    cx, cbc = OFF_SX // SSM_INNER, OFF_BC // BC_WIDTH

    def prev_map(col):
        return lambda i: (jnp.maximum(i * halo_per_tile - 1, 0), col)

    def next_map(col):
        return lambda i: (jnp.minimum((i + 1) * halo_per_tile, n_halo_blocks - 1), col)

    def const2(i):
        return (0, 0)

    return pl.pallas_call(
        _ssd_conv_kernel,
        grid=(N_TOK // CONV_TQ,),
        in_specs=[
            pl.BlockSpec((CONV_TQ, SSM_INNER), lambda i: (i, cx)),
            pl.BlockSpec((HALO, SSM_INNER), prev_map(cx)),
            pl.BlockSpec((HALO, SSM_INNER), next_map(cx)),
            pl.BlockSpec((CONV_TQ, BC_WIDTH), lambda i: (i, cbc)),
            pl.BlockSpec((HALO, BC_WIDTH), prev_map(cbc)),
            pl.BlockSpec((HALO, BC_WIDTH), next_map(cbc)),
            pl.BlockSpec((3, SSM_INNER), const2),
            pl.BlockSpec((1, SSM_INNER), const2),
            pl.BlockSpec((3, BC_WIDTH), const2),
            pl.BlockSpec((1, BC_WIDTH), const2),
        ],
        out_specs=[
            pl.BlockSpec((CONV_TQ, SSM_INNER), lambda i: (i, 0)),
            pl.BlockSpec((CONV_TQ, BC_WIDTH), lambda i: (i, 0)),
        ],
        out_shape=[
            jax.ShapeDtypeStruct((N_TOK, SSM_INNER), F32),
            jax.ShapeDtypeStruct((N_TOK, BC_WIDTH), F32),
        ],
        compiler_params=_cparams(("arbitrary",)),
        name="ssd_conv",
    )(p, p, p, p, p, p, conv_wx, conv_bx, conv_wbc, conv_bbc)


def _ssd_tables(row0, n_seq, seq_len, tq):
    blk_f, blk_b, seq, first = [], [], [], []
    nt = seq_len // tq
    for s in range(n_seq):
        base = (row0 + s * seq_len) // tq
        for k in range(nt):
            blk_f.append(base + k)
            blk_b.append(base + nt - 1 - k)
            seq.append(s)
            first.append(int(k == 0))
    return [np.asarray(a, np.int32) for a in (blk_f, blk_b, seq, first)]


def _split3(a):
    a1 = a.astype(BF16)
    r1 = a - a1.astype(F32)
    a2 = r1.astype(BF16)
    a3 = (r1 - a2.astype(F32)).astype(BF16)
    return a1, a2, a3


def _dot3(lhs_bf16, a):
    a1, a2, a3 = _split3(a)
    return (jnp.dot(lhs_bf16, a1, preferred_element_type=F32)
            + jnp.dot(lhs_bf16, a2, preferred_element_type=F32)
            + jnp.dot(lhs_bf16, a3, preferred_element_type=F32))


def _softplus(x):
    return jnp.maximum(x, 0.0) + jnp.log1p(jnp.exp(-jnp.abs(x)))


def _ssd_kernel(*refs, n_chunks, context):
    blkf_t, blkb_t, seq_t, first_t = refs[:4]
    xf_ref, bcf_ref, dtf_ref, xb_ref, bcb_ref, dtb_ref, alog_ref, dtbias_ref, dsk_ref = refs[4:13]
    if context:
        yf_ref, yb_ref, finf_ref, finb_ref, stf_ref, stb_ref = refs[13:]
    else:
        h0f_ref, h0b_ref, yf_ref, yb_ref, stf_ref, stb_ref = refs[13:]
    s = pl.program_id(0)
    dirs = ((xf_ref, bcf_ref, dtf_ref, stf_ref, yf_ref), (xb_ref, bcb_ref, dtb_ref, stb_ref, yb_ref))

    if context:
        stf_ref[...] = jnp.zeros_like(stf_ref)
        stb_ref[...] = jnp.zeros_like(stb_ref)
    else:
        @pl.when(first_t[s] == 1)
        def _():
            stf_ref[...] = h0f_ref[...].T
            stb_ref[...] = h0b_ref[...].T

    ii = lax.broadcasted_iota(jnp.int32, (CHUNK, CHUNK), 0)
    jj = lax.broadcasted_iota(jnp.int32, (CHUNK, CHUNK), 1)
    masks = (jj <= ii, jj >= ii)
    masks_b = tuple(jnp.where(m, 1.0, 0.0).astype(BF16) for m in masks)
    masks_neg = tuple(jnp.where(m, 0.0, NEG_BIG) for m in masks)
    lo = jj < SSM_HEAD_DIM

    e_r = lax.broadcasted_iota(jnp.int32, (LANE, SSM_INNER), 0)
    e_c = lax.broadcasted_iota(jnp.int32, (LANE, SSM_INNER), 1)
    expand = jnp.where(jnp.right_shift(e_c, 6) == e_r, 1.0, 0.0).astype(BF16)

    def chunk_setup(d, c):
        _, _, dt_ref, _, _ = dirs[d]
        rows = slice(c * CHUNK, (c + 1) * CHUNK)
        a_row = -jnp.exp(alog_ref[d])
        dt = _softplus(dt_ref[rows, :] + dtbias_ref[d])
        cum = _dot3(masks_b[d], dt * a_row)
        cum_t = cum.T
        dt_t = dt.T
        end = CHUNK - 1 if d == 0 else 0
        tot_row = cum[end:end + 1, :]
        tot_col = cum_t[:, end:end + 1]
        return dict(
            rows=rows, cum=cum, cum_t=cum_t, dt_t=dt_t,
            w_rows=jnp.exp(tot_col - cum_t) * dt_t,
            sdec=_dot3_rows(jnp.exp(tot_row), expand))

    def group_setup(d, cs, g):
        _, bc_ref, _, _, _ = dirs[d]
        b_g = bc_ref[cs["rows"], g * SSM_STATE:(g + 1) * SSM_STATE]
        c_g = bc_ref[cs["rows"], (SSM_GROUPS + g) * SSM_STATE:(SSM_GROUPS + g + 1) * SSM_STATE].astype(BF16)
        b_gt = b_g.T
        cb = jnp.dot(c_g, b_gt.astype(BF16), preferred_element_type=F32)
        return c_g, b_gt, cb

    def pair_step(d, cs, gs, kp):
        x_ref, _, _, st_ref, y_ref = dirs[d]
        c_g, b_gt, cb = gs
        cum, cum_t, dt_t = cs["cum"], cs["cum_t"], cs["dt_t"]
        h_a, h_b = 2 * kp, 2 * kp + 1
        ls = slice(kp * LANE, (kp + 1) * LANE)
        x_pair = x_ref[cs["rows"], ls]
        rhs = jnp.concatenate([jnp.where(lo, x_pair, 0.0).astype(BF16),
                               jnp.where(lo, 0.0, x_pair).astype(BF16)], axis=0)

        def head_lhs(h):
            col = jnp.broadcast_to(cum[:, h:h + 1], (CHUNK, CHUNK))
            seg = col - cum_t[h:h + 1, :]
            dec = jnp.exp(seg + masks_neg[d])
            w_intra = dec * cb * dt_t[h:h + 1, :]
            w_state = b_gt * cs["w_rows"][h:h + 1, :]
            return w_intra.astype(BF16), w_state.astype(BF16), col

        wi_a, ws_a, col_a = head_lhs(h_a)
        wi_b, ws_b, col_b = head_lhs(h_b)
        lhs = jnp.concatenate([jnp.concatenate([wi_a, wi_b], axis=1),
                               jnp.concatenate([ws_a, ws_b], axis=1)], axis=0)
        both = jnp.dot(lhs, rhs, preferred_element_type=F32)
        y_diag = both[0:CHUNK, :]
        d_state = both[CHUNK:2 * CHUNK, :]

        h_pair = st_ref[:, ls]
        y_off = jnp.dot(c_g, h_pair.astype(BF16), preferred_element_type=F32)
        e_pair = jnp.exp(jnp.where(lo, col_a, col_b))
        y_ref[cs["rows"], ls] = (y_diag + y_off * e_pair + x_pair * dsk_ref[d, :, ls]).astype(y_ref.dtype)
        st_ref[:, ls] = h_pair * cs["sdec"][:, ls] + d_state

    pairs_per_group = SSM_HEADS // SSM_GROUPS // 2
    setups = [(chunk_setup(0, k), chunk_setup(1, n_chunks - 1 - k)) for k in range(n_chunks)]
    for k in range(n_chunks):
        cs = setups[k]
        for g in range(SSM_GROUPS):
            gs = (group_setup(0, cs[0], g), group_setup(1, cs[1], g))
            for kp in range(g * pairs_per_group, (g + 1) * pairs_per_group):
                pair_step(0, cs[0], gs[0], kp)
                pair_step(1, cs[1], gs[1], kp)

    if context:
        finf_ref[...] = stf_ref[...].T
        finb_ref[...] = stb_ref[...].T


def _dot3_rows(row, rhs_bf16):
    r8 = jnp.broadcast_to(row, (SUBLANE, row.shape[1]))
    r1, r2, r3 = _split3(r8)
    out = (jnp.dot(r1, rhs_bf16, preferred_element_type=F32)
           + jnp.dot(r2, rhs_bf16, preferred_element_type=F32)
           + jnp.dot(r3, rhs_bf16, preferred_element_type=F32))
    return out[0:1, :]


def _ssd(ps, xc, bcc, alog, dtb, dskip, *, row0, n_seq, seq_len, tq, h0=None):
    context = h0 is None
    assert not context or seq_len == tq
    tables = [jnp.asarray(t) for t in _ssd_tables(row0, n_seq, seq_len, tq)]
    n_steps = int(tables[0].shape[0])
    n_rows = n_seq * seq_len
    blk0 = row0 // tq
    cdt = (OFF_DT - NP_MAIN) // LANE

    def fwd_tile(col, off=0):
        return lambda s, blkf, *_: (blkf[s] - off, col)

    def bwd_tile(col, off=0):
        return lambda s, blkf, blkb, *_: (blkb[s] - off, col)

    def tile_specs(tile):
        return [
            pl.BlockSpec((tq, SSM_INNER), tile(0)),
            pl.BlockSpec((tq, BC_WIDTH), tile(0)),
            pl.BlockSpec((tq, LANE), tile(cdt)),
        ]

    def const3(s, *_):
        return (0, 0, 0)

    def seq_map(s, blkf, blkb, seq, *_):
        return (seq[s], 0, 0)

    in_specs = tile_specs(fwd_tile) + tile_specs(bwd_tile) + [
        pl.BlockSpec((2, 1, LANE), const3),
        pl.BlockSpec((2, 1, LANE), const3),
        pl.BlockSpec((2, 1, SSM_INNER), const3),
    ]
    out_specs = [pl.BlockSpec((tq, SSM_INNER), fwd_tile(0, blk0)), pl.BlockSpec((tq, SSM_INNER), bwd_tile(0, blk0))]
    out_shape = [jax.ShapeDtypeStruct((n_rows, SSM_INNER), BF16), jax.ShapeDtypeStruct((n_rows, SSM_INNER), BF16)]
    operands = [xc, bcc, ps, xc, bcc, ps, alog, dtb, dskip]
    if context:
        out_specs += [pl.BlockSpec((None, SSM_INNER, SSM_STATE), seq_map)] * 2
        out_shape += [jax.ShapeDtypeStruct((n_seq, SSM_INNER, SSM_STATE), F32)] * 2
    else:
        h0f, h0b, layer = h0

        def h0_map(s, blkf, blkb, seq, *_):
            return (seq[s], layer, 0, 0)

        in_specs += [pl.BlockSpec((None, None, SSM_INNER, SSM_STATE), h0_map)] * 2
        operands += [h0f, h0b]

    grid_spec = pltpu.PrefetchScalarGridSpec(
        num_scalar_prefetch=4,
        grid=(n_steps,),
        in_specs=in_specs,
        out_specs=out_specs,
        scratch_shapes=[
            pltpu.VMEM((SSM_STATE, SSM_INNER), F32),
            pltpu.VMEM((SSM_STATE, SSM_INNER), F32),
        ],
    )
    return pl.pallas_call(
        functools.partial(_ssd_kernel, n_chunks=tq // CHUNK, context=context),
        grid_spec=grid_spec,
        out_shape=out_shape,
        compiler_params=_cparams(("arbitrary",)),
        name="ssd_scan",
    )(*tables, *operands)


QK_PAD = 128
N_QROWS = MLA_HEADS * QK_PAD
N_VROWS = MLA_HEADS * V_DIM
NT_DIMS = (((1,), (1,)), ((), ()))
Q_PRESCALE = (1.0 / math.sqrt(QK_DIM)) * math.log2(math.e)


def _mla_prep_kernel(*refs, tm, normalize, with_q, keep_ckv):
    if with_q:
        (cq_ref, ckv_ref, kr_ref, krs_ref, cos_ref, sin_ref, cost_ref, sint_ref,
         qnw_ref, kvnw_ref, wq_ref, wk_ref, wv_ref, qt_ref, k_ref, vt_ref) = refs[:16]
    else:
        (ckv_ref, kr_ref, krs_ref, cos_ref, sin_ref, kvnw_ref, wk_ref, wv_ref,
         k_ref, vt_ref) = refs

    ckv = ckv_ref[...]
    if normalize:
        ckv = _rms(ckv, kvnw_ref[...])
    if keep_ckv:
        refs[16][...] = ckv
    ckv_b = ckv.astype(BF16)
    kn = jnp.dot(ckv_b, wk_ref[...], preferred_element_type=F32)
    kr = kr_ref[...] * cos_ref[...] + krs_ref[...] * sin_ref[...]
    for h in range(MLA_HEADS):
        hs = slice(h * QK_PAD, (h + 1) * QK_PAD)
        k_ref[:, hs] = (kn[:, hs] + kr).astype(BF16)
    vt_ref[...] = lax.dot_general(wv_ref[...], ckv_b, NT_DIMS, preferred_element_type=F32).astype(BF16)

    if with_q:
        cqn = _rms(cq_ref[...], qnw_ref[...]).astype(BF16)
        qq = lax.dot_general(wq_ref[...], cqn, NT_DIMS, preferred_element_type=F32)
        for h in range(MLA_HEADS):
            q_h = qq[h * QK_PAD:(h + 1) * QK_PAD, :]
            qs_h = qq[N_QROWS + h * QK_PAD:N_QROWS + (h + 1) * QK_PAD, :]
            q_rot = q_h * cost_ref[...] + qs_h * sint_ref[...]
            qt_ref[h * QK_PAD:(h + 1) * QK_PAD, :] = (q_rot * Q_PRESCALE).astype(BF16)


def _mla_prep_tokens(p, row0, n_seq, seq_len, tm, cos, sin, cos_t, sin_t, qnw, kvnw, wq_t, wk, wv_t, layer,
                     keep_ckv):
    b0 = row0 // tm
    tps = seq_len // tm
    nt = n_seq * tps
    n_rows = n_seq * seq_len
    kernel = functools.partial(_mla_prep_kernel, tm=tm, normalize=True, with_q=True, keep_ckv=keep_ckv)

    def tile(t, b):
        return b * tps + t

    def pcol(width, off):
        return pl.BlockSpec((tm, width), lambda t, b: (b0 + tile(t, b), (off - NP_MAIN) // width))

    def full(a):
        return pl.BlockSpec(a.shape, lambda t, b: (0,) * a.ndim)

    return pl.pallas_call(
        kernel,
        grid=(tps, n_seq),
        in_specs=[
            pcol(Q_LORA, OFF_CQ), pcol(KV_LORA, OFF_CKV), pcol(LANE, OFF_KR), pcol(LANE, OFF_KRS),
            pl.BlockSpec((tm, LANE), lambda t, b: (t, 0)),
            pl.BlockSpec((tm, LANE), lambda t, b: (t, 0)),
            pl.BlockSpec((LANE, tm), lambda t, b: (0, t)),
            pl.BlockSpec((LANE, tm), lambda t, b: (0, t)),
            full(qnw), full(kvnw), _layer_spec(wq_t, layer, 2), _layer_spec(wk, layer, 2), _layer_spec(wv_t, layer, 2),
        ],
        out_specs=[
            pl.BlockSpec((N_QROWS, tm), lambda t, b: (0, tile(t, b))),
            pl.BlockSpec((None, tm, N_QROWS), lambda t, b: (tile(t, b), 0, 0)),
            pl.BlockSpec((None, N_VROWS, tm), lambda t, b: (tile(t, b), 0, 0)),
        ] + ([pl.BlockSpec((tm, KV_LORA), lambda t, b: (tile(t, b), 0))] if keep_ckv else []),
        out_shape=[
            jax.ShapeDtypeStruct((N_QROWS, n_rows), BF16),
            jax.ShapeDtypeStruct((nt, tm, N_QROWS), BF16),
            jax.ShapeDtypeStruct((nt, N_VROWS, tm), BF16),
        ] + ([jax.ShapeDtypeStruct((n_rows, KV_LORA), F32)] if keep_ckv else []),
        compiler_params=_cparams(("arbitrary", "arbitrary")),
        name="mla_prep",
    )(p, p, p, p, cos, sin, cos_t, sin_t, qnw, kvnw, wq_t, wk, wv_t)


def _mla_prep_cache(ckv, kr_pad, ones_tab, zeros_tab, kvnw, wk, wv_t, tm, layer):
    n_rows = ckv.shape[0]
    nt = n_rows // tm
    kernel = functools.partial(_mla_prep_kernel, tm=tm, normalize=False, with_q=False, keep_ckv=False)

    def full(a):
        return pl.BlockSpec(a.shape, lambda i: (0,) * a.ndim)

    return pl.pallas_call(
        kernel,
        grid=(nt,),
        in_specs=[
            pl.BlockSpec((tm, KV_LORA), lambda i: (i, 0)),
            pl.BlockSpec((tm, LANE), lambda i: (i, 0)),
            pl.BlockSpec((tm, LANE), lambda i: (i, 0)),
            pl.BlockSpec((tm, LANE), lambda i: (0, 0)),
            pl.BlockSpec((tm, LANE), lambda i: (0, 0)),
            full(kvnw), _layer_spec(wk, layer, 1), _layer_spec(wv_t, layer, 1),
        ],
        out_specs=[
            pl.BlockSpec((None, tm, N_QROWS), lambda i: (i, 0, 0)),
            pl.BlockSpec((None, N_VROWS, tm), lambda i: (i, 0, 0)),
        ],
        out_shape=[
            jax.ShapeDtypeStruct((nt, tm, N_QROWS), BF16),
            jax.ShapeDtypeStruct((nt, N_VROWS, tm), BF16),
        ],
        compiler_params=_cparams(("arbitrary",)),
        name="mla_prep_cache",
    )(ckv, kr_pad, kr_pad, ones_tab, zeros_tab, kvnw, wk, wv_t)


ATT_TQ = 512
ATT_SUB = 256


def _attn_kernel(*refs, n_kt, tq, tk, with_cache):
    if with_cache:
        qt_ref, kp_ref, vtp_ref, k_ref, vt_ref, o_ref, ot_ref, s_ref = refs
    else:
        qt_ref, k_ref, vt_ref, o_ref, ot_ref, s_ref = refs

    def k_tile(kt, h):
        cols = slice(h * QK_PAD, (h + 1) * QK_PAD)
        if with_cache:
            return kp_ref[:, cols] if kt == 0 else k_ref[kt - 1, :, cols]
        return k_ref[kt, :, cols]

    def vt_tile(kt, h):
        rows = slice(h * V_DIM, (h + 1) * V_DIM)
        if with_cache:
            return vtp_ref[rows, :] if kt == 0 else vt_ref[kt - 1, rows, :]
        return vt_ref[kt, rows, :]

    sub = min(ATT_SUB, tk)

    def scores_step(h, kt, m8):
        q_t = qt_ref[h * QK_PAD:(h + 1) * QK_PAD, :]
        k = k_tile(kt, h)
        for r in range(0, tk, sub):
            s = jnp.dot(k[r:r + sub, :], q_t, preferred_element_type=F32)
            s_ref[h % 2, kt, r:r + sub, :] = s
            m8 = jnp.maximum(m8, jnp.max(s.reshape(sub // SUBLANE, SUBLANE, tq), axis=0))
        return m8

    def probs_step(h, kt, m, l8, acc):
        v_t = vt_tile(kt, h)
        for r in range(0, tk, sub):
            pr = jnp.exp2(s_ref[h % 2, kt, r:r + sub, :] - m)
            l8 = l8 + jnp.sum(pr.reshape(sub // SUBLANE, SUBLANE, tq), axis=0)
            acc = acc + jnp.dot(v_t[:, r:r + sub], pr.astype(BF16), preferred_element_type=F32)
        return l8, acc

    m8_init = jnp.full((SUBLANE, tq), NEG_BIG, F32)
    m8 = m8_init
    for kt in range(n_kt):
        m8 = scores_step(0, kt, m8)
    for h in range(MLA_HEADS):
        m = jnp.max(m8, axis=0, keepdims=True)
        l8 = jnp.zeros((SUBLANE, tq), F32)
        acc = jnp.zeros((V_DIM, tq), F32)
        m8 = m8_init
        for kt in range(n_kt):
            l8, acc = probs_step(h, kt, m, l8, acc)
            if h + 1 < MLA_HEADS:
                m8 = scores_step(h + 1, kt, m8)
        ot_ref[h * V_DIM:(h + 1) * V_DIM, :] = acc / jnp.sum(l8, axis=0, keepdims=True)
    o_ref[...] = ot_ref[...].T.astype(o_ref.dtype)


def _attention(qt, k3, vt3, n_batch, lq, tk, cache=None):
    tq = min(ATT_TQ, lq)
    nq = lq // tq
    n_new = k3.shape[0] // n_batch
    k4 = k3.reshape(n_batch, n_new, tk, N_QROWS)
    v4 = vt3.reshape(n_batch, n_new, N_VROWS, tk)
    n_kt = n_new + (0 if cache is None else 1)
    kernel = functools.partial(_attn_kernel, n_kt=n_kt, tq=tq, tk=tk, with_cache=cache is not None)
    cache_specs = [] if cache is None else [
        pl.BlockSpec((None, tk, N_QROWS), lambda b, i: (b, 0, 0)),
        pl.BlockSpec((None, N_VROWS, tk), lambda b, i: (b, 0, 0)),
    ]
    return pl.pallas_call(
        kernel,
        grid=(n_batch, nq),
        in_specs=[pl.BlockSpec((N_QROWS, tq), lambda b, i: (0, b * nq + i))] + cache_specs + [
            pl.BlockSpec((None, n_new, tk, N_QROWS), lambda b, i: (b, 0, 0, 0)),
            pl.BlockSpec((None, n_new, N_VROWS, tk), lambda b, i: (b, 0, 0, 0)),
        ],
        out_specs=pl.BlockSpec((tq, N_VROWS), lambda b, i: (b * nq + i, 0)),
        out_shape=jax.ShapeDtypeStruct((n_batch * lq, N_VROWS), BF16),
        scratch_shapes=[pltpu.VMEM((N_VROWS, tq), F32), pltpu.VMEM((2, n_kt, tk, tq), F32)],
        compiler_params=_cparams(("arbitrary", "arbitrary")),
        name="attention",
    )(qt, *(() if cache is None else cache), k4, v4)


MG_TM = 512


def _merge_kernel(*refs, n_x):
    x_refs = refs[:n_x]
    zc_refs, yf_refs, yb_refs = refs[n_x:n_x + 2], refs[n_x + 2:n_x + 4], refs[n_x + 4:n_x + 6]
    (ax_ref, axp_ref, axn_ref, ac_ref, acp_ref, acn_ref, ab_ref,
     g_ref, z_ref, gm_ref,
     cw_ref, wa_ref, nw_ref, wb_ref, wc_ref, wo_ref, o_ref) = refs[n_x + 6:]
    i = pl.program_id(0)
    hp, hn = _tile_neighbours(i, MG_TM)
    all_rows = slice(0, MG_TM)

    def f32(v):
        return v.astype(F32)

    u = f32(ac_ref[...]) * f32(ax_ref[...])
    u_prev = f32(acp_ref[HALO - 1:HALO, :]) * f32(axp_ref[HALO - 1:HALO, :]) * hp
    u_next = f32(acn_ref[0:1, :]) * f32(axn_ref[0:1, :]) * hn
    za = f32(ab_ref[...]) * _conv3_tile(u, u_prev, u_next, cw_ref, MG_TM, _context_edge_masks(i, MG_TM))
    y_a = jnp.dot(za.astype(BF16), wa_ref[...], preferred_element_type=F32)

    y_scan = f32(_stream_rows(yf_refs, i, MG_TM, all_rows)) + f32(_stream_rows(yb_refs, i, MG_TM, all_rows))
    yb = y_scan * _silu(f32(z_ref[...]))
    zb = _rms(yb, nw_ref[...])
    y_b = jnp.dot(zb.astype(BF16), wb_ref[...], preferred_element_type=F32)

    y_c = jnp.dot(_stream_rows(zc_refs, i, MG_TM, all_rows), wc_ref[...], preferred_element_type=F32)

    merged = (_sigmoid(f32(g_ref[:, 0:D_MODEL])) * y_a
              + _sigmoid(f32(g_ref[:, D_MODEL:2 * D_MODEL])) * y_b
              + _sigmoid(f32(g_ref[:, 2 * D_MODEL:3 * D_MODEL])) * y_c)
    o = jnp.dot(merged.astype(BF16), wo_ref[...], preferred_element_type=F32)
    o_ref[...] = _stream_rows(x_refs, i, MG_TM, all_rows) + gm_ref[...] * o


def _merge(p, yfs, ybs, zcs, xs, mod_l, conv_w, wa, nw, wb, wc, wo, layer):
    halo_per_tile = MG_TM // HALO
    n_halo_blocks = N_TOK // HALO
    row = functools.partial(_mod_row, tile_rows=MG_TM)

    def pcol(width, off):
        return pl.BlockSpec((MG_TM, width), lambda i: (i, off // width))

    def pprev(off):
        return pl.BlockSpec((HALO, A_WIDTH), lambda i: (jnp.maximum(i * halo_per_tile - 1, 0), off // A_WIDTH))

    def pnext(off):
        return pl.BlockSpec((HALO, A_WIDTH),
                            lambda i: (jnp.minimum((i + 1) * halo_per_tile, n_halo_blocks - 1), off // A_WIDTH))

    def full(a):
        return pl.BlockSpec(a.shape, lambda i: (0,) * a.ndim)

    return pl.pallas_call(
        functools.partial(_merge_kernel, n_x=len(xs)),
        grid=(N_TOK // MG_TM,),
        in_specs=(_stream_specs(xs, MG_TM, 1) + _stream_specs(zcs, MG_TM, 1)
                  + _stream_specs(yfs, MG_TM, 1) + _stream_specs(ybs, MG_TM, 1)) + [
            pcol(A_WIDTH, OFF_AX), pprev(OFF_AX), pnext(OFF_AX),
            pcol(A_WIDTH, OFF_AC), pprev(OFF_AC), pnext(OFF_AC),
            pcol(A_WIDTH, OFF_AB),
            pcol(3 * D_MODEL, OFF_G),
            pcol(SSM_INNER, OFF_Z),
            pl.BlockSpec((None, None, 1, D_MODEL), lambda i: (row(i), 2, 0, 0)),
            full(conv_w), _layer_spec(wa, layer, 1), full(nw), _layer_spec(wb, layer, 1),
            _layer_spec(wc, layer, 1), _layer_spec(wo, layer, 1),
        ],
        out_specs=pl.BlockSpec((MG_TM, D_MODEL), lambda i: (i, 0)),
        out_shape=jax.ShapeDtypeStruct((N_TOK, D_MODEL), F32),
        compiler_params=_cparams(("arbitrary",)),
        name="merge",
    )(*xs, *zcs, *yfs, *ybs, p, p, p, p, p, p, p, p, p, mod_l, conv_w, wa, nw, wb, wc, wo)


FF_TM = 512
FF_CHUNK = 256


def _ffn_kernel(*refs, final):
    x_ref, nw_ref, sh_ref, sc_ref, gm_ref, w1_ref, w3_ref, w2_ref = refs[:8]
    if final:
        fw_ref, oc_ref, ol_ref, h_ref, g_ref = refs[8:]
    else:
        o_ref, h_ref, g_ref = refs[8:]
    _modulated_norm_to(h_ref, lambda rows: x_ref[rows, :], nw_ref, sc_ref, sh_ref, FF_TM)
    h = h_ref[...]
    for c in range(0, FF_DIM, FF_CHUNK):
        a = jnp.dot(h, w1_ref[:, c:c + FF_CHUNK], preferred_element_type=F32)
        b = jnp.dot(h, w3_ref[:, c:c + FF_CHUNK], preferred_element_type=F32)
        g_ref[:, c:c + FF_CHUNK] = (_silu(a) * b).astype(BF16)
    ff = jnp.dot(g_ref[...], w2_ref[...], preferred_element_type=F32)
    x_new = x_ref[...] + gm_ref[...] * ff
    if final:
        y = _rms(x_new, fw_ref[...])
        ol_ref[...] = y

        @pl.when(pl.program_id(0) < N_CTX_TOK // FF_TM)
        def _():
            oc_ref[...] = y
    else:
        o_ref[...] = x_new


def _ffn(x, mod_l, norm_w, w1, w3, w2, layer, final_w=None):
    row = functools.partial(_mod_row, tile_rows=FF_TM)
    final = final_w is not None
    n_ctx_tiles = N_CTX_TOK // FF_TM
    if final:
        extra_in = [pl.BlockSpec((1, D_MODEL), lambda i: (0, 0))]
        out_specs = [pl.BlockSpec((FF_TM, D_MODEL), lambda i: (jnp.minimum(i, n_ctx_tiles - 1), 0)),
                     pl.BlockSpec((FF_TM, D_MODEL), lambda i: (jnp.maximum(i - n_ctx_tiles, 0), 0))]
        out_shape = [jax.ShapeDtypeStruct((N_CTX_TOK, D_MODEL), F32), jax.ShapeDtypeStruct((N_LAT_TOK, D_MODEL), F32)]
    else:
        extra_in = []
        out_specs = pl.BlockSpec((FF_TM, D_MODEL), lambda i: (i, 0))
        out_shape = jax.ShapeDtypeStruct((N_TOK, D_MODEL), F32)
    return pl.pallas_call(
        functools.partial(_ffn_kernel, final=final),
        grid=(N_TOK // FF_TM,),
        in_specs=[
            pl.BlockSpec((FF_TM, D_MODEL), lambda i: (i, 0)),
            pl.BlockSpec((1, D_MODEL), lambda i: (0, 0)),
            pl.BlockSpec((None, None, 1, D_MODEL), lambda i: (row(i), 3, 0, 0)),
            pl.BlockSpec((None, None, 1, D_MODEL), lambda i: (row(i), 4, 0, 0)),
            pl.BlockSpec((None, None, 1, D_MODEL), lambda i: (row(i), 5, 0, 0)),
            _layer_spec(w1, layer, 1), _layer_spec(w3, layer, 1), _layer_spec(w2, layer, 1),
        ] + extra_in,
        out_specs=out_specs,
        out_shape=out_shape,
        scratch_shapes=[pltpu.VMEM((FF_TM, D_MODEL), BF16), pltpu.VMEM((FF_TM, FF_DIM), BF16)],
        compiler_params=_cparams(("arbitrary",)),
        name="ffn",
    )(x, norm_w, mod_l, mod_l, mod_l, w1, w3, w2, *(() if final_w is None else (final_w,)))


def _pad_in_weights(w_in):
    src_ax, src_z, src_bc, src_dt, src_cq, src_kr, src_g = 0, 1536, 3584, 4096, 4112, 4624, 4656
    half = ROPE_DIM // 2
    pieces = [
        (OFF_G, src_g, 3 * D_MODEL),
        (OFF_Z, src_z, 2 * SSM_INNER),
        (OFF_AX, src_ax, 3 * A_WIDTH),
        (OFF_BC, src_bc, BC_WIDTH),
        (OFF_CQ, src_cq, Q_LORA + KV_LORA),
        (OFF_DT, src_dt, SSM_HEADS),
        (OFF_KR + 64, src_kr, ROPE_DIM),
        (OFF_KRS + 64, src_kr + half, half),
        (OFF_KRS + 64 + half, src_kr, half),
    ]
    w16 = w_in.astype(BF16)
    out = jnp.zeros(w_in.shape[:-1] + (NP,), BF16)
    for dst, src, width in pieces:
        out = out.at[..., dst:dst + width].set(w16[..., src:src + width])
    return out


def _q_weights_t(w_uq):
    w = w_uq.reshape(DEPTH, Q_LORA, MLA_HEADS, QK_DIM)
    nope, x1, x2 = w[..., :NOPE_DIM], w[..., NOPE_DIM:NOPE_DIM + 16], w[..., NOPE_DIM + 16:]
    z32 = jnp.zeros_like(w[..., :32])
    z64 = jnp.zeros_like(nope)
    q = jnp.concatenate([nope, x1, x2, z32], axis=-1).reshape(DEPTH, Q_LORA, N_QROWS)
    qs = jnp.concatenate([z64, x2, x1, z32], axis=-1).reshape(DEPTH, Q_LORA, N_QROWS)
    return jnp.swapaxes(jnp.concatenate([q, qs], axis=-1), 1, 2).astype(BF16)


def _kv_weights(w_ukv):
    w = w_ukv.reshape(DEPTH, KV_LORA, MLA_HEADS, NOPE_DIM + V_DIM)
    kn = jnp.concatenate([w[..., :NOPE_DIM], jnp.zeros_like(w[..., :QK_PAD - NOPE_DIM])], axis=-1)
    wk = kn.reshape(DEPTH, KV_LORA, N_QROWS).astype(BF16)
    wv_t = jnp.swapaxes(w[..., NOPE_DIM:].reshape(DEPTH, KV_LORA, N_VROWS), 1, 2).astype(BF16)
    return wk, wv_t


def _rope_tables(n_tokens, lead_rows):
    n_rows = n_tokens // GRID_W
    row = jnp.repeat(jnp.arange(n_rows, dtype=F32), GRID_W)
    col = jnp.tile(jnp.arange(GRID_W, dtype=F32), n_rows)
    pairs = ROPE_DIM // 4
    inv = ROPE_BASE ** (-jnp.arange(pairs, dtype=F32) / pairs)
    ang = jnp.concatenate([row[:, None] * inv, col[:, None] * inv], axis=-1)
    cos, sin = jnp.cos(ang), jnp.sin(ang)
    ones = jnp.ones((n_tokens, NOPE_DIM), F32)
    z32 = jnp.zeros((n_tokens, 32), F32)
    cos_l = jnp.concatenate([ones, cos, cos, z32], axis=-1)
    sin_l = jnp.concatenate([jnp.zeros_like(ones), -sin, sin, z32], axis=-1)
    ident_c = jnp.concatenate([jnp.ones((lead_rows, NOPE_DIM + ROPE_DIM), F32), jnp.zeros((lead_rows, 32), F32)], -1)
    ident_s = jnp.zeros((lead_rows, LANE), F32)
    return jnp.concatenate([ident_c, cos_l], axis=0), jnp.concatenate([ident_s, sin_l], axis=0)


PREP_TM_CTX = SEQ
PREP_TM_LAT = 512


def kernel(x_prompt, x_sample, c, cache_ckv, cache_krope, state_ssm_fwd, state_ssm_bwd, c_ctx, w_in, a_conv_w, w_a_out, ssm_conv_w, ssm_conv_b, ssm_a_log, ssm_dt_bias, ssm_d, ssm_norm_w, w_b_out, q_norm_w, w_uq, kv_norm_w, w_ukv, w_c_out, w_o, w_ada, b_ada, norm1_w, norm2_w, w_ff1, w_ff3, w_ff2, final_norm_w):
    w_in_p = _pad_in_weights(w_in)
    wq_t = _q_weights_t(w_uq)
    wk, wv_t = _kv_weights(w_ukv)
    wa, wb, wc, wo = (w.astype(BF16) for w in (w_a_out, w_b_out, w_c_out, w_o))
    w1, w3, w2 = (w.astype(BF16) for w in (w_ff1, w_ff3, w_ff2))

    cond = jnp.concatenate([c_ctx[None, :], c, jnp.zeros((N_MOD_ROWS - 1 - DEC_BATCH, D_MODEL), F32)], axis=0)
    mod = _modulation(cond, w_ada, b_ada).reshape(DEPTH, N_MOD_ROWS, 6, 1, D_MODEL)

    conv_wx = ssm_conv_w[..., :SSM_INNER]
    conv_wbc = ssm_conv_w[..., SSM_INNER:]
    conv_bx = ssm_conv_b[:, None, :SSM_INNER]
    conv_bbc = ssm_conv_b[:, None, SSM_INNER:]
    pad_h = ((0, 0), (0, 0), (0, 0), (0, LANE - SSM_HEADS))
    alog = jnp.pad(ssm_a_log[:, :, None, :], pad_h)
    dtb = jnp.pad(ssm_dt_bias[:, :, None, :], pad_h)
    dskip = jnp.repeat(ssm_d, SSM_HEAD_DIM, axis=-1)[:, :, None, :]
    h0f = state_ssm_fwd.reshape(DEC_BATCH, DEPTH, SSM_INNER, SSM_STATE)
    h0b = state_ssm_bwd.reshape(DEC_BATCH, DEPTH, SSM_INNER, SSM_STATE)

    cos_c, sin_c = _rope_tables(DEC_SEQ, PREP_TM_CTX)
    cos_l, sin_l = cos_c[PREP_TM_CTX:], sin_c[PREP_TM_CTX:]
    ident_cos, ident_sin = cos_c[:PREP_TM_CTX], sin_c[:PREP_TM_CTX]
    ones_tab = jnp.concatenate([ident_cos, ident_cos], axis=0)
    zeros_tab = jnp.zeros_like(ones_tab)
    cache_kr_pad = jnp.pad(cache_krope, ((0, 0), (0, 0), (0, 0), (64, 32)))

    xs = (x_prompt.reshape(N_CTX_TOK, D_MODEL), x_sample.reshape(N_LAT_TOK, D_MODEL))

    new_ckv, new_kr, new_f, new_b = [], [], [], []
    for l in range(DEPTH):
        p, ps = _in_projection(xs, mod[l], norm1_w[l][None, :], w_in_p, l)

        xc, bcc = _ssd_conv(p, conv_wx[l], conv_bx[l], conv_wbc[l], conv_bbc[l])
        yc_f, yc_b, fin_f, fin_b = _ssd(ps, xc, bcc, alog[l], dtb[l], dskip[l],
                                        row0=0, n_seq=BATCH, seq_len=SEQ, tq=SEQ)
        yl_f, yl_b = _ssd(ps, xc, bcc, alog[l], dtb[l], dskip[l],
                          row0=N_CTX_TOK, n_seq=DEC_BATCH, seq_len=DEC_SEQ, tq=SSD_TQ_LAT, h0=(h0f, h0b, l))
        new_f.append(fin_f)
        new_b.append(fin_b)

        qnw, kvnw = q_norm_w[l][None, :], kv_norm_w[l][None, :]
        qt_c, k_c, vt_c, ckv_c = _mla_prep_tokens(
            ps, 0, BATCH, SEQ, PREP_TM_CTX, ident_cos, ident_sin, ident_cos.T, ident_sin.T,
            qnw, kvnw, wq_t, wk, wv_t, l, keep_ckv=True)
        qt_l, k_l, vt_l = _mla_prep_tokens(
            ps, N_CTX_TOK, DEC_BATCH, DEC_SEQ, PREP_TM_LAT, cos_l, sin_l, cos_l.T, sin_l.T,
            qnw, kvnw, wq_t, wk, wv_t, l, keep_ckv=False)
        k_p, vt_p = _mla_prep_cache(cache_ckv[:, l].reshape(DEC_BATCH * PAST_LEN, KV_LORA),
                                    cache_kr_pad[:, l].reshape(DEC_BATCH * PAST_LEN, LANE),
                                    ones_tab, zeros_tab, kvnw, wk, wv_t, PAST_LEN, l)
        new_ckv.append(ckv_c.reshape(BATCH, SEQ, KV_LORA))
        kr0 = OFF_KR - NP_MAIN + 64
        new_kr.append(ps[:N_CTX_TOK, kr0:kr0 + ROPE_DIM].reshape(BATCH, SEQ, ROPE_DIM))

        zc_c = _attention(qt_c, k_c, vt_c, BATCH, SEQ, SEQ)
        zc_l = _attention(qt_l, k_l, vt_l, DEC_BATCH, DEC_SEQ, PREP_TM_LAT, cache=(k_p, vt_p))

        x = _merge(p, (yc_f, yl_f), (yc_b, yl_b), (zc_c, zc_l), xs, mod[l], a_conv_w[l], wa, ssm_norm_w[l][None, :],
                   wb, wc, wo, l)
        if l + 1 < DEPTH:
            xs = (_ffn(x, mod[l], norm2_w[l][None, :], w1, w3, w2, l),)
        else:
            y_ctx, y_lat = _ffn(x, mod[l], norm2_w[l][None, :], w1, w3, w2, l, final_w=final_norm_w[None, :])

    y_prompt = y_ctx.reshape(BATCH, SEQ, D_MODEL)
    y_sample = y_lat.reshape(DEC_BATCH, DEC_SEQ, D_MODEL)
    hshape = (BATCH, DEPTH, SSM_HEADS, SSM_HEAD_DIM, SSM_STATE)
    return (y_prompt, y_sample,
            jnp.stack(new_ckv, axis=1), jnp.stack(new_kr, axis=1),
            jnp.stack(new_f, axis=1).reshape(hshape), jnp.stack(new_b, axis=1).reshape(hshape))
```

```python
import functools
import math

import jax
import jax.numpy as jnp
import numpy as np
from jax import lax
from jax.experimental import pallas as pl
from jax.experimental.pallas import tpu as pltpu

F32 = jnp.float32
BF16 = jnp.bfloat16

D_MODEL = 1024
BATCH = 16
SEQ = 256
DEPTH = 4
DEC_BATCH = 4
DEC_SEQ = 4096
PAST_LEN = 512
GRID_W = 64
EPS = 1e-6
A_WIDTH = 512
SSM_INNER = 1024
SSM_HEAD_DIM = 64
SSM_HEADS = 16
SSM_GROUPS = 2
SSM_STATE = 128
CHUNK = 128
MLA_HEADS = 8
Q_LORA = 256
KV_LORA = 256
NOPE_DIM = 64
ROPE_DIM = 32
V_DIM = 64
QK_DIM = NOPE_DIM + ROPE_DIM
ROPE_BASE = 10000.0
FF_DIM = 2816

N_CTX_TOK = BATCH * SEQ
N_LAT_TOK = DEC_BATCH * DEC_SEQ
N_TOK = N_CTX_TOK + N_LAT_TOK
N_MOD_ROWS = 8

LANE = 128
SUBLANE = 8
VMEM_LIMIT = 56 * 1024 * 1024

OFF_G = 0
OFF_Z = 3072
OFF_SX = 4096
OFF_AX = 5120
OFF_AB = 5632
OFF_AC = 6144
OFF_BC = 6656
NP_MAIN = 7168
OFF_CQ = 7168
OFF_CKV = 7424
OFF_DT = 7680
OFF_KR = 7808
OFF_KRS = 7936
NP = 8192
N_MAIN_TILES = 7

NEG_BIG = -1e30


def _cparams(sem):
    return pltpu.CompilerParams(dimension_semantics=sem, vmem_limit_bytes=VMEM_LIMIT)


def _rms(x, w):
    ms = jnp.mean(x * x, axis=-1, keepdims=True)
    return x * lax.rsqrt(ms + EPS) * w


def _sigmoid(x):
    return 0.5 * jnp.tanh(0.5 * x) + 0.5


def _silu(x):
    h = 0.5 * x
    return h * jnp.tanh(h) + h


def _layer_spec(w, layer, grid_rank):
    zeros = (0,) * (w.ndim - 1)
    imap = (lambda i: (layer,) + zeros) if grid_rank == 1 else (lambda i, j: (layer,) + zeros)
    return pl.BlockSpec((None,) + tuple(w.shape[1:]), imap)


def _mod_row(tile, tile_rows):
    n_ctx_tiles = N_CTX_TOK // tile_rows
    tiles_per_lat = DEC_SEQ // tile_rows
    return jnp.where(tile < n_ctx_tiles, 0, 1 + (tile - n_ctx_tiles) // tiles_per_lat)


MOD_TN = 1536


def _mod_kernel(c_ref, w_ref, b_ref, o_ref):
    c = c_ref[...]
    s = _silu(c).astype(BF16)
    o_ref[...] = jnp.dot(s, w_ref[...].astype(BF16), preferred_element_type=F32) + b_ref[...]


def _modulation(cond, w_ada, b_ada):
    n_col = 6 * D_MODEL
    return pl.pallas_call(
        _mod_kernel,
        grid=(DEPTH, n_col // MOD_TN),
        in_specs=[
            pl.BlockSpec((N_MOD_ROWS, D_MODEL), lambda l, j: (0, 0)),
            pl.BlockSpec((None, D_MODEL, MOD_TN), lambda l, j: (l, 0, j)),
            pl.BlockSpec((None, 1, MOD_TN), lambda l, j: (l, 0, j)),
        ],
        out_specs=pl.BlockSpec((None, N_MOD_ROWS, MOD_TN), lambda l, j: (l, 0, j)),
        out_shape=jax.ShapeDtypeStruct((DEPTH, N_MOD_ROWS, n_col), F32),
        compiler_params=_cparams(("arbitrary", "arbitrary")),
        name="modulation",
    )(cond, w_ada, b_ada.reshape(DEPTH, 1, n_col))


IN_TM = 2048
IN_TN = 1024
NORM_ROWS = 256


def _stream_specs(xs, tm, grid_rank):
    def imap(f):
        return (lambda i: f(i)) if grid_rank == 1 else (lambda i, j: f(i))

    width = xs[0].shape[1]
    if len(xs) == 1:
        return [pl.BlockSpec((tm, width), imap(lambda i: (i, 0)))]
    n_ctx_tiles = N_CTX_TOK // tm
    return [pl.BlockSpec((tm, width), imap(lambda i: (jnp.minimum(i, n_ctx_tiles - 1), 0))),
            pl.BlockSpec((tm, width), imap(lambda i: (jnp.maximum(i - n_ctx_tiles, 0), 0)))]


def _stream_rows(x_refs, tile, tm, rows):
    if len(x_refs) == 1:
        return x_refs[0][rows, :]
    return jnp.where(tile < N_CTX_TOK // tm, x_refs[0][rows, :], x_refs[1][rows, :])


def _modulated_norm_to(h_ref, x_rows, nw_ref, sc_ref, sh_ref, rows):
    for r in range(0, rows, NORM_ROWS):
        x = x_rows(slice(r, r + NORM_ROWS))
        h = _rms(x, nw_ref[...]) * (1.0 + sc_ref[...]) + sh_ref[...]
        h_ref[r:r + NORM_ROWS, :] = h.astype(BF16)


def _inproj_kernel(*refs, n_x, tm):
    x_refs = refs[:n_x]
    nw_ref, sh_ref, sc_ref, w_ref, om_ref, os_ref, h_ref = refs[n_x:]
    i = pl.program_id(0)
    j = pl.program_id(1)

    @pl.when(j == 0)
    def _():
        _modulated_norm_to(h_ref, lambda rows: _stream_rows(x_refs, i, tm, rows),
                           nw_ref, sc_ref, sh_ref, tm)

    @pl.when(j < N_MAIN_TILES)
    def _():
        om_ref[...] = jnp.dot(h_ref[...], w_ref[...], preferred_element_type=F32).astype(BF16)

    @pl.when(j == N_MAIN_TILES)
    def _():
        os_ref[...] = jnp.dot(h_ref[...], w_ref[...], preferred_element_type=F32)


def _in_projection(xs, mod_l, norm_w, w_in_p, layer):
    tm = IN_TM if len(xs) == 1 else IN_TM // 2
    row = functools.partial(_mod_row, tile_rows=tm)
    return pl.pallas_call(
        functools.partial(_inproj_kernel, n_x=len(xs), tm=tm),
        grid=(N_TOK // tm, NP // IN_TN),
        in_specs=_stream_specs(xs, tm, 2) + [
            pl.BlockSpec((1, D_MODEL), lambda i, j: (0, 0)),
            pl.BlockSpec((None, None, 1, D_MODEL), lambda i, j: (row(i), 0, 0, 0)),
            pl.BlockSpec((None, None, 1, D_MODEL), lambda i, j: (row(i), 1, 0, 0)),
            pl.BlockSpec((None, D_MODEL, IN_TN), lambda i, j: (layer, 0, j)),
        ],
        out_specs=[
            pl.BlockSpec((tm, IN_TN), lambda i, j: (i, jnp.minimum(j, N_MAIN_TILES - 1))),
            pl.BlockSpec((tm, IN_TN), lambda i, j: (i, 0)),
        ],
        out_shape=[
            jax.ShapeDtypeStruct((N_TOK, NP_MAIN), BF16),
            jax.ShapeDtypeStruct((N_TOK, IN_TN), F32),
        ],
        scratch_shapes=[pltpu.VMEM((tm, D_MODEL), BF16)],
        compiler_params=_cparams(("arbitrary", "arbitrary")),
        name="in_projection",
    )(*xs, norm_w, mod_l, mod_l, w_in_p)


def _conv3_tile(u, prev_row, next_row, w_ref, rows, inner_masks=None):
    ridx = lax.broadcasted_iota(jnp.int32, (SUBLANE, 1), 0)
    up = pltpu.roll(u, 1, axis=0)
    up = jnp.concatenate([jnp.where(ridx == 0, prev_row, up[0:SUBLANE]), up[SUBLANE:]], axis=0)
    dn = pltpu.roll(u, rows - 1, axis=0)
    dn = jnp.concatenate([dn[:rows - SUBLANE], jnp.where(ridx == SUBLANE - 1, next_row, dn[rows - SUBLANE:])],
                         axis=0)
    if inner_masks is not None:
        up = up * inner_masks[0]
        dn = dn * inner_masks[1]
    return up * w_ref[0:1, :] + u * w_ref[1:2, :] + dn * w_ref[2:3, :]


CONV_TQ = 1024
SSD_TQ_LAT = 512
HALO = 2 * SUBLANE
BC_WIDTH = 2 * SSM_GROUPS * SSM_STATE


def _tile_neighbours(i, tile_rows):
    n_ctx_tiles = N_CTX_TOK // tile_rows
    tiles_per_lat = DEC_SEQ // tile_rows
    t_in_seq = (i - n_ctx_tiles) % tiles_per_lat
    is_lat = i >= n_ctx_tiles
    has_prev = jnp.logical_and(is_lat, t_in_seq > 0).astype(F32)
    has_next = jnp.logical_and(is_lat, t_in_seq < tiles_per_lat - 1).astype(F32)
    return has_prev, has_next


def _context_edge_masks(tile, tile_rows):
    edge_keep = jnp.where(tile < N_CTX_TOK // tile_rows, 0.0, 1.0)
    pos = lax.broadcasted_iota(jnp.int32, (tile_rows, 1), 0) & (SEQ - 1)
    return jnp.where(pos == 0, edge_keep, 1.0), jnp.where(pos == SEQ - 1, edge_keep, 1.0)


def _ssd_conv_kernel(x_ref, xp_ref, xn_ref, bc_ref, bcp_ref, bcn_ref,
                     cwx_ref, cbx_ref, cwb_ref, cbb_ref, xc_ref, bcc_ref):
    i = pl.program_id(0)
    hp, hn = _tile_neighbours(i, CONV_TQ)
    masks = _context_edge_masks(i, CONV_TQ)

    def conv_silu(u_ref, up_ref, un_ref, w_ref, b_ref):
        prev_row = up_ref[HALO - 1:HALO, :].astype(F32) * hp
        next_row = un_ref[0:1, :].astype(F32) * hn
        conv = _conv3_tile(u_ref[...].astype(F32), prev_row, next_row, w_ref, CONV_TQ, masks)
        return _silu(conv + b_ref[...])

    xc_ref[...] = conv_silu(x_ref, xp_ref, xn_ref, cwx_ref, cbx_ref)
    bcc_ref[...] = conv_silu(bc_ref, bcp_ref, bcn_ref, cwb_ref, cbb_ref)


def _ssd_conv(p, conv_wx, conv_bx, conv_wbc, conv_bbc):
    halo_per_tile = CONV_TQ // HALO
    n_halo_blocks = N_TOK // HALO
    cx, cbc = OFF_SX // SSM_INNER, OFF_BC // BC_WIDTH

    def prev_map(col):
        return lambda i: (jnp.maximum(i * halo_per_tile - 1, 0), col)

    def next_map(col):
        return lambda i: (jnp.minimum((i + 1) * halo_per_tile, n_halo_blocks - 1), col)

    def const2(i):
        return (0, 0)

    return pl.pallas_call(
        _ssd_conv_kernel,
        grid=(N_TOK // CONV_TQ,),
        in_specs=[
            pl.BlockSpec((CONV_TQ, SSM_INNER), lambda i: (i, cx)),
            pl.BlockSpec((HALO, SSM_INNER), prev_map(cx)),
            pl.BlockSpec((HALO, SSM_INNER), next_map(cx)),
            pl.BlockSpec((CONV_TQ, BC_WIDTH), lambda i: (i, cbc)),
            pl.BlockSpec((HALO, BC_WIDTH), prev_map(cbc)),
            pl.BlockSpec((HALO, BC_WIDTH), next_map(cbc)),
            pl.BlockSpec((3, SSM_INNER), const2),
            pl.BlockSpec((1, SSM_INNER), const2),
            pl.BlockSpec((3, BC_WIDTH), const2),
            pl.BlockSpec((1, BC_WIDTH), const2),
        ],
        out_specs=[
            pl.BlockSpec((CONV_TQ, SSM_INNER), lambda i: (i, 0)),
            pl.BlockSpec((CONV_TQ, BC_WIDTH), lambda i: (i, 0)),
        ],
        out_shape=[
            jax.ShapeDtypeStruct((N_TOK, SSM_INNER), F32),
            jax.ShapeDtypeStruct((N_TOK, BC_WIDTH), F32),
        ],
        compiler_params=_cparams(("arbitrary",)),
        name="ssd_conv",
    )(p, p, p, p, p, p, conv_wx, conv_bx, conv_wbc, conv_bbc)


def _ssd_tables(row0, n_seq, seq_len, tq):
    blk_f, blk_b, seq, first = [], [], [], []
    nt = seq_len // tq
    for s in range(n_seq):
        base = (row0 + s * seq_len) // tq
        for k in range(nt):
            blk_f.append(base + k)
            blk_b.append(base + nt - 1 - k)
            seq.append(s)
            first.append(int(k == 0))
    return [np.asarray(a, np.int32) for a in (blk_f, blk_b, seq, first)]


def _split3(a):
    a1 = a.astype(BF16)
    r1 = a - a1.astype(F32)
    a2 = r1.astype(BF16)
    a3 = (r1 - a2.astype(F32)).astype(BF16)
    return a1, a2, a3


def _dot3(lhs_bf16, a):
    a1, a2, a3 = _split3(a)
    return (jnp.dot(lhs_bf16, a1, preferred_element_type=F32)
            + jnp.dot(lhs_bf16, a2, preferred_element_type=F32)
            + jnp.dot(lhs_bf16, a3, preferred_element_type=F32))


def _softplus(x):
    return jnp.maximum(x, 0.0) + jnp.log1p(jnp.exp(-jnp.abs(x)))


def _ssd_kernel(*refs, n_chunks, context, n_carried=0):
    blkf_t, blkb_t, seq_t, first_t = refs[:4]
    xf_ref, bcf_ref, dtf_ref, xb_ref, bcb_ref, dtb_ref, alog_ref, dtbias_ref, dsk_ref = refs[4:13]
    if context:
        yf_ref, yb_ref, finf_ref, finb_ref, stf_ref, stb_ref = refs[13 + n_carried:]
    else:
        h0f_ref, h0b_ref, yf_ref, yb_ref, stf_ref, stb_ref = refs[13:]
    s = pl.program_id(0)
    dirs = ((xf_ref, bcf_ref, dtf_ref, stf_ref, yf_ref), (xb_ref, bcb_ref, dtb_ref, stb_ref, yb_ref))

    if context:
        stf_ref[...] = jnp.zeros_like(stf_ref)
        stb_ref[...] = jnp.zeros_like(stb_ref)
    else:
        @pl.when(first_t[s] == 1)
        def _():
            stf_ref[...] = h0f_ref[...].T
            stb_ref[...] = h0b_ref[...].T

    ii = lax.broadcasted_iota(jnp.int32, (CHUNK, CHUNK), 0)
    jj = lax.broadcasted_iota(jnp.int32, (CHUNK, CHUNK), 1)
    masks = (jj <= ii, jj >= ii)
    masks_b = tuple(jnp.where(m, 1.0, 0.0).astype(BF16) for m in masks)
    masks_neg = tuple(jnp.where(m, 0.0, NEG_BIG) for m in masks)
    lo = jj < SSM_HEAD_DIM

    e_r = lax.broadcasted_iota(jnp.int32, (LANE, SSM_INNER), 0)
    e_c = lax.broadcasted_iota(jnp.int32, (LANE, SSM_INNER), 1)
    expand = jnp.where(jnp.right_shift(e_c, 6) == e_r, 1.0, 0.0).astype(BF16)

    def chunk_setup(d, c):
        _, _, dt_ref, _, _ = dirs[d]
        rows = slice(c * CHUNK, (c + 1) * CHUNK)
        a_row = -jnp.exp(alog_ref[d])
        dt = _softplus(dt_ref[rows, :] + dtbias_ref[d])
        cum = _dot3(masks_b[d], dt * a_row)
        cum_t = cum.T
        dt_t = dt.T
        end = CHUNK - 1 if d == 0 else 0
        tot_row = cum[end:end + 1, :]
        tot_col = cum_t[:, end:end + 1]
        return dict(
            rows=rows, cum=cum, cum_t=cum_t, dt_t=dt_t,
            w_rows=jnp.exp(tot_col - cum_t) * dt_t,
            sdec=_dot3_rows(jnp.exp(tot_row), expand))

    def group_setup(d, cs, g):
        _, bc_ref, _, _, _ = dirs[d]
        b_g = bc_ref[cs["rows"], g * SSM_STATE:(g + 1) * SSM_STATE]
        c_g = bc_ref[cs["rows"], (SSM_GROUPS + g) * SSM_STATE:(SSM_GROUPS + g + 1) * SSM_STATE].astype(BF16)
        b_gt = b_g.T
        cb = jnp.dot(c_g, b_gt.astype(BF16), preferred_element_type=F32)
        return c_g, b_gt, cb

    def pair_step(d, cs, gs, kp):
        x_ref, _, _, st_ref, y_ref = dirs[d]
        c_g, b_gt, cb = gs
        cum, cum_t, dt_t = cs["cum"], cs["cum_t"], cs["dt_t"]
        h_a, h_b = 2 * kp, 2 * kp + 1
        ls = slice(kp * LANE, (kp + 1) * LANE)
        x_pair = x_ref[cs["rows"], ls]
        rhs = jnp.concatenate([jnp.where(lo, x_pair, 0.0).astype(BF16),
                               jnp.where(lo, 0.0, x_pair).astype(BF16)], axis=0)

        def head_lhs(h):
            col = jnp.broadcast_to(cum[:, h:h + 1], (CHUNK, CHUNK))
            seg = col - cum_t[h:h + 1, :]
            dec = jnp.exp(seg + masks_neg[d])
            w_intra = dec * cb * dt_t[h:h + 1, :]
            w_state = b_gt * cs["w_rows"][h:h + 1, :]
            return w_intra.astype(BF16), w_state.astype(BF16), col

        wi_a, ws_a, col_a = head_lhs(h_a)
        wi_b, ws_b, col_b = head_lhs(h_b)
        lhs = jnp.concatenate([jnp.concatenate([wi_a, wi_b], axis=1),
                               jnp.concatenate([ws_a, ws_b], axis=1)], axis=0)
        both = jnp.dot(lhs, rhs, preferred_element_type=F32)
        y_diag = both[0:CHUNK, :]
        d_state = both[CHUNK:2 * CHUNK, :]

        h_pair = st_ref[:, ls]
        y_off = jnp.dot(c_g, h_pair.astype(BF16), preferred_element_type=F32)
        e_pair = jnp.exp(jnp.where(lo, col_a, col_b))
        y_ref[cs["rows"], ls] = (y_diag + y_off * e_pair + x_pair * dsk_ref[d, :, ls]).astype(y_ref.dtype)
        st_ref[:, ls] = h_pair * cs["sdec"][:, ls] + d_state

    pairs_per_group = SSM_HEADS // SSM_GROUPS // 2
    setups = [(chunk_setup(0, k), chunk_setup(1, n_chunks - 1 - k)) for k in range(n_chunks)]
    for k in range(n_chunks):
        cs = setups[k]
        for g in range(SSM_GROUPS):
            gs = (group_setup(0, cs[0], g), group_setup(1, cs[1], g))
            for kp in range(g * pairs_per_group, (g + 1) * pairs_per_group):
                pair_step(0, cs[0], gs[0], kp)
                pair_step(1, cs[1], gs[1], kp)

    if context:
        finf_ref[...] = stf_ref[...].T
        finb_ref[...] = stb_ref[...].T


def _dot3_rows(row, rhs_bf16):
    r8 = jnp.broadcast_to(row, (SUBLANE, row.shape[1]))
    r1, r2, r3 = _split3(r8)
    out = (jnp.dot(r1, rhs_bf16, preferred_element_type=F32)
           + jnp.dot(r2, rhs_bf16, preferred_element_type=F32)
           + jnp.dot(r3, rhs_bf16, preferred_element_type=F32))
    return out[0:1, :]


def _ssd(ps, xc, bcc, alog, dtb, dskip, *, row0, n_seq, seq_len, tq, h0=None, fin_layer=None, fin_prev=None):
    context = h0 is None
    assert not context or seq_len == tq
    tables = [jnp.asarray(t) for t in _ssd_tables(row0, n_seq, seq_len, tq)]
    n_steps = int(tables[0].shape[0])
    n_rows = n_seq * seq_len
    blk0 = row0 // tq
    cdt = (OFF_DT - NP_MAIN) // LANE

    def fwd_tile(col, off=0):
        return lambda s, blkf, *_: (blkf[s] - off, col)

    def bwd_tile(col, off=0):
        return lambda s, blkf, blkb, *_: (blkb[s] - off, col)

    def tile_specs(tile):
        return [
            pl.BlockSpec((tq, SSM_INNER), tile(0)),
            pl.BlockSpec((tq, BC_WIDTH), tile(0)),
            pl.BlockSpec((tq, LANE), tile(cdt)),
        ]

    def const3(s, *_):
        return (0, 0, 0)

    in_specs = tile_specs(fwd_tile) + tile_specs(bwd_tile) + [
        pl.BlockSpec((2, 1, LANE), const3),
        pl.BlockSpec((2, 1, LANE), const3),
        pl.BlockSpec((2, 1, SSM_INNER), const3),
    ]
    out_specs = [pl.BlockSpec((tq, SSM_INNER), fwd_tile(0, blk0)), pl.BlockSpec((tq, SSM_INNER), bwd_tile(0, blk0))]
    out_shape = [jax.ShapeDtypeStruct((n_rows, SSM_INNER), BF16), jax.ShapeDtypeStruct((n_rows, SSM_INNER), BF16)]
    operands = [xc, bcc, ps, xc, bcc, ps, alog, dtb, dskip]
    aliases = {}
    if context:
        def fin_map(s, blkf, blkb, seq, *_):
            return (seq[s], fin_layer, 0, 0)

        out_specs += [pl.BlockSpec((None, None, SSM_INNER, SSM_STATE), fin_map)] * 2
        out_shape += [jax.ShapeDtypeStruct((n_seq, DEPTH, SSM_INNER, SSM_STATE), F32)] * 2
        if fin_prev is not None:
            n_before = len(tables) + len(operands)
            in_specs += [pl.BlockSpec(memory_space=pl.ANY)] * 2
            operands += list(fin_prev)
            aliases = {n_before: 2, n_before + 1: 3}
    else:
        h0f, h0b, layer = h0

        def h0_map(s, blkf, blkb, seq, *_):
            return (seq[s], layer, 0, 0)

        in_specs += [pl.BlockSpec((None, None, SSM_INNER, SSM_STATE), h0_map)] * 2
        operands += [h0f, h0b]

    grid_spec = pltpu.PrefetchScalarGridSpec(
        num_scalar_prefetch=4,
        grid=(n_steps,),
        in_specs=in_specs,
        out_specs=out_specs,
        scratch_shapes=[
            pltpu.VMEM((SSM_STATE, SSM_INNER), F32),
            pltpu.VMEM((SSM_STATE, SSM_INNER), F32),
        ],
    )
    return pl.pallas_call(
        functools.partial(_ssd_kernel, n_chunks=tq // CHUNK, context=context, n_carried=len(aliases)),
        grid_spec=grid_spec,
        out_shape=out_shape,
        input_output_aliases=aliases,
        compiler_params=_cparams(("arbitrary",)),
        name="ssd_scan",
    )(*tables, *operands)


QK_PAD = 128
N_QROWS = MLA_HEADS * QK_PAD
N_VROWS = MLA_HEADS * V_DIM
NT_DIMS = (((1,), (1,)), ((), ()))
Q_PRESCALE = (1.0 / math.sqrt(QK_DIM)) * math.log2(math.e)


def _mla_prep_kernel(*refs, tm, normalize, with_q, keep_ckv):
    if with_q:
        (cq_ref, ckv_ref, kr_ref, krs_ref, cos_ref, sin_ref, cost_ref, sint_ref,
         qnw_ref, kvnw_ref, wq_ref, wk_ref, wv_ref, qt_ref, k_ref, vt_ref) = refs[:16]
    else:
        (ckv_ref, kr_ref, krs_ref, cos_ref, sin_ref, kvnw_ref, wk_ref, wv_ref,
         k_ref, vt_ref) = refs

    ckv = ckv_ref[...]
    if normalize:
        ckv = _rms(ckv, kvnw_ref[...])
    if keep_ckv:
        refs[16][...] = ckv
    ckv_b = ckv.astype(BF16)
    kn = jnp.dot(ckv_b, wk_ref[...], preferred_element_type=F32)
    kr = kr_ref[...] * cos_ref[...] + krs_ref[...] * sin_ref[...]
    for h in range(MLA_HEADS):
        hs = slice(h * QK_PAD, (h + 1) * QK_PAD)
        k_ref[:, hs] = (kn[:, hs] + kr).astype(BF16)
    vt_ref[...] = lax.dot_general(wv_ref[...], ckv_b, NT_DIMS, preferred_element_type=F32).astype(BF16)

    if with_q:
        cqn = _rms(cq_ref[...], qnw_ref[...]).astype(BF16)
        qq = lax.dot_general(wq_ref[...], cqn, NT_DIMS, preferred_element_type=F32)
        for h in range(MLA_HEADS):
            q_h = qq[h * QK_PAD:(h + 1) * QK_PAD, :]
            qs_h = qq[N_QROWS + h * QK_PAD:N_QROWS + (h + 1) * QK_PAD, :]
            q_rot = q_h * cost_ref[...] + qs_h * sint_ref[...]
            qt_ref[h * QK_PAD:(h + 1) * QK_PAD, :] = (q_rot * Q_PRESCALE).astype(BF16)


def _mla_prep_tokens(p, row0, n_seq, seq_len, tm, cos, sin, cos_t, sin_t, qnw, kvnw, wq_t, wk, wv_t, layer,
                     keep_ckv):
    b0 = row0 // tm
    tps = seq_len // tm
    nt = n_seq * tps
    n_rows = n_seq * seq_len
    kernel = functools.partial(_mla_prep_kernel, tm=tm, normalize=True, with_q=True, keep_ckv=keep_ckv)

    def tile(t, b):
        return b * tps + t

    def pcol(width, off):
        return pl.BlockSpec((tm, width), lambda t, b: (b0 + tile(t, b), (off - NP_MAIN) // width))

    def full(a):
        return pl.BlockSpec(a.shape, lambda t, b: (0,) * a.ndim)

    return pl.pallas_call(
        kernel,
        grid=(tps, n_seq),
        in_specs=[
            pcol(Q_LORA, OFF_CQ), pcol(KV_LORA, OFF_CKV), pcol(LANE, OFF_KR), pcol(LANE, OFF_KRS),
            pl.BlockSpec((tm, LANE), lambda t, b: (t, 0)),
            pl.BlockSpec((tm, LANE), lambda t, b: (t, 0)),
            pl.BlockSpec((LANE, tm), lambda t, b: (0, t)),
            pl.BlockSpec((LANE, tm), lambda t, b: (0, t)),
            full(qnw), full(kvnw), _layer_spec(wq_t, layer, 2), _layer_spec(wk, layer, 2), _layer_spec(wv_t, layer, 2),
        ],
        out_specs=[
            pl.BlockSpec((N_QROWS, tm), lambda t, b: (0, tile(t, b))),
            pl.BlockSpec((None, tm, N_QROWS), lambda t, b: (tile(t, b), 0, 0)),
            pl.BlockSpec((None, N_VROWS, tm), lambda t, b: (tile(t, b), 0, 0)),
        ] + ([pl.BlockSpec((tm, KV_LORA), lambda t, b: (tile(t, b), 0))] if keep_ckv else []),
        out_shape=[
            jax.ShapeDtypeStruct((N_QROWS, n_rows), BF16),
            jax.ShapeDtypeStruct((nt, tm, N_QROWS), BF16),
            jax.ShapeDtypeStruct((nt, N_VROWS, tm), BF16),
        ] + ([jax.ShapeDtypeStruct((n_rows, KV_LORA), F32)] if keep_ckv else []),
        compiler_params=_cparams(("arbitrary", "arbitrary")),
        name="mla_prep",
    )(p, p, p, p, cos, sin, cos_t, sin_t, qnw, kvnw, wq_t, wk, wv_t)


def _mla_prep_cache(ckv, kr_pad, ones_tab, zeros_tab, kvnw, wk, wv_t, tm, layer):
    n_rows = ckv.shape[0]
    nt = n_rows // tm
    kernel = functools.partial(_mla_prep_kernel, tm=tm, normalize=False, with_q=False, keep_ckv=False)

    def full(a):
        return pl.BlockSpec(a.shape, lambda i: (0,) * a.ndim)

    return pl.pallas_call(
        kernel,
        grid=(nt,),
        in_specs=[
            pl.BlockSpec((tm, KV_LORA), lambda i: (i, 0)),
            pl.BlockSpec((tm, LANE), lambda i: (i, 0)),
            pl.BlockSpec((tm, LANE), lambda i: (i, 0)),
            pl.BlockSpec((tm, LANE), lambda i: (0, 0)),
            pl.BlockSpec((tm, LANE), lambda i: (0, 0)),
            full(kvnw), _layer_spec(wk, layer, 1), _layer_spec(wv_t, layer, 1),
        ],
        out_specs=[
            pl.BlockSpec((None, tm, N_QROWS), lambda i: (i, 0, 0)),
            pl.BlockSpec((None, N_VROWS, tm), lambda i: (i, 0, 0)),
        ],
        out_shape=[
            jax.ShapeDtypeStruct((nt, tm, N_QROWS), BF16),
            jax.ShapeDtypeStruct((nt, N_VROWS, tm), BF16),
        ],
        compiler_params=_cparams(("arbitrary",)),
        name="mla_prep_cache",
    )(ckv, kr_pad, kr_pad, ones_tab, zeros_tab, kvnw, wk, wv_t)


ATT_TQ = 512
ATT_SUB = 256


def _attn_kernel(*refs, n_kt, tq, tk, with_cache):
    if with_cache:
        qt_ref, kp_ref, vtp_ref, k_ref, vt_ref, o_ref, ot_ref, s_ref = refs
    else:
        qt_ref, k_ref, vt_ref, o_ref, ot_ref, s_ref = refs

    def k_tile(kt, h):
        cols = slice(h * QK_PAD, (h + 1) * QK_PAD)
        if with_cache:
            return kp_ref[:, cols] if kt == 0 else k_ref[kt - 1, :, cols]
        return k_ref[kt, :, cols]

    def vt_tile(kt, h):
        rows = slice(h * V_DIM, (h + 1) * V_DIM)
        if with_cache:
            return vtp_ref[rows, :] if kt == 0 else vt_ref[kt - 1, rows, :]
        return vt_ref[kt, rows, :]

    sub = min(ATT_SUB, tk)

    def scores_step(h, kt, m8):
        q_t = qt_ref[h * QK_PAD:(h + 1) * QK_PAD, :]
        k = k_tile(kt, h)
        for r in range(0, tk, sub):
            s = jnp.dot(k[r:r + sub, :], q_t, preferred_element_type=F32)
            s_ref[h % 2, kt, r:r + sub, :] = s
            m8 = jnp.maximum(m8, jnp.max(s.reshape(sub // SUBLANE, SUBLANE, tq), axis=0))
        return m8

    def probs_step(h, kt, m, l8, acc):
        v_t = vt_tile(kt, h)
        for r in range(0, tk, sub):
            pr = jnp.exp2(s_ref[h % 2, kt, r:r + sub, :] - m)
            l8 = l8 + jnp.sum(pr.reshape(sub // SUBLANE, SUBLANE, tq), axis=0)
            acc = acc + jnp.dot(v_t[:, r:r + sub], pr.astype(BF16), preferred_element_type=F32)
        return l8, acc

    m8_init = jnp.full((SUBLANE, tq), NEG_BIG, F32)
    m8 = m8_init
    for kt in range(n_kt):
        m8 = scores_step(0, kt, m8)
    for h in range(MLA_HEADS):
        m = jnp.max(m8, axis=0, keepdims=True)
        l8 = jnp.zeros((SUBLANE, tq), F32)
        acc = jnp.zeros((V_DIM, tq), F32)
        m8 = m8_init
        for kt in range(n_kt):
            l8, acc = probs_step(h, kt, m, l8, acc)
            if h + 1 < MLA_HEADS:
                m8 = scores_step(h + 1, kt, m8)
        ot_ref[h * V_DIM:(h + 1) * V_DIM, :] = acc / jnp.sum(l8, axis=0, keepdims=True)
    o_ref[...] = ot_ref[...].T.astype(o_ref.dtype)


def _attention(qt, k3, vt3, n_batch, lq, tk, cache=None):
    tq = min(ATT_TQ, lq)
    nq = lq // tq
    n_new = k3.shape[0] // n_batch
    k4 = k3.reshape(n_batch, n_new, tk, N_QROWS)
    v4 = vt3.reshape(n_batch, n_new, N_VROWS, tk)
    n_kt = n_new + (0 if cache is None else 1)
    kernel = functools.partial(_attn_kernel, n_kt=n_kt, tq=tq, tk=tk, with_cache=cache is not None)
    cache_specs = [] if cache is None else [
        pl.BlockSpec((None, tk, N_QROWS), lambda b, i: (b, 0, 0)),
        pl.BlockSpec((None, N_VROWS, tk), lambda b, i: (b, 0, 0)),
    ]
    return pl.pallas_call(
        kernel,
        grid=(n_batch, nq),
        in_specs=[pl.BlockSpec((N_QROWS, tq), lambda b, i: (0, b * nq + i))] + cache_specs + [
            pl.BlockSpec((None, n_new, tk, N_QROWS), lambda b, i: (b, 0, 0, 0)),
            pl.BlockSpec((None, n_new, N_VROWS, tk), lambda b, i: (b, 0, 0, 0)),
        ],
        out_specs=pl.BlockSpec((tq, N_VROWS), lambda b, i: (b * nq + i, 0)),
        out_shape=jax.ShapeDtypeStruct((n_batch * lq, N_VROWS), BF16),
        scratch_shapes=[pltpu.VMEM((N_VROWS, tq), F32), pltpu.VMEM((2, n_kt, tk, tq), F32)],
        compiler_params=_cparams(("arbitrary", "arbitrary")),
        name="attention",
    )(qt, *(() if cache is None else cache), k4, v4)


MG_TM = 512


def _merge_kernel(*refs, n_x):
    x_refs = refs[:n_x]
    zc_refs, yf_refs, yb_refs = refs[n_x:n_x + 2], refs[n_x + 2:n_x + 4], refs[n_x + 4:n_x + 6]
    (ax_ref, axp_ref, axn_ref, ac_ref, acp_ref, acn_ref, ab_ref,
     g_ref, z_ref, gm_ref,
     cw_ref, wa_ref, nw_ref, wb_ref, wc_ref, wo_ref, o_ref) = refs[n_x + 6:]
    i = pl.program_id(0)
    hp, hn = _tile_neighbours(i, MG_TM)
    all_rows = slice(0, MG_TM)

    def f32(v):
        return v.astype(F32)

    u = f32(ac_ref[...]) * f32(ax_ref[...])
    u_prev = f32(acp_ref[HALO - 1:HALO, :]) * f32(axp_ref[HALO - 1:HALO, :]) * hp
    u_next = f32(acn_ref[0:1, :]) * f32(axn_ref[0:1, :]) * hn
    za = f32(ab_ref[...]) * _conv3_tile(u, u_prev, u_next, cw_ref, MG_TM, _context_edge_masks(i, MG_TM))
    y_a = jnp.dot(za.astype(BF16), wa_ref[...], preferred_element_type=F32)

    y_scan = f32(_stream_rows(yf_refs, i, MG_TM, all_rows)) + f32(_stream_rows(yb_refs, i, MG_TM, all_rows))
    yb = y_scan * _silu(f32(z_ref[...]))
    zb = _rms(yb, nw_ref[...])
    y_b = jnp.dot(zb.astype(BF16), wb_ref[...], preferred_element_type=F32)

    y_c = jnp.dot(_stream_rows(zc_refs, i, MG_TM, all_rows), wc_ref[...], preferred_element_type=F32)

    merged = (_sigmoid(f32(g_ref[:, 0:D_MODEL])) * y_a
              + _sigmoid(f32(g_ref[:, D_MODEL:2 * D_MODEL])) * y_b
              + _sigmoid(f32(g_ref[:, 2 * D_MODEL:3 * D_MODEL])) * y_c)
    o = jnp.dot(merged.astype(BF16), wo_ref[...], preferred_element_type=F32)
    o_ref[...] = _stream_rows(x_refs, i, MG_TM, all_rows) + gm_ref[...] * o


def _merge(p, yfs, ybs, zcs, xs, mod_l, conv_w, wa, nw, wb, wc, wo, layer):
    halo_per_tile = MG_TM // HALO
    n_halo_blocks = N_TOK // HALO
    row = functools.partial(_mod_row, tile_rows=MG_TM)

    def pcol(width, off):
        return pl.BlockSpec((MG_TM, width), lambda i: (i, off // width))

    def pprev(off):
        return pl.BlockSpec((HALO, A_WIDTH), lambda i: (jnp.maximum(i * halo_per_tile - 1, 0), off // A_WIDTH))

    def pnext(off):
        return pl.BlockSpec((HALO, A_WIDTH),
                            lambda i: (jnp.minimum((i + 1) * halo_per_tile, n_halo_blocks - 1), off // A_WIDTH))

    def full(a):
        return pl.BlockSpec(a.shape, lambda i: (0,) * a.ndim)

    return pl.pallas_call(
        functools.partial(_merge_kernel, n_x=len(xs)),
        grid=(N_TOK // MG_TM,),
        in_specs=(_stream_specs(xs, MG_TM, 1) + _stream_specs(zcs, MG_TM, 1)
                  + _stream_specs(yfs, MG_TM, 1) + _stream_specs(ybs, MG_TM, 1)) + [
            pcol(A_WIDTH, OFF_AX), pprev(OFF_AX), pnext(OFF_AX),
            pcol(A_WIDTH, OFF_AC), pprev(OFF_AC), pnext(OFF_AC),
            pcol(A_WIDTH, OFF_AB),
            pcol(3 * D_MODEL, OFF_G),
            pcol(SSM_INNER, OFF_Z),
            pl.BlockSpec((None, None, 1, D_MODEL), lambda i: (row(i), 2, 0, 0)),
            full(conv_w), _layer_spec(wa, layer, 1), full(nw), _layer_spec(wb, layer, 1),
            _layer_spec(wc, layer, 1), _layer_spec(wo, layer, 1),
        ],
        out_specs=pl.BlockSpec((MG_TM, D_MODEL), lambda i: (i, 0)),
        out_shape=jax.ShapeDtypeStruct((N_TOK, D_MODEL), F32),
        compiler_params=_cparams(("arbitrary",)),
        name="merge",
    )(*xs, *zcs, *yfs, *ybs, p, p, p, p, p, p, p, p, p, mod_l, conv_w, wa, nw, wb, wc, wo)


FF_TM = 512
FF_CHUNK = 256


def _ffn_kernel(*refs, final):
    x_ref, nw_ref, sh_ref, sc_ref, gm_ref, w1_ref, w3_ref, w2_ref = refs[:8]
    if final:
        fw_ref, oc_ref, ol_ref, h_ref, g_ref = refs[8:]
    else:
        o_ref, h_ref, g_ref = refs[8:]
    _modulated_norm_to(h_ref, lambda rows: x_ref[rows, :], nw_ref, sc_ref, sh_ref, FF_TM)
    h = h_ref[...]
    for c in range(0, FF_DIM, FF_CHUNK):
        a = jnp.dot(h, w1_ref[:, c:c + FF_CHUNK], preferred_element_type=F32)
        b = jnp.dot(h, w3_ref[:, c:c + FF_CHUNK], preferred_element_type=F32)
        g_ref[:, c:c + FF_CHUNK] = (_silu(a) * b).astype(BF16)
    ff = jnp.dot(g_ref[...], w2_ref[...], preferred_element_type=F32)
    x_new = x_ref[...] + gm_ref[...] * ff
    if final:
        y = _rms(x_new, fw_ref[...])
        ol_ref[...] = y

        @pl.when(pl.program_id(0) < N_CTX_TOK // FF_TM)
        def _():
            oc_ref[...] = y
    else:
        o_ref[...] = x_new


def _ffn(x, mod_l, norm_w, w1, w3, w2, layer, final_w=None):
    row = functools.partial(_mod_row, tile_rows=FF_TM)
    final = final_w is not None
    n_ctx_tiles = N_CTX_TOK // FF_TM
    if final:
        extra_in = [pl.BlockSpec((1, D_MODEL), lambda i: (0, 0))]
        out_specs = [pl.BlockSpec((FF_TM, D_MODEL), lambda i: (jnp.minimum(i, n_ctx_tiles - 1), 0)),
                     pl.BlockSpec((FF_TM, D_MODEL), lambda i: (jnp.maximum(i - n_ctx_tiles, 0), 0))]
        out_shape = [jax.ShapeDtypeStruct((N_CTX_TOK, D_MODEL), F32), jax.ShapeDtypeStruct((N_LAT_TOK, D_MODEL), F32)]
    else:
        extra_in = []
        out_specs = pl.BlockSpec((FF_TM, D_MODEL), lambda i: (i, 0))
        out_shape = jax.ShapeDtypeStruct((N_TOK, D_MODEL), F32)
    return pl.pallas_call(
        functools.partial(_ffn_kernel, final=final),
        grid=(N_TOK // FF_TM,),
        in_specs=[
            pl.BlockSpec((FF_TM, D_MODEL), lambda i: (i, 0)),
            pl.BlockSpec((1, D_MODEL), lambda i: (0, 0)),
            pl.BlockSpec((None, None, 1, D_MODEL), lambda i: (row(i), 3, 0, 0)),
            pl.BlockSpec((None, None, 1, D_MODEL), lambda i: (row(i), 4, 0, 0)),
            pl.BlockSpec((None, None, 1, D_MODEL), lambda i: (row(i), 5, 0, 0)),
            _layer_spec(w1, layer, 1), _layer_spec(w3, layer, 1), _layer_spec(w2, layer, 1),
        ] + extra_in,
        out_specs=out_specs,
        out_shape=out_shape,
        scratch_shapes=[pltpu.VMEM((FF_TM, D_MODEL), BF16), pltpu.VMEM((FF_TM, FF_DIM), BF16)],
        compiler_params=_cparams(("arbitrary",)),
        name="ffn",
    )(x, norm_w, mod_l, mod_l, mod_l, w1, w3, w2, *(() if final_w is None else (final_w,)))


def _pad_in_weights(w_in):
    src_ax, src_z, src_bc, src_dt, src_cq, src_kr, src_g = 0, 1536, 3584, 4096, 4112, 4624, 4656
    half = ROPE_DIM // 2
    pieces = [
        (OFF_G, src_g, 3 * D_MODEL),
        (OFF_Z, src_z, 2 * SSM_INNER),
        (OFF_AX, src_ax, 3 * A_WIDTH),
        (OFF_BC, src_bc, BC_WIDTH),
        (OFF_CQ, src_cq, Q_LORA + KV_LORA),
    ]
    w16 = w_in.astype(BF16)

    def z(n):
        return jnp.zeros(w_in.shape[:-1] + (n,), BF16)

    dt, kr = w16[..., src_dt:src_dt + SSM_HEADS], w16[..., src_kr:src_kr + ROPE_DIM]
    narrow = jnp.concatenate([dt, z(LANE - SSM_HEADS),
                              z(64), kr, z(LANE - 64 - ROPE_DIM),
                              z(64), kr[..., half:], kr[..., :half], z(LANE - 64 - ROPE_DIM)], axis=-1)
    out = jnp.zeros(w_in.shape[:-1] + (NP,), BF16)
    for dst, src, width in pieces:
        out = out.at[..., dst:dst + width].set(w16[..., src:src + width])
    return out.at[..., OFF_DT:OFF_DT + 3 * LANE].set(narrow)


def _q_weights_t(w_uq):
    w = w_uq.reshape(DEPTH, Q_LORA, MLA_HEADS, QK_DIM)
    nope, x1, x2 = w[..., :NOPE_DIM], w[..., NOPE_DIM:NOPE_DIM + 16], w[..., NOPE_DIM + 16:]
    z32 = jnp.zeros_like(w[..., :32])
    z64 = jnp.zeros_like(nope)
    q = jnp.concatenate([nope, x1, x2, z32], axis=-1).reshape(DEPTH, Q_LORA, N_QROWS)
    qs = jnp.concatenate([z64, x2, x1, z32], axis=-1).reshape(DEPTH, Q_LORA, N_QROWS)
    return jnp.swapaxes(jnp.concatenate([q, qs], axis=-1), 1, 2).astype(BF16)


def _kv_weights(w_ukv):
    w = w_ukv.reshape(DEPTH, KV_LORA, MLA_HEADS, NOPE_DIM + V_DIM)
    kn = jnp.concatenate([w[..., :NOPE_DIM], jnp.zeros_like(w[..., :QK_PAD - NOPE_DIM])], axis=-1)
    wk = kn.reshape(DEPTH, KV_LORA, N_QROWS).astype(BF16)
    wv_t = jnp.swapaxes(w[..., NOPE_DIM:].reshape(DEPTH, KV_LORA, N_VROWS), 1, 2).astype(BF16)
    return wk, wv_t


def _rope_tables(n_tokens, lead_rows):
    n_rows = n_tokens // GRID_W
    row = jnp.repeat(jnp.arange(n_rows, dtype=F32), GRID_W)
    col = jnp.tile(jnp.arange(GRID_W, dtype=F32), n_rows)
    pairs = ROPE_DIM // 4
    inv = ROPE_BASE ** (-jnp.arange(pairs, dtype=F32) / pairs)
    ang = jnp.concatenate([row[:, None] * inv, col[:, None] * inv], axis=-1)
    cos, sin = jnp.cos(ang), jnp.sin(ang)
    ones = jnp.ones((n_tokens, NOPE_DIM), F32)
    z32 = jnp.zeros((n_tokens, 32), F32)
    cos_l = jnp.concatenate([ones, cos, cos, z32], axis=-1)
    sin_l = jnp.concatenate([jnp.zeros_like(ones), -sin, sin, z32], axis=-1)
    ident_c = jnp.concatenate([jnp.ones((lead_rows, NOPE_DIM + ROPE_DIM), F32), jnp.zeros((lead_rows, 32), F32)], -1)
    ident_s = jnp.zeros((lead_rows, LANE), F32)
    return jnp.concatenate([ident_c, cos_l], axis=0), jnp.concatenate([ident_s, sin_l], axis=0)


PREP_TM_CTX = SEQ
PREP_TM_LAT = 512


def kernel(x_prompt, x_sample, c, cache_ckv, cache_krope, state_ssm_fwd, state_ssm_bwd, c_ctx, w_in, a_conv_w, w_a_out, ssm_conv_w, ssm_conv_b, ssm_a_log, ssm_dt_bias, ssm_d, ssm_norm_w, w_b_out, q_norm_w, w_uq, kv_norm_w, w_ukv, w_c_out, w_o, w_ada, b_ada, norm1_w, norm2_w, w_ff1, w_ff3, w_ff2, final_norm_w):
    w_in_p = _pad_in_weights(w_in)
    wq_t = _q_weights_t(w_uq)
    wk, wv_t = _kv_weights(w_ukv)
    wa, wb, wc, wo = (w.astype(BF16) for w in (w_a_out, w_b_out, w_c_out, w_o))
    w1, w3, w2 = (w.astype(BF16) for w in (w_ff1, w_ff3, w_ff2))

    cond = jnp.concatenate([c_ctx[None, :], c, jnp.zeros((N_MOD_ROWS - 1 - DEC_BATCH, D_MODEL), F32)], axis=0)
    mod = _modulation(cond, w_ada, b_ada).reshape(DEPTH, N_MOD_ROWS, 6, 1, D_MODEL)

    conv_wx = ssm_conv_w[..., :SSM_INNER]
    conv_wbc = ssm_conv_w[..., SSM_INNER:]
    conv_bx = ssm_conv_b[:, None, :SSM_INNER]
    conv_bbc = ssm_conv_b[:, None, SSM_INNER:]
    pad_h = ((0, 0), (0, 0), (0, 0), (0, LANE - SSM_HEADS))
    alog = jnp.pad(ssm_a_log[:, :, None, :], pad_h)
    dtb = jnp.pad(ssm_dt_bias[:, :, None, :], pad_h)
    dskip = jnp.repeat(ssm_d, SSM_HEAD_DIM, axis=-1)[:, :, None, :]
    h0f = state_ssm_fwd.reshape(DEC_BATCH, DEPTH, SSM_INNER, SSM_STATE)
    h0b = state_ssm_bwd.reshape(DEC_BATCH, DEPTH, SSM_INNER, SSM_STATE)

    cos_c, sin_c = _rope_tables(DEC_SEQ, PREP_TM_CTX)
    cos_l, sin_l = cos_c[PREP_TM_CTX:], sin_c[PREP_TM_CTX:]
    ident_cos, ident_sin = cos_c[:PREP_TM_CTX], sin_c[:PREP_TM_CTX]
    ones_tab = jnp.concatenate([ident_cos, ident_cos], axis=0)
    zeros_tab = jnp.zeros_like(ones_tab)
    cache_kr_pad = jnp.pad(cache_krope, ((0, 0), (0, 0), (0, 0), (64, 32)))

    xs = (x_prompt.reshape(N_CTX_TOK, D_MODEL), x_sample.reshape(N_LAT_TOK, D_MODEL))

    new_ckv, new_kr = [], []
    fin = None
    for l in range(DEPTH):
        p, ps = _in_projection(xs, mod[l], norm1_w[l][None, :], w_in_p, l)

        xc, bcc = _ssd_conv(p, conv_wx[l], conv_bx[l], conv_wbc[l], conv_bbc[l])
        yc_f, yc_b, *fin = _ssd(ps, xc, bcc, alog[l], dtb[l], dskip[l],
                                row0=0, n_seq=BATCH, seq_len=SEQ, tq=SEQ, fin_layer=l, fin_prev=fin)
        yl_f, yl_b = _ssd(ps, xc, bcc, alog[l], dtb[l], dskip[l],
                          row0=N_CTX_TOK, n_seq=DEC_BATCH, seq_len=DEC_SEQ, tq=SSD_TQ_LAT, h0=(h0f, h0b, l))

        qnw, kvnw = q_norm_w[l][None, :], kv_norm_w[l][None, :]
        qt_c, k_c, vt_c, ckv_c = _mla_prep_tokens(
            ps, 0, BATCH, SEQ, PREP_TM_CTX, ident_cos, ident_sin, ident_cos.T, ident_sin.T,
            qnw, kvnw, wq_t, wk, wv_t, l, keep_ckv=True)
        qt_l, k_l, vt_l = _mla_prep_tokens(
            ps, N_CTX_TOK, DEC_BATCH, DEC_SEQ, PREP_TM_LAT, cos_l, sin_l, cos_l.T, sin_l.T,
            qnw, kvnw, wq_t, wk, wv_t, l, keep_ckv=False)
        k_p, vt_p = _mla_prep_cache(cache_ckv[:, l].reshape(DEC_BATCH * PAST_LEN, KV_LORA),
                                    cache_kr_pad[:, l].reshape(DEC_BATCH * PAST_LEN, LANE),
                                    ones_tab, zeros_tab, kvnw, wk, wv_t, PAST_LEN, l)
        new_ckv.append(ckv_c.reshape(BATCH, SEQ, KV_LORA))
        kr0 = OFF_KR - NP_MAIN + 64
        new_kr.append(ps[:N_CTX_TOK, kr0:kr0 + ROPE_DIM].reshape(BATCH, SEQ, ROPE_DIM))

        zc_c = _attention(qt_c, k_c, vt_c, BATCH, SEQ, SEQ)
        zc_l = _attention(qt_l, k_l, vt_l, DEC_BATCH, DEC_SEQ, PREP_TM_LAT, cache=(k_p, vt_p))

        x = _merge(p, (yc_f, yl_f), (yc_b, yl_b), (zc_c, zc_l), xs, mod[l], a_conv_w[l], wa, ssm_norm_w[l][None, :],
                   wb, wc, wo, l)
        if l + 1 < DEPTH:
            xs = (_ffn(x, mod[l], norm2_w[l][None, :], w1, w3, w2, l),)
        else:
            y_ctx, y_lat = _ffn(x, mod[l], norm2_w[l][None, :], w1, w3, w2, l, final_w=final_norm_w[None, :])

    y_prompt = y_ctx.reshape(BATCH, SEQ, D_MODEL)
    y_sample = y_lat.reshape(DEC_BATCH, DEC_SEQ, D_MODEL)
    hshape = (BATCH, DEPTH, SSM_HEADS, SSM_HEAD_DIM, SSM_STATE)
    return (y_prompt, y_sample,
            jnp.stack(new_ckv, axis=1), jnp.stack(new_kr, axis=1),
            fin[0].reshape(hshape), fin[1].reshape(hshape))
```

```python
import functools
import math

import jax
import jax.numpy as jnp
import numpy as np
from jax import lax
from jax.experimental import pallas as pl
from jax.experimental.pallas import tpu as pltpu

F32 = jnp.float32
BF16 = jnp.bfloat16

D_MODEL = 1024
BATCH = 16
SEQ = 256
DEPTH = 4
DEC_BATCH = 4
DEC_SEQ = 4096
PAST_LEN = 512
GRID_W = 64
EPS = 1e-6
A_WIDTH = 512
SSM_INNER = 1024
SSM_HEAD_DIM = 64
SSM_HEADS = 16
SSM_GROUPS = 2
SSM_STATE = 128
CHUNK = 128
MLA_HEADS = 8
Q_LORA = 256
KV_LORA = 256
NOPE_DIM = 64
ROPE_DIM = 32
V_DIM = 64
QK_DIM = NOPE_DIM + ROPE_DIM
ROPE_BASE = 10000.0
FF_DIM = 2816

N_CTX_TOK = BATCH * SEQ
N_LAT_TOK = DEC_BATCH * DEC_SEQ
N_TOK = N_CTX_TOK + N_LAT_TOK
N_MOD_ROWS = 8

LANE = 128
SUBLANE = 8
VMEM_LIMIT = 56 * 1024 * 1024

OFF_G = 0
OFF_Z = 3072
OFF_SX = 4096
OFF_AX = 5120
OFF_AB = 5632
OFF_AC = 6144
OFF_BC = 6656
NP_MAIN = 7168
OFF_CQ = 7168
OFF_CKV = 7424
OFF_DT = 7680
OFF_KR = 7808
OFF_KRS = 7936
NP = 8192
N_MAIN_TILES = 7

NEG_BIG = -1e30


def _cparams(sem):
    return pltpu.CompilerParams(dimension_semantics=sem, vmem_limit_bytes=VMEM_LIMIT)


def _rms(x, w):
    ms = jnp.mean(x * x, axis=-1, keepdims=True)
    return x * lax.rsqrt(ms + EPS) * w


def _sigmoid(x):
    return 0.5 * jnp.tanh(0.5 * x) + 0.5


def _silu(x):
    h = 0.5 * x
    return h * jnp.tanh(h) + h


def _layer_spec(w, layer, grid_rank):
    zeros = (0,) * (w.ndim - 1)
    imap = (lambda i: (layer,) + zeros) if grid_rank == 1 else (lambda i, j: (layer,) + zeros)
    return pl.BlockSpec((None,) + tuple(w.shape[1:]), imap)


def _mod_row(tile, tile_rows):
    n_ctx_tiles = N_CTX_TOK // tile_rows
    tiles_per_lat = DEC_SEQ // tile_rows
    return jnp.where(tile < n_ctx_tiles, 0, 1 + (tile - n_ctx_tiles) // tiles_per_lat)


MOD_TN = 1536


def _mod_kernel(c_ref, w_ref, b_ref, o_ref):
    c = c_ref[...]
    s = _silu(c).astype(BF16)
    o_ref[...] = jnp.dot(s, w_ref[...].astype(BF16), preferred_element_type=F32) + b_ref[...]


def _modulation(cond, w_ada, b_ada):
    n_col = 6 * D_MODEL
    return pl.pallas_call(
        _mod_kernel,
        grid=(DEPTH, n_col // MOD_TN),
        in_specs=[
            pl.BlockSpec((N_MOD_ROWS, D_MODEL), lambda l, j: (0, 0)),
            pl.BlockSpec((None, D_MODEL, MOD_TN), lambda l, j: (l, 0, j)),
            pl.BlockSpec((None, 1, MOD_TN), lambda l, j: (l, 0, j)),
        ],
        out_specs=pl.BlockSpec((None, N_MOD_ROWS, MOD_TN), lambda l, j: (l, 0, j)),
        out_shape=jax.ShapeDtypeStruct((DEPTH, N_MOD_ROWS, n_col), F32),
        compiler_params=_cparams(("arbitrary", "arbitrary")),
        name="modulation",
    )(cond, w_ada, b_ada.reshape(DEPTH, 1, n_col))


IN_TM = 2048
IN_TN = 1024
NORM_ROWS = 256


def _stream_specs(xs, tm, grid_rank):
    def imap(f):
        return (lambda i: f(i)) if grid_rank == 1 else (lambda i, j: f(i))

    width = xs[0].shape[1]
    if len(xs) == 1:
        return [pl.BlockSpec((tm, width), imap(lambda i: (i, 0)))]
    n_ctx_tiles = N_CTX_TOK // tm
    return [pl.BlockSpec((tm, width), imap(lambda i: (jnp.minimum(i, n_ctx_tiles - 1), 0))),
            pl.BlockSpec((tm, width), imap(lambda i: (jnp.maximum(i - n_ctx_tiles, 0), 0)))]


def _stream_rows(x_refs, tile, tm, rows):
    if len(x_refs) == 1:
        return x_refs[0][rows, :]
    return jnp.where(tile < N_CTX_TOK // tm, x_refs[0][rows, :], x_refs[1][rows, :])


def _modulated_norm_to(h_ref, x_rows, nw_ref, sc_ref, sh_ref, rows):
    for r in range(0, rows, NORM_ROWS):
        x = x_rows(slice(r, r + NORM_ROWS))
        h = _rms(x, nw_ref[...]) * (1.0 + sc_ref[...]) + sh_ref[...]
        h_ref[r:r + NORM_ROWS, :] = h.astype(BF16)


def _inproj_kernel(*refs, n_x, tm):
    x_refs = refs[:n_x]
    nw_ref, sh_ref, sc_ref, w_ref, om_ref, os_ref, h_ref = refs[n_x:]
    i = pl.program_id(0)
    j = pl.program_id(1)

    @pl.when(j == 0)
    def _():
        _modulated_norm_to(h_ref, lambda rows: _stream_rows(x_refs, i, tm, rows),
                           nw_ref, sc_ref, sh_ref, tm)

    @pl.when(j < N_MAIN_TILES)
    def _():
        om_ref[...] = jnp.dot(h_ref[...], w_ref[...], preferred_element_type=F32).astype(BF16)

    @pl.when(j == N_MAIN_TILES)
    def _():
        os_ref[...] = jnp.dot(h_ref[...], w_ref[...], preferred_element_type=F32)


def _in_projection(xs, mod_l, norm_w, w_in_p, layer):
    tm = IN_TM if len(xs) == 1 else IN_TM // 2
    row = functools.partial(_mod_row, tile_rows=tm)
    return pl.pallas_call(
        functools.partial(_inproj_kernel, n_x=len(xs), tm=tm),
        grid=(N_TOK // tm, NP // IN_TN),
        in_specs=_stream_specs(xs, tm, 2) + [
            pl.BlockSpec((1, D_MODEL), lambda i, j: (0, 0)),
            pl.BlockSpec((None, None, 1, D_MODEL), lambda i, j: (row(i), 0, 0, 0)),
            pl.BlockSpec((None, None, 1, D_MODEL), lambda i, j: (row(i), 1, 0, 0)),
            pl.BlockSpec((None, D_MODEL, IN_TN), lambda i, j: (layer, 0, j)),
        ],
        out_specs=[
            pl.BlockSpec((tm, IN_TN), lambda i, j: (i, jnp.minimum(j, N_MAIN_TILES - 1))),
            pl.BlockSpec((tm, IN_TN), lambda i, j: (i, 0)),
        ],
        out_shape=[
            jax.ShapeDtypeStruct((N_TOK, NP_MAIN), BF16),
            jax.ShapeDtypeStruct((N_TOK, IN_TN), F32),
        ],
        scratch_shapes=[pltpu.VMEM((tm, D_MODEL), BF16)],
        compiler_params=_cparams(("arbitrary", "arbitrary")),
        name="in_projection",
    )(*xs, norm_w, mod_l, mod_l, w_in_p)


def _conv3_tile(u, prev_row, next_row, w_ref, rows, inner_masks=None):
    ridx = lax.broadcasted_iota(jnp.int32, (SUBLANE, 1), 0)
    up = pltpu.roll(u, 1, axis=0)
    up = jnp.concatenate([jnp.where(ridx == 0, prev_row, up[0:SUBLANE]), up[SUBLANE:]], axis=0)
    dn = pltpu.roll(u, rows - 1, axis=0)
    dn = jnp.concatenate([dn[:rows - SUBLANE], jnp.where(ridx == SUBLANE - 1, next_row, dn[rows - SUBLANE:])],
                         axis=0)
    if inner_masks is not None:
        up = up * inner_masks[0]
        dn = dn * inner_masks[1]
    return up * w_ref[0:1, :] + u * w_ref[1:2, :] + dn * w_ref[2:3, :]


CONV_TQ = 1024
SSD_TQ_LAT = 512
HALO = 2 * SUBLANE
BC_WIDTH = 2 * SSM_GROUPS * SSM_STATE


def _tile_neighbours(i, tile_rows):
    n_ctx_tiles = N_CTX_TOK // tile_rows
    tiles_per_lat = DEC_SEQ // tile_rows
    t_in_seq = (i - n_ctx_tiles) % tiles_per_lat
    is_lat = i >= n_ctx_tiles
    has_prev = jnp.logical_and(is_lat, t_in_seq > 0).astype(F32)
    has_next = jnp.logical_and(is_lat, t_in_seq < tiles_per_lat - 1).astype(F32)
    return has_prev, has_next


def _context_edge_masks(tile, tile_rows):
    edge_keep = jnp.where(tile < N_CTX_TOK // tile_rows, 0.0, 1.0)
    pos = lax.broadcasted_iota(jnp.int32, (tile_rows, 1), 0) & (SEQ - 1)
    return jnp.where(pos == 0, edge_keep, 1.0), jnp.where(pos == SEQ - 1, edge_keep, 1.0)


def _ssd_conv_kernel(x_ref, xp_ref, xn_ref, bc_ref, bcp_ref, bcn_ref,
                     cwx_ref, cbx_ref, cwb_ref, cbb_ref, xc_ref, bcc_ref):
    i = pl.program_id(0)
    hp, hn = _tile_neighbours(i, CONV_TQ)
    masks = _context_edge_masks(i, CONV_TQ)

    def conv_silu(u_ref, up_ref, un_ref, w_ref, b_ref):
        prev_row = up_ref[HALO - 1:HALO, :].astype(F32) * hp
        next_row = un_ref[0:1, :].astype(F32) * hn
        conv = _conv3_tile(u_ref[...].astype(F32), prev_row, next_row, w_ref, CONV_TQ, masks)
        return _silu(conv + b_ref[...])

    xc_ref[...] = conv_silu(x_ref, xp_ref, xn_ref, cwx_ref, cbx_ref)
    bcc_ref[...] = conv_silu(bc_ref, bcp_ref, bcn_ref, cwb_ref, cbb_ref)


def _ssd_conv(p, conv_wx, conv_bx, conv_wbc, conv_bbc):
    halo_per_tile = CONV_TQ // HALO
    n_halo_blocks = N_TOK // HALO
    cx, cbc = OFF_SX // SSM_INNER, OFF_BC // BC_WIDTH

    def prev_map(col):
        return lambda i: (jnp.maximum(i * halo_per_tile - 1, 0), col)

    def next_map(col):
        return lambda i: (jnp.minimum((i + 1) * halo_per_tile, n_halo_blocks - 1), col)

    def const2(i):
        return (0, 0)

    return pl.pallas_call(
        _ssd_conv_kernel,
        grid=(N_TOK // CONV_TQ,),
        in_specs=[
            pl.BlockSpec((CONV_TQ, SSM_INNER), lambda i: (i, cx)),
            pl.BlockSpec((HALO, SSM_INNER), prev_map(cx)),
            pl.BlockSpec((HALO, SSM_INNER), next_map(cx)),
            pl.BlockSpec((CONV_TQ, BC_WIDTH), lambda i: (i, cbc)),
            pl.BlockSpec((HALO, BC_WIDTH), prev_map(cbc)),
            pl.BlockSpec((HALO, BC_WIDTH), next_map(cbc)),
            pl.BlockSpec((3, SSM_INNER), const2),
            pl.BlockSpec((1, SSM_INNER), const2),
            pl.BlockSpec((3, BC_WIDTH), const2),
            pl.BlockSpec((1, BC_WIDTH), const2),
        ],
        out_specs=[
            pl.BlockSpec((CONV_TQ, SSM_INNER), lambda i: (i, 0)),
            pl.BlockSpec((CONV_TQ, BC_WIDTH), lambda i: (i, 0)),
        ],
        out_shape=[
            jax.ShapeDtypeStruct((N_TOK, SSM_INNER), F32),
            jax.ShapeDtypeStruct((N_TOK, BC_WIDTH), F32),
        ],
        compiler_params=_cparams(("arbitrary",)),
        name="ssd_conv",
    )(p, p, p, p, p, p, conv_wx, conv_bx, conv_wbc, conv_bbc)


def _ssd_tables(row0, n_seq, seq_len, tq):
    blk_f, blk_b, seq, first = [], [], [], []
    nt = seq_len // tq
    for s in range(n_seq):
        base = (row0 + s * seq_len) // tq
        for k in range(nt):
            blk_f.append(base + k)
            blk_b.append(base + nt - 1 - k)
            seq.append(s)
            first.append(int(k == 0))
    return [np.asarray(a, np.int32) for a in (blk_f, blk_b, seq, first)]


def _split3(a):
    a1 = a.astype(BF16)
    r1 = a - a1.astype(F32)
    a2 = r1.astype(BF16)
    a3 = (r1 - a2.astype(F32)).astype(BF16)
    return a1, a2, a3


def _dot3(lhs_bf16, a):
    a1, a2, a3 = _split3(a)
    return (jnp.dot(lhs_bf16, a1, preferred_element_type=F32)
            + jnp.dot(lhs_bf16, a2, preferred_element_type=F32)
            + jnp.dot(lhs_bf16, a3, preferred_element_type=F32))


def _softplus(x):
    return jnp.maximum(x, 0.0) + jnp.log1p(jnp.exp(-jnp.abs(x)))


def _ssd_kernel(*refs, n_chunks, context, n_carried=0):
    blkf_t, blkb_t, seq_t, first_t = refs[:4]
    xf_ref, bcf_ref, dtf_ref, xb_ref, bcb_ref, dtb_ref, alog_ref, dtbias_ref, dsk_ref = refs[4:13]
    if context:
        yf_ref, yb_ref, finf_ref, finb_ref, stf_ref, stb_ref = refs[13 + n_carried:]
    else:
        h0f_ref, h0b_ref, yf_ref, yb_ref, stf_ref, stb_ref = refs[13:]
    s = pl.program_id(0)
    dirs = ((xf_ref, bcf_ref, dtf_ref, stf_ref, yf_ref), (xb_ref, bcb_ref, dtb_ref, stb_ref, yb_ref))

    if context:
        stf_ref[...] = jnp.zeros_like(stf_ref)
        stb_ref[...] = jnp.zeros_like(stb_ref)
    else:
        @pl.when(first_t[s] == 1)
        def _():
            stf_ref[...] = h0f_ref[...].T
            stb_ref[...] = h0b_ref[...].T

    ii = lax.broadcasted_iota(jnp.int32, (CHUNK, CHUNK), 0)
    jj = lax.broadcasted_iota(jnp.int32, (CHUNK, CHUNK), 1)
    masks = (jj <= ii, jj >= ii)
    masks_b = tuple(jnp.where(m, 1.0, 0.0).astype(BF16) for m in masks)
    masks_neg = tuple(jnp.where(m, 0.0, NEG_BIG) for m in masks)
    lo = jj < SSM_HEAD_DIM

    e_r = lax.broadcasted_iota(jnp.int32, (LANE, SSM_INNER), 0)
    e_c = lax.broadcasted_iota(jnp.int32, (LANE, SSM_INNER), 1)
    expand = jnp.where(jnp.right_shift(e_c, 6) == e_r, 1.0, 0.0).astype(BF16)

    def chunk_setup(d, c):
        _, _, dt_ref, _, _ = dirs[d]
        rows = slice(c * CHUNK, (c + 1) * CHUNK)
        a_row = -jnp.exp(alog_ref[d])
        dt = _softplus(dt_ref[rows, :] + dtbias_ref[d])
        cum = _dot3(masks_b[d], dt * a_row)
        cum_t = cum.T
        dt_t = dt.T
        end = CHUNK - 1 if d == 0 else 0
        tot_row = cum[end:end + 1, :]
        tot_col = cum_t[:, end:end + 1]
        return dict(
            rows=rows, cum=cum, cum_t=cum_t, dt_t=dt_t,
            w_rows=jnp.exp(tot_col - cum_t) * dt_t,
            sdec=_dot3_rows(jnp.exp(tot_row), expand))

    def group_setup(d, cs, g):
        _, bc_ref, _, _, _ = dirs[d]
        b_g = bc_ref[cs["rows"], g * SSM_STATE:(g + 1) * SSM_STATE]
        c_g = bc_ref[cs["rows"], (SSM_GROUPS + g) * SSM_STATE:(SSM_GROUPS + g + 1) * SSM_STATE].astype(BF16)
        b_gt = b_g.T
        cb = jnp.dot(c_g, b_gt.astype(BF16), preferred_element_type=F32)
        return c_g, b_gt, cb

    def pair_step(d, cs, gs, kp):
        x_ref, _, _, st_ref, y_ref = dirs[d]
        c_g, b_gt, cb = gs
        cum, cum_t, dt_t = cs["cum"], cs["cum_t"], cs["dt_t"]
        h_a, h_b = 2 * kp, 2 * kp + 1
        ls = slice(kp * LANE, (kp + 1) * LANE)
        x_pair = x_ref[cs["rows"], ls]
        rhs = jnp.concatenate([jnp.where(lo, x_pair, 0.0).astype(BF16),
                               jnp.where(lo, 0.0, x_pair).astype(BF16)], axis=0)

        def head_lhs(h):
            col = jnp.broadcast_to(cum[:, h:h + 1], (CHUNK, CHUNK))
            seg = col - cum_t[h:h + 1, :]
            dec = jnp.exp(seg + masks_neg[d])
            w_intra = dec * cb * dt_t[h:h + 1, :]
            w_state = b_gt * cs["w_rows"][h:h + 1, :]
            return w_intra.astype(BF16), w_state.astype(BF16), col

        wi_a, ws_a, col_a = head_lhs(h_a)
        wi_b, ws_b, col_b = head_lhs(h_b)
        lhs = jnp.concatenate([jnp.concatenate([wi_a, wi_b], axis=1),
                               jnp.concatenate([ws_a, ws_b], axis=1)], axis=0)
        both = jnp.dot(lhs, rhs, preferred_element_type=F32)
        y_diag = both[0:CHUNK, :]
        d_state = both[CHUNK:2 * CHUNK, :]

        h_pair = st_ref[:, ls]
        y_off = jnp.dot(c_g, h_pair.astype(BF16), preferred_element_type=F32)
        e_pair = jnp.exp(jnp.where(lo, col_a, col_b))
        y_ref[cs["rows"], ls] = (y_diag + y_off * e_pair + x_pair * dsk_ref[d, :, ls]).astype(y_ref.dtype)
        st_ref[:, ls] = h_pair * cs["sdec"][:, ls] + d_state

    pairs_per_group = SSM_HEADS // SSM_GROUPS // 2
    setups = [(chunk_setup(0, k), chunk_setup(1, n_chunks - 1 - k)) for k in range(n_chunks)]
    for k in range(n_chunks):
        cs = setups[k]
        for g in range(SSM_GROUPS):
            gs = (group_setup(0, cs[0], g), group_setup(1, cs[1], g))
            for kp in range(g * pairs_per_group, (g + 1) * pairs_per_group):
                pair_step(0, cs[0], gs[0], kp)
                pair_step(1, cs[1], gs[1], kp)

    if context:
        finf_ref[...] = stf_ref[...].T
        finb_ref[...] = stb_ref[...].T


def _dot3_rows(row, rhs_bf16):
    r8 = jnp.broadcast_to(row, (SUBLANE, row.shape[1]))
    r1, r2, r3 = _split3(r8)
    out = (jnp.dot(r1, rhs_bf16, preferred_element_type=F32)
           + jnp.dot(r2, rhs_bf16, preferred_element_type=F32)
           + jnp.dot(r3, rhs_bf16, preferred_element_type=F32))
    return out[0:1, :]


def _ssd(ps, xc, bcc, alog, dtb, dskip, *, row0, n_seq, seq_len, tq, h0=None, fin_layer=None, fin_prev=None):
    context = h0 is None
    assert not context or seq_len == tq
    tables = [jnp.asarray(t) for t in _ssd_tables(row0, n_seq, seq_len, tq)]
    n_steps = int(tables[0].shape[0])
    n_rows = n_seq * seq_len
    blk0 = row0 // tq
    cdt = (OFF_DT - NP_MAIN) // LANE

    def fwd_tile(col, off=0):
        return lambda s, blkf, *_: (blkf[s] - off, col)

    def bwd_tile(col, off=0):
        return lambda s, blkf, blkb, *_: (blkb[s] - off, col)

    def tile_specs(tile):
        return [
            pl.BlockSpec((tq, SSM_INNER), tile(0)),
            pl.BlockSpec((tq, BC_WIDTH), tile(0)),
            pl.BlockSpec((tq, LANE), tile(cdt)),
        ]

    def const3(s, *_):
        return (0, 0, 0)

    in_specs = tile_specs(fwd_tile) + tile_specs(bwd_tile) + [
        pl.BlockSpec((2, 1, LANE), const3),
        pl.BlockSpec((2, 1, LANE), const3),
        pl.BlockSpec((2, 1, SSM_INNER), const3),
    ]
    out_specs = [pl.BlockSpec((tq, SSM_INNER), fwd_tile(0, blk0)), pl.BlockSpec((tq, SSM_INNER), bwd_tile(0, blk0))]
    out_shape = [jax.ShapeDtypeStruct((n_rows, SSM_INNER), BF16), jax.ShapeDtypeStruct((n_rows, SSM_INNER), BF16)]
    operands = [xc, bcc, ps, xc, bcc, ps, alog, dtb, dskip]
    aliases = {}
    if context:
        def fin_map(s, blkf, blkb, seq, *_):
            return (seq[s], fin_layer, 0, 0)

        out_specs += [pl.BlockSpec((None, None, SSM_INNER, SSM_STATE), fin_map)] * 2
        out_shape += [jax.ShapeDtypeStruct((n_seq, DEPTH, SSM_INNER, SSM_STATE), F32)] * 2
        if fin_prev is not None:
            n_before = len(tables) + len(operands)
            in_specs += [pl.BlockSpec(memory_space=pl.ANY)] * 2
            operands += list(fin_prev)
            aliases = {n_before: 2, n_before + 1: 3}
    else:
        h0f, h0b, layer = h0

        def h0_map(s, blkf, blkb, seq, *_):
            return (seq[s], layer, 0, 0)

        in_specs += [pl.BlockSpec((None, None, SSM_INNER, SSM_STATE), h0_map)] * 2
        operands += [h0f, h0b]

    grid_spec = pltpu.PrefetchScalarGridSpec(
        num_scalar_prefetch=4,
        grid=(n_steps,),
        in_specs=in_specs,
        out_specs=out_specs,
        scratch_shapes=[
            pltpu.VMEM((SSM_STATE, SSM_INNER), F32),
            pltpu.VMEM((SSM_STATE, SSM_INNER), F32),
        ],
    )
    return pl.pallas_call(
        functools.partial(_ssd_kernel, n_chunks=tq // CHUNK, context=context, n_carried=len(aliases)),
        grid_spec=grid_spec,
        out_shape=out_shape,
        input_output_aliases=aliases,
        compiler_params=_cparams(("arbitrary",)),
        name="ssd_scan",
    )(*tables, *operands)


QK_PAD = 128
N_QROWS = MLA_HEADS * QK_PAD
N_VROWS = MLA_HEADS * V_DIM
NT_DIMS = (((1,), (1,)), ((), ()))
Q_PRESCALE = (1.0 / math.sqrt(QK_DIM)) * math.log2(math.e)


def _mla_prep_kernel(*refs, tm, normalize, with_q, keep_ckv):
    if with_q:
        (cq_ref, ckv_ref, kr_ref, krs_ref, cos_ref, sin_ref, cost_ref, sint_ref,
         qnw_ref, kvnw_ref, wq_ref, wk_ref, wv_ref, qt_ref, k_ref, vt_ref) = refs[:16]
    else:
        (ckv_ref, kr_ref, krs_ref, cos_ref, sin_ref, kvnw_ref, wk_ref, wv_ref,
         k_ref, vt_ref) = refs

    ckv = ckv_ref[...]
    if normalize:
        ckv = _rms(ckv, kvnw_ref[...])
    if keep_ckv:
        refs[16][...] = ckv
    ckv_b = ckv.astype(BF16)
    kn = jnp.dot(ckv_b, wk_ref[...], preferred_element_type=F32)
    kr = kr_ref[...] * cos_ref[...] + krs_ref[...] * sin_ref[...]
    for h in range(MLA_HEADS):
        hs = slice(h * QK_PAD, (h + 1) * QK_PAD)
        k_ref[:, hs] = (kn[:, hs] + kr).astype(BF16)
    vt_ref[...] = lax.dot_general(wv_ref[...], ckv_b, NT_DIMS, preferred_element_type=F32).astype(BF16)

    if with_q:
        cqn = _rms(cq_ref[...], qnw_ref[...]).astype(BF16)
        qq = lax.dot_general(wq_ref[...], cqn, NT_DIMS, preferred_element_type=F32)
        for h in range(MLA_HEADS):
            q_h = qq[h * QK_PAD:(h + 1) * QK_PAD, :]
            qs_h = qq[N_QROWS + h * QK_PAD:N_QROWS + (h + 1) * QK_PAD, :]
            q_rot = q_h * cost_ref[...] + qs_h * sint_ref[...]
            qt_ref[h * QK_PAD:(h + 1) * QK_PAD, :] = (q_rot * Q_PRESCALE).astype(BF16)


def _mla_prep_tokens(p, row0, n_seq, seq_len, tm, cos, sin, cos_t, sin_t, qnw, kvnw, wq_t, wk, wv_t, layer,
                     keep_ckv):
    b0 = row0 // tm
    tps = seq_len // tm
    nt = n_seq * tps
    n_rows = n_seq * seq_len
    kernel = functools.partial(_mla_prep_kernel, tm=tm, normalize=True, with_q=True, keep_ckv=keep_ckv)

    def tile(t, b):
        return b * tps + t

    def pcol(width, off):
        return pl.BlockSpec((tm, width), lambda t, b: (b0 + tile(t, b), (off - NP_MAIN) // width))

    def full(a):
        return pl.BlockSpec(a.shape, lambda t, b: (0,) * a.ndim)

    return pl.pallas_call(
        kernel,
        grid=(tps, n_seq),
        in_specs=[
            pcol(Q_LORA, OFF_CQ), pcol(KV_LORA, OFF_CKV), pcol(LANE, OFF_KR), pcol(LANE, OFF_KRS),
            pl.BlockSpec((tm, LANE), lambda t, b: (t, 0)),
            pl.BlockSpec((tm, LANE), lambda t, b: (t, 0)),
            pl.BlockSpec((LANE, tm), lambda t, b: (0, t)),
            pl.BlockSpec((LANE, tm), lambda t, b: (0, t)),
            full(qnw), full(kvnw), _layer_spec(wq_t, layer, 2), _layer_spec(wk, layer, 2), _layer_spec(wv_t, layer, 2),
        ],
        out_specs=[
            pl.BlockSpec((N_QROWS, tm), lambda t, b: (0, tile(t, b))),
            pl.BlockSpec((None, tm, N_QROWS), lambda t, b: (tile(t, b), 0, 0)),
            pl.BlockSpec((None, N_VROWS, tm), lambda t, b: (tile(t, b), 0, 0)),
        ] + ([pl.BlockSpec((tm, KV_LORA), lambda t, b: (tile(t, b), 0))] if keep_ckv else []),
        out_shape=[
            jax.ShapeDtypeStruct((N_QROWS, n_rows), BF16),
            jax.ShapeDtypeStruct((nt, tm, N_QROWS), BF16),
            jax.ShapeDtypeStruct((nt, N_VROWS, tm), BF16),
        ] + ([jax.ShapeDtypeStruct((n_rows, KV_LORA), F32)] if keep_ckv else []),
        compiler_params=_cparams(("arbitrary", "arbitrary")),
        name="mla_prep",
    )(p, p, p, p, cos, sin, cos_t, sin_t, qnw, kvnw, wq_t, wk, wv_t)


def _mla_prep_cache(ckv, kr_pad, ones_tab, zeros_tab, kvnw, wk, wv_t, tm, layer):
    n_rows = ckv.shape[0]
    nt = n_rows // tm
    kernel = functools.partial(_mla_prep_kernel, tm=tm, normalize=False, with_q=False, keep_ckv=False)

    def full(a):
        return pl.BlockSpec(a.shape, lambda i: (0,) * a.ndim)

    return pl.pallas_call(
        kernel,
        grid=(nt,),
        in_specs=[
            pl.BlockSpec((tm, KV_LORA), lambda i: (i, 0)),
            pl.BlockSpec((tm, LANE), lambda i: (i, 0)),
            pl.BlockSpec((tm, LANE), lambda i: (i, 0)),
            pl.BlockSpec((tm, LANE), lambda i: (0, 0)),
            pl.BlockSpec((tm, LANE), lambda i: (0, 0)),
            full(kvnw), _layer_spec(wk, layer, 1), _layer_spec(wv_t, layer, 1),
        ],
        out_specs=[
            pl.BlockSpec((None, tm, N_QROWS), lambda i: (i, 0, 0)),
            pl.BlockSpec((None, N_VROWS, tm), lambda i: (i, 0, 0)),
        ],
        out_shape=[
            jax.ShapeDtypeStruct((nt, tm, N_QROWS), BF16),
            jax.ShapeDtypeStruct((nt, N_VROWS, tm), BF16),
        ],
        compiler_params=_cparams(("arbitrary",)),
        name="mla_prep_cache",
    )(ckv, kr_pad, kr_pad, ones_tab, zeros_tab, kvnw, wk, wv_t)


ATT_TQ = 512
ATT_SUB = 256


def _attn_kernel(*refs, n_kt, tq, tk, with_cache):
    if with_cache:
        qt_ref, kp_ref, vtp_ref, k_ref, vt_ref, o_ref, ot_ref, s_ref = refs
    else:
        qt_ref, k_ref, vt_ref, o_ref, ot_ref, s_ref = refs

    def k_tile(kt, h):
        cols = slice(h * QK_PAD, (h + 1) * QK_PAD)
        if with_cache:
            return kp_ref[:, cols] if kt == 0 else k_ref[kt - 1, :, cols]
        return k_ref[kt, :, cols]

    def vt_tile(kt, h):
        rows = slice(h * V_DIM, (h + 1) * V_DIM)
        if with_cache:
            return vtp_ref[rows, :] if kt == 0 else vt_ref[kt - 1, rows, :]
        return vt_ref[kt, rows, :]

    sub = min(ATT_SUB, tk)

    def scores_step(h, kt, m8):
        q_t = qt_ref[h * QK_PAD:(h + 1) * QK_PAD, :]
        k = k_tile(kt, h)
        for r in range(0, tk, sub):
            s = jnp.dot(k[r:r + sub, :], q_t, preferred_element_type=F32)
            s_ref[h % 2, kt, r:r + sub, :] = s
            m8 = jnp.maximum(m8, jnp.max(s.reshape(sub // SUBLANE, SUBLANE, tq), axis=0))
        return m8

    def probs_step(h, kt, m, l8, acc):
        v_t = vt_tile(kt, h)
        for r in range(0, tk, sub):
            pr = jnp.exp2(s_ref[h % 2, kt, r:r + sub, :] - m)
            l8 = l8 + jnp.sum(pr.reshape(sub // SUBLANE, SUBLANE, tq), axis=0)
            acc = acc + jnp.dot(v_t[:, r:r + sub], pr.astype(BF16), preferred_element_type=F32)
        return l8, acc

    m8_init = jnp.full((SUBLANE, tq), NEG_BIG, F32)
    m8 = m8_init
    for kt in range(n_kt):
        m8 = scores_step(0, kt, m8)
    for h in range(MLA_HEADS):
        m = jnp.max(m8, axis=0, keepdims=True)
        l8 = jnp.zeros((SUBLANE, tq), F32)
        acc = jnp.zeros((V_DIM, tq), F32)
        m8 = m8_init
        for kt in range(n_kt):
            l8, acc = probs_step(h, kt, m, l8, acc)
            if h + 1 < MLA_HEADS:
                m8 = scores_step(h + 1, kt, m8)
        ot_ref[h * V_DIM:(h + 1) * V_DIM, :] = acc / jnp.sum(l8, axis=0, keepdims=True)
    o_ref[...] = ot_ref[...].T.astype(o_ref.dtype)


def _attention(qt, k3, vt3, n_batch, lq, tk, cache=None):
    tq = min(ATT_TQ, lq)
    nq = lq // tq
    n_new = k3.shape[0] // n_batch
    k4 = k3.reshape(n_batch, n_new, tk, N_QROWS)
    v4 = vt3.reshape(n_batch, n_new, N_VROWS, tk)
    n_kt = n_new + (0 if cache is None else 1)
    kernel = functools.partial(_attn_kernel, n_kt=n_kt, tq=tq, tk=tk, with_cache=cache is not None)
    cache_specs = [] if cache is None else [
        pl.BlockSpec((None, tk, N_QROWS), lambda b, i: (b, 0, 0)),
        pl.BlockSpec((None, N_VROWS, tk), lambda b, i: (b, 0, 0)),
    ]
    return pl.pallas_call(
        kernel,
        grid=(n_batch, nq),
        in_specs=[pl.BlockSpec((N_QROWS, tq), lambda b, i: (0, b * nq + i))] + cache_specs + [
            pl.BlockSpec((None, n_new, tk, N_QROWS), lambda b, i: (b, 0, 0, 0)),
            pl.BlockSpec((None, n_new, N_VROWS, tk), lambda b, i: (b, 0, 0, 0)),
        ],
        out_specs=pl.BlockSpec((tq, N_VROWS), lambda b, i: (b * nq + i, 0)),
        out_shape=jax.ShapeDtypeStruct((n_batch * lq, N_VROWS), BF16),
        scratch_shapes=[pltpu.VMEM((N_VROWS, tq), F32), pltpu.VMEM((2, n_kt, tk, tq), F32)],
        compiler_params=_cparams(("arbitrary", "arbitrary")),
        name="attention",
    )(qt, *(() if cache is None else cache), k4, v4)


MG_TM = 512


def _merge_kernel(*refs, n_x):
    x_refs = refs[:n_x]
    zc_refs, yf_refs, yb_refs = refs[n_x:n_x + 2], refs[n_x + 2:n_x + 4], refs[n_x + 4:n_x + 6]
    (ax_ref, axp_ref, axn_ref, ac_ref, acp_ref, acn_ref, ab_ref,
     g_ref, z_ref, gm_ref,
     cw_ref, wa_ref, nw_ref, wb_ref, wc_ref, wo_ref, o_ref) = refs[n_x + 6:]
    i = pl.program_id(0)
    hp, hn = _tile_neighbours(i, MG_TM)
    all_rows = slice(0, MG_TM)

    def f32(v):
        return v.astype(F32)

    u = f32(ac_ref[...]) * f32(ax_ref[...])
    u_prev = f32(acp_ref[HALO - 1:HALO, :]) * f32(axp_ref[HALO - 1:HALO, :]) * hp
    u_next = f32(acn_ref[0:1, :]) * f32(axn_ref[0:1, :]) * hn
    za = f32(ab_ref[...]) * _conv3_tile(u, u_prev, u_next, cw_ref, MG_TM, _context_edge_masks(i, MG_TM))
    y_a = jnp.dot(za.astype(BF16), wa_ref[...], preferred_element_type=F32)

    y_scan = f32(_stream_rows(yf_refs, i, MG_TM, all_rows)) + f32(_stream_rows(yb_refs, i, MG_TM, all_rows))
    yb = y_scan * _silu(f32(z_ref[...]))
    zb = _rms(yb, nw_ref[...])
    y_b = jnp.dot(zb.astype(BF16), wb_ref[...], preferred_element_type=F32)

    y_c = jnp.dot(_stream_rows(zc_refs, i, MG_TM, all_rows), wc_ref[...], preferred_element_type=F32)

    merged = (_sigmoid(f32(g_ref[:, 0:D_MODEL])) * y_a
              + _sigmoid(f32(g_ref[:, D_MODEL:2 * D_MODEL])) * y_b
              + _sigmoid(f32(g_ref[:, 2 * D_MODEL:3 * D_MODEL])) * y_c)
    o = jnp.dot(merged.astype(BF16), wo_ref[...], preferred_element_type=F32)
    o_ref[...] = _stream_rows(x_refs, i, MG_TM, all_rows) + gm_ref[...] * o


def _merge(p, yfs, ybs, zcs, xs, mod_l, conv_w, wa, nw, wb, wc, wo, layer):
    halo_per_tile = MG_TM // HALO
    n_halo_blocks = N_TOK // HALO
    row = functools.partial(_mod_row, tile_rows=MG_TM)

    def pcol(width, off):
        return pl.BlockSpec((MG_TM, width), lambda i: (i, off // width))

    def pprev(off):
        return pl.BlockSpec((HALO, A_WIDTH), lambda i: (jnp.maximum(i * halo_per_tile - 1, 0), off // A_WIDTH))

    def pnext(off):
        return pl.BlockSpec((HALO, A_WIDTH),
                            lambda i: (jnp.minimum((i + 1) * halo_per_tile, n_halo_blocks - 1), off // A_WIDTH))

    def full(a):
        return pl.BlockSpec(a.shape, lambda i: (0,) * a.ndim)

    return pl.pallas_call(
        functools.partial(_merge_kernel, n_x=len(xs)),
        grid=(N_TOK // MG_TM,),
        in_specs=(_stream_specs(xs, MG_TM, 1) + _stream_specs(zcs, MG_TM, 1)
                  + _stream_specs(yfs, MG_TM, 1) + _stream_specs(ybs, MG_TM, 1)) + [
            pcol(A_WIDTH, OFF_AX), pprev(OFF_AX), pnext(OFF_AX),
            pcol(A_WIDTH, OFF_AC), pprev(OFF_AC), pnext(OFF_AC),
            pcol(A_WIDTH, OFF_AB),
            pcol(3 * D_MODEL, OFF_G),
            pcol(SSM_INNER, OFF_Z),
            pl.BlockSpec((None, None, 1, D_MODEL), lambda i: (row(i), 2, 0, 0)),
            full(conv_w), _layer_spec(wa, layer, 1), full(nw), _layer_spec(wb, layer, 1),
            _layer_spec(wc, layer, 1), _layer_spec(wo, layer, 1),
        ],
        out_specs=pl.BlockSpec((MG_TM, D_MODEL), lambda i: (i, 0)),
        out_shape=jax.ShapeDtypeStruct((N_TOK, D_MODEL), F32),
        compiler_params=_cparams(("arbitrary",)),
        name="merge",
    )(*xs, *zcs, *yfs, *ybs, p, p, p, p, p, p, p, p, p, mod_l, conv_w, wa, nw, wb, wc, wo)


FF_TM = 1024
FF_TM_FINAL = 512
FF_CHUNK = 256


def _ffn_kernel(*refs, final, tm):
    x_ref, nw_ref, sh_ref, sc_ref, gm_ref, w1_ref, w3_ref, w2_ref = refs[:8]
    if final:
        fw_ref, oc_ref, ol_ref, h_ref, g_ref = refs[8:]
    else:
        o_ref, h_ref, g_ref = refs[8:]
    _modulated_norm_to(h_ref, lambda rows: x_ref[rows, :], nw_ref, sc_ref, sh_ref, tm)
    h = h_ref[...]
    for c in range(0, FF_DIM, FF_CHUNK):
        a = jnp.dot(h, w1_ref[:, c:c + FF_CHUNK], preferred_element_type=F32)
        b = jnp.dot(h, w3_ref[:, c:c + FF_CHUNK], preferred_element_type=F32)
        g_ref[:, c:c + FF_CHUNK] = (_silu(a) * b).astype(BF16)
    ff = jnp.dot(g_ref[...], w2_ref[...], preferred_element_type=F32)
    x_new = x_ref[...] + gm_ref[...] * ff
    if final:
        y = _rms(x_new, fw_ref[...])
        ol_ref[...] = y

        @pl.when(pl.program_id(0) < N_CTX_TOK // tm)
        def _():
            oc_ref[...] = y
    else:
        o_ref[...] = x_new


def _ffn(x, mod_l, norm_w, w1, w3, w2, layer, final_w=None):
    final = final_w is not None
    tm = FF_TM_FINAL if final else FF_TM
    row = functools.partial(_mod_row, tile_rows=tm)
    n_ctx_tiles = N_CTX_TOK // tm
    if final:
        extra_in = [pl.BlockSpec((1, D_MODEL), lambda i: (0, 0))]
        out_specs = [pl.BlockSpec((tm, D_MODEL), lambda i: (jnp.minimum(i, n_ctx_tiles - 1), 0)),
                     pl.BlockSpec((tm, D_MODEL), lambda i: (jnp.maximum(i - n_ctx_tiles, 0), 0))]
        out_shape = [jax.ShapeDtypeStruct((N_CTX_TOK, D_MODEL), F32), jax.ShapeDtypeStruct((N_LAT_TOK, D_MODEL), F32)]
    else:
        extra_in = []
        out_specs = pl.BlockSpec((tm, D_MODEL), lambda i: (i, 0))
        out_shape = jax.ShapeDtypeStruct((N_TOK, D_MODEL), F32)

    def weight_spec(w):
        return pl.BlockSpec((None,) + tuple(w.shape[1:]), lambda i: (layer, 0, 0), pipeline_mode=pl.Buffered(1))

    return pl.pallas_call(
        functools.partial(_ffn_kernel, final=final, tm=tm),
        grid=(N_TOK // tm,),
        in_specs=[
            pl.BlockSpec((tm, D_MODEL), lambda i: (i, 0)),
            pl.BlockSpec((1, D_MODEL), lambda i: (0, 0)),
            pl.BlockSpec((None, None, 1, D_MODEL), lambda i: (row(i), 3, 0, 0)),
            pl.BlockSpec((None, None, 1, D_MODEL), lambda i: (row(i), 4, 0, 0)),
            pl.BlockSpec((None, None, 1, D_MODEL), lambda i: (row(i), 5, 0, 0)),
            weight_spec(w1), weight_spec(w3), weight_spec(w2),
        ] + extra_in,
        out_specs=out_specs,
        out_shape=out_shape,
        scratch_shapes=[pltpu.VMEM((tm, D_MODEL), BF16), pltpu.VMEM((tm, FF_DIM), BF16)],
        compiler_params=_cparams(("arbitrary",)),
        name="ffn",
    )(x, norm_w, mod_l, mod_l, mod_l, w1, w3, w2, *(() if final_w is None else (final_w,)))


def _pad_in_weights(w_in):
    src_ax, src_z, src_bc, src_dt, src_cq, src_kr, src_g = 0, 1536, 3584, 4096, 4112, 4624, 4656
    half = ROPE_DIM // 2
    pieces = [
        (OFF_G, src_g, 3 * D_MODEL),
        (OFF_Z, src_z, 2 * SSM_INNER),
        (OFF_AX, src_ax, 3 * A_WIDTH),
        (OFF_BC, src_bc, BC_WIDTH),
        (OFF_CQ, src_cq, Q_LORA + KV_LORA),
    ]
    w16 = w_in.astype(BF16)

    def z(n):
        return jnp.zeros(w_in.shape[:-1] + (n,), BF16)

    dt, kr = w16[..., src_dt:src_dt + SSM_HEADS], w16[..., src_kr:src_kr + ROPE_DIM]
    narrow = jnp.concatenate([dt, z(LANE - SSM_HEADS),
                              z(64), kr, z(LANE - 64 - ROPE_DIM),
                              z(64), kr[..., half:], kr[..., :half], z(LANE - 64 - ROPE_DIM)], axis=-1)
    out = jnp.zeros(w_in.shape[:-1] + (NP,), BF16)
    for dst, src, width in pieces:
        out = out.at[..., dst:dst + width].set(w16[..., src:src + width])
    return out.at[..., OFF_DT:OFF_DT + 3 * LANE].set(narrow)


def _q_weights_t(w_uq):
    w = w_uq.reshape(DEPTH, Q_LORA, MLA_HEADS, QK_DIM)
    nope, x1, x2 = w[..., :NOPE_DIM], w[..., NOPE_DIM:NOPE_DIM + 16], w[..., NOPE_DIM + 16:]
    z32 = jnp.zeros_like(w[..., :32])
    z64 = jnp.zeros_like(nope)
    q = jnp.concatenate([nope, x1, x2, z32], axis=-1).reshape(DEPTH, Q_LORA, N_QROWS)
    qs = jnp.concatenate([z64, x2, x1, z32], axis=-1).reshape(DEPTH, Q_LORA, N_QROWS)
    return jnp.swapaxes(jnp.concatenate([q, qs], axis=-1), 1, 2).astype(BF16)


def _kv_weights(w_ukv):
    w = w_ukv.reshape(DEPTH, KV_LORA, MLA_HEADS, NOPE_DIM + V_DIM)
    kn = jnp.concatenate([w[..., :NOPE_DIM], jnp.zeros_like(w[..., :QK_PAD - NOPE_DIM])], axis=-1)
    wk = kn.reshape(DEPTH, KV_LORA, N_QROWS).astype(BF16)
    wv_t = jnp.swapaxes(w[..., NOPE_DIM:].reshape(DEPTH, KV_LORA, N_VROWS), 1, 2).astype(BF16)
    return wk, wv_t


def _rope_tables(n_tokens, lead_rows):
    n_rows = n_tokens // GRID_W
    row = jnp.repeat(jnp.arange(n_rows, dtype=F32), GRID_W)
    col = jnp.tile(jnp.arange(GRID_W, dtype=F32), n_rows)
    pairs = ROPE_DIM // 4
    inv = ROPE_BASE ** (-jnp.arange(pairs, dtype=F32) / pairs)
    ang = jnp.concatenate([row[:, None] * inv, col[:, None] * inv], axis=-1)
    cos, sin = jnp.cos(ang), jnp.sin(ang)
    ones = jnp.ones((n_tokens, NOPE_DIM), F32)
    z32 = jnp.zeros((n_tokens, 32), F32)
    cos_l = jnp.concatenate([ones, cos, cos, z32], axis=-1)
    sin_l = jnp.concatenate([jnp.zeros_like(ones), -sin, sin, z32], axis=-1)
    ident_c = jnp.concatenate([jnp.ones((lead_rows, NOPE_DIM + ROPE_DIM), F32), jnp.zeros((lead_rows, 32), F32)], -1)
    ident_s = jnp.zeros((lead_rows, LANE), F32)
    return jnp.concatenate([ident_c, cos_l], axis=0), jnp.concatenate([ident_s, sin_l], axis=0)


PREP_TM_CTX = SEQ
PREP_TM_LAT = 512


def kernel(x_prompt, x_sample, c, cache_ckv, cache_krope, state_ssm_fwd, state_ssm_bwd, c_ctx, w_in, a_conv_w, w_a_out, ssm_conv_w, ssm_conv_b, ssm_a_log, ssm_dt_bias, ssm_d, ssm_norm_w, w_b_out, q_norm_w, w_uq, kv_norm_w, w_ukv, w_c_out, w_o, w_ada, b_ada, norm1_w, norm2_w, w_ff1, w_ff3, w_ff2, final_norm_w):
    w_in_p = _pad_in_weights(w_in)
    wq_t = _q_weights_t(w_uq)
    wk, wv_t = _kv_weights(w_ukv)
    wa, wb, wc, wo = (w.astype(BF16) for w in (w_a_out, w_b_out, w_c_out, w_o))
    w1, w3, w2 = (w.astype(BF16) for w in (w_ff1, w_ff3, w_ff2))

    cond = jnp.concatenate([c_ctx[None, :], c, jnp.zeros((N_MOD_ROWS - 1 - DEC_BATCH, D_MODEL), F32)], axis=0)
    mod = _modulation(cond, w_ada, b_ada).reshape(DEPTH, N_MOD_ROWS, 6, 1, D_MODEL)

    conv_wx = ssm_conv_w[..., :SSM_INNER]
    conv_wbc = ssm_conv_w[..., SSM_INNER:]
    conv_bx = ssm_conv_b[:, None, :SSM_INNER]
    conv_bbc = ssm_conv_b[:, None, SSM_INNER:]
    pad_h = ((0, 0), (0, 0), (0, 0), (0, LANE - SSM_HEADS))
    alog = jnp.pad(ssm_a_log[:, :, None, :], pad_h)
    dtb = jnp.pad(ssm_dt_bias[:, :, None, :], pad_h)
    dskip = jnp.repeat(ssm_d, SSM_HEAD_DIM, axis=-1)[:, :, None, :]
    h0f = state_ssm_fwd.reshape(DEC_BATCH, DEPTH, SSM_INNER, SSM_STATE)
    h0b = state_ssm_bwd.reshape(DEC_BATCH, DEPTH, SSM_INNER, SSM_STATE)

    cos_c, sin_c = _rope_tables(DEC_SEQ, PREP_TM_CTX)
    cos_l, sin_l = cos_c[PREP_TM_CTX:], sin_c[PREP_TM_CTX:]
    ident_cos, ident_sin = cos_c[:PREP_TM_CTX], sin_c[:PREP_TM_CTX]
    ones_tab = jnp.concatenate([ident_cos, ident_cos], axis=0)
    zeros_tab = jnp.zeros_like(ones_tab)
    cache_kr_pad = jnp.pad(cache_krope, ((0, 0), (0, 0), (0, 0), (64, 32)))

    xs = (x_prompt.reshape(N_CTX_TOK, D_MODEL), x_sample.reshape(N_LAT_TOK, D_MODEL))

    new_ckv, new_kr = [], []
    fin = None
    for l in range(DEPTH):
        p, ps = _in_projection(xs, mod[l], norm1_w[l][None, :], w_in_p, l)

        xc, bcc = _ssd_conv(p, conv_wx[l], conv_bx[l], conv_wbc[l], conv_bbc[l])
        yc_f, yc_b, *fin = _ssd(ps, xc, bcc, alog[l], dtb[l], dskip[l],
                                row0=0, n_seq=BATCH, seq_len=SEQ, tq=SEQ, fin_layer=l, fin_prev=fin)
        yl_f, yl_b = _ssd(ps, xc, bcc, alog[l], dtb[l], dskip[l],
                          row0=N_CTX_TOK, n_seq=DEC_BATCH, seq_len=DEC_SEQ, tq=SSD_TQ_LAT, h0=(h0f, h0b, l))

        qnw, kvnw = q_norm_w[l][None, :], kv_norm_w[l][None, :]
        qt_c, k_c, vt_c, ckv_c = _mla_prep_tokens(
            ps, 0, BATCH, SEQ, PREP_TM_CTX, ident_cos, ident_sin, ident_cos.T, ident_sin.T,
            qnw, kvnw, wq_t, wk, wv_t, l, keep_ckv=True)
        qt_l, k_l, vt_l = _mla_prep_tokens(
            ps, N_CTX_TOK, DEC_BATCH, DEC_SEQ, PREP_TM_LAT, cos_l, sin_l, cos_l.T, sin_l.T,
            qnw, kvnw, wq_t, wk, wv_t, l, keep_ckv=False)
        k_p, vt_p = _mla_prep_cache(cache_ckv[:, l].reshape(DEC_BATCH * PAST_LEN, KV_LORA),
                                    cache_kr_pad[:, l].reshape(DEC_BATCH * PAST_LEN, LANE),
                                    ones_tab, zeros_tab, kvnw, wk, wv_t, PAST_LEN, l)
        new_ckv.append(ckv_c.reshape(BATCH, SEQ, KV_LORA))
        kr0 = OFF_KR - NP_MAIN + 64
        new_kr.append(ps[:N_CTX_TOK, kr0:kr0 + ROPE_DIM].reshape(BATCH, SEQ, ROPE_DIM))

        zc_c = _attention(qt_c, k_c, vt_c, BATCH, SEQ, SEQ)
        zc_l = _attention(qt_l, k_l, vt_l, DEC_BATCH, DEC_SEQ, PREP_TM_LAT, cache=(k_p, vt_p))

        x = _merge(p, (yc_f, yl_f), (yc_b, yl_b), (zc_c, zc_l), xs, mod[l], a_conv_w[l], wa, ssm_norm_w[l][None, :],
                   wb, wc, wo, l)
        if l + 1 < DEPTH:
            xs = (_ffn(x, mod[l], norm2_w[l][None, :], w1, w3, w2, l),)
        else:
            y_ctx, y_lat = _ffn(x, mod[l], norm2_w[l][None, :], w1, w3, w2, l, final_w=final_norm_w[None, :])

    y_prompt = y_ctx.reshape(BATCH, SEQ, D_MODEL)
    y_sample = y_lat.reshape(DEC_BATCH, DEC_SEQ, D_MODEL)
    hshape = (BATCH, DEPTH, SSM_HEADS, SSM_HEAD_DIM, SSM_STATE)
    return (y_prompt, y_sample,
            jnp.stack(new_ckv, axis=1), jnp.stack(new_kr, axis=1),
            fin[0].reshape(hshape), fin[1].reshape(hshape))
```

```python
import functools
import math

import jax
import jax.numpy as jnp
import numpy as np
from jax import lax
from jax.experimental import pallas as pl
from jax.experimental.pallas import tpu as pltpu

F32 = jnp.float32
BF16 = jnp.bfloat16

D_MODEL = 1024
BATCH = 16
SEQ = 256
DEPTH = 4
DEC_BATCH = 4
DEC_SEQ = 4096
PAST_LEN = 512
GRID_W = 64
EPS = 1e-6
A_WIDTH = 512
SSM_INNER = 1024
SSM_HEAD_DIM = 64
SSM_HEADS = 16
SSM_GROUPS = 2
SSM_STATE = 128
CHUNK = 128
MLA_HEADS = 8
Q_LORA = 256
KV_LORA = 256
NOPE_DIM = 64
ROPE_DIM = 32
V_DIM = 64
QK_DIM = NOPE_DIM + ROPE_DIM
ROPE_BASE = 10000.0
FF_DIM = 2816

N_CTX_TOK = BATCH * SEQ
N_LAT_TOK = DEC_BATCH * DEC_SEQ
N_TOK = N_CTX_TOK + N_LAT_TOK
N_MOD_ROWS = 8

LANE = 128
SUBLANE = 8
VMEM_LIMIT = 56 * 1024 * 1024

OFF_G = 0
OFF_Z = 3072
OFF_SX = 4096
OFF_AX = 5120
OFF_AB = 5632
OFF_AC = 6144
OFF_BC = 6656
NP_MAIN = 7168
OFF_CQ = 7168
OFF_CKV = 7424
OFF_DT = 7680
OFF_KR = 7808
OFF_KRS = 7936
NP = 8192
N_MAIN_TILES = 7

NEG_BIG = -1e30


def _cparams(sem):
    return pltpu.CompilerParams(dimension_semantics=sem, vmem_limit_bytes=VMEM_LIMIT)


def _rms(x, w):
    ms = jnp.mean(x * x, axis=-1, keepdims=True)
    return x * lax.rsqrt(ms + EPS) * w


def _sigmoid(x):
    return 0.5 * jnp.tanh(0.5 * x) + 0.5


def _silu(x):
    h = 0.5 * x
    return h * jnp.tanh(h) + h


def _layer_spec(w, layer, grid_rank):
    zeros = (0,) * (w.ndim - 1)
    imap = (lambda i: (layer,) + zeros) if grid_rank == 1 else (lambda i, j: (layer,) + zeros)
    return pl.BlockSpec((None,) + tuple(w.shape[1:]), imap)


def _mod_row(tile, tile_rows):
    n_ctx_tiles = N_CTX_TOK // tile_rows
    tiles_per_lat = DEC_SEQ // tile_rows
    return jnp.where(tile < n_ctx_tiles, 0, 1 + (tile - n_ctx_tiles) // tiles_per_lat)


MOD_TN = 1536


def _mod_kernel(c_ref, w_ref, b_ref, o_ref):
    c = c_ref[...]
    s = _silu(c).astype(BF16)
    o_ref[...] = jnp.dot(s, w_ref[...].astype(BF16), preferred_element_type=F32) + b_ref[...]


def _modulation(cond, w_ada, b_ada):
    n_col = 6 * D_MODEL
    return pl.pallas_call(
        _mod_kernel,
        grid=(DEPTH, n_col // MOD_TN),
        in_specs=[
            pl.BlockSpec((N_MOD_ROWS, D_MODEL), lambda l, j: (0, 0)),
            pl.BlockSpec((None, D_MODEL, MOD_TN), lambda l, j: (l, 0, j)),
            pl.BlockSpec((None, 1, MOD_TN), lambda l, j: (l, 0, j)),
        ],
        out_specs=pl.BlockSpec((None, N_MOD_ROWS, MOD_TN), lambda l, j: (l, 0, j)),
        out_shape=jax.ShapeDtypeStruct((DEPTH, N_MOD_ROWS, n_col), F32),
        compiler_params=_cparams(("arbitrary", "arbitrary")),
        name="modulation",
    )(cond, w_ada, b_ada.reshape(DEPTH, 1, n_col))


IN_TM = 2048
IN_TN = 1024
NORM_ROWS = 256


def _stream_specs(xs, tm, grid_rank):
    def imap(f):
        return (lambda i: f(i)) if grid_rank == 1 else (lambda i, j: f(i))

    width = xs[0].shape[1]
    if len(xs) == 1:
        return [pl.BlockSpec((tm, width), imap(lambda i: (i, 0)))]
    n_ctx_tiles = N_CTX_TOK // tm
    return [pl.BlockSpec((tm, width), imap(lambda i: (jnp.minimum(i, n_ctx_tiles - 1), 0))),
            pl.BlockSpec((tm, width), imap(lambda i: (jnp.maximum(i - n_ctx_tiles, 0), 0)))]


def _stream_rows(x_refs, tile, tm, rows):
    if len(x_refs) == 1:
        return x_refs[0][rows, :]
    return jnp.where(tile < N_CTX_TOK // tm, x_refs[0][rows, :], x_refs[1][rows, :])


def _modulated_norm_to(h_ref, x_rows, nw_ref, sc_ref, sh_ref, rows):
    for r in range(0, rows, NORM_ROWS):
        x = x_rows(slice(r, r + NORM_ROWS))
        h = _rms(x, nw_ref[...]) * (1.0 + sc_ref[...]) + sh_ref[...]
        h_ref[r:r + NORM_ROWS, :] = h.astype(BF16)


def _inproj_kernel(*refs, n_x, tm):
    x_refs = refs[:n_x]
    nw_ref, sh_ref, sc_ref, w_ref, om_ref, os_ref, h_ref = refs[n_x:]
    i = pl.program_id(0)
    j = pl.program_id(1)

    @pl.when(j == 0)
    def _():
        _modulated_norm_to(h_ref, lambda rows: _stream_rows(x_refs, i, tm, rows),
                           nw_ref, sc_ref, sh_ref, tm)

    @pl.when(j < N_MAIN_TILES)
    def _():
        om_ref[...] = jnp.dot(h_ref[...], w_ref[...], preferred_element_type=F32).astype(BF16)

    @pl.when(j == N_MAIN_TILES)
    def _():
        os_ref[...] = jnp.dot(h_ref[...], w_ref[...], preferred_element_type=F32)


def _in_projection(xs, mod_l, norm_w, w_in_p, layer):
    tm = IN_TM if len(xs) == 1 else IN_TM // 2
    row = functools.partial(_mod_row, tile_rows=tm)
    return pl.pallas_call(
        functools.partial(_inproj_kernel, n_x=len(xs), tm=tm),
        grid=(N_TOK // tm, NP // IN_TN),
        in_specs=_stream_specs(xs, tm, 2) + [
            pl.BlockSpec((1, D_MODEL), lambda i, j: (0, 0)),
            pl.BlockSpec((None, None, 1, D_MODEL), lambda i, j: (row(i), 0, 0, 0)),
            pl.BlockSpec((None, None, 1, D_MODEL), lambda i, j: (row(i), 1, 0, 0)),
            pl.BlockSpec((None, D_MODEL, IN_TN), lambda i, j: (layer, 0, j)),
        ],
        out_specs=[
            pl.BlockSpec((tm, IN_TN), lambda i, j: (i, jnp.minimum(j, N_MAIN_TILES - 1))),
            pl.BlockSpec((tm, IN_TN), lambda i, j: (i, 0)),
        ],
        out_shape=[
            jax.ShapeDtypeStruct((N_TOK, NP_MAIN), BF16),
            jax.ShapeDtypeStruct((N_TOK, IN_TN), F32),
        ],
        scratch_shapes=[pltpu.VMEM((tm, D_MODEL), BF16)],
        compiler_params=_cparams(("arbitrary", "arbitrary")),
        name="in_projection",
    )(*xs, norm_w, mod_l, mod_l, w_in_p)


def _conv3_tile(u, prev_row, next_row, w_ref, rows, inner_masks=None):
    ridx = lax.broadcasted_iota(jnp.int32, (SUBLANE, 1), 0)
    up = pltpu.roll(u, 1, axis=0)
    up = jnp.concatenate([jnp.where(ridx == 0, prev_row, up[0:SUBLANE]), up[SUBLANE:]], axis=0)
    dn = pltpu.roll(u, rows - 1, axis=0)
    dn = jnp.concatenate([dn[:rows - SUBLANE], jnp.where(ridx == SUBLANE - 1, next_row, dn[rows - SUBLANE:])],
                         axis=0)
    if inner_masks is not None:
        up = up * inner_masks[0]
        dn = dn * inner_masks[1]
    return up * w_ref[0:1, :] + u * w_ref[1:2, :] + dn * w_ref[2:3, :]


CONV_TQ = 1024
SSD_TQ_LAT = 512
HALO = 2 * SUBLANE
BC_WIDTH = 2 * SSM_GROUPS * SSM_STATE


def _tile_neighbours(i, tile_rows):
    n_ctx_tiles = N_CTX_TOK // tile_rows
    tiles_per_lat = DEC_SEQ // tile_rows
    t_in_seq = (i - n_ctx_tiles) % tiles_per_lat
    is_lat = i >= n_ctx_tiles
    has_prev = jnp.logical_and(is_lat, t_in_seq > 0).astype(F32)
    has_next = jnp.logical_and(is_lat, t_in_seq < tiles_per_lat - 1).astype(F32)
    return has_prev, has_next


def _context_edge_masks(tile, tile_rows):
    edge_keep = jnp.where(tile < N_CTX_TOK // tile_rows, 0.0, 1.0)
    pos = lax.broadcasted_iota(jnp.int32, (tile_rows, 1), 0) & (SEQ - 1)
    return jnp.where(pos == 0, edge_keep, 1.0), jnp.where(pos == SEQ - 1, edge_keep, 1.0)


def _ssd_conv_kernel(x_ref, xp_ref, xn_ref, bc_ref, bcp_ref, bcn_ref,
                     cwx_ref, cbx_ref, cwb_ref, cbb_ref, xc_ref, bcc_ref):
    i = pl.program_id(0)
    hp, hn = _tile_neighbours(i, CONV_TQ)
    masks = _context_edge_masks(i, CONV_TQ)

    def conv_silu(u_ref, up_ref, un_ref, w_ref, b_ref):
        prev_row = up_ref[HALO - 1:HALO, :].astype(F32) * hp
        next_row = un_ref[0:1, :].astype(F32) * hn
        conv = _conv3_tile(u_ref[...].astype(F32), prev_row, next_row, w_ref, CONV_TQ, masks)
        return _silu(conv + b_ref[...])

    xc_ref[...] = conv_silu(x_ref, xp_ref, xn_ref, cwx_ref, cbx_ref)
    bcc_ref[...] = conv_silu(bc_ref, bcp_ref, bcn_ref, cwb_ref, cbb_ref)


def _ssd_conv(p, conv_wx, conv_bx, conv_wbc, conv_bbc):
    halo_per_tile = CONV_TQ // HALO
    n_halo_blocks = N_TOK // HALO
    cx, cbc = OFF_SX // SSM_INNER, OFF_BC // BC_WIDTH

    def prev_map(col):
        return lambda i: (jnp.maximum(i * halo_per_tile - 1, 0), col)

    def next_map(col):
        return lambda i: (jnp.minimum((i + 1) * halo_per_tile, n_halo_blocks - 1), col)

    def const2(i):
        return (0, 0)

    return pl.pallas_call(
        _ssd_conv_kernel,
        grid=(N_TOK // CONV_TQ,),
        in_specs=[
            pl.BlockSpec((CONV_TQ, SSM_INNER), lambda i: (i, cx)),
            pl.BlockSpec((HALO, SSM_INNER), prev_map(cx)),
            pl.BlockSpec((HALO, SSM_INNER), next_map(cx)),
            pl.BlockSpec((CONV_TQ, BC_WIDTH), lambda i: (i, cbc)),
            pl.BlockSpec((HALO, BC_WIDTH), prev_map(cbc)),
            pl.BlockSpec((HALO, BC_WIDTH), next_map(cbc)),
            pl.BlockSpec((3, SSM_INNER), const2),
            pl.BlockSpec((1, SSM_INNER), const2),
            pl.BlockSpec((3, BC_WIDTH), const2),
            pl.BlockSpec((1, BC_WIDTH), const2),
        ],
        out_specs=[
            pl.BlockSpec((CONV_TQ, SSM_INNER), lambda i: (i, 0)),
            pl.BlockSpec((CONV_TQ, BC_WIDTH), lambda i: (i, 0)),
        ],
        out_shape=[
            jax.ShapeDtypeStruct((N_TOK, SSM_INNER), F32),
            jax.ShapeDtypeStruct((N_TOK, BC_WIDTH), F32),
        ],
        compiler_params=_cparams(("arbitrary",)),
        name="ssd_conv",
    )(p, p, p, p, p, p, conv_wx, conv_bx, conv_wbc, conv_bbc)


def _ssd_tables(row0, n_seq, seq_len, tq):
    blk_f, blk_b, seq, first = [], [], [], []
    nt = seq_len // tq
    for s in range(n_seq):
        base = (row0 + s * seq_len) // tq
        for k in range(nt):
            blk_f.append(base + k)
            blk_b.append(base + nt - 1 - k)
            seq.append(s)
            first.append(int(k == 0))
    return [np.asarray(a, np.int32) for a in (blk_f, blk_b, seq, first)]


def _split3(a):
    a1 = a.astype(BF16)
    r1 = a - a1.astype(F32)
    a2 = r1.astype(BF16)
    a3 = (r1 - a2.astype(F32)).astype(BF16)
    return a1, a2, a3


def _dot3(lhs_bf16, a):
    a1, a2, a3 = _split3(a)
    return (jnp.dot(lhs_bf16, a1, preferred_element_type=F32)
            + jnp.dot(lhs_bf16, a2, preferred_element_type=F32)
            + jnp.dot(lhs_bf16, a3, preferred_element_type=F32))


def _softplus(x):
    return jnp.maximum(x, 0.0) + jnp.log1p(jnp.exp(-jnp.abs(x)))


def _ssd_kernel(*refs, n_chunks, context, n_carried=0):
    blkf_t, blkb_t, seq_t, first_t = refs[:4]
    xf_ref, bcf_ref, dtf_ref, xb_ref, bcb_ref, dtb_ref, alog_ref, dtbias_ref, dsk_ref = refs[4:13]
    if context:
        yf_ref, yb_ref, finf_ref, finb_ref, stf_ref, stb_ref = refs[13 + n_carried:]
    else:
        h0f_ref, h0b_ref, yf_ref, yb_ref, stf_ref, stb_ref = refs[13:]
    s = pl.program_id(0)
    dirs = ((xf_ref, bcf_ref, dtf_ref, stf_ref, yf_ref), (xb_ref, bcb_ref, dtb_ref, stb_ref, yb_ref))

    if context:
        stf_ref[...] = jnp.zeros_like(stf_ref)
        stb_ref[...] = jnp.zeros_like(stb_ref)
    else:
        @pl.when(first_t[s] == 1)
        def _():
            stf_ref[...] = h0f_ref[...].T
            stb_ref[...] = h0b_ref[...].T

    ii = lax.broadcasted_iota(jnp.int32, (CHUNK, CHUNK), 0)
    jj = lax.broadcasted_iota(jnp.int32, (CHUNK, CHUNK), 1)
    masks = (jj <= ii, jj >= ii)
    masks_b = tuple(jnp.where(m, 1.0, 0.0).astype(BF16) for m in masks)
    masks_neg = tuple(jnp.where(m, 0.0, NEG_BIG) for m in masks)
    lo = jj < SSM_HEAD_DIM

    e_r = lax.broadcasted_iota(jnp.int32, (LANE, SSM_INNER), 0)
    e_c = lax.broadcasted_iota(jnp.int32, (LANE, SSM_INNER), 1)
    expand = jnp.where(jnp.right_shift(e_c, 6) == e_r, 1.0, 0.0).astype(BF16)

    def chunk_setup(d, c):
        _, _, dt_ref, _, _ = dirs[d]
        rows = slice(c * CHUNK, (c + 1) * CHUNK)
        a_row = -jnp.exp(alog_ref[d])
        dt = _softplus(dt_ref[rows, :] + dtbias_ref[d])
        cum = _dot3(masks_b[d], dt * a_row)
        cum_t = cum.T
        dt_t = dt.T
        end = CHUNK - 1 if d == 0 else 0
        tot_row = cum[end:end + 1, :]
        tot_col = cum_t[:, end:end + 1]
        return dict(
            rows=rows, cum=cum, cum_t=cum_t, dt_t=dt_t,
            w_rows=jnp.exp(tot_col - cum_t) * dt_t,
            sdec=_dot3_rows(jnp.exp(tot_row), expand))

    def group_setup(d, cs, g):
        _, bc_ref, _, _, _ = dirs[d]
        b_g = bc_ref[cs["rows"], g * SSM_STATE:(g + 1) * SSM_STATE]
        c_g = bc_ref[cs["rows"], (SSM_GROUPS + g) * SSM_STATE:(SSM_GROUPS + g + 1) * SSM_STATE].astype(BF16)
        b_gt = b_g.T
        cb = jnp.dot(c_g, b_gt.astype(BF16), preferred_element_type=F32)
        return c_g, b_gt, cb

    def pair_step(d, cs, gs, kp):
        x_ref, _, _, st_ref, y_ref = dirs[d]
        c_g, b_gt, cb = gs
        cum, cum_t, dt_t = cs["cum"], cs["cum_t"], cs["dt_t"]
        h_a, h_b = 2 * kp, 2 * kp + 1
        ls = slice(kp * LANE, (kp + 1) * LANE)
        x_pair = x_ref[cs["rows"], ls]
        rhs = jnp.concatenate([jnp.where(lo, x_pair, 0.0).astype(BF16),
                               jnp.where(lo, 0.0, x_pair).astype(BF16)], axis=0)

        def head_lhs(h):
            col = jnp.broadcast_to(cum[:, h:h + 1], (CHUNK, CHUNK))
            seg = col - cum_t[h:h + 1, :]
            dec = jnp.exp(seg + masks_neg[d])
            w_intra = dec * cb * dt_t[h:h + 1, :]
            w_state = b_gt * cs["w_rows"][h:h + 1, :]
            return w_intra.astype(BF16), w_state.astype(BF16), col

        wi_a, ws_a, col_a = head_lhs(h_a)
        wi_b, ws_b, col_b = head_lhs(h_b)
        lhs = jnp.concatenate([jnp.concatenate([wi_a, wi_b], axis=1),
                               jnp.concatenate([ws_a, ws_b], axis=1)], axis=0)
        both = jnp.dot(lhs, rhs, preferred_element_type=F32)
        y_diag = both[0:CHUNK, :]
        d_state = both[CHUNK:2 * CHUNK, :]

        h_pair = st_ref[:, ls]
        y_off = jnp.dot(c_g, h_pair.astype(BF16), preferred_element_type=F32)
        e_pair = jnp.exp(jnp.where(lo, col_a, col_b))
        y_ref[cs["rows"], ls] = (y_diag + y_off * e_pair + x_pair * dsk_ref[d, :, ls]).astype(y_ref.dtype)
        st_ref[:, ls] = h_pair * cs["sdec"][:, ls] + d_state

    pairs_per_group = SSM_HEADS // SSM_GROUPS // 2
    setups = [(chunk_setup(0, k), chunk_setup(1, n_chunks - 1 - k)) for k in range(n_chunks)]
    for k in range(n_chunks):
        cs = setups[k]
        for g in range(SSM_GROUPS):
            gs = (group_setup(0, cs[0], g), group_setup(1, cs[1], g))
            for kp in range(g * pairs_per_group, (g + 1) * pairs_per_group):
                pair_step(0, cs[0], gs[0], kp)
                pair_step(1, cs[1], gs[1], kp)

    if context:
        finf_ref[...] = stf_ref[...].T
        finb_ref[...] = stb_ref[...].T


def _dot3_rows(row, rhs_bf16):
    r8 = jnp.broadcast_to(row, (SUBLANE, row.shape[1]))
    r1, r2, r3 = _split3(r8)
    out = (jnp.dot(r1, rhs_bf16, preferred_element_type=F32)
           + jnp.dot(r2, rhs_bf16, preferred_element_type=F32)
           + jnp.dot(r3, rhs_bf16, preferred_element_type=F32))
    return out[0:1, :]


def _ssd(ps, xc, bcc, alog, dtb, dskip, *, row0, n_seq, seq_len, tq, h0=None, fin_layer=None, fin_prev=None):
    context = h0 is None
    assert not context or seq_len == tq
    tables = [jnp.asarray(t) for t in _ssd_tables(row0, n_seq, seq_len, tq)]
    n_steps = int(tables[0].shape[0])
    n_rows = n_seq * seq_len
    blk0 = row0 // tq
    cdt = (OFF_DT - NP_MAIN) // LANE

    def fwd_tile(col, off=0):
        return lambda s, blkf, *_: (blkf[s] - off, col)

    def bwd_tile(col, off=0):
        return lambda s, blkf, blkb, *_: (blkb[s] - off, col)

    def tile_specs(tile):
        return [
            pl.BlockSpec((tq, SSM_INNER), tile(0)),
            pl.BlockSpec((tq, BC_WIDTH), tile(0)),
            pl.BlockSpec((tq, LANE), tile(cdt)),
        ]

    def const3(s, *_):
        return (0, 0, 0)

    in_specs = tile_specs(fwd_tile) + tile_specs(bwd_tile) + [
        pl.BlockSpec((2, 1, LANE), const3),
        pl.BlockSpec((2, 1, LANE), const3),
        pl.BlockSpec((2, 1, SSM_INNER), const3),
    ]
    out_specs = [pl.BlockSpec((tq, SSM_INNER), fwd_tile(0, blk0)), pl.BlockSpec((tq, SSM_INNER), bwd_tile(0, blk0))]
    out_shape = [jax.ShapeDtypeStruct((n_rows, SSM_INNER), BF16), jax.ShapeDtypeStruct((n_rows, SSM_INNER), BF16)]
    operands = [xc, bcc, ps, xc, bcc, ps, alog, dtb, dskip]
    aliases = {}
    if context:
        def fin_map(s, blkf, blkb, seq, *_):
            return (seq[s], fin_layer, 0, 0)

        out_specs += [pl.BlockSpec((None, None, SSM_INNER, SSM_STATE), fin_map)] * 2
        out_shape += [jax.ShapeDtypeStruct((n_seq, DEPTH, SSM_INNER, SSM_STATE), F32)] * 2
        if fin_prev is not None:
            n_before = len(tables) + len(operands)
            in_specs += [pl.BlockSpec(memory_space=pl.ANY)] * 2
            operands += list(fin_prev)
            aliases = {n_before: 2, n_before + 1: 3}
    else:
        h0f, h0b, layer = h0

        def h0_map(s, blkf, blkb, seq, *_):
            return (seq[s], layer, 0, 0)

        in_specs += [pl.BlockSpec((None, None, SSM_INNER, SSM_STATE), h0_map)] * 2
        operands += [h0f, h0b]

    grid_spec = pltpu.PrefetchScalarGridSpec(
        num_scalar_prefetch=4,
        grid=(n_steps,),
        in_specs=in_specs,
        out_specs=out_specs,
        scratch_shapes=[
            pltpu.VMEM((SSM_STATE, SSM_INNER), F32),
            pltpu.VMEM((SSM_STATE, SSM_INNER), F32),
        ],
    )
    return pl.pallas_call(
        functools.partial(_ssd_kernel, n_chunks=tq // CHUNK, context=context, n_carried=len(aliases)),
        grid_spec=grid_spec,
        out_shape=out_shape,
        input_output_aliases=aliases,
        compiler_params=_cparams(("arbitrary",)),
        name="ssd_scan",
    )(*tables, *operands)


QK_PAD = 128
N_QROWS = MLA_HEADS * QK_PAD
N_VROWS = MLA_HEADS * V_DIM
NT_DIMS = (((1,), (1,)), ((), ()))
Q_PRESCALE = (1.0 / math.sqrt(QK_DIM)) * math.log2(math.e)


def _mla_prep_kernel(*refs, tm, normalize, with_q, keep_ckv):
    if with_q:
        (cq_ref, ckv_ref, kr_ref, krs_ref, cos_ref, sin_ref, cost_ref, sint_ref,
         qnw_ref, kvnw_ref, wq_ref, wk_ref, wv_ref, qt_ref, k_ref, vt_ref) = refs[:16]
    else:
        (ckv_ref, kr_ref, krs_ref, cos_ref, sin_ref, kvnw_ref, wk_ref, wv_ref,
         k_ref, vt_ref) = refs

    ckv = ckv_ref[...]
    if normalize:
        ckv = _rms(ckv, kvnw_ref[...])
    if keep_ckv:
        refs[16][...] = ckv
    ckv_b = ckv.astype(BF16)
    kn = jnp.dot(ckv_b, wk_ref[...], preferred_element_type=F32)
    kr = kr_ref[...] * cos_ref[...] + krs_ref[...] * sin_ref[...]
    for h in range(MLA_HEADS):
        hs = slice(h * QK_PAD, (h + 1) * QK_PAD)
        k_ref[:, hs] = (kn[:, hs] + kr).astype(BF16)
    vt_ref[...] = lax.dot_general(wv_ref[...], ckv_b, NT_DIMS, preferred_element_type=F32).astype(BF16)

    if with_q:
        cqn = _rms(cq_ref[...], qnw_ref[...]).astype(BF16)
        qq = lax.dot_general(wq_ref[...], cqn, NT_DIMS, preferred_element_type=F32)
        for h in range(MLA_HEADS):
            q_h = qq[h * QK_PAD:(h + 1) * QK_PAD, :]
            qs_h = qq[N_QROWS + h * QK_PAD:N_QROWS + (h + 1) * QK_PAD, :]
            q_rot = q_h * cost_ref[...] + qs_h * sint_ref[...]
            qt_ref[h * QK_PAD:(h + 1) * QK_PAD, :] = (q_rot * Q_PRESCALE).astype(BF16)


def _mla_prep_tokens(p, row0, n_seq, seq_len, tm, cos, sin, cos_t, sin_t, qnw, kvnw, wq_t, wk, wv_t, layer,
                     keep_ckv):
    b0 = row0 // tm
    tps = seq_len // tm
    nt = n_seq * tps
    n_rows = n_seq * seq_len
    kernel = functools.partial(_mla_prep_kernel, tm=tm, normalize=True, with_q=True, keep_ckv=keep_ckv)

    def tile(t, b):
        return b * tps + t

    def pcol(width, off):
        return pl.BlockSpec((tm, width), lambda t, b: (b0 + tile(t, b), (off - NP_MAIN) // width))

    def full(a):
        return pl.BlockSpec(a.shape, lambda t, b: (0,) * a.ndim)

    return pl.pallas_call(
        kernel,
        grid=(tps, n_seq),
        in_specs=[
            pcol(Q_LORA, OFF_CQ), pcol(KV_LORA, OFF_CKV), pcol(LANE, OFF_KR), pcol(LANE, OFF_KRS),
            pl.BlockSpec((tm, LANE), lambda t, b: (t, 0)),
            pl.BlockSpec((tm, LANE), lambda t, b: (t, 0)),
            pl.BlockSpec((LANE, tm), lambda t, b: (0, t)),
            pl.BlockSpec((LANE, tm), lambda t, b: (0, t)),
            full(qnw), full(kvnw), _layer_spec(wq_t, layer, 2), _layer_spec(wk, layer, 2), _layer_spec(wv_t, layer, 2),
        ],
        out_specs=[
            pl.BlockSpec((N_QROWS, tm), lambda t, b: (0, tile(t, b))),
            pl.BlockSpec((None, tm, N_QROWS), lambda t, b: (tile(t, b), 0, 0)),
            pl.BlockSpec((None, N_VROWS, tm), lambda t, b: (tile(t, b), 0, 0)),
        ] + ([pl.BlockSpec((tm, KV_LORA), lambda t, b: (tile(t, b), 0))] if keep_ckv else []),
        out_shape=[
            jax.ShapeDtypeStruct((N_QROWS, n_rows), BF16),
            jax.ShapeDtypeStruct((nt, tm, N_QROWS), BF16),
            jax.ShapeDtypeStruct((nt, N_VROWS, tm), BF16),
        ] + ([jax.ShapeDtypeStruct((n_rows, KV_LORA), F32)] if keep_ckv else []),
        compiler_params=_cparams(("arbitrary", "arbitrary")),
        name="mla_prep",
    )(p, p, p, p, cos, sin, cos_t, sin_t, qnw, kvnw, wq_t, wk, wv_t)


def _mla_prep_cache(ckv, kr_pad, ones_tab, zeros_tab, kvnw, wk, wv_t, tm, layer):
    n_rows = ckv.shape[0]
    nt = n_rows // tm
    kernel = functools.partial(_mla_prep_kernel, tm=tm, normalize=False, with_q=False, keep_ckv=False)

    def full(a):
        return pl.BlockSpec(a.shape, lambda i: (0,) * a.ndim)

    return pl.pallas_call(
        kernel,
        grid=(nt,),
        in_specs=[
            pl.BlockSpec((tm, KV_LORA), lambda i: (i, 0)),
            pl.BlockSpec((tm, LANE), lambda i: (i, 0)),
            pl.BlockSpec((tm, LANE), lambda i: (i, 0)),
            pl.BlockSpec((tm, LANE), lambda i: (0, 0)),
            pl.BlockSpec((tm, LANE), lambda i: (0, 0)),
            full(kvnw), _layer_spec(wk, layer, 1), _layer_spec(wv_t, layer, 1),
        ],
        out_specs=[
            pl.BlockSpec((None, tm, N_QROWS), lambda i: (i, 0, 0)),
            pl.BlockSpec((None, N_VROWS, tm), lambda i: (i, 0, 0)),
        ],
        out_shape=[
            jax.ShapeDtypeStruct((nt, tm, N_QROWS), BF16),
            jax.ShapeDtypeStruct((nt, N_VROWS, tm), BF16),
        ],
        compiler_params=_cparams(("arbitrary",)),
        name="mla_prep_cache",
    )(ckv, kr_pad, kr_pad, ones_tab, zeros_tab, kvnw, wk, wv_t)


ATT_TQ = 512
ATT_SUB = 256


def _attn_kernel(*refs, n_kt, tq, tk, with_cache):
    if with_cache:
        qt_ref, kp_ref, vtp_ref, k_ref, vt_ref, o_ref, ot_ref, s_ref = refs
    else:
        qt_ref, k_ref, vt_ref, o_ref, ot_ref, s_ref = refs

    def k_tile(kt, h):
        cols = slice(h * QK_PAD, (h + 1) * QK_PAD)
        if with_cache:
            return kp_ref[:, cols] if kt == 0 else k_ref[kt - 1, :, cols]
        return k_ref[kt, :, cols]

    def vt_tile(kt, h):
        rows = slice(h * V_DIM, (h + 1) * V_DIM)
        if with_cache:
            return vtp_ref[rows, :] if kt == 0 else vt_ref[kt - 1, rows, :]
        return vt_ref[kt, rows, :]

    sub = min(ATT_SUB, tk)

    def scores_step(h, kt, m8):
        q_t = qt_ref[h * QK_PAD:(h + 1) * QK_PAD, :]
        k = k_tile(kt, h)
        for r in range(0, tk, sub):
            s = jnp.dot(k[r:r + sub, :], q_t, preferred_element_type=F32)
            s_ref[h % 2, kt, r:r + sub, :] = s
            m8 = jnp.maximum(m8, jnp.max(s.reshape(sub // SUBLANE, SUBLANE, tq), axis=0))
        return m8

    def probs_step(h, kt, m, l8, acc):
        v_t = vt_tile(kt, h)
        for r in range(0, tk, sub):
            pr = jnp.exp2(s_ref[h % 2, kt, r:r + sub, :] - m)
            l8 = l8 + jnp.sum(pr.reshape(sub // SUBLANE, SUBLANE, tq), axis=0)
            acc = acc + jnp.dot(v_t[:, r:r + sub], pr.astype(BF16), preferred_element_type=F32)
        return l8, acc

    m8_init = jnp.full((SUBLANE, tq), NEG_BIG, F32)
    m8 = m8_init
    for kt in range(n_kt):
        m8 = scores_step(0, kt, m8)
    for h in range(MLA_HEADS):
        m = jnp.max(m8, axis=0, keepdims=True)
        l8 = jnp.zeros((SUBLANE, tq), F32)
        acc = jnp.zeros((V_DIM, tq), F32)
        m8 = m8_init
        for kt in range(n_kt):
            l8, acc = probs_step(h, kt, m, l8, acc)
            if h + 1 < MLA_HEADS:
                m8 = scores_step(h + 1, kt, m8)
        ot_ref[h * V_DIM:(h + 1) * V_DIM, :] = acc / jnp.sum(l8, axis=0, keepdims=True)
    o_ref[...] = ot_ref[...].T.astype(o_ref.dtype)


def _attention(qt, k3, vt3, n_batch, lq, tk, cache=None):
    tq = min(ATT_TQ, lq)
    nq = lq // tq
    n_new = k3.shape[0] // n_batch
    k4 = k3.reshape(n_batch, n_new, tk, N_QROWS)
    v4 = vt3.reshape(n_batch, n_new, N_VROWS, tk)
    n_kt = n_new + (0 if cache is None else 1)
    kernel = functools.partial(_attn_kernel, n_kt=n_kt, tq=tq, tk=tk, with_cache=cache is not None)
    cache_specs = [] if cache is None else [
        pl.BlockSpec((None, tk, N_QROWS), lambda b, i: (b, 0, 0)),
        pl.BlockSpec((None, N_VROWS, tk), lambda b, i: (b, 0, 0)),
    ]
    return pl.pallas_call(
        kernel,
        grid=(n_batch, nq),
        in_specs=[pl.BlockSpec((N_QROWS, tq), lambda b, i: (0, b * nq + i))] + cache_specs + [
            pl.BlockSpec((None, n_new, tk, N_QROWS), lambda b, i: (b, 0, 0, 0)),
            pl.BlockSpec((None, n_new, N_VROWS, tk), lambda b, i: (b, 0, 0, 0)),
        ],
        out_specs=pl.BlockSpec((tq, N_VROWS), lambda b, i: (b * nq + i, 0)),
        out_shape=jax.ShapeDtypeStruct((n_batch * lq, N_VROWS), BF16),
        scratch_shapes=[pltpu.VMEM((N_VROWS, tq), F32), pltpu.VMEM((2, n_kt, tk, tq), F32)],
        compiler_params=_cparams(("arbitrary", "arbitrary")),
        name="attention",
    )(qt, *(() if cache is None else cache), k4, v4)


MG_TM = 512


def _merge_kernel(*refs, n_x):
    x_refs = refs[:n_x]
    zc_refs, yf_refs, yb_refs = refs[n_x:n_x + 2], refs[n_x + 2:n_x + 4], refs[n_x + 4:n_x + 6]
    (ax_ref, axp_ref, axn_ref, ac_ref, acp_ref, acn_ref, ab_ref,
     g_ref, z_ref, gm_ref,
     cw_ref, wa_ref, nw_ref, wb_ref, wc_ref, wo_ref, o_ref) = refs[n_x + 6:]
    i = pl.program_id(0)
    hp, hn = _tile_neighbours(i, MG_TM)
    all_rows = slice(0, MG_TM)

    def f32(v):
        return v.astype(F32)

    u = f32(ac_ref[...]) * f32(ax_ref[...])
    u_prev = f32(acp_ref[HALO - 1:HALO, :]) * f32(axp_ref[HALO - 1:HALO, :]) * hp
    u_next = f32(acn_ref[0:1, :]) * f32(axn_ref[0:1, :]) * hn
    za = f32(ab_ref[...]) * _conv3_tile(u, u_prev, u_next, cw_ref, MG_TM, _context_edge_masks(i, MG_TM))
    y_a = jnp.dot(za.astype(BF16), wa_ref[...], preferred_element_type=F32)

    y_scan = f32(_stream_rows(yf_refs, i, MG_TM, all_rows)) + f32(_stream_rows(yb_refs, i, MG_TM, all_rows))
    yb = y_scan * _silu(f32(z_ref[...]))
    zb = _rms(yb, nw_ref[...])
    y_b = jnp.dot(zb.astype(BF16), wb_ref[...], preferred_element_type=F32)

    y_c = jnp.dot(_stream_rows(zc_refs, i, MG_TM, all_rows), wc_ref[...], preferred_element_type=F32)

    merged = (_sigmoid(f32(g_ref[:, 0:D_MODEL])) * y_a
              + _sigmoid(f32(g_ref[:, D_MODEL:2 * D_MODEL])) * y_b
              + _sigmoid(f32(g_ref[:, 2 * D_MODEL:3 * D_MODEL])) * y_c)
    o = jnp.dot(merged.astype(BF16), wo_ref[...], preferred_element_type=F32)
    o_ref[...] = _stream_rows(x_refs, i, MG_TM, all_rows) + gm_ref[...] * o


def _merge(p, yfs, ybs, zcs, xs, mod_l, conv_w, wa, nw, wb, wc, wo, layer):
    halo_per_tile = MG_TM // HALO
    n_halo_blocks = N_TOK // HALO
    row = functools.partial(_mod_row, tile_rows=MG_TM)

    def pcol(width, off):
        return pl.BlockSpec((MG_TM, width), lambda i: (i, off // width))

    def pprev(off):
        return pl.BlockSpec((HALO, A_WIDTH), lambda i: (jnp.maximum(i * halo_per_tile - 1, 0), off // A_WIDTH))

    def pnext(off):
        return pl.BlockSpec((HALO, A_WIDTH),
                            lambda i: (jnp.minimum((i + 1) * halo_per_tile, n_halo_blocks - 1), off // A_WIDTH))

    def full(a):
        return pl.BlockSpec(a.shape, lambda i: (0,) * a.ndim)

    return pl.pallas_call(
        functools.partial(_merge_kernel, n_x=len(xs)),
        grid=(N_TOK // MG_TM,),
        in_specs=(_stream_specs(xs, MG_TM, 1) + _stream_specs(zcs, MG_TM, 1)
                  + _stream_specs(yfs, MG_TM, 1) + _stream_specs(ybs, MG_TM, 1)) + [
            pcol(A_WIDTH, OFF_AX), pprev(OFF_AX), pnext(OFF_AX),
            pcol(A_WIDTH, OFF_AC), pprev(OFF_AC), pnext(OFF_AC),
            pcol(A_WIDTH, OFF_AB),
            pcol(3 * D_MODEL, OFF_G),
            pcol(SSM_INNER, OFF_Z),
            pl.BlockSpec((None, None, 1, D_MODEL), lambda i: (row(i), 2, 0, 0)),
            full(conv_w), _layer_spec(wa, layer, 1), full(nw), _layer_spec(wb, layer, 1),
            _layer_spec(wc, layer, 1), _layer_spec(wo, layer, 1),
        ],
        out_specs=pl.BlockSpec((MG_TM, D_MODEL), lambda i: (i, 0)),
        out_shape=jax.ShapeDtypeStruct((N_TOK, D_MODEL), F32),
        compiler_params=_cparams(("arbitrary",)),
        name="merge",
    )(*xs, *zcs, *yfs, *ybs, p, p, p, p, p, p, p, p, p, mod_l, conv_w, wa, nw, wb, wc, wo)


FF_TM = 512
FF_CHUNK = 256


def _ffn_kernel(*refs, final):
    x_ref, nw_ref, sh_ref, sc_ref, gm_ref, w1_ref, w3_ref, w2_ref = refs[:8]
    if final:
        fw_ref, oc_ref, ol_ref, h_ref, g_ref = refs[8:]
    else:
        o_ref, h_ref, g_ref = refs[8:]
    _modulated_norm_to(h_ref, lambda rows: x_ref[rows, :], nw_ref, sc_ref, sh_ref, FF_TM)
    h = h_ref[...]
    for c in range(0, FF_DIM, FF_CHUNK):
        a = jnp.dot(h, w1_ref[:, c:c + FF_CHUNK], preferred_element_type=F32)
        b = jnp.dot(h, w3_ref[:, c:c + FF_CHUNK], preferred_element_type=F32)
        g_ref[:, c:c + FF_CHUNK] = (_silu(a) * b).astype(BF16)
    ff = jnp.dot(g_ref[...], w2_ref[...], preferred_element_type=F32)
    x_new = x_ref[...] + gm_ref[...] * ff
    if final:
        y = _rms(x_new, fw_ref[...])
        ol_ref[...] = y

        @pl.when(pl.program_id(0) < N_CTX_TOK // FF_TM)
        def _():
            oc_ref[...] = y
    else:
        o_ref[...] = x_new


def _ffn(x, mod_l, norm_w, w1, w3, w2, layer, final_w=None):
    row = functools.partial(_mod_row, tile_rows=FF_TM)
    final = final_w is not None
    n_ctx_tiles = N_CTX_TOK // FF_TM
    if final:
        extra_in = [pl.BlockSpec((1, D_MODEL), lambda i: (0, 0))]
        out_specs = [pl.BlockSpec((FF_TM, D_MODEL), lambda i: (jnp.minimum(i, n_ctx_tiles - 1), 0)),
                     pl.BlockSpec((FF_TM, D_MODEL), lambda i: (jnp.maximum(i - n_ctx_tiles, 0), 0))]
        out_shape = [jax.ShapeDtypeStruct((N_CTX_TOK, D_MODEL), F32), jax.ShapeDtypeStruct((N_LAT_TOK, D_MODEL), F32)]
    else:
        extra_in = []
        out_specs = pl.BlockSpec((FF_TM, D_MODEL), lambda i: (i, 0))
        out_shape = jax.ShapeDtypeStruct((N_TOK, D_MODEL), F32)
    return pl.pallas_call(
        functools.partial(_ffn_kernel, final=final),
        grid=(N_TOK // FF_TM,),
        in_specs=[
            pl.BlockSpec((FF_TM, D_MODEL), lambda i: (i, 0)),
            pl.BlockSpec((1, D_MODEL), lambda i: (0, 0)),
            pl.BlockSpec((None, None, 1, D_MODEL), lambda i: (row(i), 3, 0, 0)),
            pl.BlockSpec((None, None, 1, D_MODEL), lambda i: (row(i), 4, 0, 0)),
            pl.BlockSpec((None, None, 1, D_MODEL), lambda i: (row(i), 5, 0, 0)),
            _layer_spec(w1, layer, 1), _layer_spec(w3, layer, 1), _layer_spec(w2, layer, 1),
        ] + extra_in,
        out_specs=out_specs,
        out_shape=out_shape,
        scratch_shapes=[pltpu.VMEM((FF_TM, D_MODEL), BF16), pltpu.VMEM((FF_TM, FF_DIM), BF16)],
        compiler_params=_cparams(("arbitrary",)),
        name="ffn",
    )(x, norm_w, mod_l, mod_l, mod_l, w1, w3, w2, *(() if final_w is None else (final_w,)))


def _pad_in_weights(w_in):
    src_ax, src_z, src_bc, src_dt, src_cq, src_kr, src_g = 0, 1536, 3584, 4096, 4112, 4624, 4656
    half = ROPE_DIM // 2
    pieces = [
        (OFF_G, src_g, 3 * D_MODEL),
        (OFF_Z, src_z, 2 * SSM_INNER),
        (OFF_AX, src_ax, 3 * A_WIDTH),
        (OFF_BC, src_bc, BC_WIDTH),
        (OFF_CQ, src_cq, Q_LORA + KV_LORA),
    ]
    w16 = w_in.astype(BF16)

    def z(n):
        return jnp.zeros(w_in.shape[:-1] + (n,), BF16)

    dt, kr = w16[..., src_dt:src_dt + SSM_HEADS], w16[..., src_kr:src_kr + ROPE_DIM]
    narrow = jnp.concatenate([dt, z(LANE - SSM_HEADS),
                              z(64), kr, z(LANE - 64 - ROPE_DIM),
                              z(64), kr[..., half:], kr[..., :half], z(LANE - 64 - ROPE_DIM)], axis=-1)
    out = jnp.zeros(w_in.shape[:-1] + (NP,), BF16)
    for dst, src, width in pieces:
        out = out.at[..., dst:dst + width].set(w16[..., src:src + width])
    return out.at[..., OFF_DT:OFF_DT + 3 * LANE].set(narrow)


def _q_weights_t(w_uq):
    w = w_uq.reshape(DEPTH, Q_LORA, MLA_HEADS, QK_DIM)
    nope, x1, x2 = w[..., :NOPE_DIM], w[..., NOPE_DIM:NOPE_DIM + 16], w[..., NOPE_DIM + 16:]
    z32 = jnp.zeros_like(w[..., :32])
    z64 = jnp.zeros_like(nope)
    q = jnp.concatenate([nope, x1, x2, z32], axis=-1).reshape(DEPTH, Q_LORA, N_QROWS)
    qs = jnp.concatenate([z64, x2, x1, z32], axis=-1).reshape(DEPTH, Q_LORA, N_QROWS)
    return jnp.swapaxes(jnp.concatenate([q, qs], axis=-1), 1, 2).astype(BF16)


def _kv_weights(w_ukv):
    w = w_ukv.reshape(DEPTH, KV_LORA, MLA_HEADS, NOPE_DIM + V_DIM)
    kn = jnp.concatenate([w[..., :NOPE_DIM], jnp.zeros_like(w[..., :QK_PAD - NOPE_DIM])], axis=-1)
    wk = kn.reshape(DEPTH, KV_LORA, N_QROWS).astype(BF16)
    wv_t = jnp.swapaxes(w[..., NOPE_DIM:].reshape(DEPTH, KV_LORA, N_VROWS), 1, 2).astype(BF16)
    return wk, wv_t


def _rope_tables(n_tokens, lead_rows):
    n_rows = n_tokens // GRID_W
    row = jnp.repeat(jnp.arange(n_rows, dtype=F32), GRID_W)
    col = jnp.tile(jnp.arange(GRID_W, dtype=F32), n_rows)
    pairs = ROPE_DIM // 4
    inv = ROPE_BASE ** (-jnp.arange(pairs, dtype=F32) / pairs)
    ang = jnp.concatenate([row[:, None] * inv, col[:, None] * inv], axis=-1)
    cos, sin = jnp.cos(ang), jnp.sin(ang)
    ones = jnp.ones((n_tokens, NOPE_DIM), F32)
    z32 = jnp.zeros((n_tokens, 32), F32)
    cos_l = jnp.concatenate([ones, cos, cos, z32], axis=-1)
    sin_l = jnp.concatenate([jnp.zeros_like(ones), -sin, sin, z32], axis=-1)
    ident_c = jnp.concatenate([jnp.ones((lead_rows, NOPE_DIM + ROPE_DIM), F32), jnp.zeros((lead_rows, 32), F32)], -1)
    ident_s = jnp.zeros((lead_rows, LANE), F32)
    return jnp.concatenate([ident_c, cos_l], axis=0), jnp.concatenate([ident_s, sin_l], axis=0)


PREP_TM_CTX = SEQ
PREP_TM_LAT = 512


def kernel(x_prompt, x_sample, c, cache_ckv, cache_krope, state_ssm_fwd, state_ssm_bwd, c_ctx, w_in, a_conv_w, w_a_out, ssm_conv_w, ssm_conv_b, ssm_a_log, ssm_dt_bias, ssm_d, ssm_norm_w, w_b_out, q_norm_w, w_uq, kv_norm_w, w_ukv, w_c_out, w_o, w_ada, b_ada, norm1_w, norm2_w, w_ff1, w_ff3, w_ff2, final_norm_w):
    w_in_p = _pad_in_weights(w_in)
    wq_t = _q_weights_t(w_uq)
    wk, wv_t = _kv_weights(w_ukv)
    wa, wb, wc, wo = (w.astype(BF16) for w in (w_a_out, w_b_out, w_c_out, w_o))
    w1, w3, w2 = (w.astype(BF16) for w in (w_ff1, w_ff3, w_ff2))

    cond = jnp.concatenate([c_ctx[None, :], c, jnp.zeros((N_MOD_ROWS - 1 - DEC_BATCH, D_MODEL), F32)], axis=0)
    mod = _modulation(cond, w_ada, b_ada).reshape(DEPTH, N_MOD_ROWS, 6, 1, D_MODEL)

    conv_wx = ssm_conv_w[..., :SSM_INNER]
    conv_wbc = ssm_conv_w[..., SSM_INNER:]
    conv_bx = ssm_conv_b[:, None, :SSM_INNER]
    conv_bbc = ssm_conv_b[:, None, SSM_INNER:]
    pad_h = ((0, 0), (0, 0), (0, 0), (0, LANE - SSM_HEADS))
    alog = jnp.pad(ssm_a_log[:, :, None, :], pad_h)
    dtb = jnp.pad(ssm_dt_bias[:, :, None, :], pad_h)
    dskip = jnp.repeat(ssm_d, SSM_HEAD_DIM, axis=-1)[:, :, None, :]
    h0f = state_ssm_fwd.reshape(DEC_BATCH, DEPTH, SSM_INNER, SSM_STATE)
    h0b = state_ssm_bwd.reshape(DEC_BATCH, DEPTH, SSM_INNER, SSM_STATE)

    cos_c, sin_c = _rope_tables(DEC_SEQ, PREP_TM_CTX)
    cos_l, sin_l = cos_c[PREP_TM_CTX:], sin_c[PREP_TM_CTX:]
    ident_cos, ident_sin = cos_c[:PREP_TM_CTX], sin_c[:PREP_TM_CTX]
    ones_tab = jnp.concatenate([ident_cos, ident_cos], axis=0)
    zeros_tab = jnp.zeros_like(ones_tab)
    cache_kr_pad = jnp.pad(cache_krope, ((0, 0), (0, 0), (0, 0), (64, 32)))

    xs = (x_prompt.reshape(N_CTX_TOK, D_MODEL), x_sample.reshape(N_LAT_TOK, D_MODEL))

    new_ckv, new_kr = [], []
    fin = None
    for l in range(DEPTH):
        p, ps = _in_projection(xs, mod[l], norm1_w[l][None, :], w_in_p, l)

        xc, bcc = _ssd_conv(p, conv_wx[l], conv_bx[l], conv_wbc[l], conv_bbc[l])
        yc_f, yc_b, *fin = _ssd(ps, xc, bcc, alog[l], dtb[l], dskip[l],
                                row0=0, n_seq=BATCH, seq_len=SEQ, tq=SEQ, fin_layer=l, fin_prev=fin)
        yl_f, yl_b = _ssd(ps, xc, bcc, alog[l], dtb[l], dskip[l],
                          row0=N_CTX_TOK, n_seq=DEC_BATCH, seq_len=DEC_SEQ, tq=SSD_TQ_LAT, h0=(h0f, h0b, l))

        qnw, kvnw = q_norm_w[l][None, :], kv_norm_w[l][None, :]
        qt_c, k_c, vt_c, ckv_c = _mla_prep_tokens(
            ps, 0, BATCH, SEQ, PREP_TM_CTX, ident_cos, ident_sin, ident_cos.T, ident_sin.T,
            qnw, kvnw, wq_t, wk, wv_t, l, keep_ckv=True)
        qt_l, k_l, vt_l = _mla_prep_tokens(
            ps, N_CTX_TOK, DEC_BATCH, DEC_SEQ, PREP_TM_LAT, cos_l, sin_l, cos_l.T, sin_l.T,
            qnw, kvnw, wq_t, wk, wv_t, l, keep_ckv=False)
        k_p, vt_p = _mla_prep_cache(cache_ckv[:, l].reshape(DEC_BATCH * PAST_LEN, KV_LORA),
                                    cache_kr_pad[:, l].reshape(DEC_BATCH * PAST_LEN, LANE),
                                    ones_tab, zeros_tab, kvnw, wk, wv_t, PAST_LEN, l)
        new_ckv.append(ckv_c.reshape(BATCH, SEQ, KV_LORA))
        kr0 = OFF_KR - NP_MAIN + 64
        new_kr.append(ps[:N_CTX_TOK, kr0:kr0 + ROPE_DIM].reshape(BATCH, SEQ, ROPE_DIM))

        zc_c = _attention(qt_c, k_c, vt_c, BATCH, SEQ, SEQ)
        zc_l = _attention(qt_l, k_l, vt_l, DEC_BATCH, DEC_SEQ, PREP_TM_LAT, cache=(k_p, vt_p))

        x = _merge(p, (yc_f, yl_f), (yc_b, yl_b), (zc_c, zc_l), xs, mod[l], a_conv_w[l], wa, ssm_norm_w[l][None, :],
                   wb, wc, wo, l)
        if l + 1 < DEPTH:
            xs = (_ffn(x, mod[l], norm2_w[l][None, :], w1, w3, w2, l),)
        else:
            y_ctx, y_lat = _ffn(x, mod[l], norm2_w[l][None, :], w1, w3, w2, l, final_w=final_norm_w[None, :])

    y_prompt = y_ctx.reshape(BATCH, SEQ, D_MODEL)
    y_sample = y_lat.reshape(DEC_BATCH, DEC_SEQ, D_MODEL)
    hshape = (BATCH, DEPTH, SSM_HEADS, SSM_HEAD_DIM, SSM_STATE)
    return (y_prompt, y_sample,
            jnp.stack(new_ckv, axis=1), jnp.stack(new_kr, axis=1),
            fin[0].reshape(hshape), fin[1].reshape(hshape))
```

```python
import functools
import math

import jax
import jax.numpy as jnp
import numpy as np
from jax import lax
from jax.experimental import pallas as pl
from jax.experimental.pallas import tpu as pltpu

F32 = jnp.float32
BF16 = jnp.bfloat16

D_MODEL = 1024
BATCH = 16
SEQ = 256
DEPTH = 4
DEC_BATCH = 4
DEC_SEQ = 4096
PAST_LEN = 512
GRID_W = 64
EPS = 1e-6
A_WIDTH = 512
SSM_INNER = 1024
SSM_HEAD_DIM = 64
SSM_HEADS = 16
SSM_GROUPS = 2
SSM_STATE = 128
CHUNK = 128
MLA_HEADS = 8
Q_LORA = 256
KV_LORA = 256
NOPE_DIM = 64
ROPE_DIM = 32
V_DIM = 64
QK_DIM = NOPE_DIM + ROPE_DIM
ROPE_BASE = 10000.0
FF_DIM = 2816

N_CTX_TOK = BATCH * SEQ
N_LAT_TOK = DEC_BATCH * DEC_SEQ
N_TOK = N_CTX_TOK + N_LAT_TOK
N_MOD_ROWS = 8

LANE = 128
SUBLANE = 8
VMEM_LIMIT = 56 * 1024 * 1024

OFF_G = 0
OFF_Z = 3072
OFF_SX = 4096
OFF_AX = 5120
OFF_AB = 5632
OFF_AC = 6144
OFF_BC = 6656
NP_MAIN = 7168
OFF_CQ = 7168
OFF_CKV = 7424
OFF_DT = 7680
OFF_KR = 7808
OFF_KRS = 7936
NP = 8192
N_MAIN_TILES = 7

NEG_BIG = -1e30


def _cparams(sem):
    return pltpu.CompilerParams(dimension_semantics=sem, vmem_limit_bytes=VMEM_LIMIT)


def _rms(x, w):
    ms = jnp.mean(x * x, axis=-1, keepdims=True)
    return x * lax.rsqrt(ms + EPS) * w


def _sigmoid(x):
    return 0.5 * jnp.tanh(0.5 * x) + 0.5


def _silu(x):
    h = 0.5 * x
    return h * jnp.tanh(h) + h


def _layer_spec(w, layer, grid_rank):
    zeros = (0,) * (w.ndim - 1)
    imap = (lambda i: (layer,) + zeros) if grid_rank == 1 else (lambda i, j: (layer,) + zeros)
    return pl.BlockSpec((None,) + tuple(w.shape[1:]), imap)


def _mod_row(tile, tile_rows):
    n_ctx_tiles = N_CTX_TOK // tile_rows
    tiles_per_lat = DEC_SEQ // tile_rows
    return jnp.where(tile < n_ctx_tiles, 0, 1 + (tile - n_ctx_tiles) // tiles_per_lat)


MOD_TN = 1536


def _mod_kernel(c_ref, w_ref, b_ref, o_ref):
    c = c_ref[...]
    s = _silu(c).astype(BF16)
    o_ref[...] = jnp.dot(s, w_ref[...].astype(BF16), preferred_element_type=F32) + b_ref[...]


def _modulation(cond, w_ada, b_ada):
    n_col = 6 * D_MODEL
    return pl.pallas_call(
        _mod_kernel,
        grid=(DEPTH, n_col // MOD_TN),
        in_specs=[
            pl.BlockSpec((N_MOD_ROWS, D_MODEL), lambda l, j: (0, 0)),
            pl.BlockSpec((None, D_MODEL, MOD_TN), lambda l, j: (l, 0, j)),
            pl.BlockSpec((None, 1, MOD_TN), lambda l, j: (l, 0, j)),
        ],
        out_specs=pl.BlockSpec((None, N_MOD_ROWS, MOD_TN), lambda l, j: (l, 0, j)),
        out_shape=jax.ShapeDtypeStruct((DEPTH, N_MOD_ROWS, n_col), F32),
        compiler_params=_cparams(("arbitrary", "arbitrary")),
        name="modulation",
    )(cond, w_ada, b_ada.reshape(DEPTH, 1, n_col))


IN_TM = 2048
IN_TN = 1024
NORM_ROWS = 256


def _stream_specs(xs, tm, grid_rank):
    def imap(f):
        return (lambda i: f(i)) if grid_rank == 1 else (lambda i, j: f(i))

    width = xs[0].shape[1]
    if len(xs) == 1:
        return [pl.BlockSpec((tm, width), imap(lambda i: (i, 0)))]
    n_ctx_tiles = N_CTX_TOK // tm
    return [pl.BlockSpec((tm, width), imap(lambda i: (jnp.minimum(i, n_ctx_tiles - 1), 0))),
            pl.BlockSpec((tm, width), imap(lambda i: (jnp.maximum(i - n_ctx_tiles, 0), 0)))]


def _stream_rows(x_refs, tile, tm, rows):
    if len(x_refs) == 1:
        return x_refs[0][rows, :]
    return jnp.where(tile < N_CTX_TOK // tm, x_refs[0][rows, :], x_refs[1][rows, :])


def _modulated_norm_to(h_ref, x_rows, nw_ref, sc_ref, sh_ref, rows):
    for r in range(0, rows, NORM_ROWS):
        x = x_rows(slice(r, r + NORM_ROWS))
        h = _rms(x, nw_ref[...]) * (1.0 + sc_ref[...]) + sh_ref[...]
        h_ref[r:r + NORM_ROWS, :] = h.astype(BF16)


def _inproj_kernel(*refs, n_x, tm):
    x_refs = refs[:n_x]
    nw_ref, sh_ref, sc_ref, w_ref, om_ref, os_ref, h_ref = refs[n_x:]
    i = pl.program_id(0)
    j = pl.program_id(1)

    @pl.when(j == 0)
    def _():
        _modulated_norm_to(h_ref, lambda rows: _stream_rows(x_refs, i, tm, rows),
                           nw_ref, sc_ref, sh_ref, tm)

    @pl.when(j < N_MAIN_TILES)
    def _():
        om_ref[...] = jnp.dot(h_ref[...], w_ref[...], preferred_element_type=F32).astype(BF16)

    @pl.when(j == N_MAIN_TILES)
    def _():
        os_ref[...] = jnp.dot(h_ref[...], w_ref[...], preferred_element_type=F32)


def _in_projection(xs, mod_l, norm_w, w_in_p, layer):
    tm = IN_TM if len(xs) == 1 else IN_TM // 2
    row = functools.partial(_mod_row, tile_rows=tm)
    return pl.pallas_call(
        functools.partial(_inproj_kernel, n_x=len(xs), tm=tm),
        grid=(N_TOK // tm, NP // IN_TN),
        in_specs=_stream_specs(xs, tm, 2) + [
            pl.BlockSpec((1, D_MODEL), lambda i, j: (0, 0)),
            pl.BlockSpec((None, None, 1, D_MODEL), lambda i, j: (row(i), 0, 0, 0)),
            pl.BlockSpec((None, None, 1, D_MODEL), lambda i, j: (row(i), 1, 0, 0)),
            pl.BlockSpec((None, D_MODEL, IN_TN), lambda i, j: (layer, 0, j)),
        ],
        out_specs=[
            pl.BlockSpec((tm, IN_TN), lambda i, j: (i, jnp.minimum(j, N_MAIN_TILES - 1))),
            pl.BlockSpec((tm, IN_TN), lambda i, j: (i, 0)),
        ],
        out_shape=[
            jax.ShapeDtypeStruct((N_TOK, NP_MAIN), BF16),
            jax.ShapeDtypeStruct((N_TOK, IN_TN), F32),
        ],
        scratch_shapes=[pltpu.VMEM((tm, D_MODEL), BF16)],
        compiler_params=_cparams(("arbitrary", "arbitrary")),
        name="in_projection",
    )(*xs, norm_w, mod_l, mod_l, w_in_p)


def _conv3_tile(u, prev_row, next_row, w_ref, rows, inner_masks=None):
    ridx = lax.broadcasted_iota(jnp.int32, (SUBLANE, 1), 0)
    up = pltpu.roll(u, 1, axis=0)
    up = jnp.concatenate([jnp.where(ridx == 0, prev_row, up[0:SUBLANE]), up[SUBLANE:]], axis=0)
    dn = pltpu.roll(u, rows - 1, axis=0)
    dn = jnp.concatenate([dn[:rows - SUBLANE], jnp.where(ridx == SUBLANE - 1, next_row, dn[rows - SUBLANE:])],
                         axis=0)
    if inner_masks is not None:
        up = up * inner_masks[0]
        dn = dn * inner_masks[1]
    return up * w_ref[0:1, :] + u * w_ref[1:2, :] + dn * w_ref[2:3, :]


CONV_TQ = 1024
SSD_TQ_LAT = 512
HALO = 2 * SUBLANE
BC_WIDTH = 2 * SSM_GROUPS * SSM_STATE


def _tile_neighbours(i, tile_rows):
    n_ctx_tiles = N_CTX_TOK // tile_rows
    tiles_per_lat = DEC_SEQ // tile_rows
    t_in_seq = (i - n_ctx_tiles) % tiles_per_lat
    is_lat = i >= n_ctx_tiles
    has_prev = jnp.logical_and(is_lat, t_in_seq > 0).astype(F32)
    has_next = jnp.logical_and(is_lat, t_in_seq < tiles_per_lat - 1).astype(F32)
    return has_prev, has_next


def _context_edge_masks(tile, tile_rows):
    edge_keep = jnp.where(tile < N_CTX_TOK // tile_rows, 0.0, 1.0)
    pos = lax.broadcasted_iota(jnp.int32, (tile_rows, 1), 0) & (SEQ - 1)
    return jnp.where(pos == 0, edge_keep, 1.0), jnp.where(pos == SEQ - 1, edge_keep, 1.0)


def _ssd_conv_kernel(x_ref, xp_ref, xn_ref, bc_ref, bcp_ref, bcn_ref,
                     cwx_ref, cbx_ref, cwb_ref, cbb_ref, xc_ref, bcc_ref):
    i = pl.program_id(0)
    hp, hn = _tile_neighbours(i, CONV_TQ)
    masks = _context_edge_masks(i, CONV_TQ)

    def conv_silu(u_ref, up_ref, un_ref, w_ref, b_ref):
        prev_row = up_ref[HALO - 1:HALO, :].astype(F32) * hp
        next_row = un_ref[0:1, :].astype(F32) * hn
        conv = _conv3_tile(u_ref[...].astype(F32), prev_row, next_row, w_ref, CONV_TQ, masks)
        return _silu(conv + b_ref[...])

    xc_ref[...] = conv_silu(x_ref, xp_ref, xn_ref, cwx_ref, cbx_ref)
    bcc_ref[...] = conv_silu(bc_ref, bcp_ref, bcn_ref, cwb_ref, cbb_ref)


def _ssd_conv(p, conv_wx, conv_bx, conv_wbc, conv_bbc):
    halo_per_tile = CONV_TQ // HALO
    n_halo_blocks = N_TOK // HALO
    cx, cbc = OFF_SX // SSM_INNER, OFF_BC // BC_WIDTH

    def prev_map(col):
        return lambda i: (jnp.maximum(i * halo_per_tile - 1, 0), col)

    def next_map(col):
        return lambda i: (jnp.minimum((i + 1) * halo_per_tile, n_halo_blocks - 1), col)

    def const2(i):
        return (0, 0)

    return pl.pallas_call(
        _ssd_conv_kernel,
        grid=(N_TOK // CONV_TQ,),
        in_specs=[
            pl.BlockSpec((CONV_TQ, SSM_INNER), lambda i: (i, cx)),
            pl.BlockSpec((HALO, SSM_INNER), prev_map(cx)),
            pl.BlockSpec((HALO, SSM_INNER), next_map(cx)),
            pl.BlockSpec((CONV_TQ, BC_WIDTH), lambda i: (i, cbc)),
            pl.BlockSpec((HALO, BC_WIDTH), prev_map(cbc)),
            pl.BlockSpec((HALO, BC_WIDTH), next_map(cbc)),
            pl.BlockSpec((3, SSM_INNER), const2),
            pl.BlockSpec((1, SSM_INNER), const2),
            pl.BlockSpec((3, BC_WIDTH), const2),
            pl.BlockSpec((1, BC_WIDTH), const2),
        ],
        out_specs=[
            pl.BlockSpec((CONV_TQ, SSM_INNER), lambda i: (i, 0)),
            pl.BlockSpec((CONV_TQ, BC_WIDTH), lambda i: (i, 0)),
        ],
        out_shape=[
            jax.ShapeDtypeStruct((N_TOK, SSM_INNER), F32),
            jax.ShapeDtypeStruct((N_TOK, BC_WIDTH), F32),
        ],
        compiler_params=_cparams(("arbitrary",)),
        name="ssd_conv",
    )(p, p, p, p, p, p, conv_wx, conv_bx, conv_wbc, conv_bbc)


def _ssd_tables(row0, n_seq, seq_len, tq):
    blk_f, blk_b, seq, first = [], [], [], []
    nt = seq_len // tq
    for s in range(n_seq):
        base = (row0 + s * seq_len) // tq
        for k in range(nt):
            blk_f.append(base + k)
            blk_b.append(base + nt - 1 - k)
            seq.append(s)
            first.append(int(k == 0))
    return [np.asarray(a, np.int32) for a in (blk_f, blk_b, seq, first)]


def _split3(a):
    a1 = a.astype(BF16)
    r1 = a - a1.astype(F32)
    a2 = r1.astype(BF16)
    a3 = (r1 - a2.astype(F32)).astype(BF16)
    return a1, a2, a3


def _dot3(lhs_bf16, a):
    a1, a2, a3 = _split3(a)
    return (jnp.dot(lhs_bf16, a1, preferred_element_type=F32)
            + jnp.dot(lhs_bf16, a2, preferred_element_type=F32)
            + jnp.dot(lhs_bf16, a3, preferred_element_type=F32))


def _softplus(x):
    return jnp.maximum(x, 0.0) + jnp.log1p(jnp.exp(-jnp.abs(x)))


def _ssd_kernel(*refs, n_chunks, context, n_carried=0):
    blkf_t, blkb_t, seq_t, first_t = refs[:4]
    xf_ref, bcf_ref, dtf_ref, xb_ref, bcb_ref, dtb_ref, alog_ref, dtbias_ref, dsk_ref = refs[4:13]
    if context:
        yf_ref, yb_ref, finf_ref, finb_ref, stf_ref, stb_ref = refs[13 + n_carried:]
    else:
        h0f_ref, h0b_ref, yf_ref, yb_ref, stf_ref, stb_ref = refs[13:]
    s = pl.program_id(0)
    dirs = ((xf_ref, bcf_ref, dtf_ref, stf_ref, yf_ref), (xb_ref, bcb_ref, dtb_ref, stb_ref, yb_ref))

    if context:
        stf_ref[...] = jnp.zeros_like(stf_ref)
        stb_ref[...] = jnp.zeros_like(stb_ref)
    else:
        @pl.when(first_t[s] == 1)
        def _():
            stf_ref[...] = h0f_ref[...].T
            stb_ref[...] = h0b_ref[...].T

    ii = lax.broadcasted_iota(jnp.int32, (CHUNK, CHUNK), 0)
    jj = lax.broadcasted_iota(jnp.int32, (CHUNK, CHUNK), 1)
    masks = (jj <= ii, jj >= ii)
    masks_b = tuple(jnp.where(m, 1.0, 0.0).astype(BF16) for m in masks)
    masks_neg = tuple(jnp.where(m, 0.0, NEG_BIG) for m in masks)
    lo = jj < SSM_HEAD_DIM

    e_r = lax.broadcasted_iota(jnp.int32, (LANE, SSM_INNER), 0)
    e_c = lax.broadcasted_iota(jnp.int32, (LANE, SSM_INNER), 1)
    expand = jnp.where(jnp.right_shift(e_c, 6) == e_r, 1.0, 0.0).astype(BF16)

    def chunk_setup(d, c):
        _, _, dt_ref, _, _ = dirs[d]
        rows = slice(c * CHUNK, (c + 1) * CHUNK)
        a_row = -jnp.exp(alog_ref[d])
        dt = _softplus(dt_ref[rows, :] + dtbias_ref[d])
        cum = _dot3(masks_b[d], dt * a_row)
        cum_t = cum.T
        dt_t = dt.T
        end = CHUNK - 1 if d == 0 else 0
        tot_row = cum[end:end + 1, :]
        tot_col = cum_t[:, end:end + 1]
        return dict(
            rows=rows, cum=cum, cum_t=cum_t, dt_t=dt_t,
            w_rows=jnp.exp(tot_col - cum_t) * dt_t,
            sdec=_dot3_rows(jnp.exp(tot_row), expand))

    def group_setup(d, cs, g):
        _, bc_ref, _, _, _ = dirs[d]
        b_g = bc_ref[cs["rows"], g * SSM_STATE:(g + 1) * SSM_STATE]
        c_g = bc_ref[cs["rows"], (SSM_GROUPS + g) * SSM_STATE:(SSM_GROUPS + g + 1) * SSM_STATE].astype(BF16)
        b_gt = b_g.T
        cb = jnp.dot(c_g, b_gt.astype(BF16), preferred_element_type=F32)
        return c_g, b_gt, cb

    def pair_step(d, cs, gs, kp):
        x_ref, _, _, st_ref, y_ref = dirs[d]
        c_g, b_gt, cb = gs
        cum, cum_t, dt_t = cs["cum"], cs["cum_t"], cs["dt_t"]
        h_a, h_b = 2 * kp, 2 * kp + 1
        ls = slice(kp * LANE, (kp + 1) * LANE)
        x_pair = x_ref[cs["rows"], ls]
        rhs = jnp.concatenate([jnp.where(lo, x_pair, 0.0).astype(BF16),
                               jnp.where(lo, 0.0, x_pair).astype(BF16)], axis=0)

        def head_lhs(h):
            col = jnp.broadcast_to(cum[:, h:h + 1], (CHUNK, CHUNK))
            seg = col - cum_t[h:h + 1, :]
            dec = jnp.exp(seg + masks_neg[d])
            w_intra = dec * cb * dt_t[h:h + 1, :]
            w_state = b_gt * cs["w_rows"][h:h + 1, :]
            return w_intra.astype(BF16), w_state.astype(BF16), col

        wi_a, ws_a, col_a = head_lhs(h_a)
        wi_b, ws_b, col_b = head_lhs(h_b)
        lhs = jnp.concatenate([jnp.concatenate([wi_a, wi_b], axis=1),
                               jnp.concatenate([ws_a, ws_b], axis=1)], axis=0)
        both = jnp.dot(lhs, rhs, preferred_element_type=F32)
        y_diag = both[0:CHUNK, :]
        d_state = both[CHUNK:2 * CHUNK, :]

        h_pair = st_ref[:, ls]
        y_off = jnp.dot(c_g, h_pair.astype(BF16), preferred_element_type=F32)
        e_pair = jnp.exp(jnp.where(lo, col_a, col_b))
        y_ref[cs["rows"], ls] = (y_diag + y_off * e_pair + x_pair * dsk_ref[d, :, ls]).astype(y_ref.dtype)
        st_ref[:, ls] = h_pair * cs["sdec"][:, ls] + d_state

    pairs_per_group = SSM_HEADS // SSM_GROUPS // 2
    setups = [(chunk_setup(0, k), chunk_setup(1, n_chunks - 1 - k)) for k in range(n_chunks)]
    for k in range(n_chunks):
        cs = setups[k]
        for g in range(SSM_GROUPS):
            gs = (group_setup(0, cs[0], g), group_setup(1, cs[1], g))
            for kp in range(g * pairs_per_group, (g + 1) * pairs_per_group):
                pair_step(0, cs[0], gs[0], kp)
                pair_step(1, cs[1], gs[1], kp)

    if context:
        finf_ref[...] = stf_ref[...].T
        finb_ref[...] = stb_ref[...].T


def _dot3_rows(row, rhs_bf16):
    r8 = jnp.broadcast_to(row, (SUBLANE, row.shape[1]))
    r1, r2, r3 = _split3(r8)
    out = (jnp.dot(r1, rhs_bf16, preferred_element_type=F32)
           + jnp.dot(r2, rhs_bf16, preferred_element_type=F32)
           + jnp.dot(r3, rhs_bf16, preferred_element_type=F32))
    return out[0:1, :]


def _ssd(ps, xc, bcc, alog, dtb, dskip, *, row0, n_seq, seq_len, tq, h0=None, fin_layer=None, fin_prev=None):
    context = h0 is None
    assert not context or seq_len == tq
    tables = [jnp.asarray(t) for t in _ssd_tables(row0, n_seq, seq_len, tq)]
    n_steps = int(tables[0].shape[0])
    n_rows = n_seq * seq_len
    blk0 = row0 // tq
    cdt = (OFF_DT - NP_MAIN) // LANE

    def fwd_tile(col, off=0):
        return lambda s, blkf, *_: (blkf[s] - off, col)

    def bwd_tile(col, off=0):
        return lambda s, blkf, blkb, *_: (blkb[s] - off, col)

    def tile_specs(tile):
        return [
            pl.BlockSpec((tq, SSM_INNER), tile(0)),
            pl.BlockSpec((tq, BC_WIDTH), tile(0)),
            pl.BlockSpec((tq, LANE), tile(cdt)),
        ]

    def const3(s, *_):
        return (0, 0, 0)

    in_specs = tile_specs(fwd_tile) + tile_specs(bwd_tile) + [
        pl.BlockSpec((2, 1, LANE), const3),
        pl.BlockSpec((2, 1, LANE), const3),
        pl.BlockSpec((2, 1, SSM_INNER), const3),
    ]
    out_specs = [pl.BlockSpec((tq, SSM_INNER), fwd_tile(0, blk0)), pl.BlockSpec((tq, SSM_INNER), bwd_tile(0, blk0))]
    out_shape = [jax.ShapeDtypeStruct((n_rows, SSM_INNER), BF16), jax.ShapeDtypeStruct((n_rows, SSM_INNER), BF16)]
    operands = [xc, bcc, ps, xc, bcc, ps, alog, dtb, dskip]
    aliases = {}
    if context:
        def fin_map(s, blkf, blkb, seq, *_):
            return (seq[s], fin_layer, 0, 0)

        out_specs += [pl.BlockSpec((None, None, SSM_INNER, SSM_STATE), fin_map)] * 2
        out_shape += [jax.ShapeDtypeStruct((n_seq, DEPTH, SSM_INNER, SSM_STATE), F32)] * 2
        if fin_prev is not None:
            n_before = len(tables) + len(operands)
            in_specs += [pl.BlockSpec(memory_space=pl.ANY)] * 2
            operands += list(fin_prev)
            aliases = {n_before: 2, n_before + 1: 3}
    else:
        h0f, h0b, layer = h0

        def h0_map(s, blkf, blkb, seq, *_):
            return (seq[s], layer, 0, 0)

        in_specs += [pl.BlockSpec((None, None, SSM_INNER, SSM_STATE), h0_map)] * 2
        operands += [h0f, h0b]

    grid_spec = pltpu.PrefetchScalarGridSpec(
        num_scalar_prefetch=4,
        grid=(n_steps,),
        in_specs=in_specs,
        out_specs=out_specs,
        scratch_shapes=[
            pltpu.VMEM((SSM_STATE, SSM_INNER), F32),
            pltpu.VMEM((SSM_STATE, SSM_INNER), F32),
        ],
    )
    return pl.pallas_call(
        functools.partial(_ssd_kernel, n_chunks=tq // CHUNK, context=context, n_carried=len(aliases)),
        grid_spec=grid_spec,
        out_shape=out_shape,
        input_output_aliases=aliases,
        compiler_params=_cparams(("arbitrary",)),
        name="ssd_scan",
    )(*tables, *operands)


QK_PAD = 128
N_QROWS = MLA_HEADS * QK_PAD
N_VROWS = MLA_HEADS * V_DIM
NT_DIMS = (((1,), (1,)), ((), ()))
Q_PRESCALE = (1.0 / math.sqrt(QK_DIM)) * math.log2(math.e)


def _mla_prep_kernel(*refs, tm, normalize, with_q, keep_ckv):
    if with_q:
        (cq_ref, ckv_ref, kr_ref, krs_ref, cos_ref, sin_ref, cost_ref, sint_ref,
         qnw_ref, kvnw_ref, wq_ref, wk_ref, wv_ref, qt_ref, k_ref, vt_ref) = refs[:16]
    else:
        (ckv_ref, kr_ref, krs_ref, cos_ref, sin_ref, kvnw_ref, wk_ref, wv_ref,
         k_ref, vt_ref) = refs

    ckv = ckv_ref[...]
    if normalize:
        ckv = _rms(ckv, kvnw_ref[...])
    if keep_ckv:
        refs[16][...] = ckv
    ckv_b = ckv.astype(BF16)
    kn = jnp.dot(ckv_b, wk_ref[...], preferred_element_type=F32)
    kr = kr_ref[...] * cos_ref[...] + krs_ref[...] * sin_ref[...]
    for h in range(MLA_HEADS):
        hs = slice(h * QK_PAD, (h + 1) * QK_PAD)
        k_ref[:, hs] = (kn[:, hs] + kr).astype(BF16)
    vt_ref[...] = lax.dot_general(wv_ref[...], ckv_b, NT_DIMS, preferred_element_type=F32).astype(BF16)

    if with_q:
        cqn = _rms(cq_ref[...], qnw_ref[...]).astype(BF16)
        qq = lax.dot_general(wq_ref[...], cqn, NT_DIMS, preferred_element_type=F32)
        for h in range(MLA_HEADS):
            q_h = qq[h * QK_PAD:(h + 1) * QK_PAD, :]
            qs_h = qq[N_QROWS + h * QK_PAD:N_QROWS + (h + 1) * QK_PAD, :]
            q_rot = q_h * cost_ref[...] + qs_h * sint_ref[...]
            qt_ref[h * QK_PAD:(h + 1) * QK_PAD, :] = (q_rot * Q_PRESCALE).astype(BF16)


def _mla_prep_tokens(p, row0, n_seq, seq_len, tm, cos, sin, cos_t, sin_t, qnw, kvnw, wq_t, wk, wv_t, layer,
                     keep_ckv):
    b0 = row0 // tm
    tps = seq_len // tm
    nt = n_seq * tps
    n_rows = n_seq * seq_len
    kernel = functools.partial(_mla_prep_kernel, tm=tm, normalize=True, with_q=True, keep_ckv=keep_ckv)

    def tile(t, b):
        return b * tps + t

    def pcol(width, off):
        return pl.BlockSpec((tm, width), lambda t, b: (b0 + tile(t, b), (off - NP_MAIN) // width))

    def full(a):
        return pl.BlockSpec(a.shape, lambda t, b: (0,) * a.ndim)

    return pl.pallas_call(
        kernel,
        grid=(tps, n_seq),
        in_specs=[
            pcol(Q_LORA, OFF_CQ), pcol(KV_LORA, OFF_CKV), pcol(LANE, OFF_KR), pcol(LANE, OFF_KRS),
            pl.BlockSpec((tm, LANE), lambda t, b: (t, 0)),
            pl.BlockSpec((tm, LANE), lambda t, b: (t, 0)),
            pl.BlockSpec((LANE, tm), lambda t, b: (0, t)),
            pl.BlockSpec((LANE, tm), lambda t, b: (0, t)),
            full(qnw), full(kvnw), _layer_spec(wq_t, layer, 2), _layer_spec(wk, layer, 2), _layer_spec(wv_t, layer, 2),
        ],
        out_specs=[
            pl.BlockSpec((N_QROWS, tm), lambda t, b: (0, tile(t, b))),
            pl.BlockSpec((None, tm, N_QROWS), lambda t, b: (tile(t, b), 0, 0)),
            pl.BlockSpec((None, N_VROWS, tm), lambda t, b: (tile(t, b), 0, 0)),
        ] + ([pl.BlockSpec((tm, KV_LORA), lambda t, b: (tile(t, b), 0))] if keep_ckv else []),
        out_shape=[
            jax.ShapeDtypeStruct((N_QROWS, n_rows), BF16),
            jax.ShapeDtypeStruct((nt, tm, N_QROWS), BF16),
            jax.ShapeDtypeStruct((nt, N_VROWS, tm), BF16),
        ] + ([jax.ShapeDtypeStruct((n_rows, KV_LORA), F32)] if keep_ckv else []),
        compiler_params=_cparams(("arbitrary", "arbitrary")),
        name="mla_prep",
    )(p, p, p, p, cos, sin, cos_t, sin_t, qnw, kvnw, wq_t, wk, wv_t)


def _mla_prep_cache(ckv, kr_pad, ones_tab, zeros_tab, kvnw, wk, wv_t, tm, layer):
    n_rows = ckv.shape[0]
    nt = n_rows // tm
    kernel = functools.partial(_mla_prep_kernel, tm=tm, normalize=False, with_q=False, keep_ckv=False)

    def full(a):
        return pl.BlockSpec(a.shape, lambda i: (0,) * a.ndim)

    return pl.pallas_call(
        kernel,
        grid=(nt,),
        in_specs=[
            pl.BlockSpec((tm, KV_LORA), lambda i: (i, 0)),
            pl.BlockSpec((tm, LANE), lambda i: (i, 0)),
            pl.BlockSpec((tm, LANE), lambda i: (i, 0)),
            pl.BlockSpec((tm, LANE), lambda i: (0, 0)),
            pl.BlockSpec((tm, LANE), lambda i: (0, 0)),
            full(kvnw), _layer_spec(wk, layer, 1), _layer_spec(wv_t, layer, 1),
        ],
        out_specs=[
            pl.BlockSpec((None, tm, N_QROWS), lambda i: (i, 0, 0)),
            pl.BlockSpec((None, N_VROWS, tm), lambda i: (i, 0, 0)),
        ],
        out_shape=[
            jax.ShapeDtypeStruct((nt, tm, N_QROWS), BF16),
            jax.ShapeDtypeStruct((nt, N_VROWS, tm), BF16),
        ],
        compiler_params=_cparams(("arbitrary",)),
        name="mla_prep_cache",
    )(ckv, kr_pad, kr_pad, ones_tab, zeros_tab, kvnw, wk, wv_t)


ATT_TQ = 512
ATT_SUB = 256


def _attn_kernel(*refs, n_kt, tq, tk, with_cache):
    if with_cache:
        qt_ref, kp_ref, vtp_ref, k_ref, vt_ref, o_ref, ot_ref, s_ref = refs
    else:
        qt_ref, k_ref, vt_ref, o_ref, ot_ref, s_ref = refs

    def k_tile(kt, h):
        cols = slice(h * QK_PAD, (h + 1) * QK_PAD)
        if with_cache:
            return kp_ref[:, cols] if kt == 0 else k_ref[kt - 1, :, cols]
        return k_ref[kt, :, cols]

    def vt_tile(kt, h):
        rows = slice(h * V_DIM, (h + 1) * V_DIM)
        if with_cache:
            return vtp_ref[rows, :] if kt == 0 else vt_ref[kt - 1, rows, :]
        return vt_ref[kt, rows, :]

    sub = min(ATT_SUB, tk)

    def scores_step(h, kt, m8):
        q_t = qt_ref[h * QK_PAD:(h + 1) * QK_PAD, :]
        k = k_tile(kt, h)
        for r in range(0, tk, sub):
            s = jnp.dot(k[r:r + sub, :], q_t, preferred_element_type=F32)
            s_ref[h % 2, kt, r:r + sub, :] = s
            m8 = jnp.maximum(m8, jnp.max(s.reshape(sub // SUBLANE, SUBLANE, tq), axis=0))
        return m8

    def probs_step(h, kt, m, l8, acc):
        v_t = vt_tile(kt, h)
        for r in range(0, tk, sub):
            pr = jnp.exp2(s_ref[h % 2, kt, r:r + sub, :] - m)
            l8 = l8 + jnp.sum(pr.reshape(sub // SUBLANE, SUBLANE, tq), axis=0)
            acc = acc + jnp.dot(v_t[:, r:r + sub], pr.astype(BF16), preferred_element_type=F32)
        return l8, acc

    m8_init = jnp.full((SUBLANE, tq), NEG_BIG, F32)
    m8 = m8_init
    for kt in range(n_kt):
        m8 = scores_step(0, kt, m8)
    for h in range(MLA_HEADS):
        m = jnp.max(m8, axis=0, keepdims=True)
        l8 = jnp.zeros((SUBLANE, tq), F32)
        acc = jnp.zeros((V_DIM, tq), F32)
        m8 = m8_init
        for kt in range(n_kt):
            l8, acc = probs_step(h, kt, m, l8, acc)
            if h + 1 < MLA_HEADS:
                m8 = scores_step(h + 1, kt, m8)
        ot_ref[h * V_DIM:(h + 1) * V_DIM, :] = acc / jnp.sum(l8, axis=0, keepdims=True)
    o_ref[...] = ot_ref[...].T.astype(o_ref.dtype)


def _attention(qt, k3, vt3, n_batch, lq, tk, cache=None):
    tq = min(ATT_TQ, lq)
    nq = lq // tq
    n_new = k3.shape[0] // n_batch
    k4 = k3.reshape(n_batch, n_new, tk, N_QROWS)
    v4 = vt3.reshape(n_batch, n_new, N_VROWS, tk)
    n_kt = n_new + (0 if cache is None else 1)
    kernel = functools.partial(_attn_kernel, n_kt=n_kt, tq=tq, tk=tk, with_cache=cache is not None)
    cache_specs = [] if cache is None else [
        pl.BlockSpec((None, tk, N_QROWS), lambda b, i: (b, 0, 0)),
        pl.BlockSpec((None, N_VROWS, tk), lambda b, i: (b, 0, 0)),
    ]
    return pl.pallas_call(
        kernel,
        grid=(n_batch, nq),
        in_specs=[pl.BlockSpec((N_QROWS, tq), lambda b, i: (0, b * nq + i))] + cache_specs + [
            pl.BlockSpec((None, n_new, tk, N_QROWS), lambda b, i: (b, 0, 0, 0)),
            pl.BlockSpec((None, n_new, N_VROWS, tk), lambda b, i: (b, 0, 0, 0)),
        ],
        out_specs=pl.BlockSpec((tq, N_VROWS), lambda b, i: (b * nq + i, 0)),
        out_shape=jax.ShapeDtypeStruct((n_batch * lq, N_VROWS), BF16),
        scratch_shapes=[pltpu.VMEM((N_VROWS, tq), F32), pltpu.VMEM((2, n_kt, tk, tq), F32)],
        compiler_params=_cparams(("arbitrary", "arbitrary")),
        name="attention",
    )(qt, *(() if cache is None else cache), k4, v4)


MG_TM = 512


def _merge_kernel(*refs, n_x):
    x_refs = refs[:n_x]
    zc_refs, yf_refs, yb_refs = refs[n_x:n_x + 2], refs[n_x + 2:n_x + 4], refs[n_x + 4:n_x + 6]
    (ax_ref, axp_ref, axn_ref, ac_ref, acp_ref, acn_ref, ab_ref,
     g_ref, z_ref, gm_ref,
     cw_ref, wa_ref, nw_ref, wb_ref, wc_ref, wo_ref, o_ref) = refs[n_x + 6:]
    i = pl.program_id(0)
    hp, hn = _tile_neighbours(i, MG_TM)
    all_rows = slice(0, MG_TM)

    def f32(v):
        return v.astype(F32)

    u = f32(ac_ref[...]) * f32(ax_ref[...])
    u_prev = f32(acp_ref[HALO - 1:HALO, :]) * f32(axp_ref[HALO - 1:HALO, :]) * hp
    u_next = f32(acn_ref[0:1, :]) * f32(axn_ref[0:1, :]) * hn
    za = f32(ab_ref[...]) * _conv3_tile(u, u_prev, u_next, cw_ref, MG_TM, _context_edge_masks(i, MG_TM))
    y_a = jnp.dot(za.astype(BF16), wa_ref[...], preferred_element_type=F32)

    y_scan = f32(_stream_rows(yf_refs, i, MG_TM, all_rows)) + f32(_stream_rows(yb_refs, i, MG_TM, all_rows))
    yb = y_scan * _silu(f32(z_ref[...]))
    zb = _rms(yb, nw_ref[...])
    y_b = jnp.dot(zb.astype(BF16), wb_ref[...], preferred_element_type=F32)

    y_c = jnp.dot(_stream_rows(zc_refs, i, MG_TM, all_rows), wc_ref[...], preferred_element_type=F32)

    merged = (_sigmoid(f32(g_ref[:, 0:D_MODEL])) * y_a
              + _sigmoid(f32(g_ref[:, D_MODEL:2 * D_MODEL])) * y_b
              + _sigmoid(f32(g_ref[:, 2 * D_MODEL:3 * D_MODEL])) * y_c)
    o = jnp.dot(merged.astype(BF16), wo_ref[...], preferred_element_type=F32)
    o_ref[...] = _stream_rows(x_refs, i, MG_TM, all_rows) + gm_ref[...] * o


def _merge(p, yfs, ybs, zcs, xs, mod_l, conv_w, wa, nw, wb, wc, wo, layer):
    halo_per_tile = MG_TM // HALO
    n_halo_blocks = N_TOK // HALO
    row = functools.partial(_mod_row, tile_rows=MG_TM)

    def pcol(width, off):
        return pl.BlockSpec((MG_TM, width), lambda i: (i, off // width))

    def pprev(off):
        return pl.BlockSpec((HALO, A_WIDTH), lambda i: (jnp.maximum(i * halo_per_tile - 1, 0), off // A_WIDTH))

    def pnext(off):
        return pl.BlockSpec((HALO, A_WIDTH),
                            lambda i: (jnp.minimum((i + 1) * halo_per_tile, n_halo_blocks - 1), off // A_WIDTH))

    def full(a):
        return pl.BlockSpec(a.shape, lambda i: (0,) * a.ndim)

    return pl.pallas_call(
        functools.partial(_merge_kernel, n_x=len(xs)),
        grid=(N_TOK // MG_TM,),
        in_specs=(_stream_specs(xs, MG_TM, 1) + _stream_specs(zcs, MG_TM, 1)
                  + _stream_specs(yfs, MG_TM, 1) + _stream_specs(ybs, MG_TM, 1)) + [
            pcol(A_WIDTH, OFF_AX), pprev(OFF_AX), pnext(OFF_AX),
            pcol(A_WIDTH, OFF_AC), pprev(OFF_AC), pnext(OFF_AC),
            pcol(A_WIDTH, OFF_AB),
            pcol(3 * D_MODEL, OFF_G),
            pcol(SSM_INNER, OFF_Z),
            pl.BlockSpec((None, None, 1, D_MODEL), lambda i: (row(i), 2, 0, 0)),
            full(conv_w), _layer_spec(wa, layer, 1), full(nw), _layer_spec(wb, layer, 1),
            _layer_spec(wc, layer, 1), _layer_spec(wo, layer, 1),
        ],
        out_specs=pl.BlockSpec((MG_TM, D_MODEL), lambda i: (i, 0)),
        out_shape=jax.ShapeDtypeStruct((N_TOK, D_MODEL), F32),
        compiler_params=_cparams(("arbitrary",)),
        name="merge",
    )(*xs, *zcs, *yfs, *ybs, p, p, p, p, p, p, p, p, p, mod_l, conv_w, wa, nw, wb, wc, wo)


FF_TM = 512
FF_CHUNK = 256


def _ffn_kernel(*refs, final):
    x_ref, nw_ref, sh_ref, sc_ref, gm_ref, w1_ref, w3_ref, w2_ref = refs[:8]
    if final:
        fw_ref, oc_ref, ol_ref, h_ref, g_ref = refs[8:]
    else:
        o_ref, h_ref, g_ref = refs[8:]
    _modulated_norm_to(h_ref, lambda rows: x_ref[rows, :], nw_ref, sc_ref, sh_ref, FF_TM)
    h = h_ref[...]
    for c in range(0, FF_DIM, FF_CHUNK):
        a = jnp.dot(h, w1_ref[:, c:c + FF_CHUNK], preferred_element_type=F32)
        b = jnp.dot(h, w3_ref[:, c:c + FF_CHUNK], preferred_element_type=F32)
        g_ref[:, c:c + FF_CHUNK] = (_silu(a) * b).astype(BF16)
    ff = jnp.dot(g_ref[...], w2_ref[...], preferred_element_type=F32)
    x_new = x_ref[...] + gm_ref[...] * ff
    if final:
        y = _rms(x_new, fw_ref[...])
        ol_ref[...] = y

        @pl.when(pl.program_id(0) < N_CTX_TOK // FF_TM)
        def _():
            oc_ref[...] = y
    else:
        o_ref[...] = x_new


def _ffn(x, mod_l, norm_w, w1, w3, w2, layer, final_w=None):
    row = functools.partial(_mod_row, tile_rows=FF_TM)
    final = final_w is not None
    n_ctx_tiles = N_CTX_TOK // FF_TM
    if final:
        extra_in = [pl.BlockSpec((1, D_MODEL), lambda i: (0, 0))]
        out_specs = [pl.BlockSpec((FF_TM, D_MODEL), lambda i: (jnp.minimum(i, n_ctx_tiles - 1), 0)),
                     pl.BlockSpec((FF_TM, D_MODEL), lambda i: (jnp.maximum(i - n_ctx_tiles, 0), 0))]
        out_shape = [jax.ShapeDtypeStruct((N_CTX_TOK, D_MODEL), F32), jax.ShapeDtypeStruct((N_LAT_TOK, D_MODEL), F32)]
    else:
        extra_in = []
        out_specs = pl.BlockSpec((FF_TM, D_MODEL), lambda i: (i, 0))
        out_shape = jax.ShapeDtypeStruct((N_TOK, D_MODEL), F32)
    return pl.pallas_call(
        functools.partial(_ffn_kernel, final=final),
        grid=(N_TOK // FF_TM,),
        in_specs=[
            pl.BlockSpec((FF_TM, D_MODEL), lambda i: (i, 0)),
            pl.BlockSpec((1, D_MODEL), lambda i: (0, 0)),
            pl.BlockSpec((None, None, 1, D_MODEL), lambda i: (row(i), 3, 0, 0)),
            pl.BlockSpec((None, None, 1, D_MODEL), lambda i: (row(i), 4, 0, 0)),
            pl.BlockSpec((None, None, 1, D_MODEL), lambda i: (row(i), 5, 0, 0)),
            _layer_spec(w1, layer, 1), _layer_spec(w3, layer, 1), _layer_spec(w2, layer, 1),
        ] + extra_in,
        out_specs=out_specs,
        out_shape=out_shape,
        scratch_shapes=[pltpu.VMEM((FF_TM, D_MODEL), BF16), pltpu.VMEM((FF_TM, FF_DIM), BF16)],
        compiler_params=_cparams(("arbitrary",)),
        name="ffn",
    )(x, norm_w, mod_l, mod_l, mod_l, w1, w3, w2, *(() if final_w is None else (final_w,)))


def _pad_in_weights(w_in):
    src_ax, src_z, src_bc, src_dt, src_cq, src_kr, src_g = 0, 1536, 3584, 4096, 4112, 4624, 4656
    half = ROPE_DIM // 2
    pieces = [
        (OFF_G, src_g, 3 * D_MODEL),
        (OFF_Z, src_z, 2 * SSM_INNER),
        (OFF_AX, src_ax, 3 * A_WIDTH),
        (OFF_BC, src_bc, BC_WIDTH),
        (OFF_CQ, src_cq, Q_LORA + KV_LORA),
    ]
    w16 = w_in.astype(BF16)

    def z(n):
        return jnp.zeros(w_in.shape[:-1] + (n,), BF16)

    dt, kr = w16[..., src_dt:src_dt + SSM_HEADS], w16[..., src_kr:src_kr + ROPE_DIM]
    narrow = jnp.concatenate([dt, z(LANE - SSM_HEADS),
                              z(64), kr, z(LANE - 64 - ROPE_DIM),
                              z(64), kr[..., half:], kr[..., :half], z(LANE - 64 - ROPE_DIM)], axis=-1)
    out = jnp.zeros(w_in.shape[:-1] + (NP,), BF16)
    for dst, src, width in pieces:
        out = out.at[..., dst:dst + width].set(w16[..., src:src + width])
    return out.at[..., OFF_DT:OFF_DT + 3 * LANE].set(narrow)


def _q_weights_t(w_uq):
    w = w_uq.reshape(DEPTH, Q_LORA, MLA_HEADS, QK_DIM)
    nope, x1, x2 = w[..., :NOPE_DIM], w[..., NOPE_DIM:NOPE_DIM + 16], w[..., NOPE_DIM + 16:]
    z32 = jnp.zeros_like(w[..., :32])
    z64 = jnp.zeros_like(nope)
    q = jnp.concatenate([nope, x1, x2, z32], axis=-1).reshape(DEPTH, Q_LORA, N_QROWS)
    qs = jnp.concatenate([z64, x2, x1, z32], axis=-1).reshape(DEPTH, Q_LORA, N_QROWS)
    return jnp.swapaxes(jnp.concatenate([q, qs], axis=-1), 1, 2).astype(BF16)


def _kv_weights(w_ukv):
    w = w_ukv.reshape(DEPTH, KV_LORA, MLA_HEADS, NOPE_DIM + V_DIM)
    kn = jnp.concatenate([w[..., :NOPE_DIM], jnp.zeros_like(w[..., :QK_PAD - NOPE_DIM])], axis=-1)
    wk = kn.reshape(DEPTH, KV_LORA, N_QROWS).astype(BF16)
    wv_t = jnp.swapaxes(w[..., NOPE_DIM:].reshape(DEPTH, KV_LORA, N_VROWS), 1, 2).astype(BF16)
    return wk, wv_t


def _rope_tables(n_tokens, lead_rows):
    n_rows = n_tokens // GRID_W
    row = jnp.repeat(jnp.arange(n_rows, dtype=F32), GRID_W)
    col = jnp.tile(jnp.arange(GRID_W, dtype=F32), n_rows)
    pairs = ROPE_DIM // 4
    inv = ROPE_BASE ** (-jnp.arange(pairs, dtype=F32) / pairs)
    ang = jnp.concatenate([row[:, None] * inv, col[:, None] * inv], axis=-1)
    cos, sin = jnp.cos(ang), jnp.sin(ang)
    ones = jnp.ones((n_tokens, NOPE_DIM), F32)
    z32 = jnp.zeros((n_tokens, 32), F32)
    cos_l = jnp.concatenate([ones, cos, cos, z32], axis=-1)
    sin_l = jnp.concatenate([jnp.zeros_like(ones), -sin, sin, z32], axis=-1)
    ident_c = jnp.concatenate([jnp.ones((lead_rows, NOPE_DIM + ROPE_DIM), F32), jnp.zeros((lead_rows, 32), F32)], -1)
    ident_s = jnp.zeros((lead_rows, LANE), F32)
    return jnp.concatenate([ident_c, cos_l], axis=0), jnp.concatenate([ident_s, sin_l], axis=0)


PREP_TM_CTX = SEQ
PREP_TM_LAT = 512


def kernel(x_prompt, x_sample, c, cache_ckv, cache_krope, state_ssm_fwd, state_ssm_bwd, c_ctx, w_in, a_conv_w, w_a_out, ssm_conv_w, ssm_conv_b, ssm_a_log, ssm_dt_bias, ssm_d, ssm_norm_w, w_b_out, q_norm_w, w_uq, kv_norm_w, w_ukv, w_c_out, w_o, w_ada, b_ada, norm1_w, norm2_w, w_ff1, w_ff3, w_ff2, final_norm_w):
    w_in_p = _pad_in_weights(w_in)
    wq_t = _q_weights_t(w_uq)
    wk, wv_t = _kv_weights(w_ukv)
    wa, wb, wc, wo = (w.astype(BF16) for w in (w_a_out, w_b_out, w_c_out, w_o))
    w1, w3, w2 = (w.astype(BF16) for w in (w_ff1, w_ff3, w_ff2))

    cond = jnp.concatenate([c_ctx[None, :], c, jnp.zeros((N_MOD_ROWS - 1 - DEC_BATCH, D_MODEL), F32)], axis=0)
    mod = _modulation(cond, w_ada, b_ada).reshape(DEPTH, N_MOD_ROWS, 6, 1, D_MODEL)

    conv_wx = ssm_conv_w[..., :SSM_INNER]
    conv_wbc = ssm_conv_w[..., SSM_INNER:]
    conv_bx = ssm_conv_b[:, None, :SSM_INNER]
    conv_bbc = ssm_conv_b[:, None, SSM_INNER:]
    pad_h = ((0, 0), (0, 0), (0, 0), (0, LANE - SSM_HEADS))
    alog = jnp.pad(ssm_a_log[:, :, None, :], pad_h)
    dtb = jnp.pad(ssm_dt_bias[:, :, None, :], pad_h)
    dskip = jnp.repeat(ssm_d, SSM_HEAD_DIM, axis=-1)[:, :, None, :]
    h0f = state_ssm_fwd.reshape(DEC_BATCH, DEPTH, SSM_INNER, SSM_STATE)
    h0b = state_ssm_bwd.reshape(DEC_BATCH, DEPTH, SSM_INNER, SSM_STATE)

    cos_c, sin_c = _rope_tables(DEC_SEQ, PREP_TM_CTX)
    cos_l, sin_l = cos_c[PREP_TM_CTX:], sin_c[PREP_TM_CTX:]
    ident_cos, ident_sin = cos_c[:PREP_TM_CTX], sin_c[:PREP_TM_CTX]
    ones_tab = jnp.concatenate([ident_cos, ident_cos], axis=0)
    zeros_tab = jnp.zeros_like(ones_tab)
    cache_kr_pad = jnp.pad(cache_krope, ((0, 0), (0, 0), (0, 0), (64, 32)))

    xs = (x_prompt.reshape(N_CTX_TOK, D_MODEL), x_sample.reshape(N_LAT_TOK, D_MODEL))

    new_ckv, new_kr = [], []
    fin = [jnp.zeros((BATCH, DEPTH, SSM_INNER, SSM_STATE), F32) for _ in range(2)]
    for l in range(DEPTH):
        p, ps = _in_projection(xs, mod[l], norm1_w[l][None, :], w_in_p, l)

        xc, bcc = _ssd_conv(p, conv_wx[l], conv_bx[l], conv_wbc[l], conv_bbc[l])
        yc_f, yc_b, *fin = _ssd(ps, xc, bcc, alog[l], dtb[l], dskip[l],
                                row0=0, n_seq=BATCH, seq_len=SEQ, tq=SEQ, fin_layer=l, fin_prev=fin)
        yl_f, yl_b = _ssd(ps, xc, bcc, alog[l], dtb[l], dskip[l],
                          row0=N_CTX_TOK, n_seq=DEC_BATCH, seq_len=DEC_SEQ, tq=SSD_TQ_LAT, h0=(h0f, h0b, l))

        qnw, kvnw = q_norm_w[l][None, :], kv_norm_w[l][None, :]
        qt_c, k_c, vt_c, ckv_c = _mla_prep_tokens(
            ps, 0, BATCH, SEQ, PREP_TM_CTX, ident_cos, ident_sin, ident_cos.T, ident_sin.T,
            qnw, kvnw, wq_t, wk, wv_t, l, keep_ckv=True)
        qt_l, k_l, vt_l = _mla_prep_tokens(
            ps, N_CTX_TOK, DEC_BATCH, DEC_SEQ, PREP_TM_LAT, cos_l, sin_l, cos_l.T, sin_l.T,
            qnw, kvnw, wq_t, wk, wv_t, l, keep_ckv=False)
        k_p, vt_p = _mla_prep_cache(cache_ckv[:, l].reshape(DEC_BATCH * PAST_LEN, KV_LORA),
                                    cache_kr_pad[:, l].reshape(DEC_BATCH * PAST_LEN, LANE),
                                    ones_tab, zeros_tab, kvnw, wk, wv_t, PAST_LEN, l)
        new_ckv.append(ckv_c.reshape(BATCH, SEQ, KV_LORA))
        kr0 = OFF_KR - NP_MAIN + 64
        new_kr.append(ps[:N_CTX_TOK, kr0:kr0 + ROPE_DIM].reshape(BATCH, SEQ, ROPE_DIM))

        zc_c = _attention(qt_c, k_c, vt_c, BATCH, SEQ, SEQ)
        zc_l = _attention(qt_l, k_l, vt_l, DEC_BATCH, DEC_SEQ, PREP_TM_LAT, cache=(k_p, vt_p))

        x = _merge(p, (yc_f, yl_f), (yc_b, yl_b), (zc_c, zc_l), xs, mod[l], a_conv_w[l], wa, ssm_norm_w[l][None, :],
                   wb, wc, wo, l)
        if l + 1 < DEPTH:
            xs = (_ffn(x, mod[l], norm2_w[l][None, :], w1, w3, w2, l),)
        else:
            y_ctx, y_lat = _ffn(x, mod[l], norm2_w[l][None, :], w1, w3, w2, l, final_w=final_norm_w[None, :])

    y_prompt = y_ctx.reshape(BATCH, SEQ, D_MODEL)
    y_sample = y_lat.reshape(DEC_BATCH, DEC_SEQ, D_MODEL)
    hshape = (BATCH, DEPTH, SSM_HEADS, SSM_HEAD_DIM, SSM_STATE)
    return (y_prompt, y_sample,
            jnp.stack(new_ckv, axis=1), jnp.stack(new_kr, axis=1),
            fin[0].reshape(hshape), fin[1].reshape(hshape))
```
